```python
import math
import jax
import jax.numpy as jnp
from jax import lax
import numpy as np

D_MODEL = 1024
BATCH = 8
SEQ = 4096
DEPTH = 2

GRID_W = 64
CTX_LEN = 256
ROPE_THETA = 10000.0
BLOCK_Q = 128
MOE_BLOCK = 128

A_HEADS = 4
A_QK_DIM = 64
A_V_DIM = 128
A_WIDTH = A_HEADS * A_V_DIM
A_QK_COLS = 2 * A_HEADS * A_QK_DIM
B_HEADS = 4
B_HEAD = 64
B_WIDTH = B_HEADS * B_HEAD
B_DECAY_RANK = 64
B_A_RANK = 64
B_GATE_RANK = 128
C_HEADS = 4
C_NOPE = 64
C_ROPE = 32
C_V = 64
C_WIDTH = C_HEADS * C_V
C_Q_RANK = 256
C_KV_RANK = 128
MIX_WIDTH = A_WIDTH + B_WIDTH + C_WIDTH

N_A = 2 * A_QK_COLS + A_WIDTH
N_B = 3 * B_WIDTH + 2 * B_DECAY_RANK + 2 * B_A_RANK + B_GATE_RANK
N_C = C_Q_RANK + C_KV_RANK + C_ROPE
N_IN = N_A + N_B + N_C

D_FF = 3584
N_EXPERTS = 8
TOP_K = 2
N_DENSE = (DEPTH + 1) // 2
N_MOE = DEPTH // 2

ALPHA = (2.0 * DEPTH) ** 0.25
BETA = (8.0 * DEPTH) ** -0.25
LN_EPS = 1e-6
RMS_EPS = 1e-6
GN_EPS = 64e-5
F32 = jnp.float32

kernel_name = 'hybrid_diff_rwkv7_mla_moe_trunk'


def _layernorm(x, g=None, b=None, eps=LN_EPS):
    xf = x.astype(F32)
    xc = xf - jnp.mean(xf, -1, keepdims=True)
    y = xc * lax.rsqrt(jnp.mean(xc * xc, -1, keepdims=True) + eps)
    if g is not None:
        y = y * g.astype(F32) + b.astype(F32)
    return y.astype(x.dtype)


def _rmsnorm(x, g, eps=RMS_EPS):
    xf = x.astype(F32)
    y = xf * lax.rsqrt(jnp.mean(xf * xf, -1, keepdims=True) + eps) * g.astype(F32)
    return y.astype(x.dtype)


def _modulate(x, shift, scale):
    return _layernorm(x) * (1 + scale) + shift


def _lambda_init(layer):
    return 0.8 - 0.6 * math.exp(-0.3 * layer)


def _axial_angles(rows, cols, dim):
    quarter = dim // 4
    inv = ROPE_THETA ** (-jnp.arange(quarter, dtype=F32) / quarter)
    return rows[:, None] * inv, cols[:, None] * inv


def _rotate_pairs(x, ang):
    x1, x2 = jnp.split(x, 2, axis=-1)
    cs, sn = jnp.cos(ang)[:, None, :], jnp.sin(ang)[:, None, :]
    return jnp.concatenate([x1 * cs - x2 * sn, x1 * sn + x2 * cs], -1)


def _rope2d(x, ang_r, ang_c):
    xr, xc = jnp.split(x.astype(F32), 2, axis=-1)
    return jnp.concatenate([_rotate_pairs(xr, ang_r), _rotate_pairs(xc, ang_c)], -1).astype(x.dtype)


def _probs(q, k, scale):
    s = jnp.einsum('bhqd,bhkd->bhqk', q.astype(F32), k.astype(F32)) * scale
    return jax.nn.softmax(s, axis=-1)


def _attn_core(q, k, v, scale):
    return jnp.einsum('bhqk,bhkd->bhqd', _probs(q, k, scale), v.astype(F32))


def _diff_core(q1, q2, k1, k2, v, lam, scale):
    p = _probs(q1, k1, scale) - lam * _probs(q2, k2, scale)
    return jnp.einsum('bhqk,bhkd->bhqd', p, v.astype(F32))


def _sweep_query_blocks(fn, queries):
    Bn, H, L, _ = queries[0].shape
    nb = L // BLOCK_Q
    blocks = tuple(q.reshape(Bn, H, nb, BLOCK_Q, q.shape[-1]).transpose(2, 0, 1, 3, 4) for q in queries)
    out = lax.map(lambda qs: fn(*qs), blocks)
    return out.transpose(1, 2, 0, 3, 4).reshape(Bn, H, L, out.shape[-1])


def _merge_heads(o, dtype):
    Bn, H, L, d = o.shape
    return o.transpose(0, 2, 1, 3).reshape(Bn, L, H * d).astype(dtype)


def _diff_attention(pa_l, pa_c, lq1, lk1, lq2, lk2, norm_g, lam_init, rope, with_ctx):
    lam = (jnp.exp(jnp.sum(lq1.astype(F32) * lk1.astype(F32)))
           - jnp.exp(jnp.sum(lq2.astype(F32) * lk2.astype(F32))) + lam_init)
    scale = A_QK_DIM ** -0.5

    def heads(pa, rotary):
        Bn, L, _ = pa.shape
        q = pa[..., :A_QK_COLS].reshape(Bn, L, 2 * A_HEADS, A_QK_DIM)
        k = pa[..., A_QK_COLS:2 * A_QK_COLS].reshape(Bn, L, 2 * A_HEADS, A_QK_DIM)
        v = pa[..., 2 * A_QK_COLS:].reshape(Bn, L, A_HEADS, A_V_DIM)
        if rotary:
            q, k = _rope2d(q, *rope), _rope2d(k, *rope)
        return tuple(t.transpose(0, 2, 1, 3).astype(F32) for t in (q, k, v))

    def finish(o, dtype):
        return _merge_heads(_rmsnorm(o, norm_g) * (1.0 - lam_init), dtype)

    ql, kl, vl = heads(pa_l, True)
    qc, kc, vc = heads(pa_c, False)
    k_all = jnp.concatenate([kc, kl], axis=2)
    v_all = jnp.concatenate([vc, vl], axis=2)
    k1, k2 = k_all[:, 0::2], k_all[:, 1::2]
    o_l = _sweep_query_blocks(lambda q1, q2: _diff_core(q1, q2, k1, k2, v_all, lam, scale),
                              (ql[:, 0::2], ql[:, 1::2]))
    out_l = finish(o_l, pa_l.dtype)
    if not with_ctx:
        return out_l, None
    o_c = _diff_core(qc[:, 0::2], qc[:, 1::2], kc[:, 0::2], kc[:, 1::2], vc, lam, scale)
    return out_l, finish(o_c, pa_c.dtype)


def _token_shift(z, mu):
    zp = jnp.pad(z, ((0, 0), (1, 1), (0, 0)))
    return z + mu * (0.5 * (zp[:, :-2] + zp[:, 2:]) - z)


def _rwkv_prep(pb, shift_mu, w0, w2, a0, a2, g2, k_k, k_a):
    Bn, L, _ = pb.shape
    z = _token_shift(pb.astype(F32), shift_mu.astype(F32))
    heads = lambda t: t.reshape(t.shape[:-1] + (B_HEADS, B_HEAD))
    r, k, v = z[..., :B_WIDTH], z[..., B_WIDTH:2 * B_WIDTH], z[..., 2 * B_WIDTH:3 * B_WIDTH]
    o = 3 * B_WIDTH
    wd = z[..., o:o + 2 * B_DECAY_RANK].reshape(Bn, L, 2, B_DECAY_RANK)
    o += 2 * B_DECAY_RANK
    ad = z[..., o:o + 2 * B_A_RANK].reshape(Bn, L, 2, B_A_RANK)
    o += 2 * B_A_RANK
    gd = z[..., o:]
    w = -jax.nn.softplus(-(w0.astype(F32) + jnp.einsum('bldr,drc->bldc', jnp.tanh(wd), w2.astype(F32)))) - 0.5
    decay = jnp.exp(-jnp.exp(w))
    a = jax.nn.sigmoid(a0.astype(F32) + jnp.einsum('bldr,drc->bldc', ad, a2.astype(F32)))
    g = jnp.matmul(jax.nn.sigmoid(gd), g2.astype(F32))
    kk = heads(k * k_k.astype(F32))
    kk = kk / jnp.maximum(jnp.linalg.norm(kk, axis=-1, keepdims=True), 1e-12)
    k_dir = k[:, :, None, :] * (1.0 + (a - 1.0) * k_a.astype(F32))
    return heads(r), heads(v), kk, heads(decay), heads(k_dir), heads(a), g


def _wkv_scan(S0, r, w, k, v, a, b, reverse, emit):
    xs = tuple(jnp.swapaxes(t, 0, 1) for t in (r, w, k, v, a, b))

    def step(S, inp):
        r_t, w_t, k_t, v_t, a_t, b_t = inp
        sa = jnp.einsum('bhij,bhj->bhi', S, a_t)
        S = S * w_t[:, :, None, :] + sa[..., None] * b_t[:, :, None, :] + v_t[..., None] * k_t[:, :, None, :]
        return S, (jnp.einsum('bhij,bhj->bhi', S, r_t) if emit else None)

    S, ys = lax.scan(step, S0, xs, reverse=reverse)
    return S, (jnp.swapaxes(ys, 0, 1) if emit else None)


def _rwkv_bonus(r, k_d, v, r_k):
    return jnp.sum(r * k_d * r_k, axis=-1, keepdims=True) * v


def _rwkv_out(y, g, lnx_g, lnx_b, dtype):
    Bn, L = y.shape[:2]
    yn = _layernorm(y, lnx_g.reshape(B_HEADS, B_HEAD), lnx_b.reshape(B_HEADS, B_HEAD), GN_EPS)
    return (yn.reshape(Bn, L, B_WIDTH) * g).astype(dtype)


def _rwkv7(pb_l, pb_c, shift_mu, w0, w2, a0, a2, g2, k_k, k_a, r_k, lnx_g, lnx_b, with_ctx):
    prm = (shift_mu, w0, w2, a0, a2, g2, k_k, k_a)
    r_l, v_l, kk_l, dec_l, kd_l, a_l, g_l = _rwkv_prep(pb_l, *prm)
    r_c, v_c, kk_c, dec_c, kd_c, a_c, g_c = _rwkv_prep(pb_c, *prm)
    r_k = r_k.astype(F32)
    S0 = jnp.zeros((pb_l.shape[0], B_HEADS, B_HEAD, B_HEAD), F32)
    y_l, y_c = 0.0, 0.0
    for d, rev in ((0, False), (1, True)):
        S_ctx, yc = _wkv_scan(S0, r_c, dec_c[:, :, d], kd_c[:, :, d], v_c, -kk_c, kk_c * a_c[:, :, d], rev, with_ctx)
        _, yl = _wkv_scan(S_ctx, r_l, dec_l[:, :, d], kd_l[:, :, d], v_l, -kk_l, kk_l * a_l[:, :, d], rev, True)
        y_l = y_l + yl + _rwkv_bonus(r_l, kd_l[:, :, d], v_l, r_k)
        if with_ctx:
            y_c = y_c + yc + _rwkv_bonus(r_c, kd_c[:, :, d], v_c, r_k)
    out_l = _rwkv_out(y_l, g_l, lnx_g, lnx_b, pb_l.dtype)
    if not with_ctx:
        return out_l, None
    return out_l, _rwkv_out(y_c, g_c, lnx_g, lnx_b, pb_c.dtype)


def _mla(pc_l, pc_c, q_norm_g, w_uq, kv_norm_g, w_ukv, rope, with_ctx):
    scale = (C_NOPE + C_ROPE) ** -0.5

    def heads(pc, rotary):
        Bn, L, _ = pc.shape
        cq = _rmsnorm(pc[..., :C_Q_RANK], q_norm_g)
        ckv = _rmsnorm(pc[..., C_Q_RANK:C_Q_RANK + C_KV_RANK], kv_norm_g)
        k_pe = pc[..., C_Q_RANK + C_KV_RANK:][:, :, None, :]
        q = (cq @ w_uq).reshape(Bn, L, C_HEADS, C_NOPE + C_ROPE)
        kv = (ckv @ w_ukv).reshape(Bn, L, C_HEADS, C_NOPE + C_V)
        q_nope, q_pe = q[..., :C_NOPE], q[..., C_NOPE:]
        k_nope, v = kv[..., :C_NOPE], kv[..., C_NOPE:]
        if rotary:
            q_pe, k_pe = _rope2d(q_pe, *rope), _rope2d(k_pe, *rope)
        q = jnp.concatenate([q_nope, q_pe], -1)
        k = jnp.concatenate([k_nope, jnp.broadcast_to(k_pe, (Bn, L, C_HEADS, C_ROPE))], -1)
        return tuple(t.transpose(0, 2, 1, 3).astype(F32) for t in (q, k, v))

    ql, kl, vl = heads(pc_l, True)
    qc, kc, vc = heads(pc_c, False)
    k_all = jnp.concatenate([kc, kl], axis=2)
    v_all = jnp.concatenate([vc, vl], axis=2)
    o_l = _sweep_query_blocks(lambda q: _attn_core(q, k_all, v_all, scale), (ql,))
    out_l = _merge_heads(o_l, pc_l.dtype)
    if not with_ctx:
        return out_l, None
    return out_l, _merge_heads(_attn_core(qc, kc, vc, scale), pc_c.dtype)


def _swiglu(h, w1, w3, w2):
    return jnp.matmul(jax.nn.silu(h @ w1) * (h @ w3), w2)


def _moe(h, router, w1, w3, w2):
    Bn, L, D = h.shape
    t = h.reshape(-1, D)
    T = t.shape[0]
    logits = jnp.matmul(t, router).astype(F32)
    top_v, top_i = lax.top_k(logits, TOP_K)
    gates = jax.nn.softmax(top_v, axis=-1).astype(t.dtype)
    e_flat = top_i.reshape(-1)
    order = jnp.argsort(e_flat)
    e_sorted = e_flat[order]
    tok_sorted = order // TOP_K
    gate_sorted = gates.reshape(-1)[order]
    counts = jnp.bincount(e_flat, length=N_EXPERTS)
    padded = (counts + MOE_BLOCK - 1) // MOE_BLOCK * MOE_BLOCK
    start = jnp.cumsum(counts) - counts
    pend = jnp.cumsum(padded)
    pstart = pend - padded
    dest = pstart[e_sorted] + jnp.arange(T * TOP_K) - start[e_sorted]
    n_blocks = T * TOP_K // MOE_BLOCK + N_EXPERTS
    buf = jnp.zeros((n_blocks * MOE_BLOCK, D), t.dtype).at[dest].set(t[tok_sorted])
    block_e = jnp.minimum(jnp.searchsorted(pend, jnp.arange(n_blocks) * MOE_BLOCK, side='right'), N_EXPERTS - 1)

    def expert_block(args):
        xb, e = args
        return _swiglu(xb, w1[e], w3[e], w2[e])

    y_buf = lax.map(expert_block, (buf.reshape(n_blocks, MOE_BLOCK, D), block_e)).reshape(-1, D)
    out = jnp.zeros_like(t).at[tok_sorted].add(y_buf[dest] * gate_sorted[:, None])
    return out.reshape(Bn, L, D)


def setup_inputs(seed: int = 0) -> dict:
    key = jax.random.key(seed)
    keys = list(jax.random.split(key, 48))

    def nrm(shape, s):
        return jax.random.normal(keys.pop(), shape, jnp.float32) * s

    def unif(shape, lo, hi):
        return jax.random.uniform(keys.pop(), shape, jnp.float32, lo, hi)

    L = DEPTH
    return {
        'x': nrm((BATCH, SEQ, D_MODEL), 1.0),
        'c': nrm((BATCH, D_MODEL), 1.0),
        'ctx': nrm((BATCH, CTX_LEN, D_MODEL), 1.0),
        'c_ctx': nrm((D_MODEL,), 1.0),
        'ada_w': nrm((L, D_MODEL, 6 * D_MODEL), 0.5 * D_MODEL ** -0.5),
        'ada_b': nrm((L, 6 * D_MODEL), 0.02),
        'w_in': nrm((L, D_MODEL, N_IN), D_MODEL ** -0.5),
        'w_out': nrm((L, MIX_WIDTH, D_MODEL), BETA * MIX_WIDTH ** -0.5),
        'ln1_g': 1.0 + nrm((L, D_MODEL), 0.02),
        'ln1_b': nrm((L, D_MODEL), 0.02),
        'ln2_g': 1.0 + nrm((L, D_MODEL), 0.02),
        'ln2_b': nrm((L, D_MODEL), 0.02),
        'lam_q1': nrm((L, A_QK_DIM), 0.1),
        'lam_k1': nrm((L, A_QK_DIM), 0.1),
        'lam_q2': nrm((L, A_QK_DIM), 0.1),
        'lam_k2': nrm((L, A_QK_DIM), 0.1),
        'diff_norm_g': 1.0 + nrm((L, A_V_DIM), 0.02),
        'shift_mu': unif((L, N_B), 0.0, 1.0),
        'w0': unif((L, 2, B_WIDTH), -6.5, -0.5),
        'w2': nrm((L, 2, B_DECAY_RANK, B_WIDTH), 0.1 * B_DECAY_RANK ** -0.5),
        'a0': nrm((L, 2, B_WIDTH), 0.1),
        'a2': nrm((L, 2, B_A_RANK, B_WIDTH), 0.1 * B_A_RANK ** -0.5),
        'g2': nrm((L, B_GATE_RANK, B_WIDTH), B_GATE_RANK ** -0.5),
        'k_k': 0.85 + nrm((L, B_WIDTH), 0.02),
        'k_a': 1.0 + nrm((L, B_WIDTH), 0.02),
        'r_k': nrm((L, B_HEADS, B_HEAD), 0.1),
        'lnx_g': 1.0 + nrm((L, B_WIDTH), 0.02),
        'lnx_b': nrm((L, B_WIDTH), 0.02),
        'q_norm_g': 1.0 + nrm((L, C_Q_RANK), 0.02),
        'w_uq': nrm((L, C_Q_RANK, C_HEADS * (C_NOPE + C_ROPE)), C_Q_RANK ** -0.5),
        'kv_norm_g': 1.0 + nrm((L, C_KV_RANK), 0.02),
        'w_ukv': nrm((L, C_KV_RANK, C_HEADS * (C_NOPE + C_V)), C_KV_RANK ** -0.5),
        'ff_w1': nrm((N_DENSE, D_MODEL, D_FF), D_MODEL ** -0.5),
        'ff_w3': nrm((N_DENSE, D_MODEL, D_FF), D_MODEL ** -0.5),
        'ff_w2': nrm((N_DENSE, D_FF, D_MODEL), BETA * D_FF ** -0.5),
        'router': nrm((N_MOE, D_MODEL, N_EXPERTS), D_MODEL ** -0.5),
        'moe_w1': nrm((N_MOE, N_EXPERTS, D_MODEL, D_FF), D_MODEL ** -0.5),
        'moe_w3': nrm((N_MOE, N_EXPERTS, D_MODEL, D_FF), D_MODEL ** -0.5),
        'moe_w2': nrm((N_MOE, N_EXPERTS, D_FF, D_MODEL), BETA * D_FF ** -0.5),
    }


def reference(x, c, ctx, c_ctx, ada_w, ada_b, w_in, w_out, ln1_g, ln1_b, ln2_g, ln2_b,
              lam_q1, lam_k1, lam_q2, lam_k2, diff_norm_g, shift_mu, w0, w2, a0, a2, g2,
              k_k, k_a, r_k, lnx_g, lnx_b, q_norm_g, w_uq, kv_norm_g, w_ukv,
              ff_w1, ff_w3, ff_w2, router, moe_w1, moe_w3, moe_w2):
    n_rows = x.shape[1] // GRID_W
    rows = jnp.repeat(jnp.arange(n_rows, dtype=F32), GRID_W)
    cols = jnp.tile(jnp.arange(GRID_W, dtype=F32), n_rows)
    rope_a = _axial_angles(rows, cols, A_QK_DIM)
    rope_c = _axial_angles(rows, cols, C_ROPE)
    cond_l = jax.nn.silu(c)
    cond_c = jax.nn.silu(c_ctx)
    sa, sb, sc = slice(0, N_A), slice(N_A, N_A + N_B), slice(N_A + N_B, N_IN)
    xl, xc = x, ctx
    for i in range(DEPTH):
        with_ctx = i < DEPTH - 1
        ml = jnp.split((cond_l @ ada_w[i] + ada_b[i])[:, None, :], 6, axis=-1)
        mc = jnp.split((cond_c @ ada_w[i] + ada_b[i])[None, None, :], 6, axis=-1)
        p_l = _modulate(xl, ml[0], ml[1]) @ w_in[i]
        p_c = _modulate(xc, mc[0], mc[1]) @ w_in[i]
        a_l, a_c = _diff_attention(p_l[..., sa], p_c[..., sa], lam_q1[i], lam_k1[i], lam_q2[i], lam_k2[i],
                                   diff_norm_g[i], _lambda_init(i), rope_a, with_ctx)
        b_l, b_c = _rwkv7(p_l[..., sb], p_c[..., sb], shift_mu[i], w0[i], w2[i], a0[i], a2[i], g2[i],
                          k_k[i], k_a[i], r_k[i], lnx_g[i], lnx_b[i], with_ctx)
        c_l, c_c = _mla(p_l[..., sc], p_c[..., sc], q_norm_g[i], w_uq[i], kv_norm_g[i], w_ukv[i], rope_c, with_ctx)

        def ffn(h):
            j = i // 2
            if i % 2 == 0:
                return _swiglu(h, ff_w1[j], ff_w3[j], ff_w2[j])
            return _moe(h, router[j], moe_w1[j], moe_w3[j], moe_w2[j])

        o_l = jnp.concatenate([a_l, b_l, c_l], axis=-1) @ w_out[i]
        xl = _layernorm(ALPHA * xl + ml[2] * o_l, ln1_g[i], ln1_b[i])
        xl = _layernorm(ALPHA * xl + ml[5] * ffn(_modulate(xl, ml[3], ml[4])), ln2_g[i], ln2_b[i])
        if with_ctx:
            o_c = jnp.concatenate([a_c, b_c, c_c], axis=-1) @ w_out[i]
            xc = _layernorm(ALPHA * xc + mc[2] * o_c, ln1_g[i], ln1_b[i])
            xc = _layernorm(ALPHA * xc + mc[5] * ffn(_modulate(xc, mc[3], mc[4])), ln2_g[i], ln2_b[i])
    return xl
```

```python
import functools
import math

import jax
import jax.numpy as jnp
from jax import lax
from jax.experimental import pallas as pl
from jax.experimental.pallas import tpu as pltpu

F32 = jnp.float32
BF16 = jnp.bfloat16

D_MODEL = 1024
GRID_W = 64
ROPE_THETA = 10000.0
A_HEADS, A_QK_DIM, A_V_DIM = 4, 64, 128
A_WIDTH = A_HEADS * A_V_DIM
A_QK_COLS = 2 * A_HEADS * A_QK_DIM
B_HEADS, B_HEAD = 4, 64
B_WIDTH = B_HEADS * B_HEAD
B_DECAY_RANK, B_A_RANK, B_GATE_RANK = 64, 64, 128
C_HEADS, C_NOPE, C_ROPE, C_V = 4, 64, 32, 64
C_WIDTH = C_HEADS * C_V
C_Q_RANK, C_KV_RANK = 256, 128
N_A = 2 * A_QK_COLS + A_WIDTH
N_B = 3 * B_WIDTH + 2 * B_DECAY_RANK + 2 * B_A_RANK + B_GATE_RANK
N_C = C_Q_RANK + C_KV_RANK + C_ROPE
N_C_PAD = C_Q_RANK + C_KV_RANK + 128
D_FF = 3584
N_EXPERTS = 8
TOP_K = 2
LN_EPS = 1e-6
RMS_EPS = 1e-6
GN_EPS = 64e-5

LANES = 128
TM = 256
CHUNK = 64
ATT_TK = 256
MOE_ROWS = 512
GATHER_ROWS = 512
FF_CHUNK = 512
VMEM_LIMIT = 56 * 1024 * 1024

NN = (((1,), (0,)), ((), ()))
NT = (((1,), (1,)), ((), ()))
TN = (((0,), (0,)), ((), ()))


def _params(sem, vmem=VMEM_LIMIT):
    return pltpu.CompilerParams(dimension_semantics=sem, vmem_limit_bytes=vmem)


def _split(x, n):
    parts, r = [], x
    for _ in range(n):
        p = r.astype(BF16)
        parts.append(p)
        r = r - p.astype(F32)
    return parts


def _dot(a, b, dn=NN):
    return lax.dot_general(a, b, dn, preferred_element_type=F32)


def _dot1(a, b, dn=NN):
    return _dot(a.astype(BF16), b.astype(BF16), dn)


def _dot3(a, b, dn=NN):
    a0, a1 = _split(a, 2)
    b0, b1 = _split(b, 2)
    return _dot(a0, b0, dn) + (_dot(a0, b1, dn) + _dot(a1, b0, dn))


def _dot6(a, b, dn=NN):
    a0, a1, a2 = _split(a, 3)
    b0, b1, b2 = _split(b, 3)
    lo = _dot(a1, b1, dn) + (_dot(a0, b2, dn) + _dot(a2, b0, dn))
    return _dot(a0, b0, dn) + ((_dot(a0, b1, dn) + _dot(a1, b0, dn)) + lo)


def _dot_exact_lhs(a_bf16, b, dn=NN, n=3):
    parts = _split(b, n)
    out = _dot(a_bf16, parts[-1], dn)
    for p in parts[-2::-1]:
        out = out + _dot(a_bf16, p, dn)
    return out


def _dot_exact_rhs(a, b_bf16, dn=NN, n=3):
    parts = _split(a, n)
    out = _dot(parts[-1], b_bf16, dn)
    for p in parts[-2::-1]:
        out = out + _dot(p, b_bf16, dn)
    return out


def _ln(x, eps):
    mu = jnp.mean(x, axis=-1, keepdims=True)
    xc = x - mu
    return xc * lax.rsqrt(jnp.mean(xc * xc, axis=-1, keepdims=True) + eps)


def _sigmoid(x):
    return 1.0 / (1.0 + jnp.exp(-x))


def _silu(x):
    return x * _sigmoid(x)


def _group_ones(width, group):
    r = lax.broadcasted_iota(jnp.int32, (width, width), 0) // group
    c = lax.broadcasted_iota(jnp.int32, (width, width), 1) // group
    return (r == c).astype(BF16)


def _partner(x, half):
    lane = lax.broadcasted_iota(jnp.int32, x.shape, 1)
    up = pltpu.roll(x, LANES - half, 1)
    dn = pltpu.roll(x, half, 1)
    return jnp.where((lane % (2 * half)) < half, up, dn)


def _rope(x, cos, sin, half):
    return x * cos + _partner(x, half) * sin


def _ada_kernel(c_ref, w_ref, b_ref, o_ref):
    o_ref[...] = _dot3(_silu(c_ref[...]), w_ref[...]) + b_ref[...]


def _ada(cond, w, b):
    rows, d = cond.shape
    n = w.shape[1]
    tn = 1536
    return pl.pallas_call(
        _ada_kernel,
        grid=(n // tn,),
        in_specs=[pl.BlockSpec((rows, d), lambda j: (0, 0)),
                  pl.BlockSpec((d, tn), lambda j: (0, j)),
                  pl.BlockSpec((1, tn), lambda j: (0, j))],
        out_specs=pl.BlockSpec((rows, tn), lambda j: (0, j)),
        out_shape=jax.ShapeDtypeStruct((rows, n), F32),
        compiler_params=_params(("parallel",)),
        name="ada",
    )(cond, w, b.reshape(1, n))


def _inproj_kernel(x_ref, mod_ref, w_ref, cos_ref, sin_ref,
                   q_ref, k_ref, v_ref, pb_ref, pc_ref):
    mod = mod_ref[...]
    h = (_ln(x_ref[...], LN_EPS) * (1.0 + mod[1:2]) + mod[0:1]).astype(BF16)
    cos, sin = cos_ref[...], sin_ref[...]
    scale = A_QK_DIM ** -0.5
    for j in range(A_QK_COLS // LANES):
        sl = slice(j * LANES, (j + 1) * LANES)
        qj = _dot(h, w_ref[:, sl])
        q_ref[:, sl] = (_rope(qj, cos, sin, A_QK_DIM // 4) * scale).astype(BF16)
        kj = _dot(h, w_ref[:, A_QK_COLS + j * LANES:A_QK_COLS + (j + 1) * LANES])
        k_ref[:, sl] = _rope(kj, cos, sin, A_QK_DIM // 4).astype(BF16)
    v_ref[...] = _dot(h, w_ref[:, 2 * A_QK_COLS:N_A]).astype(BF16)
    pb_ref[...] = _dot(h, w_ref[:, N_A:N_A + N_B])
    pc_ref[...] = _dot(h, w_ref[:, N_A + N_B:])


def _mod_spec(ctx_len, d):
    ct = ctx_len // TM
    return pl.BlockSpec((None, None, 6, d), lambda b, i: (b, jnp.where(i >= ct, 1, 0), 0, 0))


def _inproj(xs, mods, w_p, cos_a, sin_a, ctx_len):
    bn, s, d = xs.shape
    n_tiles = s // TM
    n_w = w_p.shape[1]
    row = lambda width: pl.BlockSpec((None, TM, width), lambda b, i: (b, i, 0))
    outs = [jax.ShapeDtypeStruct((bn, s, A_QK_COLS), BF16),
            jax.ShapeDtypeStruct((bn, s, A_QK_COLS), BF16),
            jax.ShapeDtypeStruct((bn, s, A_WIDTH), BF16),
            jax.ShapeDtypeStruct((bn, s, N_B), F32),
            jax.ShapeDtypeStruct((bn, s, N_C_PAD), F32)]
    return pl.pallas_call(
        _inproj_kernel,
        grid=(bn, n_tiles),
        in_specs=[row(d),
                  _mod_spec(ctx_len, d),
                  pl.BlockSpec((d, n_w), lambda b, i: (0, 0)),
                  pl.BlockSpec((TM, LANES), lambda b, i: (i, 0)),
                  pl.BlockSpec((TM, LANES), lambda b, i: (i, 0))],
        out_specs=[row(A_QK_COLS), row(A_QK_COLS), row(A_WIDTH), row(N_B), row(N_C_PAD)],
        out_shape=outs,
        compiler_params=_params(("parallel", "parallel")),
        name="inproj",
    )(xs, mods, w_p, cos_a, sin_a)


def _mla_prep_kernel(pc_ref, qg_ref, kvg_ref, wq_ref, wk_ref, wv_ref, cos_ref, sin_ref,
                     q_ref, k_ref, v_ref):
    pc = pc_ref[...]
    cq = pc[:, :C_Q_RANK]
    cq = cq * lax.rsqrt(jnp.mean(cq * cq, axis=-1, keepdims=True) + RMS_EPS) * qg_ref[...]
    ckv = pc[:, C_Q_RANK:C_Q_RANK + C_KV_RANK]
    ckv = ckv * lax.rsqrt(jnp.mean(ckv * ckv, axis=-1, keepdims=True) + RMS_EPS) * kvg_ref[...]
    cos, sin = cos_ref[...], sin_ref[...]
    kpe = _rope(pc[:, C_Q_RANK + C_KV_RANK:], cos, sin, C_ROPE // 4)
    cqb, ckvb = cq.astype(BF16), ckv.astype(BF16)
    scale = (C_NOPE + C_ROPE) ** -0.5
    for h in range(C_HEADS):
        sl = slice(h * LANES, (h + 1) * LANES)
        qh = _dot(cqb, wq_ref[:, sl])
        q_ref[:, sl] = (_rope(qh, cos, sin, C_ROPE // 4) * scale).astype(BF16)
        k_ref[:, sl] = (_dot(ckvb, wk_ref[:, sl]) + kpe).astype(BF16)
    v_ref[...] = _dot(ckvb, wv_ref[...]).astype(BF16)


def _mla_prep(pc, qg, kvg, wq_p, wk_p, wv_p, cos_c, sin_c):
    bn, s, _ = pc.shape
    row = lambda width: pl.BlockSpec((None, TM, width), lambda b, i: (b, i, 0))
    full = lambda a: pl.BlockSpec(a.shape, lambda b, i: (0,) * a.ndim)
    tab = pl.BlockSpec((TM, LANES), lambda b, i: (i, 0))
    hw = C_HEADS * LANES
    return pl.pallas_call(
        _mla_prep_kernel,
        grid=(bn, s // TM),
        in_specs=[row(N_C_PAD), full(qg), full(kvg), full(wq_p), full(wk_p), full(wv_p), tab, tab],
        out_specs=[row(hw), row(hw), row(C_WIDTH)],
        out_shape=[jax.ShapeDtypeStruct((bn, s, hw), BF16),
                   jax.ShapeDtypeStruct((bn, s, hw), BF16),
                   jax.ShapeDtypeStruct((bn, s, C_WIDTH), BF16)],
        compiler_params=_params(("parallel", "parallel")),
        name="mla_prep",
    )(pc, qg, kvg, wq_p, wk_p, wv_p, cos_c, sin_c)


def _attn_kernel(lam_ref, g_ref, q_ref, k_ref, v_ref, o_ref, *, mode, ctx_tiles, ctx_chunks,
                 all_chunks, lam_init):
    qi = pl.program_id(2)
    q = q_ref[...]
    tq = q.shape[0]
    lane = lax.broadcasted_iota(jnp.int32, (1, LANES), 1)
    low = lane < (LANES // 2)
    if mode == "diff":
        qs = (jnp.where(low, q, jnp.zeros_like(q)), jnp.where(low, jnp.zeros_like(q), q))
    else:
        qs = (q[:, :LANES], q[:, LANES:])
    n_chunks = jnp.where(qi < ctx_tiles, ctx_chunks, all_chunks)

    def body(j, carry):
        off = pl.multiple_of(j * ATT_TK, ATT_TK)
        kc = k_ref[pl.ds(off, ATT_TK), :]
        vc = v_ref[pl.ds(off, ATT_TK), :]
        ks = (kc, kc) if mode == "diff" else (kc[:, :LANES], kc[:, LANES:])
        out = []
        for m in range(2):
            m_prev, l_prev, acc = carry[3 * m:3 * m + 3]
            s = _dot(qs[m], ks[m], NT)
            m_new = jnp.maximum(m_prev, jnp.max(s, axis=-1, keepdims=True))
            alpha = jnp.exp(m_prev - m_new)
            p = jnp.exp(s - m_new)
            l_new = alpha * l_prev + jnp.sum(p, axis=-1, keepdims=True)
            acc = alpha * acc + _dot(p.astype(BF16), vc)
            out += [m_new, l_new, acc]
        return tuple(out)

    neg = jnp.full((tq, 1), -1e30, F32)
    zero1 = jnp.zeros((tq, 1), F32)
    zacc = jnp.zeros((tq, LANES), F32)
    m0, l0, a0, m1, l1, a1 = lax.fori_loop(0, n_chunks, body, (neg, zero1, zacc, neg, zero1, zacc))
    o0, o1 = a0 / l0, a1 / l1
    if mode == "diff":
        lp = lam_ref[...]
        lam = (jnp.exp(jnp.sum(lp[0:1] * lp[1:2], axis=-1, keepdims=True))
               - jnp.exp(jnp.sum(lp[2:3] * lp[3:4], axis=-1, keepdims=True)) + lam_init)
        o = o0 - lam * o1
        o = o * lax.rsqrt(jnp.mean(o * o, axis=-1, keepdims=True) + RMS_EPS) * g_ref[...]
        o = o * (1.0 - lam_init)
    else:
        o = jnp.where(low, o0, o1)
    o_ref[...] = o.astype(o_ref.dtype)


def _attention(q, k, v, lam_p, g, *, mode, ctx_len, lam_init=0.0):
    bn, s, _ = q.shape
    qw = LANES if mode == "diff" else 2 * LANES
    groups = q.shape[2] // qw
    kern = functools.partial(_attn_kernel, mode=mode, ctx_tiles=ctx_len // TM,
                             ctx_chunks=ctx_len // ATT_TK, all_chunks=s // ATT_TK, lam_init=lam_init)
    return pl.pallas_call(
        kern,
        grid=(bn, groups, s // TM),
        in_specs=[pl.BlockSpec(lam_p.shape, lambda b, h, i: (0, 0)),
                  pl.BlockSpec(g.shape, lambda b, h, i: (0, 0)),
                  pl.BlockSpec((None, TM, qw), lambda b, h, i: (b, i, h)),
                  pl.BlockSpec((None, s, qw), lambda b, h, i: (b, 0, h)),
                  pl.BlockSpec((None, s, LANES), lambda b, h, i: (b, 0, h))],
        out_specs=pl.BlockSpec((None, TM, LANES), lambda b, h, i: (b, i, h)),
        out_shape=jax.ShapeDtypeStruct((bn, s, groups * LANES), BF16),
        compiler_params=_params(("parallel", "parallel", "parallel")),
        name="attn_" + mode,
    )(lam_p, g, q, k, v)


def _rwkv_prep_kernel(pb_ref, prev_ref, next_ref, mu_ref, w0_ref, w2_ref, a0_ref, a2_ref, g2_ref,
                      kk_ref, ka_ref, rk_ref,
                      r_out, v_out, kkn_out, ld_out, kd_out, beta_out, g_out, bonus_out,
                      *, ctx_tiles, n_tiles):
    i = pl.program_id(1)
    x = pb_ref[...]
    row = lax.broadcasted_iota(jnp.int32, (TM, 1), 0)
    has_prev = jnp.logical_and(i != 0, i != ctx_tiles)
    has_next = jnp.logical_and(i != ctx_tiles - 1, i != n_tiles - 1)
    prev_edge = jnp.where(has_prev, prev_ref[7:8, :], 0.0)
    next_edge = jnp.where(has_next, next_ref[0:1, :], 0.0)
    xp = jnp.where(row == 0, prev_edge, pltpu.roll(x, 1, 0))
    xn = jnp.where(row == TM - 1, next_edge, pltpu.roll(x, TM - 1, 0))
    z = x + mu_ref[...] * (0.5 * (xp + xn) - x)

    r = z[:, :B_WIDTH]
    k = z[:, B_WIDTH:2 * B_WIDTH]
    v = z[:, 2 * B_WIDTH:3 * B_WIDTH]
    o = 3 * B_WIDTH
    wd = z[:, o:o + 2 * B_DECAY_RANK]
    o += 2 * B_DECAY_RANK
    ad = z[:, o:o + 2 * B_A_RANK]
    o += 2 * B_A_RANK
    gd = z[:, o:]

    u = w0_ref[...] + _dot1(jnp.tanh(wd), w2_ref[...])
    nu = -u
    w_raw = -(jnp.maximum(nu, 0.0) + jnp.log(1.0 + jnp.exp(-jnp.abs(nu)))) - 0.5
    ld = -jnp.exp(w_raw)
    lr = _sigmoid(a0_ref[...] + _dot1(ad, a2_ref[...]))
    g_out[...] = _dot1(_sigmoid(gd), g2_ref[...])

    ones = _group_ones(B_WIDTH, B_HEAD)
    kk = k * kk_ref[...]
    norm = jnp.sqrt(_dot_exact_rhs(kk * kk, ones))
    kkn = kk / jnp.maximum(norm, 1e-12)
    ka = ka_ref[...]
    kd_sum = jnp.zeros_like(k)
    for d in range(2):
        lr_d = lr[:, d * B_WIDTH:(d + 1) * B_WIDTH]
        kd = k * (1.0 + (lr_d - 1.0) * ka)
        kd_sum = kd_sum + kd
        ld_out[d] = ld[:, d * B_WIDTH:(d + 1) * B_WIDTH]
        kd_out[d] = kd
        beta_out[d] = kkn * lr_d
    bonus_out[...] = _dot_exact_rhs(r * kd_sum * rk_ref[...], ones) * v
    r_out[...] = r
    v_out[...] = v
    kkn_out[...] = kkn


def _rwkv_prep(pb, mu, w0, w2bd, a0, a2bd, g2, k_k, k_a, r_k, ctx_len):
    bn, s, _ = pb.shape
    n_tiles = s // TM
    eight = TM // 8
    row = lambda width: pl.BlockSpec((None, TM, width), lambda b, i: (b, i, 0))
    drow = pl.BlockSpec((2, None, TM, B_WIDTH), lambda b, i: (0, b, i, 0))
    full = lambda a: pl.BlockSpec(a.shape, lambda b, i: (0,) * a.ndim)
    one = jax.ShapeDtypeStruct((bn, s, B_WIDTH), F32)
    two = jax.ShapeDtypeStruct((2, bn, s, B_WIDTH), F32)
    kern = functools.partial(_rwkv_prep_kernel, ctx_tiles=ctx_len // TM, n_tiles=n_tiles)
    params = (mu, w0, w2bd, a0, a2bd, g2, k_k, k_a, r_k)
    return pl.pallas_call(
        kern,
        grid=(bn, n_tiles),
        in_specs=[row(N_B),
                  pl.BlockSpec((None, 8, N_B), lambda b, i: (b, jnp.maximum(i * eight - 1, 0), 0)),
                  pl.BlockSpec((None, 8, N_B), lambda b, i: (b, jnp.minimum((i + 1) * eight, s // 8 - 1), 0)),
                  ] + [full(p) for p in params],
        out_specs=[row(B_WIDTH), row(B_WIDTH), row(B_WIDTH), drow, drow, drow, row(B_WIDTH), row(B_WIDTH)],
        out_shape=[one, one, one, two, two, two, one, one],
        compiler_params=_params(("parallel", "parallel")),
        name="rwkv_prep",
    )(pb, pb, pb, *params)


def _rwkv_scan_kernel(r_ref, v_ref, kk_ref, ld_ref, kd_ref, beta_ref, y_ref, h_ref):
    d = pl.program_id(0)
    c = pl.program_id(2)
    rev = d == 1

    @pl.when(c == 0)
    def _():
        h_ref[...] = jnp.zeros_like(h_ref)

    ti = lax.broadcasted_iota(jnp.int32, (CHUNK, CHUNK), 0)
    si = lax.broadcasted_iota(jnp.int32, (CHUNK, CHUNK), 1)
    diff = jnp.where(rev, si - ti, ti - si)
    strict = diff > 0
    incl = diff >= 0
    tri = incl.astype(BF16)
    ld_all = ld_ref[...]
    cl_all = _dot_exact_lhs(tri, ld_all)
    ones_cols = jnp.ones((CHUNK, LANES), BF16)

    ys = []
    for h in range(B_HEADS):
        sl = slice(h * B_HEAD, (h + 1) * B_HEAD)
        r, v, kk = r_ref[:, sl], v_ref[:, sl], kk_ref[:, sl]
        kd, beta = kd_ref[:, sl], beta_ref[:, sl]
        ld, cl = ld_all[:, sl], cl_all[:, sl]
        total = jnp.sum(ld, axis=0, keepdims=True)
        gam = jnp.exp(cl)
        inv_gam = jnp.exp(-cl)
        to_end = jnp.exp(total - cl)
        a_bar = -kk * jnp.exp(cl - ld)
        r_bar = r * gam
        b_til = beta * inv_gam
        k_til = kd * inv_gam
        b_hat = beta * to_end
        k_hat = kd * to_end
        x_mat = jnp.concatenate([a_bar, r_bar], axis=0)
        xb = _dot1(x_mat, b_til, NT)
        xk = _dot1(x_mat, k_til, NT)
        n_ab = jnp.where(strict, xb[:CHUNK], 0.0)
        l_rb = jnp.where(incl, xb[CHUNK:], 0.0)
        n_ak = jnp.where(strict, xk[:CHUNK], 0.0)
        l_rk = jnp.where(incl, xk[CHUNK:], 0.0)
        nv = _dot1(jnp.concatenate([n_ak, l_rk], axis=0), v)
        x = jnp.concatenate([a_bar, nv[:CHUNK]], axis=1)
        npow = n_ab
        steps = int(math.log2(CHUNK))
        for kstep in range(steps):
            x = x + _dot1(npow, x)
            if kstep + 1 < steps:
                npow = _dot1(npow, npow)
        lx = _dot1(l_rb, x)
        p_mat = r_bar + lx[:, :B_HEAD]
        y0 = nv[CHUNK:] + lx[:, B_HEAD:]
        bx = _dot1(b_hat, x, TN)
        m_mat = bx[:, :B_HEAD]
        g_mat = bx[:, B_HEAD:] + _dot1(k_hat, v, TN)
        gcol = jnp.exp(_dot_exact_rhs(ld, ones_cols, TN))[:, :B_HEAD]
        h0 = h_ref[h]
        ys.append(_dot3(p_mat, h0) + y0)
        h_ref[h] = gcol * h0 + (_dot3(m_mat, h0) + g_mat)
    y_ref[...] = jnp.concatenate(ys, axis=1)


def _rwkv_scan(r, v, kk, ld, kd, beta, ctx_len):
    bn, s, _ = r.shape
    n_chunks = s // CHUNK
    ctx_chunks = ctx_len // CHUNK

    def cidx(d, c):
        back = jnp.where(c < ctx_chunks, ctx_chunks - 1 - c, n_chunks - 1 + ctx_chunks - c)
        return jnp.where(d == 0, c, back)

    shared = pl.BlockSpec((None, CHUNK, B_WIDTH), lambda d, b, c: (b, cidx(d, c), 0))
    dirn = pl.BlockSpec((None, None, CHUNK, B_WIDTH), lambda d, b, c: (d, b, cidx(d, c), 0))
    return pl.pallas_call(
        _rwkv_scan_kernel,
        grid=(2, bn, n_chunks),
        in_specs=[shared, shared, shared, dirn, dirn, dirn],
        out_specs=dirn,
        out_shape=jax.ShapeDtypeStruct((2, bn, s, B_WIDTH), F32),
        scratch_shapes=[pltpu.VMEM((B_HEADS, B_HEAD, B_HEAD), F32)],
        compiler_params=_params(("parallel", "parallel", "arbitrary")),
        name="rwkv_scan",
    )(r, v, kk, ld, kd, beta)


def _outproj_kernel(x_ref, mod_ref, a_ref, yf_ref, yb_ref, bonus_ref, g_ref, c_ref,
                    wa_ref, wb_ref, wc_ref, lnxg_ref, lnxb_ref, ln1g_ref, ln1b_ref, o_ref, *, alpha):
    mod = mod_ref[...]
    y = yf_ref[...] + yb_ref[...] + bonus_ref[...]
    ones = _group_ones(B_WIDTH, B_HEAD)
    inv = 1.0 / B_HEAD
    mu = _dot_exact_rhs(y, ones) * inv
    yc = y - mu
    var = _dot_exact_rhs(yc * yc, ones) * inv
    yn = yc * lax.rsqrt(var + GN_EPS) * lnxg_ref[...] + lnxb_ref[...]
    bmix = (yn * g_ref[...]).astype(BF16)
    o = _dot(a_ref[...], wa_ref[...]) + _dot(bmix, wb_ref[...]) + _dot(c_ref[...], wc_ref[...])
    o_ref[...] = _ln(alpha * x_ref[...] + mod[2:3] * o, LN_EPS) * ln1g_ref[...] + ln1b_ref[...]


def _outproj(xs, mods, a_out, y, bonus, g, c_out, wa, wb, wc, lnxg, lnxb, ln1g, ln1b, ctx_len, alpha):
    bn, s, d = xs.shape
    row = lambda width: pl.BlockSpec((None, TM, width), lambda b, i: (b, i, 0))
    full = lambda a: pl.BlockSpec(a.shape, lambda b, i: (0,) * a.ndim)
    ydir = lambda dd: pl.BlockSpec((None, None, TM, B_WIDTH), lambda b, i: (dd, b, i, 0))
    consts = (wa, wb, wc, lnxg, lnxb, ln1g, ln1b)
    return pl.pallas_call(
        functools.partial(_outproj_kernel, alpha=alpha),
        grid=(bn, s // TM),
        in_specs=[row(d), _mod_spec(ctx_len, d),
                  row(A_WIDTH), ydir(0), ydir(1), row(B_WIDTH), row(B_WIDTH), row(C_WIDTH)]
                 + [full(p) for p in consts],
        out_specs=row(d),
        out_shape=jax.ShapeDtypeStruct((bn, s, d), F32),
        compiler_params=_params(("parallel", "parallel")),
        name="outproj",
    )(xs, mods, a_out, y, y, bonus, g, c_out, *consts)


def _swiglu_rows(h, w1_ref, w3_ref, w2_ref):
    acc = jnp.zeros((h.shape[0], w2_ref.shape[-1]), F32)
    for j in range(w1_ref.shape[-1] // FF_CHUNK):
        sl = slice(j * FF_CHUNK, (j + 1) * FF_CHUNK)
        u = _dot(h, w1_ref[:, sl])
        t = _dot(h, w3_ref[:, sl])
        acc = acc + _dot((_silu(u) * t).astype(BF16), w2_ref[sl, :])
    return acc


def _ffn_kernel(x_ref, mod_ref, w1_ref, w3_ref, w2_ref, g_ref, b_ref, o_ref, *, alpha):
    mod = mod_ref[...]
    x = x_ref[...]
    h = (_ln(x, LN_EPS) * (1.0 + mod[4:5]) + mod[3:4]).astype(BF16)
    f = _swiglu_rows(h, w1_ref, w3_ref, w2_ref)
    o_ref[...] = _ln(alpha * x + mod[5:6] * f, LN_EPS) * g_ref[...] + b_ref[...]


def _ffn(x1, mods, w1, w3, w2, g, b, ctx_len, alpha):
    bn, s, d = x1.shape
    row = pl.BlockSpec((None, TM, d), lambda bb, i: (bb, i, 0))
    resident = lambda a: pl.BlockSpec(a.shape, lambda bb, i: (0,) * a.ndim, pipeline_mode=pl.Buffered(1))
    full = lambda a: pl.BlockSpec(a.shape, lambda bb, i: (0,) * a.ndim)
    return pl.pallas_call(
        functools.partial(_ffn_kernel, alpha=alpha),
        grid=(bn, s // TM),
        in_specs=[row, _mod_spec(ctx_len, d),
                  resident(w1), resident(w3), resident(w2), full(g), full(b)],
        out_specs=row,
        out_shape=jax.ShapeDtypeStruct((bn, s, d), F32),
        compiler_params=_params(("parallel", "parallel")),
        name="ffn",
    )(x1, mods, w1, w3, w2, g, b)


def _moe_pre_kernel(x_ref, mod_ref, router_ref, h_ref, logit_ref):
    mod = mod_ref[...]
    h = _ln(x_ref[...], LN_EPS) * (1.0 + mod[4:5]) + mod[3:4]
    h_ref[...] = h
    logit_ref[...] = _dot6(h, router_ref[...])


def _moe_pre(x1, mods, router_p, ctx_len):
    bn, s, d = x1.shape
    ct = ctx_len // TM
    lt = (s - ctx_len) // TM
    return pl.pallas_call(
        _moe_pre_kernel,
        grid=(bn, lt),
        in_specs=[pl.BlockSpec((None, TM, d), lambda b, i: (b, i + ct, 0)),
                  pl.BlockSpec((None, None, 6, d), lambda b, i: (b, 1, 0, 0)),
                  pl.BlockSpec(router_p.shape, lambda b, i: (0, 0))],
        out_specs=[pl.BlockSpec((TM, d), lambda b, i: (b * lt + i, 0)),
                   pl.BlockSpec((TM, LANES), lambda b, i: (b * lt + i, 0))],
        out_shape=[jax.ShapeDtypeStruct((bn * lt * TM, d), F32),
                   jax.ShapeDtypeStruct((bn * lt * TM, LANES), F32)],
        compiler_params=_params(("parallel", "parallel")),
        name="moe_pre",
    )(x1, mods, router_p)


def _gather_kernel(idx_ref, src_ref, dst_ref, sem):
    base = pl.program_id(0) * GATHER_ROWS

    def row_copy(r):
        return pltpu.make_async_copy(src_ref.at[pl.ds(idx_ref[0, 0, r], 1)],
                                     dst_ref.at[pl.ds(base + r, 1)], sem)

    def issue(r, carry):
        row_copy(r).start()
        return carry

    def drain(r, carry):
        row_copy(r).wait()
        return carry

    lax.fori_loop(0, GATHER_ROWS, issue, 0)
    lax.fori_loop(0, GATHER_ROWS, drain, 0)


def _gather_rows(src, idx):
    n = idx.shape[0]
    steps = n // GATHER_ROWS
    return pl.pallas_call(
        _gather_kernel,
        grid=(steps,),
        in_specs=[pl.BlockSpec((1, 1, GATHER_ROWS), lambda i: (i, 0, 0), memory_space=pltpu.SMEM),
                  pl.BlockSpec(memory_space=pl.ANY)],
        out_specs=pl.BlockSpec(memory_space=pl.ANY),
        out_shape=jax.ShapeDtypeStruct((n, src.shape[1]), src.dtype),
        scratch_shapes=[pltpu.SemaphoreType.DMA(())],
        compiler_params=_params(("arbitrary",)),
        name="gather_rows",
    )(idx.reshape(steps, 1, GATHER_ROWS), src)


def _expert_kernel(be_ref, nb_ref, x_ref, w1_ref, w3_ref, w2_ref, o_ref):
    i = pl.program_id(0)

    @pl.when(i < nb_ref[0])
    def _():
        o_ref[...] = _swiglu_rows(x_ref[...].astype(BF16), w1_ref, w3_ref, w2_ref)

    @pl.when(i >= nb_ref[0])
    def _():
        o_ref[...] = jnp.zeros_like(o_ref)


def _experts(buf, block_e, n_used, w1, w3, w2):
    n, d = buf.shape
    n_blocks = n // MOE_ROWS
    ff = w1.shape[-1]
    wspec = lambda shape: pl.BlockSpec((None,) + shape, lambda i, be, nb: (be[i], 0, 0),
                                       pipeline_mode=pl.Buffered(1))
    grid_spec = pltpu.PrefetchScalarGridSpec(
        num_scalar_prefetch=2,
        grid=(n_blocks,),
        in_specs=[pl.BlockSpec((MOE_ROWS, d), lambda i, be, nb: (jnp.minimum(i, nb[0] - 1), 0)),
                  wspec((d, ff)), wspec((d, ff)), wspec((ff, d))],
        out_specs=pl.BlockSpec((MOE_ROWS, d), lambda i, be, nb: (i, 0)),
    )
    return pl.pallas_call(
        _expert_kernel,
        grid_spec=grid_spec,
        out_shape=jax.ShapeDtypeStruct((n, d), F32),
        compiler_params=_params(("arbitrary",)),
        name="experts",
    )(block_e, n_used, buf, w1, w3, w2)


def _combine_kernel(x_ref, mod_ref, y0_ref, y1_ref, gate_ref, g_ref, b_ref, o_ref, *, alpha):
    mod = mod_ref[...]
    gates = gate_ref[...]
    f = gates[:, 0:1] * y0_ref[...] + gates[:, 1:2] * y1_ref[...]
    o_ref[...] = _ln(alpha * x_ref[...] + mod[5:6] * f, LN_EPS) * g_ref[...] + b_ref[...]


def _combine(x1, mods, y_tok, gates, g, b, ctx_len, alpha):
    bn, s, d = x1.shape
    ct = ctx_len // TM
    lt = (s - ctx_len) // TM
    t = bn * lt * TM
    full = lambda a: pl.BlockSpec(a.shape, lambda bb, i: (0,) * a.ndim)
    return pl.pallas_call(
        functools.partial(_combine_kernel, alpha=alpha),
        grid=(bn, lt),
        in_specs=[pl.BlockSpec((None, TM, d), lambda bb, i: (bb, i + ct, 0)),
                  pl.BlockSpec((None, None, 6, d), lambda bb, i: (bb, 1, 0, 0)),
                  pl.BlockSpec((TM, d), lambda bb, i: (bb * lt + i, 0)),
                  pl.BlockSpec((TM, d), lambda bb, i: (t // TM + bb * lt + i, 0)),
                  pl.BlockSpec((TM, TOP_K), lambda bb, i: (bb * lt + i, 0)),
                  full(g), full(b)],
        out_specs=pl.BlockSpec((None, TM, d), lambda bb, i: (bb, i, 0)),
        out_shape=jax.ShapeDtypeStruct((bn, lt * TM, d), F32),
        compiler_params=_params(("parallel", "parallel")),
        name="moe_combine",
    )(x1, mods, y_tok, y_tok, gates, g, b)


def _moe_layer(x1, mods, router, w1, w3, w2, g, b, ctx_len, alpha):
    d = x1.shape[-1]
    router_p = jnp.pad(router, ((0, 0), (0, LANES - N_EXPERTS)))
    h, logits = _moe_pre(x1, mods, router_p, ctx_len)
    t = h.shape[0]
    top_v, top_i = lax.top_k(logits[:, :N_EXPERTS], TOP_K)
    gates = jax.nn.softmax(top_v, axis=-1)
    e_flat = top_i.reshape(-1)
    onehot = (e_flat[:, None] == jnp.arange(N_EXPERTS)[None, :]).astype(jnp.int32)
    ranks = jnp.cumsum(onehot, axis=0) - onehot
    rank = jnp.sum(ranks * onehot, axis=1)
    counts = jnp.sum(onehot, axis=0)
    padded = (counts + MOE_ROWS - 1) // MOE_ROWS * MOE_ROWS
    pend = jnp.cumsum(padded)
    pstart = pend - padded
    slot = pstart[e_flat] + rank
    n_blocks = t * TOP_K // MOE_ROWS + N_EXPERTS
    n_slots = n_blocks * MOE_ROWS
    src_idx = jnp.zeros((n_slots,), jnp.int32).at[slot].set(jnp.arange(t * TOP_K, dtype=jnp.int32) // TOP_K)
    block_e = jnp.minimum(jnp.searchsorted(pend, jnp.arange(n_blocks) * MOE_ROWS, side="right"),
                          N_EXPERTS - 1).astype(jnp.int32)
    n_used = (pend[-1:] // MOE_ROWS).astype(jnp.int32)
    buf = _gather_rows(h, src_idx)
    y_buf = _experts(buf, block_e, n_used, w1, w3, w2)
    back = slot.reshape(t, TOP_K).T.reshape(-1).astype(jnp.int32)
    y_tok = _gather_rows(y_buf, back)
    return _combine(x1, mods, y_tok, gates, g, b, ctx_len, alpha)


def _rope_tables(n_rows, ctx_len, dim, lane_lo):
    quarter = dim // 4
    inv = ROPE_THETA ** (-jnp.arange(quarter, dtype=F32) / quarter)
    rows = jnp.repeat(jnp.arange(n_rows, dtype=F32), GRID_W)
    cols = jnp.tile(jnp.arange(GRID_W, dtype=F32), n_rows)
    ang = jnp.concatenate([rows[:, None] * inv, rows[:, None] * inv,
                           cols[:, None] * inv, cols[:, None] * inv], axis=-1)
    sign = jnp.tile(jnp.concatenate([-jnp.ones(quarter, F32), jnp.ones(quarter, F32)]), 2)
    cos, sin = jnp.cos(ang), jnp.sin(ang) * sign
    length = cos.shape[0]
    if lane_lo == 0:
        reps = LANES // dim
        cos, sin = jnp.tile(cos, (1, reps)), jnp.tile(sin, (1, reps))
    else:
        pad = ((0, 0), (lane_lo, LANES - lane_lo - dim))
        cos = jnp.pad(cos, pad, constant_values=1.0)
        sin = jnp.pad(sin, pad)
    cos = jnp.concatenate([jnp.ones((ctx_len, LANES), F32), cos], axis=0)
    sin = jnp.concatenate([jnp.zeros((ctx_len, LANES), F32), sin], axis=0)
    return cos, sin


def _block_diag2(w):
    z = jnp.zeros_like(w[0])
    return jnp.concatenate([jnp.concatenate([w[0], z], axis=1), jnp.concatenate([z, w[1]], axis=1)], axis=0)


def _lambda_init(layer):
    return 0.8 - 0.6 * math.exp(-0.3 * layer)


def kernel(x, c, ctx, c_ctx, ada_w, ada_b, w_in, w_out, ln1_g, ln1_b, ln2_g, ln2_b,
           lam_q1, lam_k1, lam_q2, lam_k2, diff_norm_g, shift_mu, w0, w2, a0, a2, g2,
           k_k, k_a, r_k, lnx_g, lnx_b, q_norm_g, w_uq, kv_norm_g, w_ukv,
           ff_w1, ff_w3, ff_w2, router, moe_w1, moe_w3, moe_w2):
    bn, seq, d = x.shape
    ctx_len = ctx.shape[1]
    depth = ada_w.shape[0]
    assert d == D_MODEL and seq % TM == 0 and ctx_len % TM == 0 and seq % GRID_W == 0
    alpha = (2.0 * depth) ** 0.25
    n_grid_rows = seq // GRID_W
    cos_a, sin_a = _rope_tables(n_grid_rows, ctx_len, A_QK_DIM, 0)
    cos_c, sin_c = _rope_tables(n_grid_rows, ctx_len, C_ROPE, C_NOPE)

    cond_rows = 8 * ((bn + 1 + 7) // 8)
    cond = jnp.zeros((cond_rows, d), F32).at[:bn].set(c).at[bn].set(c_ctx)
    xs = jnp.concatenate([ctx, x], axis=1)

    for i in range(depth):
        with_ctx = i < depth - 1
        m = _ada(cond, ada_w[i], ada_b[i]).reshape(cond_rows, 6, d)
        mods = jnp.stack([jnp.broadcast_to(m[bn], (bn, 6, d)), m[:bn]], axis=1)

        wc = w_in[i][:, N_A + N_B:]
        kpe_w = jnp.pad(wc[:, C_Q_RANK + C_KV_RANK:], ((0, 0), (C_NOPE, LANES - C_NOPE - C_ROPE)))
        w_p = jnp.concatenate([w_in[i][:, :N_A + N_B], wc[:, :C_Q_RANK + C_KV_RANK], kpe_w], axis=1).astype(BF16)
        qa, ka, va, pb, pc = _inproj(xs, mods, w_p, cos_a, sin_a, ctx_len)

        lam_p = jnp.stack([lam_q1[i], lam_k1[i], lam_q2[i], lam_k2[i]])
        a_out = _attention(qa, ka, va, lam_p, diff_norm_g[i].reshape(1, A_V_DIM),
                           mode="diff", ctx_len=ctx_len, lam_init=_lambda_init(i))

        wq = w_uq[i].reshape(C_Q_RANK, C_HEADS, C_NOPE + C_ROPE)
        wq_p = jnp.pad(wq, ((0, 0), (0, 0), (0, LANES - C_NOPE - C_ROPE))).reshape(C_Q_RANK, -1).astype(BF16)
        wkv = w_ukv[i].reshape(C_KV_RANK, C_HEADS, C_NOPE + C_V)
        wk_p = jnp.pad(wkv[:, :, :C_NOPE], ((0, 0), (0, 0), (0, LANES - C_NOPE))).reshape(C_KV_RANK, -1).astype(BF16)
        wv_p = wkv[:, :, C_NOPE:].reshape(C_KV_RANK, -1).astype(BF16)
        qc, kc, vc = _mla_prep(pc, q_norm_g[i].reshape(1, -1), kv_norm_g[i].reshape(1, -1),
                               wq_p, wk_p, wv_p, cos_c, sin_c)
        c_out = _attention(qc, kc, vc, lam_p, diff_norm_g[i].reshape(1, A_V_DIM), mode="mla", ctx_len=ctx_len)

        r, v, kkn, ld, kd, beta, gate, bonus = _rwkv_prep(
            pb, shift_mu[i].reshape(1, -1), w0[i].reshape(1, -1), _block_diag2(w2[i]).astype(BF16),
            a0[i].reshape(1, -1), _block_diag2(a2[i]).astype(BF16), g2[i].astype(BF16),
            k_k[i].reshape(1, -1), k_a[i].reshape(1, -1), r_k[i].reshape(1, -1), ctx_len)
        y = _rwkv_scan(r, v, kkn, ld, kd, beta, ctx_len)

        wo = w_out[i].astype(BF16)
        x1 = _outproj(xs, mods, a_out, y, bonus, gate, c_out,
                      wo[:A_WIDTH], wo[A_WIDTH:A_WIDTH + B_WIDTH], wo[A_WIDTH + B_WIDTH:],
                      lnx_g[i].reshape(1, -1), lnx_b[i].reshape(1, -1),
                      ln1_g[i].reshape(1, -1), ln1_b[i].reshape(1, -1), ctx_len, alpha)

        j = i // 2
        g2n, b2n = ln2_g[i].reshape(1, -1), ln2_b[i].reshape(1, -1)
        if i % 2 == 0:
            xs = _ffn(x1, mods, ff_w1[j].astype(BF16), ff_w3[j].astype(BF16), ff_w2[j].astype(BF16),
                      g2n, b2n, ctx_len, alpha)
        else:
            if with_ctx:
                raise NotImplementedError("routed FFN on the context rows is not needed at this depth")
            return _moe_layer(x1, mods, router[j], moe_w1[j].astype(BF16), moe_w3[j].astype(BF16),
                              moe_w2[j].astype(BF16), g2n, b2n, ctx_len, alpha)
    return xs[:, ctx_len:]
```

```python
import functools
import math

import jax
import jax.numpy as jnp
from jax import lax
from jax.experimental import pallas as pl
from jax.experimental.pallas import tpu as pltpu

F32 = jnp.float32
BF16 = jnp.bfloat16

D_MODEL = 1024
GRID_W = 64
ROPE_THETA = 10000.0
A_HEADS, A_QK_DIM, A_V_DIM = 4, 64, 128
A_WIDTH = A_HEADS * A_V_DIM
A_QK_COLS = 2 * A_HEADS * A_QK_DIM
B_HEADS, B_HEAD = 4, 64
B_WIDTH = B_HEADS * B_HEAD
B_DECAY_RANK, B_A_RANK, B_GATE_RANK = 64, 64, 128
C_HEADS, C_NOPE, C_ROPE, C_V = 4, 64, 32, 64
C_WIDTH = C_HEADS * C_V
C_Q_RANK, C_KV_RANK = 256, 128
N_A = 2 * A_QK_COLS + A_WIDTH
N_B = 3 * B_WIDTH + 2 * B_DECAY_RANK + 2 * B_A_RANK + B_GATE_RANK
N_C = C_Q_RANK + C_KV_RANK + C_ROPE
N_C_PAD = C_Q_RANK + C_KV_RANK + 128
D_FF = 3584
N_EXPERTS = 8
TOP_K = 2
LN_EPS = 1e-6
RMS_EPS = 1e-6
GN_EPS = 64e-5

LANES = 128
TM = 256
CHUNK = 64
ATT_TK = 512
MOE_ROWS = 512
GATHER_ROWS = 512
FF_CHUNK = 512
VMEM_LIMIT = 56 * 1024 * 1024

NN = (((1,), (0,)), ((), ()))
NT = (((1,), (1,)), ((), ()))
TN = (((0,), (0,)), ((), ()))


def _params(sem, vmem=VMEM_LIMIT):
    return pltpu.CompilerParams(dimension_semantics=sem, vmem_limit_bytes=vmem)


def _split(x, n):
    parts, r = [], x
    for _ in range(n):
        p = r.astype(BF16)
        parts.append(p)
        r = r - p.astype(F32)
    return parts


def _dot(a, b, dn=NN):
    return lax.dot_general(a, b, dn, preferred_element_type=F32)


def _dot1(a, b, dn=NN):
    return _dot(a.astype(BF16), b.astype(BF16), dn)


def _dot3(a, b, dn=NN):
    a0, a1 = _split(a, 2)
    b0, b1 = _split(b, 2)
    return _dot(a0, b0, dn) + (_dot(a0, b1, dn) + _dot(a1, b0, dn))


def _dot6(a, b, dn=NN):
    a0, a1, a2 = _split(a, 3)
    b0, b1, b2 = _split(b, 3)
    lo = _dot(a1, b1, dn) + (_dot(a0, b2, dn) + _dot(a2, b0, dn))
    return _dot(a0, b0, dn) + ((_dot(a0, b1, dn) + _dot(a1, b0, dn)) + lo)


def _dot_exact_lhs(a_bf16, b, dn=NN, n=3):
    parts = _split(b, n)
    out = _dot(a_bf16, parts[-1], dn)
    for p in parts[-2::-1]:
        out = out + _dot(a_bf16, p, dn)
    return out


def _dot_exact_rhs(a, b_bf16, dn=NN, n=3):
    parts = _split(a, n)
    out = _dot(parts[-1], b_bf16, dn)
    for p in parts[-2::-1]:
        out = out + _dot(p, b_bf16, dn)
    return out


def _ln(x, eps):
    mu = jnp.mean(x, axis=-1, keepdims=True)
    xc = x - mu
    return xc * lax.rsqrt(jnp.mean(xc * xc, axis=-1, keepdims=True) + eps)


def _sigmoid(x):
    return 1.0 / (1.0 + jnp.exp(-x))


def _silu(x):
    return x * _sigmoid(x)


def _group_ones(width, group):
    r = lax.broadcasted_iota(jnp.int32, (width, width), 0) // group
    c = lax.broadcasted_iota(jnp.int32, (width, width), 1) // group
    return (r == c).astype(BF16)


def _partner(x, half):
    lane = lax.broadcasted_iota(jnp.int32, x.shape, 1)
    up = pltpu.roll(x, LANES - half, 1)
    dn = pltpu.roll(x, half, 1)
    return jnp.where((lane % (2 * half)) < half, up, dn)


def _rope(x, cos, sin, half):
    return x * cos + _partner(x, half) * sin


def _ada_kernel(c_ref, w_ref, b_ref, o_ref):
    o_ref[...] = _dot3(_silu(c_ref[...]), w_ref[...]) + b_ref[...]


def _ada(cond, w, b):
    rows, d = cond.shape
    n = w.shape[1]
    tn = 1536
    return pl.pallas_call(
        _ada_kernel,
        grid=(n // tn,),
        in_specs=[pl.BlockSpec((rows, d), lambda j: (0, 0)),
                  pl.BlockSpec((d, tn), lambda j: (0, j)),
                  pl.BlockSpec((1, tn), lambda j: (0, j))],
        out_specs=pl.BlockSpec((rows, tn), lambda j: (0, j)),
        out_shape=jax.ShapeDtypeStruct((rows, n), F32),
        compiler_params=_params(("parallel",)),
        name="ada",
    )(cond, w, b.reshape(1, n))


def _inproj_kernel(x_ref, mod_ref, w_ref, cos_ref, sin_ref,
                   q_ref, k_ref, v_ref, pb_ref, pc_ref):
    mod = mod_ref[...]
    h = (_ln(x_ref[...], LN_EPS) * (1.0 + mod[1:2]) + mod[0:1]).astype(BF16)
    cos, sin = cos_ref[...], sin_ref[...]
    scale = A_QK_DIM ** -0.5
    for j in range(A_QK_COLS // LANES):
        sl = slice(j * LANES, (j + 1) * LANES)
        qj = _dot(h, w_ref[:, sl])
        q_ref[sl, :] = (_rope(qj, cos, sin, A_QK_DIM // 4) * scale).T.astype(BF16)
        kj = _dot(h, w_ref[:, A_QK_COLS + j * LANES:A_QK_COLS + (j + 1) * LANES])
        k_ref[:, sl] = _rope(kj, cos, sin, A_QK_DIM // 4).astype(BF16)
        vj = _dot(h, w_ref[:, 2 * A_QK_COLS + j * LANES:2 * A_QK_COLS + (j + 1) * LANES])
        v_ref[sl, :] = vj.T.astype(BF16)
    pb_ref[...] = _dot(h, w_ref[:, N_A:N_A + N_B])
    pc_ref[...] = _dot(h, w_ref[:, N_A + N_B:])


def _mod_spec(ctx_len, d):
    ct = ctx_len // TM
    return pl.BlockSpec((None, None, 6, d), lambda b, i: (b, jnp.where(i >= ct, 1, 0), 0, 0))


def _inproj(xs, mods, w_p, cos_a, sin_a, ctx_len):
    bn, s, d = xs.shape
    n_tiles = s // TM
    n_w = w_p.shape[1]
    row = lambda width: pl.BlockSpec((None, TM, width), lambda b, i: (b, i, 0))
    col = lambda width: pl.BlockSpec((None, width, TM), lambda b, i: (b, 0, i))
    outs = [jax.ShapeDtypeStruct((bn, A_QK_COLS, s), BF16),
            jax.ShapeDtypeStruct((bn, s, A_QK_COLS), BF16),
            jax.ShapeDtypeStruct((bn, A_WIDTH, s), BF16),
            jax.ShapeDtypeStruct((bn, s, N_B), F32),
            jax.ShapeDtypeStruct((bn, s, N_C_PAD), F32)]
    return pl.pallas_call(
        _inproj_kernel,
        grid=(bn, n_tiles),
        in_specs=[row(d),
                  _mod_spec(ctx_len, d),
                  pl.BlockSpec((d, n_w), lambda b, i: (0, 0)),
                  pl.BlockSpec((TM, LANES), lambda b, i: (i, 0)),
                  pl.BlockSpec((TM, LANES), lambda b, i: (i, 0))],
        out_specs=[col(A_QK_COLS), row(A_QK_COLS), col(A_WIDTH), row(N_B), row(N_C_PAD)],
        out_shape=outs,
        compiler_params=_params(("parallel", "parallel")),
        name="inproj",
    )(xs, mods, w_p, cos_a, sin_a)


def _mla_prep_kernel(pc_ref, qg_ref, kvg_ref, wq_ref, wk_ref, wv_ref, cos_ref, sin_ref,
                     q_ref, k_ref, v_ref):
    pc = pc_ref[...]
    cq = pc[:, :C_Q_RANK]
    cq = cq * lax.rsqrt(jnp.mean(cq * cq, axis=-1, keepdims=True) + RMS_EPS) * qg_ref[...]
    ckv = pc[:, C_Q_RANK:C_Q_RANK + C_KV_RANK]
    ckv = ckv * lax.rsqrt(jnp.mean(ckv * ckv, axis=-1, keepdims=True) + RMS_EPS) * kvg_ref[...]
    cos, sin = cos_ref[...], sin_ref[...]
    kpe = _rope(pc[:, C_Q_RANK + C_KV_RANK:], cos, sin, C_ROPE // 4)
    cqb, ckvb = cq.astype(BF16), ckv.astype(BF16)
    scale = (C_NOPE + C_ROPE) ** -0.5
    for h in range(C_HEADS):
        sl = slice(h * LANES, (h + 1) * LANES)
        qh = _dot(cqb, wq_ref[:, sl])
        q_ref[sl, :] = (_rope(qh, cos, sin, C_ROPE // 4) * scale).T.astype(BF16)
        k_ref[:, sl] = (_dot(ckvb, wk_ref[:, sl]) + kpe).astype(BF16)
    for j in range(C_WIDTH // LANES):
        sl = slice(j * LANES, (j + 1) * LANES)
        v_ref[sl, :] = _dot(ckvb, wv_ref[:, sl]).T.astype(BF16)


def _mla_prep(pc, qg, kvg, wq_p, wk_p, wv_p, cos_c, sin_c):
    bn, s, _ = pc.shape
    row = lambda width: pl.BlockSpec((None, TM, width), lambda b, i: (b, i, 0))
    full = lambda a: pl.BlockSpec(a.shape, lambda b, i: (0,) * a.ndim)
    tab = pl.BlockSpec((TM, LANES), lambda b, i: (i, 0))
    col = lambda width: pl.BlockSpec((None, width, TM), lambda b, i: (b, 0, i))
    hw = C_HEADS * LANES
    return pl.pallas_call(
        _mla_prep_kernel,
        grid=(bn, s // TM),
        in_specs=[row(N_C_PAD), full(qg), full(kvg), full(wq_p), full(wk_p), full(wv_p), tab, tab],
        out_specs=[col(hw), row(hw), col(C_WIDTH)],
        out_shape=[jax.ShapeDtypeStruct((bn, hw, s), BF16),
                   jax.ShapeDtypeStruct((bn, s, hw), BF16),
                   jax.ShapeDtypeStruct((bn, C_WIDTH, s), BF16)],
        compiler_params=_params(("parallel", "parallel")),
        name="mla_prep",
    )(pc, qg, kvg, wq_p, wk_p, wv_p, cos_c, sin_c)


def _attn_kernel(lam_ref, g_ref, qt_ref, k_ref, vt_ref, o_ref, *, mode, ctx_tiles, ctx_len, lam_init):
    qi = pl.program_id(2)
    qt = qt_ref[...]
    tq = qt.shape[1]
    half = lax.broadcasted_iota(jnp.int32, (LANES, 1), 0) < (LANES // 2)
    if mode == "diff":
        qs = (jnp.where(half, qt, jnp.zeros_like(qt)), jnp.where(half, jnp.zeros_like(qt), qt))
    else:
        qs = (qt[:LANES], qt[LANES:])

    def step(kc, vc, carry):
        ks = (kc, kc) if mode == "diff" else (kc[:, :LANES], kc[:, LANES:])
        out = []
        for m in range(2):
            m_prev, l_prev, acc = carry[3 * m:3 * m + 3]
            s = _dot(ks[m], qs[m])
            m_new = jnp.maximum(m_prev, jnp.max(s, axis=0, keepdims=True))
            alpha = jnp.exp(m_prev - m_new)
            p = jnp.exp(s - m_new)
            l_new = alpha * l_prev + jnp.sum(p, axis=0, keepdims=True)
            acc = alpha * acc + _dot(vc, p.astype(BF16))
            out += [m_new, l_new, acc]
        return tuple(out)

    n_latent_chunks = (k_ref.shape[0] - ctx_len) // ATT_TK
    per_trip = 2 if n_latent_chunks % 2 == 0 else 1

    def body(j, carry):
        for u in range(per_trip):
            off = pl.multiple_of(ctx_len + (j * per_trip + u) * ATT_TK, LANES)
            carry = step(k_ref[pl.ds(off, ATT_TK), :], vt_ref[:, pl.ds(off, ATT_TK)], carry)
        return carry

    neg = jnp.full((1, tq), -1e30, F32)
    zero1 = jnp.zeros((1, tq), F32)
    zacc = jnp.zeros((LANES, tq), F32)
    carry = step(k_ref[0:ctx_len, :], vt_ref[:, 0:ctx_len], (neg, zero1, zacc, neg, zero1, zacc))
    n_trips = jnp.where(qi < ctx_tiles, 0, n_latent_chunks // per_trip)
    m0, l0, a0, m1, l1, a1 = lax.fori_loop(0, n_trips, body, carry)
    o0, o1 = a0 / l0, a1 / l1
    if mode == "diff":
        lp = lam_ref[...]
        lam = (jnp.exp(jnp.sum(lp[0:1] * lp[1:2], axis=-1, keepdims=True))
               - jnp.exp(jnp.sum(lp[2:3] * lp[3:4], axis=-1, keepdims=True)) + lam_init)
        o = o0 - lam * o1
        o = o * lax.rsqrt(jnp.mean(o * o, axis=0, keepdims=True) + RMS_EPS) * g_ref[...]
        o = o * (1.0 - lam_init)
    else:
        o = jnp.where(half, o0, o1)
    o_ref[...] = o.T.astype(o_ref.dtype)


def _attention(qt, k, vt, lam_p, g, *, mode, ctx_len, lam_init=0.0):
    bn, s, _ = k.shape
    qw = LANES if mode == "diff" else 2 * LANES
    groups = k.shape[2] // qw
    assert (s - ctx_len) % ATT_TK == 0 and ctx_len % LANES == 0
    kern = functools.partial(_attn_kernel, mode=mode, ctx_tiles=ctx_len // TM, ctx_len=ctx_len,
                             lam_init=lam_init)
    return pl.pallas_call(
        kern,
        grid=(bn, groups, s // TM),
        in_specs=[pl.BlockSpec(lam_p.shape, lambda b, h, i: (0, 0)),
                  pl.BlockSpec(g.shape, lambda b, h, i: (0, 0)),
                  pl.BlockSpec((None, qw, TM), lambda b, h, i: (b, h, i)),
                  pl.BlockSpec((None, s, qw), lambda b, h, i: (b, 0, h)),
                  pl.BlockSpec((None, LANES, s), lambda b, h, i: (b, h, 0))],
        out_specs=pl.BlockSpec((None, TM, LANES), lambda b, h, i: (b, i, h)),
        out_shape=jax.ShapeDtypeStruct((bn, s, groups * LANES), BF16),
        compiler_params=_params(("parallel", "parallel", "parallel")),
        name="attn_" + mode,
    )(lam_p, g, qt, k, vt)


def _rwkv_prep_kernel(pb_ref, prev_ref, next_ref, mu_ref, w0_ref, w2_ref, a0_ref, a2_ref, g2_ref,
                      kk_ref, ka_ref, rk_ref,
                      r_out, v_out, kkn_out, ld_out, kd_out, beta_out, g_out, bonus_out,
                      *, ctx_tiles, n_tiles):
    i = pl.program_id(1)
    x = pb_ref[...]
    row = lax.broadcasted_iota(jnp.int32, (TM, 1), 0)
    has_prev = jnp.logical_and(i != 0, i != ctx_tiles)
    has_next = jnp.logical_and(i != ctx_tiles - 1, i != n_tiles - 1)
    prev_edge = jnp.where(has_prev, prev_ref[7:8, :], 0.0)
    next_edge = jnp.where(has_next, next_ref[0:1, :], 0.0)
    xp = jnp.where(row == 0, prev_edge, pltpu.roll(x, 1, 0))
    xn = jnp.where(row == TM - 1, next_edge, pltpu.roll(x, TM - 1, 0))
    z = x + mu_ref[...] * (0.5 * (xp + xn) - x)

    r = z[:, :B_WIDTH]
    k = z[:, B_WIDTH:2 * B_WIDTH]
    v = z[:, 2 * B_WIDTH:3 * B_WIDTH]
    o = 3 * B_WIDTH
    wd = z[:, o:o + 2 * B_DECAY_RANK]
    o += 2 * B_DECAY_RANK
    ad = z[:, o:o + 2 * B_A_RANK]
    o += 2 * B_A_RANK
    gd = z[:, o:]

    u = w0_ref[...] + _dot1(jnp.tanh(wd), w2_ref[...])
    nu = -u
    w_raw = -(jnp.maximum(nu, 0.0) + jnp.log(1.0 + jnp.exp(-jnp.abs(nu)))) - 0.5
    ld = -jnp.exp(w_raw)
    lr = _sigmoid(a0_ref[...] + _dot1(ad, a2_ref[...]))
    g_out[...] = _dot1(_sigmoid(gd), g2_ref[...])

    ones = _group_ones(B_WIDTH, B_HEAD)
    kk = k * kk_ref[...]
    norm = jnp.sqrt(_dot_exact_rhs(kk * kk, ones))
    kkn = kk / jnp.maximum(norm, 1e-12)
    ka = ka_ref[...]
    kd_sum = jnp.zeros_like(k)
    for d in range(2):
        lr_d = lr[:, d * B_WIDTH:(d + 1) * B_WIDTH]
        kd = k * (1.0 + (lr_d - 1.0) * ka)
        kd_sum = kd_sum + kd
        ld_out[d] = ld[:, d * B_WIDTH:(d + 1) * B_WIDTH]
        kd_out[d] = kd
        beta_out[d] = kkn * lr_d
    bonus_out[...] = _dot_exact_rhs(r * kd_sum * rk_ref[...], ones) * v
    r_out[...] = r
    v_out[...] = v
    kkn_out[...] = kkn


def _rwkv_prep(pb, mu, w0, w2bd, a0, a2bd, g2, k_k, k_a, r_k, ctx_len):
    bn, s, _ = pb.shape
    n_tiles = s // TM
    eight = TM // 8
    row = lambda width: pl.BlockSpec((None, TM, width), lambda b, i: (b, i, 0))
    drow = pl.BlockSpec((2, None, TM, B_WIDTH), lambda b, i: (0, b, i, 0))
    full = lambda a: pl.BlockSpec(a.shape, lambda b, i: (0,) * a.ndim)
    one = jax.ShapeDtypeStruct((bn, s, B_WIDTH), F32)
    two = jax.ShapeDtypeStruct((2, bn, s, B_WIDTH), F32)
    kern = functools.partial(_rwkv_prep_kernel, ctx_tiles=ctx_len // TM, n_tiles=n_tiles)
    params = (mu, w0, w2bd, a0, a2bd, g2, k_k, k_a, r_k)
    return pl.pallas_call(
        kern,
        grid=(bn, n_tiles),
        in_specs=[row(N_B),
                  pl.BlockSpec((None, 8, N_B), lambda b, i: (b, jnp.maximum(i * eight - 1, 0), 0)),
                  pl.BlockSpec((None, 8, N_B), lambda b, i: (b, jnp.minimum((i + 1) * eight, s // 8 - 1), 0)),
                  ] + [full(p) for p in params],
        out_specs=[row(B_WIDTH), row(B_WIDTH), row(B_WIDTH), drow, drow, drow, row(B_WIDTH), row(B_WIDTH)],
        out_shape=[one, one, one, two, two, two, one, one],
        compiler_params=_params(("parallel", "parallel")),
        name="rwkv_prep",
    )(pb, pb, pb, *params)


def _pair_diag(x):
    lo = lax.broadcasted_iota(jnp.int32, (1, LANES), 1) < B_HEAD
    z = jnp.zeros_like(x)
    return jnp.concatenate([jnp.where(lo, x, z), jnp.where(lo, z, x)], axis=0)


def _pair_pick(x):
    lo = lax.broadcasted_iota(jnp.int32, (1, LANES), 1) < B_HEAD
    return jnp.where(lo, x[:B_HEAD], x[B_HEAD:])


def _rwkv_pair_kernel(rf_ref, vf_ref, kkf_ref, rb_ref, vb_ref, kkb_ref,
                      ldf_ref, kdf_ref, betaf_ref, ldb_ref, kdb_ref, betab_ref,
                      yf_ref, yb_ref, h_ref):
    c = pl.program_id(1)

    @pl.when(c == 0)
    def _():
        h_ref[...] = jnp.zeros_like(h_ref)

    n_pairs = B_WIDTH // LANES
    ti = lax.broadcasted_iota(jnp.int32, (CHUNK, LANES), 0)
    si = lax.broadcasted_iota(jnp.int32, (CHUNK, LANES), 1) % CHUNK
    t64 = lax.broadcasted_iota(jnp.int32, (CHUNK, CHUNK), 0)
    s64 = lax.broadcasted_iota(jnp.int32, (CHUNK, CHUNK), 1)
    eye = ti == si
    dirs = ((rf_ref, vf_ref, kkf_ref, ldf_ref, kdf_ref, betaf_ref, False),
            (rb_ref, vb_ref, kkb_ref, ldb_ref, kdb_ref, betab_ref, True))

    units = []
    for d, (r_ref, v_ref, kk_ref, ld_ref, kd_ref, beta_ref, rev) in enumerate(dirs):
        strict = (si > ti) if rev else (ti > si)
        incl = (si >= ti) if rev else (ti >= si)
        tri = ((s64 >= t64) if rev else (t64 >= s64)).astype(BF16)
        ld_all = ld_ref[...]
        cl_all = _dot_exact_lhs(tri, ld_all)
        for p in range(n_pairs):
            sl = slice(p * LANES, (p + 1) * LANES)
            ld, cl = ld_all[:, sl], cl_all[:, sl]
            total = jnp.sum(ld, axis=0, keepdims=True)
            inv_gam = jnp.exp(-cl)
            to_end = jnp.exp(total - cl)
            kk, kd, beta = kk_ref[:, sl], kd_ref[:, sl], beta_ref[:, sl]
            units.append(dict(
                d=d, p=p, sl=sl, strict=strict, incl=incl, v=v_ref[:, sl].astype(BF16),
                a_bar=-kk * jnp.exp(cl - ld), r_bar=r_ref[:, sl] * jnp.exp(cl),
                b_til=(beta * inv_gam).astype(BF16), k_til=(kd * inv_gam).astype(BF16),
                b_hat=(beta * to_end).astype(BF16), k_hat=(kd * to_end).astype(BF16),
                gam_c=jnp.exp(total)))

    for u in units:
        x_mat = jnp.concatenate([u["a_bar"], u["r_bar"]], axis=0).astype(BF16)
        rhs = jnp.concatenate([_pair_diag(u["b_til"]), _pair_diag(u["k_til"])], axis=0)
        xbk = _dot(x_mat, rhs, NT)
        u["n_ab"] = jnp.where(u["strict"], xbk[:CHUNK, :LANES], 0.0)
        u["l_rb"] = jnp.where(u["incl"], xbk[CHUNK:, :LANES], 0.0)
        n_ak = jnp.where(u["strict"], xbk[:CHUNK, LANES:], 0.0)
        l_rk = jnp.where(u["incl"], xbk[CHUNK:, LANES:], 0.0)
        u["nl"] = jnp.concatenate([n_ak, l_rk], axis=0).astype(BF16)
    for u in units:
        nv = _dot(u["nl"], _pair_diag(u["v"]))
        u["w"], u["u0"], u["lrkv"] = u["a_bar"], nv[:CHUNK], nv[CHUNK:]
        u["npow"] = u["n_ab"].astype(BF16)

    steps = int(math.log2(CHUNK))
    for kstep in range(steps):
        for u in units:
            rhs = jnp.concatenate([_pair_diag(u["w"].astype(BF16)), _pair_diag(u["u0"].astype(BF16))], axis=1)
            upd = _dot(u["npow"], rhs)
            u["w"] = u["w"] + upd[:, :LANES]
            u["u0"] = u["u0"] + upd[:, LANES:]
        if kstep + 1 < steps:
            for u in units:
                u["npow"] = _dot(u["npow"], _pair_diag(u["npow"])).astype(BF16)

    for u in units:
        wb, ub = u["w"].astype(BF16), u["u0"].astype(BF16)
        lx = _dot(u["l_rb"].astype(BF16), jnp.concatenate([_pair_diag(wb), _pair_diag(ub)], axis=1))
        u["p_mat"] = u["r_bar"] + lx[:, :LANES]
        u["y0"] = u["lrkv"] + lx[:, LANES:]
        lhs = jnp.concatenate([u["b_hat"], u["k_hat"]], axis=0)
        rhs = jnp.concatenate([jnp.concatenate([wb, ub], axis=1),
                               jnp.concatenate([jnp.zeros_like(wb), u["v"]], axis=1)], axis=0)
        mg = _dot(lhs, rhs, TN)
        u["m_full"] = _pair_pick(mg[:, :LANES]) + jnp.where(eye, u["gam_c"], 0.0)
        u["g_mat"] = _pair_pick(mg[:, LANES:])

    ys = [[None] * n_pairs for _ in range(2)]
    for u in units:
        h0 = h_ref[u["d"], u["p"]]
        a0, a1 = _split(jnp.concatenate([u["p_mat"], u["m_full"]], axis=0), 2)
        h_hi, h_lo = _split(h0, 2)
        bh, bl = _pair_diag(h_hi), _pair_diag(h_lo)
        out = _dot(a0, bh) + (_dot(a0, bl) + _dot(a1, bh))
        ys[u["d"]][u["p"]] = out[:CHUNK] + u["y0"]
        h_ref[u["d"], u["p"]] = out[CHUNK:] + u["g_mat"]
    yf_ref[...] = jnp.concatenate(ys[0], axis=1)
    yb_ref[...] = jnp.concatenate(ys[1], axis=1)


def _rwkv_pairs(r, v, kk, ld, kd, beta, ctx_len):
    bn, s, _ = r.shape
    n_chunks = s // CHUNK
    ctx_chunks = ctx_len // CHUNK

    def back(c):
        return jnp.where(c < ctx_chunks, ctx_chunks - 1 - c, n_chunks - 1 + ctx_chunks - c)

    fwd = pl.BlockSpec((None, CHUNK, B_WIDTH), lambda b, c: (b, c, 0))
    bwd = pl.BlockSpec((None, CHUNK, B_WIDTH), lambda b, c: (b, back(c), 0))
    fwd_d = pl.BlockSpec((None, None, CHUNK, B_WIDTH), lambda b, c: (0, b, c, 0))
    bwd_d = pl.BlockSpec((None, None, CHUNK, B_WIDTH), lambda b, c: (1, b, back(c), 0))
    y = jax.ShapeDtypeStruct((bn, s, B_WIDTH), F32)
    return pl.pallas_call(
        _rwkv_pair_kernel,
        grid=(bn, n_chunks),
        in_specs=[fwd, fwd, fwd, bwd, bwd, bwd, fwd_d, fwd_d, fwd_d, bwd_d, bwd_d, bwd_d],
        out_specs=[fwd, bwd],
        out_shape=[y, y],
        scratch_shapes=[pltpu.VMEM((2, B_WIDTH // LANES, B_HEAD, LANES), F32)],
        compiler_params=_params(("parallel", "arbitrary")),
        name="rwkv_scan",
    )(r, v, kk, r, v, kk, ld, kd, beta, ld, kd, beta)


def _outproj_kernel(x_ref, mod_ref, a_ref, yf_ref, yb_ref, bonus_ref, g_ref, c_ref,
                    wa_ref, wb_ref, wc_ref, lnxg_ref, lnxb_ref, ln1g_ref, ln1b_ref, o_ref, *, alpha):
    mod = mod_ref[...]
    y = yf_ref[...] + yb_ref[...] + bonus_ref[...]
    ones = _group_ones(B_WIDTH, B_HEAD)
    inv = 1.0 / B_HEAD
    mu = _dot_exact_rhs(y, ones) * inv
    yc = y - mu
    var = _dot_exact_rhs(yc * yc, ones) * inv
    yn = yc * lax.rsqrt(var + GN_EPS) * lnxg_ref[...] + lnxb_ref[...]
    bmix = (yn * g_ref[...]).astype(BF16)
    o = _dot(a_ref[...], wa_ref[...]) + _dot(bmix, wb_ref[...]) + _dot(c_ref[...], wc_ref[...])
    o_ref[...] = _ln(alpha * x_ref[...] + mod[2:3] * o, LN_EPS) * ln1g_ref[...] + ln1b_ref[...]


def _outproj(xs, mods, a_out, yf, yb, bonus, g, c_out, wa, wb, wc, lnxg, lnxb, ln1g, ln1b, ctx_len, alpha):
    bn, s, d = xs.shape
    row = lambda width: pl.BlockSpec((None, TM, width), lambda b, i: (b, i, 0))
    full = lambda a: pl.BlockSpec(a.shape, lambda b, i: (0,) * a.ndim)
    consts = (wa, wb, wc, lnxg, lnxb, ln1g, ln1b)
    return pl.pallas_call(
        functools.partial(_outproj_kernel, alpha=alpha),
        grid=(bn, s // TM),
        in_specs=[row(d), _mod_spec(ctx_len, d),
                  row(A_WIDTH), row(B_WIDTH), row(B_WIDTH), row(B_WIDTH), row(B_WIDTH), row(C_WIDTH)]
                 + [full(p) for p in consts],
        out_specs=row(d),
        out_shape=jax.ShapeDtypeStruct((bn, s, d), F32),
        compiler_params=_params(("parallel", "parallel")),
        name="outproj",
    )(xs, mods, a_out, yf, yb, bonus, g, c_out, *consts)


def _swiglu_rows(h, w1_ref, w3_ref, w2_ref):
    acc = jnp.zeros((h.shape[0], w2_ref.shape[-1]), F32)
    for j in range(w1_ref.shape[-1] // FF_CHUNK):
        sl = slice(j * FF_CHUNK, (j + 1) * FF_CHUNK)
        u = _dot(h, w1_ref[:, sl])
        t = _dot(h, w3_ref[:, sl])
        acc = acc + _dot((_silu(u) * t).astype(BF16), w2_ref[sl, :])
    return acc


def _ffn_kernel(x_ref, mod_ref, w1_ref, w3_ref, w2_ref, g_ref, b_ref, o_ref, *, alpha):
    mod = mod_ref[...]
    x = x_ref[...]
    h = (_ln(x, LN_EPS) * (1.0 + mod[4:5]) + mod[3:4]).astype(BF16)
    f = _swiglu_rows(h, w1_ref, w3_ref, w2_ref)
    o_ref[...] = _ln(alpha * x + mod[5:6] * f, LN_EPS) * g_ref[...] + b_ref[...]


def _ffn(x1, mods, w1, w3, w2, g, b, ctx_len, alpha):
    bn, s, d = x1.shape
    row = pl.BlockSpec((None, TM, d), lambda bb, i: (bb, i, 0))
    resident = lambda a: pl.BlockSpec(a.shape, lambda bb, i: (0,) * a.ndim, pipeline_mode=pl.Buffered(1))
    full = lambda a: pl.BlockSpec(a.shape, lambda bb, i: (0,) * a.ndim)
    return pl.pallas_call(
        functools.partial(_ffn_kernel, alpha=alpha),
        grid=(bn, s // TM),
        in_specs=[row, _mod_spec(ctx_len, d),
                  resident(w1), resident(w3), resident(w2), full(g), full(b)],
        out_specs=row,
        out_shape=jax.ShapeDtypeStruct((bn, s, d), F32),
        compiler_params=_params(("parallel", "parallel")),
        name="ffn",
    )(x1, mods, w1, w3, w2, g, b)


def _moe_pre_kernel(x_ref, mod_ref, router_ref, h_ref, logit_ref):
    mod = mod_ref[...]
    h = _ln(x_ref[...], LN_EPS) * (1.0 + mod[4:5]) + mod[3:4]
    h_ref[...] = h
    logit_ref[...] = _dot6(h, router_ref[...])


def _moe_pre(x1, mods, router_p, ctx_len):
    bn, s, d = x1.shape
    ct = ctx_len // TM
    lt = (s - ctx_len) // TM
    return pl.pallas_call(
        _moe_pre_kernel,
        grid=(bn, lt),
        in_specs=[pl.BlockSpec((None, TM, d), lambda b, i: (b, i + ct, 0)),
                  pl.BlockSpec((None, None, 6, d), lambda b, i: (b, 1, 0, 0)),
                  pl.BlockSpec(router_p.shape, lambda b, i: (0, 0))],
        out_specs=[pl.BlockSpec((TM, d), lambda b, i: (b * lt + i, 0)),
                   pl.BlockSpec((TM, LANES), lambda b, i: (b * lt + i, 0))],
        out_shape=[jax.ShapeDtypeStruct((bn * lt * TM, d), F32),
                   jax.ShapeDtypeStruct((bn * lt * TM, LANES), F32)],
        compiler_params=_params(("parallel", "parallel")),
        name="moe_pre",
    )(x1, mods, router_p)


def _gather_kernel(idx_ref, src_ref, dst_ref, sem):
    base = pl.program_id(0) * GATHER_ROWS

    def row_copy(r):
        return pltpu.make_async_copy(src_ref.at[pl.ds(idx_ref[0, 0, r], 1)],
                                     dst_ref.at[pl.ds(base + r, 1)], sem)

    def issue(r, carry):
        row_copy(r).start()
        return carry

    def drain(r, carry):
        row_copy(r).wait()
        return carry

    lax.fori_loop(0, GATHER_ROWS, issue, 0)
    lax.fori_loop(0, GATHER_ROWS, drain, 0)


def _gather_rows(src, idx):
    n = idx.shape[0]
    steps = n // GATHER_ROWS
    t, d = src.shape
    out = pl.pallas_call(
        _gather_kernel,
        grid=(steps,),
        in_specs=[pl.BlockSpec((1, 1, GATHER_ROWS), lambda i: (i, 0, 0), memory_space=pltpu.SMEM),
                  pl.BlockSpec(memory_space=pl.ANY)],
        out_specs=pl.BlockSpec(memory_space=pl.ANY),
        out_shape=jax.ShapeDtypeStruct((n, d // LANES, LANES), src.dtype),
        scratch_shapes=[pltpu.SemaphoreType.DMA(())],
        compiler_params=_params(("arbitrary",)),
        name="gather_rows",
    )(idx.reshape(steps, 1, GATHER_ROWS), src.reshape(t, d // LANES, LANES))
    return out.reshape(n, d)


def _expert_kernel(be_ref, nb_ref, x_ref, w1_ref, w3_ref, w2_ref, o_ref):
    i = pl.program_id(0)

    @pl.when(i < nb_ref[0])
    def _():
        o_ref[...] = _swiglu_rows(x_ref[...].astype(BF16), w1_ref, w3_ref, w2_ref)

    @pl.when(i >= nb_ref[0])
    def _():
        o_ref[...] = jnp.zeros_like(o_ref)


def _experts(buf, block_e, n_used, w1, w3, w2):
    n, d = buf.shape
    n_blocks = n // MOE_ROWS
    ff = w1.shape[-1]
    wspec = lambda shape: pl.BlockSpec((None,) + shape, lambda i, be, nb: (be[i], 0, 0),
                                       pipeline_mode=pl.Buffered(1))
    grid_spec = pltpu.PrefetchScalarGridSpec(
        num_scalar_prefetch=2,
        grid=(n_blocks,),
        in_specs=[pl.BlockSpec((MOE_ROWS, d), lambda i, be, nb: (jnp.minimum(i, nb[0] - 1), 0)),
                  wspec((d, ff)), wspec((d, ff)), wspec((ff, d))],
        out_specs=pl.BlockSpec((MOE_ROWS, d), lambda i, be, nb: (i, 0)),
    )
    return pl.pallas_call(
        _expert_kernel,
        grid_spec=grid_spec,
        out_shape=jax.ShapeDtypeStruct((n, d), F32),
        compiler_params=_params(("arbitrary",)),
        name="experts",
    )(block_e, n_used, buf, w1, w3, w2)


def _combine_kernel(x_ref, mod_ref, y0_ref, y1_ref, gate_ref, g_ref, b_ref, o_ref, *, alpha):
    mod = mod_ref[...]
    gates = gate_ref[...]
    f = gates[:, 0:1] * y0_ref[...] + gates[:, 1:2] * y1_ref[...]
    o_ref[...] = _ln(alpha * x_ref[...] + mod[5:6] * f, LN_EPS) * g_ref[...] + b_ref[...]


def _combine(x1, mods, y_tok, gates, g, b, ctx_len, alpha):
    bn, s, d = x1.shape
    ct = ctx_len // TM
    lt = (s - ctx_len) // TM
    t = bn * lt * TM
    full = lambda a: pl.BlockSpec(a.shape, lambda bb, i: (0,) * a.ndim)
    return pl.pallas_call(
        functools.partial(_combine_kernel, alpha=alpha),
        grid=(bn, lt),
        in_specs=[pl.BlockSpec((None, TM, d), lambda bb, i: (bb, i + ct, 0)),
                  pl.BlockSpec((None, None, 6, d), lambda bb, i: (bb, 1, 0, 0)),
                  pl.BlockSpec((TM, d), lambda bb, i: (bb * lt + i, 0)),
                  pl.BlockSpec((TM, d), lambda bb, i: (t // TM + bb * lt + i, 0)),
                  pl.BlockSpec((TM, TOP_K), lambda bb, i: (bb * lt + i, 0)),
                  full(g), full(b)],
        out_specs=pl.BlockSpec((None, TM, d), lambda bb, i: (bb, i, 0)),
        out_shape=jax.ShapeDtypeStruct((bn, lt * TM, d), F32),
        compiler_params=_params(("parallel", "parallel")),
        name="moe_combine",
    )(x1, mods, y_tok, y_tok, gates, g, b)


def _moe_layer(x1, mods, router, w1, w3, w2, g, b, ctx_len, alpha):
    d = x1.shape[-1]
    router_p = jnp.pad(router, ((0, 0), (0, LANES - N_EXPERTS)))
    h, logits = _moe_pre(x1, mods, router_p, ctx_len)
    t = h.shape[0]
    top_v, top_i = lax.top_k(logits[:, :N_EXPERTS], TOP_K)
    gates = jax.nn.softmax(top_v, axis=-1)
    e_flat = top_i.reshape(-1)
    onehot = (e_flat[:, None] == jnp.arange(N_EXPERTS)[None, :]).astype(jnp.int32)
    ranks = jnp.cumsum(onehot, axis=0) - onehot
    rank = jnp.sum(ranks * onehot, axis=1)
    counts = jnp.sum(onehot, axis=0)
    padded = (counts + MOE_ROWS - 1) // MOE_ROWS * MOE_ROWS
    pend = jnp.cumsum(padded)
    pstart = pend - padded
    slot = pstart[e_flat] + rank
    n_blocks = t * TOP_K // MOE_ROWS + N_EXPERTS
    n_slots = n_blocks * MOE_ROWS
    src_idx = jnp.zeros((n_slots,), jnp.int32).at[slot].set(jnp.arange(t * TOP_K, dtype=jnp.int32) // TOP_K)
    block_e = jnp.minimum(jnp.searchsorted(pend, jnp.arange(n_blocks) * MOE_ROWS, side="right"),
                          N_EXPERTS - 1).astype(jnp.int32)
    n_used = (pend[-1:] // MOE_ROWS).astype(jnp.int32)
    buf = _gather_rows(h, src_idx)
    y_buf = _experts(buf, block_e, n_used, w1, w3, w2)
    back = slot.reshape(t, TOP_K).T.reshape(-1).astype(jnp.int32)
    y_tok = _gather_rows(y_buf, back)
    return _combine(x1, mods, y_tok, gates, g, b, ctx_len, alpha)


def _rope_tables(n_rows, ctx_len, dim, lane_lo):
    quarter = dim // 4
    inv = ROPE_THETA ** (-jnp.arange(quarter, dtype=F32) / quarter)
    rows = jnp.repeat(jnp.arange(n_rows, dtype=F32), GRID_W)
    cols = jnp.tile(jnp.arange(GRID_W, dtype=F32), n_rows)
    ang = jnp.concatenate([rows[:, None] * inv, rows[:, None] * inv,
                           cols[:, None] * inv, cols[:, None] * inv], axis=-1)
    sign = jnp.tile(jnp.concatenate([-jnp.ones(quarter, F32), jnp.ones(quarter, F32)]), 2)
    cos, sin = jnp.cos(ang), jnp.sin(ang) * sign
    length = cos.shape[0]
    if lane_lo == 0:
        reps = LANES // dim
        cos, sin = jnp.tile(cos, (1, reps)), jnp.tile(sin, (1, reps))
    else:
        pad = ((0, 0), (lane_lo, LANES - lane_lo - dim))
        cos = jnp.pad(cos, pad, constant_values=1.0)
        sin = jnp.pad(sin, pad)
    cos = jnp.concatenate([jnp.ones((ctx_len, LANES), F32), cos], axis=0)
    sin = jnp.concatenate([jnp.zeros((ctx_len, LANES), F32), sin], axis=0)
    return cos, sin


def _block_diag2(w):
    z = jnp.zeros_like(w[0])
    return jnp.concatenate([jnp.concatenate([w[0], z], axis=1), jnp.concatenate([z, w[1]], axis=1)], axis=0)


def _lambda_init(layer):
    return 0.8 - 0.6 * math.exp(-0.3 * layer)


def kernel(x, c, ctx, c_ctx, ada_w, ada_b, w_in, w_out, ln1_g, ln1_b, ln2_g, ln2_b,
           lam_q1, lam_k1, lam_q2, lam_k2, diff_norm_g, shift_mu, w0, w2, a0, a2, g2,
           k_k, k_a, r_k, lnx_g, lnx_b, q_norm_g, w_uq, kv_norm_g, w_ukv,
           ff_w1, ff_w3, ff_w2, router, moe_w1, moe_w3, moe_w2):
    bn, seq, d = x.shape
    ctx_len = ctx.shape[1]
    depth = ada_w.shape[0]
    assert d == D_MODEL and seq % TM == 0 and ctx_len % TM == 0 and seq % GRID_W == 0
    alpha = (2.0 * depth) ** 0.25
    n_grid_rows = seq // GRID_W
    cos_a, sin_a = _rope_tables(n_grid_rows, ctx_len, A_QK_DIM, 0)
    cos_c, sin_c = _rope_tables(n_grid_rows, ctx_len, C_ROPE, C_NOPE)

    cond_rows = 8 * ((bn + 1 + 7) // 8)
    cond = jnp.zeros((cond_rows, d), F32).at[:bn].set(c).at[bn].set(c_ctx)
    xs = jnp.concatenate([ctx, x], axis=1)

    for i in range(depth):
        with_ctx = i < depth - 1
        m = _ada(cond, ada_w[i], ada_b[i]).reshape(cond_rows, 6, d)
        mods = jnp.stack([jnp.broadcast_to(m[bn], (bn, 6, d)), m[:bn]], axis=1)

        wc = w_in[i][:, N_A + N_B:]
        kpe_w = jnp.pad(wc[:, C_Q_RANK + C_KV_RANK:], ((0, 0), (C_NOPE, LANES - C_NOPE - C_ROPE)))
        w_p = jnp.concatenate([w_in[i][:, :N_A + N_B], wc[:, :C_Q_RANK + C_KV_RANK], kpe_w], axis=1).astype(BF16)
        qa, ka, va, pb, pc = _inproj(xs, mods, w_p, cos_a, sin_a, ctx_len)

        lam_p = jnp.stack([lam_q1[i], lam_k1[i], lam_q2[i], lam_k2[i]])
        g_col = jnp.broadcast_to(diff_norm_g[i][:, None], (A_V_DIM, TM))
        a_out = _attention(qa, ka, va, lam_p, g_col, mode="diff", ctx_len=ctx_len, lam_init=_lambda_init(i))

        wq = w_uq[i].reshape(C_Q_RANK, C_HEADS, C_NOPE + C_ROPE)
        wq_p = jnp.pad(wq, ((0, 0), (0, 0), (0, LANES - C_NOPE - C_ROPE))).reshape(C_Q_RANK, -1).astype(BF16)
        wkv = w_ukv[i].reshape(C_KV_RANK, C_HEADS, C_NOPE + C_V)
        wk_p = jnp.pad(wkv[:, :, :C_NOPE], ((0, 0), (0, 0), (0, LANES - C_NOPE))).reshape(C_KV_RANK, -1).astype(BF16)
        wv_p = wkv[:, :, C_NOPE:].reshape(C_KV_RANK, -1).astype(BF16)
        qc, kc, vc = _mla_prep(pc, q_norm_g[i].reshape(1, -1), kv_norm_g[i].reshape(1, -1),
                               wq_p, wk_p, wv_p, cos_c, sin_c)
        c_out = _attention(qc, kc, vc, lam_p, g_col, mode="mla", ctx_len=ctx_len)

        r, v, kkn, ld, kd, beta, gate, bonus = _rwkv_prep(
            pb, shift_mu[i].reshape(1, -1), w0[i].reshape(1, -1), _block_diag2(w2[i]).astype(BF16),
            a0[i].reshape(1, -1), _block_diag2(a2[i]).astype(BF16), g2[i].astype(BF16),
            k_k[i].reshape(1, -1), k_a[i].reshape(1, -1), r_k[i].reshape(1, -1), ctx_len)
        yf, yb = _rwkv_pairs(r, v, kkn, ld, kd, beta, ctx_len)

        wo = w_out[i].astype(BF16)
        x1 = _outproj(xs, mods, a_out, yf, yb, bonus, gate, c_out,
                      wo[:A_WIDTH], wo[A_WIDTH:A_WIDTH + B_WIDTH], wo[A_WIDTH + B_WIDTH:],
                      lnx_g[i].reshape(1, -1), lnx_b[i].reshape(1, -1),
                      ln1_g[i].reshape(1, -1), ln1_b[i].reshape(1, -1), ctx_len, alpha)

        j = i // 2
        g2n, b2n = ln2_g[i].reshape(1, -1), ln2_b[i].reshape(1, -1)
        if i % 2 == 0:
            xs = _ffn(x1, mods, ff_w1[j].astype(BF16), ff_w3[j].astype(BF16), ff_w2[j].astype(BF16),
                      g2n, b2n, ctx_len, alpha)
        else:
            if with_ctx:
                raise NotImplementedError("routed FFN on the context rows is not needed at this depth")
            return _moe_layer(x1, mods, router[j], moe_w1[j].astype(BF16), moe_w3[j].astype(BF16),
                              moe_w2[j].astype(BF16), g2n, b2n, ctx_len, alpha)
    return xs[:, ctx_len:]
```

```python
import functools
import math

import jax
import jax.numpy as jnp
from jax import lax
from jax.experimental import pallas as pl
from jax.experimental.pallas import tpu as pltpu

F32 = jnp.float32
BF16 = jnp.bfloat16

D_MODEL = 1024
GRID_W = 64
ROPE_THETA = 10000.0
A_HEADS, A_QK_DIM, A_V_DIM = 4, 64, 128
A_WIDTH = A_HEADS * A_V_DIM
A_QK_COLS = 2 * A_HEADS * A_QK_DIM
B_HEADS, B_HEAD = 4, 64
B_WIDTH = B_HEADS * B_HEAD
B_DECAY_RANK, B_A_RANK, B_GATE_RANK = 64, 64, 128
C_HEADS, C_NOPE, C_ROPE, C_V = 4, 64, 32, 64
C_WIDTH = C_HEADS * C_V
C_Q_RANK, C_KV_RANK = 256, 128
N_A = 2 * A_QK_COLS + A_WIDTH
N_B = 3 * B_WIDTH + 2 * B_DECAY_RANK + 2 * B_A_RANK + B_GATE_RANK
N_C = C_Q_RANK + C_KV_RANK + C_ROPE
N_C_PAD = C_Q_RANK + C_KV_RANK + 128
D_FF = 3584
N_EXPERTS = 8
TOP_K = 2
LN_EPS = 1e-6
RMS_EPS = 1e-6
GN_EPS = 64e-5

LANES = 128
TM = 256
CHUNK = 64
ATT_TK = 512
MOE_ROWS = 512
FF_CHUNK = 512
VMEM_LIMIT = 56 * 1024 * 1024

NN = (((1,), (0,)), ((), ()))
NT = (((1,), (1,)), ((), ()))
TN = (((0,), (0,)), ((), ()))


def _params(sem, vmem=VMEM_LIMIT, flags=None):
    return pltpu.CompilerParams(dimension_semantics=sem, vmem_limit_bytes=vmem, flags=flags)


def _split(x, n):
    parts, r = [], x
    for _ in range(n):
        p = r.astype(BF16)
        parts.append(p)
        r = r - p.astype(F32)
    return parts


def _dot(a, b, dn=NN):
    return lax.dot_general(a, b, dn, preferred_element_type=F32)


def _dot1(a, b, dn=NN):
    return _dot(a.astype(BF16), b.astype(BF16), dn)


def _dot3(a, b, dn=NN):
    a0, a1 = _split(a, 2)
    b0, b1 = _split(b, 2)
    return _dot(a0, b0, dn) + (_dot(a0, b1, dn) + _dot(a1, b0, dn))


def _dot6(a, b, dn=NN):
    a0, a1, a2 = _split(a, 3)
    b0, b1, b2 = _split(b, 3)
    lo = _dot(a1, b1, dn) + (_dot(a0, b2, dn) + _dot(a2, b0, dn))
    return _dot(a0, b0, dn) + ((_dot(a0, b1, dn) + _dot(a1, b0, dn)) + lo)


def _dot_exact_lhs(a_bf16, b, dn=NN, n=3):
    parts = _split(b, n)
    out = _dot(a_bf16, parts[-1], dn)
    for p in parts[-2::-1]:
        out = out + _dot(a_bf16, p, dn)
    return out


def _dot_exact_rhs(a, b_bf16, dn=NN, n=3):
    parts = _split(a, n)
    out = _dot(parts[-1], b_bf16, dn)
    for p in parts[-2::-1]:
        out = out + _dot(p, b_bf16, dn)
    return out


def _ln(x, eps):
    mu = jnp.mean(x, axis=-1, keepdims=True)
    xc = x - mu
    return xc * lax.rsqrt(jnp.mean(xc * xc, axis=-1, keepdims=True) + eps)


def _sigmoid(x):
    return 1.0 / (1.0 + jnp.exp(-x))


def _silu(x):
    return x * _sigmoid(x)


def _group_ones(width, group):
    r = lax.broadcasted_iota(jnp.int32, (width, width), 0) // group
    c = lax.broadcasted_iota(jnp.int32, (width, width), 1) // group
    return (r == c).astype(BF16)


def _partner(x, half):
    lane = lax.broadcasted_iota(jnp.int32, x.shape, 1)
    up = pltpu.roll(x, LANES - half, 1)
    dn = pltpu.roll(x, half, 1)
    return jnp.where((lane % (2 * half)) < half, up, dn)


def _rope(x, cos, sin, half):
    return x * cos + _partner(x, half) * sin


def _ada_kernel(c_ref, w_ref, b_ref, o_ref):
    o_ref[...] = _dot3(_silu(c_ref[...]), w_ref[...]) + b_ref[...]


def _ada(cond, w, b):
    rows, d = cond.shape
    n = w.shape[1]
    tn = 1536
    return pl.pallas_call(
        _ada_kernel,
        grid=(n // tn,),
        in_specs=[pl.BlockSpec((rows, d), lambda j: (0, 0)),
                  pl.BlockSpec((d, tn), lambda j: (0, j)),
                  pl.BlockSpec((1, tn), lambda j: (0, j))],
        out_specs=pl.BlockSpec((rows, tn), lambda j: (0, j)),
        out_shape=jax.ShapeDtypeStruct((rows, n), F32),
        compiler_params=_params(("parallel",)),
        name="ada",
    )(cond, w, b.reshape(1, n))


def _inproj_kernel(x_ref, mod_ref, w_ref, cos_ref, sin_ref,
                   q_ref, k_ref, v_ref, pb_ref, pc_ref):
    mod = mod_ref[...]
    h = (_ln(x_ref[...], LN_EPS) * (1.0 + mod[1:2]) + mod[0:1]).astype(BF16)
    cos, sin = cos_ref[...], sin_ref[...]
    scale = A_QK_DIM ** -0.5
    for j in range(A_QK_COLS // LANES):
        sl = slice(j * LANES, (j + 1) * LANES)
        qj = _dot(h, w_ref[:, sl])
        q_ref[sl, :] = (_rope(qj, cos, sin, A_QK_DIM // 4) * scale).T.astype(BF16)
        kj = _dot(h, w_ref[:, A_QK_COLS + j * LANES:A_QK_COLS + (j + 1) * LANES])
        k_ref[:, sl] = _rope(kj, cos, sin, A_QK_DIM // 4).astype(BF16)
        vj = _dot(h, w_ref[:, 2 * A_QK_COLS + j * LANES:2 * A_QK_COLS + (j + 1) * LANES])
        v_ref[sl, :] = vj.T.astype(BF16)
    pb_ref[...] = _dot(h, w_ref[:, N_A:N_A + N_B])
    pc_ref[...] = _dot(h, w_ref[:, N_A + N_B:])


def _mod_spec(ctx_len, d):
    ct = ctx_len // TM
    return pl.BlockSpec((None, None, 6, d), lambda b, i: (b, jnp.where(i >= ct, 1, 0), 0, 0))


def _inproj(xs, mods, w_p, cos_a, sin_a, ctx_len):
    bn, s, d = xs.shape
    n_tiles = s // TM
    n_w = w_p.shape[1]
    row = lambda width: pl.BlockSpec((None, TM, width), lambda b, i: (b, i, 0))
    col = lambda width: pl.BlockSpec((None, width, TM), lambda b, i: (b, 0, i))
    outs = [jax.ShapeDtypeStruct((bn, A_QK_COLS, s), BF16),
            jax.ShapeDtypeStruct((bn, s, A_QK_COLS), BF16),
            jax.ShapeDtypeStruct((bn, A_WIDTH, s), BF16),
            jax.ShapeDtypeStruct((bn, s, N_B), F32),
            jax.ShapeDtypeStruct((bn, s, N_C_PAD), F32)]
    return pl.pallas_call(
        _inproj_kernel,
        grid=(bn, n_tiles),
        in_specs=[row(d),
                  _mod_spec(ctx_len, d),
                  pl.BlockSpec((d, n_w), lambda b, i: (0, 0)),
                  pl.BlockSpec((TM, LANES), lambda b, i: (i, 0)),
                  pl.BlockSpec((TM, LANES), lambda b, i: (i, 0))],
        out_specs=[col(A_QK_COLS), row(A_QK_COLS), col(A_WIDTH), row(N_B), row(N_C_PAD)],
        out_shape=outs,
        compiler_params=_params(("parallel", "parallel")),
        name="inproj",
    )(xs, mods, w_p, cos_a, sin_a)


def _mla_prep_kernel(pc_ref, qg_ref, kvg_ref, wq_ref, wk_ref, wv_ref, cos_ref, sin_ref,
                     q_ref, k_ref, v_ref):
    pc = pc_ref[...]
    cq = pc[:, :C_Q_RANK]
    cq = cq * lax.rsqrt(jnp.mean(cq * cq, axis=-1, keepdims=True) + RMS_EPS) * qg_ref[...]
    ckv = pc[:, C_Q_RANK:C_Q_RANK + C_KV_RANK]
    ckv = ckv * lax.rsqrt(jnp.mean(ckv * ckv, axis=-1, keepdims=True) + RMS_EPS) * kvg_ref[...]
    cos, sin = cos_ref[...], sin_ref[...]
    kpe = _rope(pc[:, C_Q_RANK + C_KV_RANK:], cos, sin, C_ROPE // 4)
    cqb, ckvb = cq.astype(BF16), ckv.astype(BF16)
    scale = (C_NOPE + C_ROPE) ** -0.5
    for h in range(C_HEADS):
        sl = slice(h * LANES, (h + 1) * LANES)
        qh = _dot(cqb, wq_ref[:, sl])
        q_ref[sl, :] = (_rope(qh, cos, sin, C_ROPE // 4) * scale).T.astype(BF16)
        k_ref[:, sl] = (_dot(ckvb, wk_ref[:, sl]) + kpe).astype(BF16)
    for j in range(C_WIDTH // LANES):
        sl = slice(j * LANES, (j + 1) * LANES)
        v_ref[sl, :] = _dot(ckvb, wv_ref[:, sl]).T.astype(BF16)


def _mla_prep(pc, qg, kvg, wq_p, wk_p, wv_p, cos_c, sin_c):
    bn, s, _ = pc.shape
    row = lambda width: pl.BlockSpec((None, TM, width), lambda b, i: (b, i, 0))
    full = lambda a: pl.BlockSpec(a.shape, lambda b, i: (0,) * a.ndim)
    tab = pl.BlockSpec((TM, LANES), lambda b, i: (i, 0))
    col = lambda width: pl.BlockSpec((None, width, TM), lambda b, i: (b, 0, i))
    hw = C_HEADS * LANES
    return pl.pallas_call(
        _mla_prep_kernel,
        grid=(bn, s // TM),
        in_specs=[row(N_C_PAD), full(qg), full(kvg), full(wq_p), full(wk_p), full(wv_p), tab, tab],
        out_specs=[col(hw), row(hw), col(C_WIDTH)],
        out_shape=[jax.ShapeDtypeStruct((bn, hw, s), BF16),
                   jax.ShapeDtypeStruct((bn, s, hw), BF16),
                   jax.ShapeDtypeStruct((bn, C_WIDTH, s), BF16)],
        compiler_params=_params(("parallel", "parallel")),
        name="mla_prep",
    )(pc, qg, kvg, wq_p, wk_p, wv_p, cos_c, sin_c)


def _attn_kernel(lam_ref, g_ref, qt_ref, k_ref, vt_ref, o_ref, *, mode, ctx_tiles, ctx_len, lam_init):
    qi = pl.program_id(2)
    qt = qt_ref[...]
    tq = qt.shape[1]
    half = lax.broadcasted_iota(jnp.int32, (LANES, 1), 0) < (LANES // 2)
    if mode == "diff":
        q_both = jnp.concatenate([jnp.where(half, qt, jnp.zeros_like(qt)),
                                  jnp.where(half, jnp.zeros_like(qt), qt)], axis=1)
    else:
        qs = (qt[:LANES], qt[LANES:])

    def fold(x, op):
        rows = x.shape[0]
        while rows > 8 and rows % 2 == 0:
            rows //= 2
            x = op(x[:rows], x[rows:])
        return x

    def step(kc, vc, carry):
        m_prev, l_prev, acc = carry
        if mode == "diff":
            s = _dot(kc, q_both)
        else:
            s = jnp.concatenate([_dot(kc[:, :LANES], qs[0]), _dot(kc[:, LANES:], qs[1])], axis=1)
        m_new = jnp.maximum(m_prev, jnp.max(fold(s, jnp.maximum), axis=0, keepdims=True))
        alpha = jnp.exp(m_prev - m_new)
        p = jnp.exp(s - m_new)
        l_new = alpha * l_prev + jnp.sum(fold(p, jnp.add), axis=0, keepdims=True)
        acc = alpha * acc + _dot(vc, p.astype(BF16))
        return m_new, l_new, acc

    n_latent_chunks = (k_ref.shape[0] - ctx_len) // ATT_TK
    per_trip = 2 if n_latent_chunks % 2 == 0 else 1

    def body(j, carry):
        for u in range(per_trip):
            off = pl.multiple_of(ctx_len + (j * per_trip + u) * ATT_TK, LANES)
            carry = step(k_ref[pl.ds(off, ATT_TK), :], vt_ref[:, pl.ds(off, ATT_TK)], carry)
        return carry

    neg = jnp.full((1, 2 * tq), -1e30, F32)
    zero1 = jnp.zeros((1, 2 * tq), F32)
    zacc = jnp.zeros((LANES, 2 * tq), F32)
    carry = step(k_ref[0:ctx_len, :], vt_ref[:, 0:ctx_len], (neg, zero1, zacc))
    n_trips = jnp.where(qi < ctx_tiles, 0, n_latent_chunks // per_trip)
    _, l_both, a_both = lax.fori_loop(0, n_trips, body, carry)
    o_both = a_both / l_both
    o0, o1 = o_both[:, :tq], o_both[:, tq:]
    if mode == "diff":
        lp = lam_ref[...]
        lam = (jnp.exp(jnp.sum(lp[0:1] * lp[1:2], axis=-1, keepdims=True))
               - jnp.exp(jnp.sum(lp[2:3] * lp[3:4], axis=-1, keepdims=True)) + lam_init)
        o = o0 - lam * o1
        o = o * lax.rsqrt(jnp.mean(o * o, axis=0, keepdims=True) + RMS_EPS) * g_ref[...]
        o = o * (1.0 - lam_init)
    else:
        o = jnp.where(half, o0, o1)
    o_ref[...] = o.T.astype(o_ref.dtype)


def _attention(qt, k, vt, lam_p, g, *, mode, ctx_len, lam_init=0.0):
    bn, s, _ = k.shape
    qw = LANES if mode == "diff" else 2 * LANES
    groups = k.shape[2] // qw
    assert (s - ctx_len) % ATT_TK == 0 and ctx_len % LANES == 0
    kern = functools.partial(_attn_kernel, mode=mode, ctx_tiles=ctx_len // TM, ctx_len=ctx_len,
                             lam_init=lam_init)
    return pl.pallas_call(
        kern,
        grid=(bn, groups, s // TM),
        in_specs=[pl.BlockSpec(lam_p.shape, lambda b, h, i: (0, 0)),
                  pl.BlockSpec(g.shape, lambda b, h, i: (0, 0)),
                  pl.BlockSpec((None, qw, TM), lambda b, h, i: (b, h, i)),
                  pl.BlockSpec((None, s, qw), lambda b, h, i: (b, 0, h)),
                  pl.BlockSpec((None, LANES, s), lambda b, h, i: (b, h, 0))],
        out_specs=pl.BlockSpec((None, TM, LANES), lambda b, h, i: (b, i, h)),
        out_shape=jax.ShapeDtypeStruct((bn, s, groups * LANES), BF16),
        compiler_params=_params(("parallel", "parallel", "parallel")),
        name="attn_" + mode,
    )(lam_p, g, qt, k, vt)


def _rwkv_prep_kernel(pb_ref, prev_ref, next_ref, mu_ref, w0_ref, w2_ref, a0_ref, a2_ref, g2_ref,
                      kk_ref, ka_ref, rk_ref,
                      r_out, v_out, kkn_out, ld_out, kd_out, beta_out, g_out, bonus_out,
                      *, ctx_tiles, n_tiles):
    i = pl.program_id(1)
    x = pb_ref[...]
    row = lax.broadcasted_iota(jnp.int32, (TM, 1), 0)
    has_prev = jnp.logical_and(i != 0, i != ctx_tiles)
    has_next = jnp.logical_and(i != ctx_tiles - 1, i != n_tiles - 1)
    prev_edge = jnp.where(has_prev, prev_ref[7:8, :], 0.0)
    next_edge = jnp.where(has_next, next_ref[0:1, :], 0.0)
    xp = jnp.where(row == 0, prev_edge, pltpu.roll(x, 1, 0))
    xn = jnp.where(row == TM - 1, next_edge, pltpu.roll(x, TM - 1, 0))
    z = x + mu_ref[...] * (0.5 * (xp + xn) - x)

    r = z[:, :B_WIDTH]
    k = z[:, B_WIDTH:2 * B_WIDTH]
    v = z[:, 2 * B_WIDTH:3 * B_WIDTH]
    o = 3 * B_WIDTH
    wd = z[:, o:o + 2 * B_DECAY_RANK]
    o += 2 * B_DECAY_RANK
    ad = z[:, o:o + 2 * B_A_RANK]
    o += 2 * B_A_RANK
    gd = z[:, o:]

    u = w0_ref[...] + _dot1(jnp.tanh(wd), w2_ref[...])
    nu = -u
    w_raw = -(jnp.maximum(nu, 0.0) + jnp.log(1.0 + jnp.exp(-jnp.abs(nu)))) - 0.5
    ld = -jnp.exp(w_raw)
    lr = _sigmoid(a0_ref[...] + _dot1(ad, a2_ref[...]))
    g_out[...] = _dot1(_sigmoid(gd), g2_ref[...])

    ones = _group_ones(B_WIDTH, B_HEAD)
    kk = k * kk_ref[...]
    norm = jnp.sqrt(_dot_exact_rhs(kk * kk, ones))
    kkn = kk / jnp.maximum(norm, 1e-12)
    ka = ka_ref[...]
    kd_sum = jnp.zeros_like(k)
    for d in range(2):
        lr_d = lr[:, d * B_WIDTH:(d + 1) * B_WIDTH]
        kd = k * (1.0 + (lr_d - 1.0) * ka)
        kd_sum = kd_sum + kd
        ld_out[d] = ld[:, d * B_WIDTH:(d + 1) * B_WIDTH]
        kd_out[d] = kd
        beta_out[d] = kkn * lr_d
    bonus_out[...] = _dot_exact_rhs(r * kd_sum * rk_ref[...], ones) * v
    r_out[...] = r
    v_out[...] = v
    kkn_out[...] = kkn


def _rwkv_prep(pb, mu, w0, w2bd, a0, a2bd, g2, k_k, k_a, r_k, ctx_len):
    bn, s, _ = pb.shape
    n_tiles = s // TM
    eight = TM // 8
    row = lambda width: pl.BlockSpec((None, TM, width), lambda b, i: (b, i, 0))
    drow = pl.BlockSpec((2, None, TM, B_WIDTH), lambda b, i: (0, b, i, 0))
    full = lambda a: pl.BlockSpec(a.shape, lambda b, i: (0,) * a.ndim)
    one = jax.ShapeDtypeStruct((bn, s, B_WIDTH), F32)
    two = jax.ShapeDtypeStruct((2, bn, s, B_WIDTH), F32)
    kern = functools.partial(_rwkv_prep_kernel, ctx_tiles=ctx_len // TM, n_tiles=n_tiles)
    params = (mu, w0, w2bd, a0, a2bd, g2, k_k, k_a, r_k)
    return pl.pallas_call(
        kern,
        grid=(bn, n_tiles),
        in_specs=[row(N_B),
                  pl.BlockSpec((None, 8, N_B), lambda b, i: (b, jnp.maximum(i * eight - 1, 0), 0)),
                  pl.BlockSpec((None, 8, N_B), lambda b, i: (b, jnp.minimum((i + 1) * eight, s // 8 - 1), 0)),
                  ] + [full(p) for p in params],
        out_specs=[row(B_WIDTH), row(B_WIDTH), row(B_WIDTH), drow, drow, drow, row(B_WIDTH), row(B_WIDTH)],
        out_shape=[one, one, one, two, two, two, one, one],
        compiler_params=_params(("parallel", "parallel")),
        name="rwkv_prep",
    )(pb, pb, pb, *params)


def _pair_diag(x):
    lo = lax.broadcasted_iota(jnp.int32, (1, LANES), 1) < B_HEAD
    z = jnp.zeros_like(x)
    return jnp.concatenate([jnp.where(lo, x, z), jnp.where(lo, z, x)], axis=0)


def _pair_pick(x):
    lo = lax.broadcasted_iota(jnp.int32, (1, LANES), 1) < B_HEAD
    return jnp.where(lo, x[:B_HEAD], x[B_HEAD:])


def _rwkv_pair_kernel(rf_ref, vf_ref, kkf_ref, rb_ref, vb_ref, kkb_ref,
                      ldf_ref, kdf_ref, betaf_ref, ldb_ref, kdb_ref, betab_ref,
                      yf_ref, yb_ref, h_ref):
    c = pl.program_id(1)

    @pl.when(c == 0)
    def _():
        h_ref[...] = jnp.zeros_like(h_ref)

    n_pairs = B_WIDTH // LANES
    ti = lax.broadcasted_iota(jnp.int32, (CHUNK, LANES), 0)
    si = lax.broadcasted_iota(jnp.int32, (CHUNK, LANES), 1) % CHUNK
    t64 = lax.broadcasted_iota(jnp.int32, (CHUNK, CHUNK), 0)
    s64 = lax.broadcasted_iota(jnp.int32, (CHUNK, CHUNK), 1)
    eye = ti == si
    dirs = ((rf_ref, vf_ref, kkf_ref, ldf_ref, kdf_ref, betaf_ref, False),
            (rb_ref, vb_ref, kkb_ref, ldb_ref, kdb_ref, betab_ref, True))

    units = []
    for d, (r_ref, v_ref, kk_ref, ld_ref, kd_ref, beta_ref, rev) in enumerate(dirs):
        strict = (si > ti) if rev else (ti > si)
        incl = (si >= ti) if rev else (ti >= si)
        tri = ((s64 >= t64) if rev else (t64 >= s64)).astype(BF16)
        ld_all = ld_ref[...]
        cl_all = _dot_exact_lhs(tri, ld_all)
        for p in range(n_pairs):
            sl = slice(p * LANES, (p + 1) * LANES)
            ld, cl = ld_all[:, sl], cl_all[:, sl]
            total = jnp.sum(ld, axis=0, keepdims=True)
            inv_gam = jnp.exp(-cl)
            to_end = jnp.exp(total - cl)
            kk, kd, beta = kk_ref[:, sl], kd_ref[:, sl], beta_ref[:, sl]
            units.append(dict(
                d=d, p=p, sl=sl, strict=strict, incl=incl, v=v_ref[:, sl].astype(BF16),
                a_bar=-kk * jnp.exp(cl - ld), r_bar=r_ref[:, sl] * jnp.exp(cl),
                b_til=(beta * inv_gam).astype(BF16), k_til=(kd * inv_gam).astype(BF16),
                b_hat=(beta * to_end).astype(BF16), k_hat=(kd * to_end).astype(BF16),
                gam_c=jnp.exp(total)))

    for u in units:
        x_mat = jnp.concatenate([u["a_bar"], u["r_bar"]], axis=0).astype(BF16)
        rhs = jnp.concatenate([_pair_diag(u["b_til"]), _pair_diag(u["k_til"])], axis=0)
        xbk = _dot(x_mat, rhs, NT)
        u["n_ab"] = jnp.where(u["strict"], xbk[:CHUNK, :LANES], 0.0)
        u["l_rb"] = jnp.where(u["incl"], xbk[CHUNK:, :LANES], 0.0)
        n_ak = jnp.where(u["strict"], xbk[:CHUNK, LANES:], 0.0)
        l_rk = jnp.where(u["incl"], xbk[CHUNK:, LANES:], 0.0)
        u["nl"] = jnp.concatenate([n_ak, l_rk], axis=0).astype(BF16)
    for u in units:
        nv = _dot(u["nl"], _pair_diag(u["v"]))
        u["w"], u["u0"], u["lrkv"] = u["a_bar"], nv[:CHUNK], nv[CHUNK:]
        u["npow"] = u["n_ab"].astype(BF16)

    steps = int(math.log2(CHUNK))
    for kstep in range(steps):
        for u in units:
            rhs = jnp.concatenate([_pair_diag(u["w"].astype(BF16)), _pair_diag(u["u0"].astype(BF16))], axis=1)
            upd = _dot(u["npow"], rhs)
            u["w"] = u["w"] + upd[:, :LANES]
            u["u0"] = u["u0"] + upd[:, LANES:]
        if kstep + 1 < steps:
            for u in units:
                u["npow"] = _dot(u["npow"], _pair_diag(u["npow"])).astype(BF16)

    for u in units:
        wb, ub = u["w"].astype(BF16), u["u0"].astype(BF16)
        lx = _dot(u["l_rb"].astype(BF16), jnp.concatenate([_pair_diag(wb), _pair_diag(ub)], axis=1))
        u["p_mat"] = u["r_bar"] + lx[:, :LANES]
        u["y0"] = u["lrkv"] + lx[:, LANES:]
        lhs = jnp.concatenate([u["b_hat"], u["k_hat"]], axis=0)
        rhs = jnp.concatenate([jnp.concatenate([wb, ub], axis=1),
                               jnp.concatenate([jnp.zeros_like(wb), u["v"]], axis=1)], axis=0)
        mg = _dot(lhs, rhs, TN)
        u["m_full"] = _pair_pick(mg[:, :LANES]) + jnp.where(eye, u["gam_c"], 0.0)
        u["g_mat"] = _pair_pick(mg[:, LANES:])

    ys = [[None] * n_pairs for _ in range(2)]
    for u in units:
        h0 = h_ref[u["d"], u["p"]]
        a0, a1 = _split(jnp.concatenate([u["p_mat"], u["m_full"]], axis=0), 2)
        h_hi, h_lo = _split(h0, 2)
        bh, bl = _pair_diag(h_hi), _pair_diag(h_lo)
        out = _dot(a0, bh) + (_dot(a0, bl) + _dot(a1, bh))
        ys[u["d"]][u["p"]] = out[:CHUNK] + u["y0"]
        h_ref[u["d"], u["p"]] = out[CHUNK:] + u["g_mat"]
    yf_ref[...] = jnp.concatenate(ys[0], axis=1)
    yb_ref[...] = jnp.concatenate(ys[1], axis=1)


def _rwkv_pairs(r, v, kk, ld, kd, beta, ctx_len):
    bn, s, _ = r.shape
    n_chunks = s // CHUNK
    ctx_chunks = ctx_len // CHUNK

    def back(c):
        return jnp.where(c < ctx_chunks, ctx_chunks - 1 - c, n_chunks - 1 + ctx_chunks - c)

    fwd = pl.BlockSpec((None, CHUNK, B_WIDTH), lambda b, c: (b, c, 0))
    bwd = pl.BlockSpec((None, CHUNK, B_WIDTH), lambda b, c: (b, back(c), 0))
    fwd_d = pl.BlockSpec((None, None, CHUNK, B_WIDTH), lambda b, c: (0, b, c, 0))
    bwd_d = pl.BlockSpec((None, None, CHUNK, B_WIDTH), lambda b, c: (1, b, back(c), 0))
    y = jax.ShapeDtypeStruct((bn, s, B_WIDTH), F32)
    return pl.pallas_call(
        _rwkv_pair_kernel,
        grid=(bn, n_chunks),
        in_specs=[fwd, fwd, fwd, bwd, bwd, bwd, fwd_d, fwd_d, fwd_d, bwd_d, bwd_d, bwd_d],
        out_specs=[fwd, bwd],
        out_shape=[y, y],
        scratch_shapes=[pltpu.VMEM((2, B_WIDTH // LANES, B_HEAD, LANES), F32)],
        compiler_params=_params(("parallel", "arbitrary")),
        name="rwkv_scan",
    )(r, v, kk, r, v, kk, ld, kd, beta, ld, kd, beta)


def _outproj_kernel(x_ref, mod_ref, a_ref, yf_ref, yb_ref, bonus_ref, g_ref, c_ref,
                    wa_ref, wb_ref, wc_ref, lnxg_ref, lnxb_ref, ln1g_ref, ln1b_ref, o_ref, *, alpha):
    mod = mod_ref[...]
    y = yf_ref[...] + yb_ref[...] + bonus_ref[...]
    ones = _group_ones(B_WIDTH, B_HEAD)
    inv = 1.0 / B_HEAD
    mu = _dot_exact_rhs(y, ones) * inv
    yc = y - mu
    var = _dot_exact_rhs(yc * yc, ones) * inv
    yn = yc * lax.rsqrt(var + GN_EPS) * lnxg_ref[...] + lnxb_ref[...]
    bmix = (yn * g_ref[...]).astype(BF16)
    o = _dot(a_ref[...], wa_ref[...]) + _dot(bmix, wb_ref[...]) + _dot(c_ref[...], wc_ref[...])
    o_ref[...] = _ln(alpha * x_ref[...] + mod[2:3] * o, LN_EPS) * ln1g_ref[...] + ln1b_ref[...]


def _outproj(xs, mods, a_out, yf, yb, bonus, g, c_out, wa, wb, wc, lnxg, lnxb, ln1g, ln1b, ctx_len, alpha):
    bn, s, d = xs.shape
    row = lambda width: pl.BlockSpec((None, TM, width), lambda b, i: (b, i, 0))
    full = lambda a: pl.BlockSpec(a.shape, lambda b, i: (0,) * a.ndim)
    consts = (wa, wb, wc, lnxg, lnxb, ln1g, ln1b)
    return pl.pallas_call(
        functools.partial(_outproj_kernel, alpha=alpha),
        grid=(bn, s // TM),
        in_specs=[row(d), _mod_spec(ctx_len, d),
                  row(A_WIDTH), row(B_WIDTH), row(B_WIDTH), row(B_WIDTH), row(B_WIDTH), row(C_WIDTH)]
                 + [full(p) for p in consts],
        out_specs=row(d),
        out_shape=jax.ShapeDtypeStruct((bn, s, d), F32),
        compiler_params=_params(("parallel", "parallel")),
        name="outproj",
    )(xs, mods, a_out, yf, yb, bonus, g, c_out, *consts)


def _swiglu_rows(h, w1_ref, w3_ref, w2_ref):
    acc = jnp.zeros((h.shape[0], w2_ref.shape[-1]), F32)
    for j in range(w1_ref.shape[-1] // FF_CHUNK):
        sl = slice(j * FF_CHUNK, (j + 1) * FF_CHUNK)
        u = _dot(h, w1_ref[:, sl])
        t = _dot(h, w3_ref[:, sl])
        acc = acc + _dot((_silu(u) * t).astype(BF16), w2_ref[sl, :])
    return acc


def _ffn_kernel(x_ref, mod_ref, w1_ref, w3_ref, w2_ref, g_ref, b_ref, o_ref, *, alpha):
    mod = mod_ref[...]
    x = x_ref[...]
    h = (_ln(x, LN_EPS) * (1.0 + mod[4:5]) + mod[3:4]).astype(BF16)
    f = _swiglu_rows(h, w1_ref, w3_ref, w2_ref)
    o_ref[...] = _ln(alpha * x + mod[5:6] * f, LN_EPS) * g_ref[...] + b_ref[...]


def _ffn(x1, mods, w1, w3, w2, g, b, ctx_len, alpha):
    bn, s, d = x1.shape
    row = pl.BlockSpec((None, TM, d), lambda bb, i: (bb, i, 0))
    resident = lambda a: pl.BlockSpec(a.shape, lambda bb, i: (0,) * a.ndim, pipeline_mode=pl.Buffered(1))
    full = lambda a: pl.BlockSpec(a.shape, lambda bb, i: (0,) * a.ndim)
    return pl.pallas_call(
        functools.partial(_ffn_kernel, alpha=alpha),
        grid=(bn, s // TM),
        in_specs=[row, _mod_spec(ctx_len, d),
                  resident(w1), resident(w3), resident(w2), full(g), full(b)],
        out_specs=row,
        out_shape=jax.ShapeDtypeStruct((bn, s, d), F32),
        compiler_params=_params(("parallel", "parallel")),
        name="ffn",
    )(x1, mods, w1, w3, w2, g, b)


def _moe_pre_kernel(x_ref, mod_ref, router_ref, h_ref, logit_ref):
    mod = mod_ref[...]
    h = _ln(x_ref[...], LN_EPS) * (1.0 + mod[4:5]) + mod[3:4]
    h_ref[...] = h.astype(h_ref.dtype)
    logit_ref[...] = _dot6(h, router_ref[...])


def _moe_pre(x1, mods, router_p, ctx_len):
    bn, s, d = x1.shape
    ct = ctx_len // TM
    lt = (s - ctx_len) // TM
    return pl.pallas_call(
        _moe_pre_kernel,
        grid=(bn, lt),
        in_specs=[pl.BlockSpec((None, TM, d), lambda b, i: (b, i + ct, 0)),
                  pl.BlockSpec((None, None, 6, d), lambda b, i: (b, 1, 0, 0)),
                  pl.BlockSpec(router_p.shape, lambda b, i: (0, 0))],
        out_specs=[pl.BlockSpec((TM, d), lambda b, i: (b * lt + i, 0)),
                   pl.BlockSpec((TM, LANES), lambda b, i: (b * lt + i, 0))],
        out_shape=[jax.ShapeDtypeStruct((bn * lt * TM, d), BF16),
                   jax.ShapeDtypeStruct((bn * lt * TM, LANES), F32)],
        compiler_params=_params(("parallel", "parallel")),
        name="moe_pre",
    )(x1, mods, router_p)


def _slot_onehot(slots_ref, block):
    sl = slots_ref[...]
    s_iota = lax.broadcasted_iota(jnp.int32, (MOE_ROWS, TM), 0) + block * MOE_ROWS
    hit = jnp.logical_or(sl[0:1] == s_iota, sl[1:2] == s_iota)
    return jnp.where(hit, 1.0, 0.0).astype(BF16)


def _dispatch_kernel(wb_ref, wc_ref, wf_ref, wv_ref, slots_ref, h_ref, o_ref):
    w = pl.program_id(0)
    part = lambda: _dot(_slot_onehot(slots_ref, wb_ref[w]), h_ref[...])

    @pl.when(wf_ref[w] == 1)
    def _():
        o_ref[...] = part().astype(o_ref.dtype)

    @pl.when(jnp.logical_and(wf_ref[w] == 0, wv_ref[w] == 1))
    def _():
        o_ref[...] = (o_ref[...].astype(F32) + part()).astype(o_ref.dtype)


def _dispatch(h, slots, work):
    t, d = h.shape
    wb, wc, wf, wv = work
    n_slots = (t * TOP_K // MOE_ROWS + N_EXPERTS) * MOE_ROWS
    grid_spec = pltpu.PrefetchScalarGridSpec(
        num_scalar_prefetch=4,
        grid=(wb.shape[0],),
        in_specs=[pl.BlockSpec((TOP_K, TM), lambda w, b, c, f, v: (0, c[w])),
                  pl.BlockSpec((TM, d), lambda w, b, c, f, v: (c[w], 0))],
        out_specs=pl.BlockSpec((MOE_ROWS, d), lambda w, b, c, f, v: (b[w], 0)),
    )
    return pl.pallas_call(
        _dispatch_kernel,
        grid_spec=grid_spec,
        out_shape=jax.ShapeDtypeStruct((n_slots, d), BF16),
        compiler_params=_params(("arbitrary",)),
        name="moe_dispatch",
    )(wb, wc, wf, wv, slots, h)


def _collect_kernel(wb_ref, wc_ref, wf_ref, wv_ref, slots_ref, y_ref, o_ref):
    w = pl.program_id(0)
    part = lambda: _dot(_slot_onehot(slots_ref, wb_ref[w]), y_ref[...], TN)

    @pl.when(wf_ref[w] == 1)
    def _():
        o_ref[...] = part()

    @pl.when(jnp.logical_and(wf_ref[w] == 0, wv_ref[w] == 1))
    def _():
        o_ref[...] = o_ref[...] + part()


def _collect(y_buf, slots, work):
    n_slots, d = y_buf.shape
    t = slots.shape[1]
    wb, wc, wf, wv = work
    grid_spec = pltpu.PrefetchScalarGridSpec(
        num_scalar_prefetch=4,
        grid=(wb.shape[0],),
        in_specs=[pl.BlockSpec((TOP_K, TM), lambda w, b, c, f, v: (0, c[w])),
                  pl.BlockSpec((MOE_ROWS, d), lambda w, b, c, f, v: (b[w], 0))],
        out_specs=pl.BlockSpec((TM, d), lambda w, b, c, f, v: (c[w], 0)),
    )
    return pl.pallas_call(
        _collect_kernel,
        grid_spec=grid_spec,
        out_shape=jax.ShapeDtypeStruct((t, d), F32),
        compiler_params=_params(("arbitrary",)),
        name="moe_collect",
    )(wb, wc, wf, wv, slots, y_buf)


def _expert_kernel(be_ref, nb_ref, x_ref, gate_ref, w1_ref, w3_ref, w2_ref, o_ref):
    i = pl.program_id(0)

    @pl.when(i < nb_ref[0])
    def _():
        y = _swiglu_rows(x_ref[...], w1_ref, w3_ref, w2_ref)
        o_ref[...] = (y * gate_ref[...]).astype(o_ref.dtype)

    @pl.when(i >= nb_ref[0])
    def _():
        o_ref[...] = jnp.zeros_like(o_ref)


def _experts(buf, gate_slot, block_e, n_used, w1, w3, w2):
    n, d = buf.shape
    n_blocks = n // MOE_ROWS
    ff = w1.shape[-1]
    wspec = lambda shape: pl.BlockSpec((None,) + shape, lambda i, be, nb: (be[i], 0, 0),
                                       pipeline_mode=pl.Buffered(1))
    used = lambda i, be, nb: (jnp.minimum(i, nb[0] - 1), 0)
    grid_spec = pltpu.PrefetchScalarGridSpec(
        num_scalar_prefetch=2,
        grid=(n_blocks,),
        in_specs=[pl.BlockSpec((MOE_ROWS, d), used), pl.BlockSpec((MOE_ROWS, 1), used),
                  wspec((d, ff)), wspec((d, ff)), wspec((ff, d))],
        out_specs=pl.BlockSpec((MOE_ROWS, d), lambda i, be, nb: (i, 0)),
    )
    return pl.pallas_call(
        _expert_kernel,
        grid_spec=grid_spec,
        out_shape=jax.ShapeDtypeStruct((n, d), BF16),
        compiler_params=_params(("arbitrary",)),
        name="experts",
    )(block_e, n_used, buf, gate_slot, w1, w3, w2)


def _combine_kernel(x_ref, mod_ref, y_ref, g_ref, b_ref, o_ref, *, alpha):
    mod = mod_ref[...]
    o_ref[...] = _ln(alpha * x_ref[...] + mod[5:6] * y_ref[...], LN_EPS) * g_ref[...] + b_ref[...]


def _combine(x1, mods, y_tok, g, b, ctx_len, alpha):
    bn, s, d = x1.shape
    ct = ctx_len // TM
    lt = (s - ctx_len) // TM
    full = lambda a: pl.BlockSpec(a.shape, lambda bb, i: (0,) * a.ndim)
    return pl.pallas_call(
        functools.partial(_combine_kernel, alpha=alpha),
        grid=(bn, lt),
        in_specs=[pl.BlockSpec((None, TM, d), lambda bb, i: (bb, i + ct, 0)),
                  pl.BlockSpec((None, None, 6, d), lambda bb, i: (bb, 1, 0, 0)),
                  pl.BlockSpec((TM, d), lambda bb, i: (bb * lt + i, 0)),
                  full(g), full(b)],
        out_specs=pl.BlockSpec((None, TM, d), lambda bb, i: (bb, i, 0)),
        out_shape=jax.ShapeDtypeStruct((bn, lt * TM, d), F32),
        compiler_params=_params(("parallel", "parallel")),
        name="moe_combine",
    )(x1, mods, y_tok, g, b)


def _work_lists(rank_at_tile, counts, pstart, block_e, n_used, n_blocks, n_tiles):
    n_work = n_tiles * N_EXPERTS + n_blocks
    blocks = jnp.arange(n_blocks, dtype=jnp.int32)
    used = blocks < n_used
    r0 = blocks * MOE_ROWS - pstart[block_e]
    r_last = jnp.minimum(r0 + MOE_ROWS, counts[block_e]) - 1
    cols = rank_at_tile.T[block_e]
    find = jax.vmap(lambda col, val: jnp.searchsorted(col, val, side="right"))
    lo = jnp.clip(find(cols, r0) - 1, 0, n_tiles - 1)
    hi = jnp.clip(find(cols, r_last) - 1, 0, n_tiles - 1)
    n_b = jnp.where(used, hi - lo + 1, 0)
    ends = jnp.cumsum(n_b)
    starts = ends - n_b
    total = ends[-1]
    w = jnp.arange(n_work, dtype=jnp.int32)
    valid = w < total
    wl = jnp.minimum(w, total - 1)
    blk = jnp.minimum(jnp.searchsorted(ends, wl, side="right"), n_blocks - 1).astype(jnp.int32)
    tile = (lo[blk] + (wl - starts[blk])).astype(jnp.int32)
    first = jnp.logical_and(valid, w == starts[blk])
    as_i32 = lambda a: a.astype(jnp.int32)
    by_block = (blk, tile, as_i32(first), as_i32(valid))
    order = jnp.argsort(jnp.where(valid, tile, n_tiles), stable=True)
    order = order[jnp.minimum(w, total - 1)]
    tile2, blk2 = tile[order], blk[order]
    first2 = jnp.logical_and(valid, jnp.concatenate([jnp.ones((1,), bool), tile2[1:] != tile2[:-1]]))
    by_tile = (blk2, tile2, as_i32(first2), as_i32(valid))
    return by_block, by_tile


def _moe_layer(x1, mods, router, w1, w3, w2, g, b, ctx_len, alpha):
    d = x1.shape[-1]
    router_p = jnp.pad(router, ((0, 0), (0, LANES - N_EXPERTS)))
    h, logits = _moe_pre(x1, mods, router_p, ctx_len)
    t = h.shape[0]
    top_v, top_i = lax.top_k(logits[:, :N_EXPERTS], TOP_K)
    gates = jax.nn.softmax(top_v, axis=-1)
    e_flat = top_i.reshape(-1)
    onehot = (e_flat[:, None] == jnp.arange(N_EXPERTS)[None, :]).astype(jnp.int32)
    ranks = jnp.cumsum(onehot, axis=0) - onehot
    rank = jnp.sum(ranks * onehot, axis=1)
    counts = jnp.sum(onehot, axis=0)
    padded = (counts + MOE_ROWS - 1) // MOE_ROWS * MOE_ROWS
    pend = jnp.cumsum(padded)
    pstart = pend - padded
    slot = (pstart[e_flat] + rank).astype(jnp.int32)
    n_blocks = t * TOP_K // MOE_ROWS + N_EXPERTS
    n_tiles = t // TM
    block_e = jnp.minimum(jnp.searchsorted(pend, jnp.arange(n_blocks) * MOE_ROWS, side="right"),
                          N_EXPERTS - 1).astype(jnp.int32)
    n_used = (pend[-1:] // MOE_ROWS).astype(jnp.int32)
    rank_at_tile = jnp.concatenate([ranks[::TM * TOP_K], counts[None, :]], axis=0)
    by_block, by_tile = _work_lists(rank_at_tile, counts, pstart, block_e, n_used, n_blocks, n_tiles)
    slots = slot.reshape(t, TOP_K).T
    gate_slot = jnp.zeros((n_blocks * MOE_ROWS, 1), F32).at[slot, 0].set(gates.reshape(-1))
    buf = _dispatch(h, slots, by_block)
    y_buf = _experts(buf, gate_slot, block_e, n_used, w1, w3, w2)
    y_tok = _collect(y_buf, slots, by_tile)
    return _combine(x1, mods, y_tok, g, b, ctx_len, alpha)


def _rope_tables(n_rows, ctx_len, dim, lane_lo):
    quarter = dim // 4
    inv = ROPE_THETA ** (-jnp.arange(quarter, dtype=F32) / quarter)
    rows = jnp.repeat(jnp.arange(n_rows, dtype=F32), GRID_W)
    cols = jnp.tile(jnp.arange(GRID_W, dtype=F32), n_rows)
    ang = jnp.concatenate([rows[:, None] * inv, rows[:, None] * inv,
                           cols[:, None] * inv, cols[:, None] * inv], axis=-1)
    sign = jnp.tile(jnp.concatenate([-jnp.ones(quarter, F32), jnp.ones(quarter, F32)]), 2)
    cos, sin = jnp.cos(ang), jnp.sin(ang) * sign
    length = cos.shape[0]
    if lane_lo == 0:
        reps = LANES // dim
        cos, sin = jnp.tile(cos, (1, reps)), jnp.tile(sin, (1, reps))
    else:
        pad = ((0, 0), (lane_lo, LANES - lane_lo - dim))
        cos = jnp.pad(cos, pad, constant_values=1.0)
        sin = jnp.pad(sin, pad)
    cos = jnp.concatenate([jnp.ones((ctx_len, LANES), F32), cos], axis=0)
    sin = jnp.concatenate([jnp.zeros((ctx_len, LANES), F32), sin], axis=0)
    return cos, sin


def _block_diag2(w):
    z = jnp.zeros_like(w[0])
    return jnp.concatenate([jnp.concatenate([w[0], z], axis=1), jnp.concatenate([z, w[1]], axis=1)], axis=0)


def _lambda_init(layer):
    return 0.8 - 0.6 * math.exp(-0.3 * layer)


def kernel(x, c, ctx, c_ctx, ada_w, ada_b, w_in, w_out, ln1_g, ln1_b, ln2_g, ln2_b,
           lam_q1, lam_k1, lam_q2, lam_k2, diff_norm_g, shift_mu, w0, w2, a0, a2, g2,
           k_k, k_a, r_k, lnx_g, lnx_b, q_norm_g, w_uq, kv_norm_g, w_ukv,
           ff_w1, ff_w3, ff_w2, router, moe_w1, moe_w3, moe_w2):
    bn, seq, d = x.shape
    ctx_len = ctx.shape[1]
    depth = ada_w.shape[0]
    assert d == D_MODEL and seq % TM == 0 and ctx_len % TM == 0 and seq % GRID_W == 0
    alpha = (2.0 * depth) ** 0.25
    n_grid_rows = seq // GRID_W
    cos_a, sin_a = _rope_tables(n_grid_rows, ctx_len, A_QK_DIM, 0)
    cos_c, sin_c = _rope_tables(n_grid_rows, ctx_len, C_ROPE, C_NOPE)

    cond_rows = 8 * ((bn + 1 + 7) // 8)
    cond = jnp.zeros((cond_rows, d), F32).at[:bn].set(c).at[bn].set(c_ctx)
    xs = jnp.concatenate([ctx, x], axis=1)

    for i in range(depth):
        with_ctx = i < depth - 1
        m = _ada(cond, ada_w[i], ada_b[i]).reshape(cond_rows, 6, d)
        mods = jnp.stack([jnp.broadcast_to(m[bn], (bn, 6, d)), m[:bn]], axis=1)

        wc = w_in[i][:, N_A + N_B:]
        kpe_w = jnp.pad(wc[:, C_Q_RANK + C_KV_RANK:], ((0, 0), (C_NOPE, LANES - C_NOPE - C_ROPE)))
        w_p = jnp.concatenate([w_in[i][:, :N_A + N_B], wc[:, :C_Q_RANK + C_KV_RANK], kpe_w], axis=1).astype(BF16)
        qa, ka, va, pb, pc = _inproj(xs, mods, w_p, cos_a, sin_a, ctx_len)

        lam_p = jnp.stack([lam_q1[i], lam_k1[i], lam_q2[i], lam_k2[i]])
        g_col = jnp.broadcast_to(diff_norm_g[i][:, None], (A_V_DIM, TM))
        a_out = _attention(qa, ka, va, lam_p, g_col, mode="diff", ctx_len=ctx_len, lam_init=_lambda_init(i))

        wq = w_uq[i].reshape(C_Q_RANK, C_HEADS, C_NOPE + C_ROPE)
        wq_p = jnp.pad(wq, ((0, 0), (0, 0), (0, LANES - C_NOPE - C_ROPE))).reshape(C_Q_RANK, -1).astype(BF16)
        wkv = w_ukv[i].reshape(C_KV_RANK, C_HEADS, C_NOPE + C_V)
        wk_p = jnp.pad(wkv[:, :, :C_NOPE], ((0, 0), (0, 0), (0, LANES - C_NOPE))).reshape(C_KV_RANK, -1).astype(BF16)
        wv_p = wkv[:, :, C_NOPE:].reshape(C_KV_RANK, -1).astype(BF16)
        qc, kc, vc = _mla_prep(pc, q_norm_g[i].reshape(1, -1), kv_norm_g[i].reshape(1, -1),
                               wq_p, wk_p, wv_p, cos_c, sin_c)
        c_out = _attention(qc, kc, vc, lam_p, g_col, mode="mla", ctx_len=ctx_len)

        r, v, kkn, ld, kd, beta, gate, bonus = _rwkv_prep(
            pb, shift_mu[i].reshape(1, -1), w0[i].reshape(1, -1), _block_diag2(w2[i]).astype(BF16),
            a0[i].reshape(1, -1), _block_diag2(a2[i]).astype(BF16), g2[i].astype(BF16),
            k_k[i].reshape(1, -1), k_a[i].reshape(1, -1), r_k[i].reshape(1, -1), ctx_len)
        yf, yb = _rwkv_pairs(r, v, kkn, ld, kd, beta, ctx_len)

        wo = w_out[i].astype(BF16)
        x1 = _outproj(xs, mods, a_out, yf, yb, bonus, gate, c_out,
                      wo[:A_WIDTH], wo[A_WIDTH:A_WIDTH + B_WIDTH], wo[A_WIDTH + B_WIDTH:],
                      lnx_g[i].reshape(1, -1), lnx_b[i].reshape(1, -1),
                      ln1_g[i].reshape(1, -1), ln1_b[i].reshape(1, -1), ctx_len, alpha)

        j = i // 2
        g2n, b2n = ln2_g[i].reshape(1, -1), ln2_b[i].reshape(1, -1)
        if i % 2 == 0:
            xs = _ffn(x1, mods, ff_w1[j].astype(BF16), ff_w3[j].astype(BF16), ff_w2[j].astype(BF16),
                      g2n, b2n, ctx_len, alpha)
        else:
            if with_ctx:
                raise NotImplementedError("routed FFN on the context rows is not needed at this depth")
            return _moe_layer(x1, mods, router[j], moe_w1[j].astype(BF16), moe_w3[j].astype(BF16),
                              moe_w2[j].astype(BF16), g2n, b2n, ctx_len, alpha)
    return xs[:, ctx_len:]
```

```python
import functools
import math

import jax
import jax.numpy as jnp
from jax import lax
from jax.experimental import pallas as pl
from jax.experimental.pallas import tpu as pltpu

F32 = jnp.float32
BF16 = jnp.bfloat16

D_MODEL = 1024
GRID_W = 64
ROPE_THETA = 10000.0
A_HEADS, A_QK_DIM, A_V_DIM = 4, 64, 128
A_WIDTH = A_HEADS * A_V_DIM
A_QK_COLS = 2 * A_HEADS * A_QK_DIM
B_HEADS, B_HEAD = 4, 64
B_WIDTH = B_HEADS * B_HEAD
B_DECAY_RANK, B_A_RANK, B_GATE_RANK = 64, 64, 128
C_HEADS, C_NOPE, C_ROPE, C_V = 4, 64, 32, 64
C_WIDTH = C_HEADS * C_V
C_Q_RANK, C_KV_RANK = 256, 128
N_A = 2 * A_QK_COLS + A_WIDTH
N_B = 3 * B_WIDTH + 2 * B_DECAY_RANK + 2 * B_A_RANK + B_GATE_RANK
N_C = C_Q_RANK + C_KV_RANK + C_ROPE
N_C_PAD = C_Q_RANK + C_KV_RANK + 128
D_FF = 3584
N_EXPERTS = 8
TOP_K = 2
LN_EPS = 1e-6
RMS_EPS = 1e-6
GN_EPS = 64e-5

LANES = 128
TM = 256
CHUNK = 64
RWKV_ROWS = 2
ATT_TK = 512
ATT_PAR = 4
MOE_ROWS = 512
FF_CHUNK = 512
VMEM_LIMIT = 56 * 1024 * 1024

NN = (((1,), (0,)), ((), ()))
NT = (((1,), (1,)), ((), ()))
TN = (((0,), (0,)), ((), ()))


def _params(sem, vmem=VMEM_LIMIT, flags=None):
    return pltpu.CompilerParams(dimension_semantics=sem, vmem_limit_bytes=vmem, flags=flags)


def _split(x, n):
    parts, r = [], x
    for _ in range(n):
        p = r.astype(BF16)
        parts.append(p)
        r = r - p.astype(F32)
    return parts


def _dot(a, b, dn=NN):
    return lax.dot_general(a, b, dn, preferred_element_type=F32)


def _dot1(a, b, dn=NN):
    return _dot(a.astype(BF16), b.astype(BF16), dn)


def _dot3(a, b, dn=NN):
    a0, a1 = _split(a, 2)
    b0, b1 = _split(b, 2)
    return _dot(a0, b0, dn) + (_dot(a0, b1, dn) + _dot(a1, b0, dn))


def _dot6(a, b, dn=NN):
    a0, a1, a2 = _split(a, 3)
    b0, b1, b2 = _split(b, 3)
    lo = _dot(a1, b1, dn) + (_dot(a0, b2, dn) + _dot(a2, b0, dn))
    return _dot(a0, b0, dn) + ((_dot(a0, b1, dn) + _dot(a1, b0, dn)) + lo)


def _dot_exact_lhs(a_bf16, b, dn=NN, n=3):
    parts = _split(b, n)
    out = _dot(a_bf16, parts[-1], dn)
    for p in parts[-2::-1]:
        out = out + _dot(a_bf16, p, dn)
    return out


def _dot_exact_rhs(a, b_bf16, dn=NN, n=3):
    parts = _split(a, n)
    out = _dot(parts[-1], b_bf16, dn)
    for p in parts[-2::-1]:
        out = out + _dot(p, b_bf16, dn)
    return out


def _ln(x, eps):
    mu = jnp.mean(x, axis=-1, keepdims=True)
    xc = x - mu
    return xc * lax.rsqrt(jnp.mean(xc * xc, axis=-1, keepdims=True) + eps)


def _sigmoid(x):
    return 1.0 / (1.0 + jnp.exp(-x))


def _silu(x):
    return x * _sigmoid(x)


def _group_ones(width, group):
    r = lax.broadcasted_iota(jnp.int32, (width, width), 0) // group
    c = lax.broadcasted_iota(jnp.int32, (width, width), 1) // group
    return (r == c).astype(BF16)


def _partner(x, half):
    lane = lax.broadcasted_iota(jnp.int32, x.shape, 1)
    up = pltpu.roll(x, LANES - half, 1)
    dn = pltpu.roll(x, half, 1)
    return jnp.where((lane % (2 * half)) < half, up, dn)


def _rope(x, cos, sin, half):
    return x * cos + _partner(x, half) * sin


def _ada_kernel(c_ref, w_ref, b_ref, o_ref):
    o_ref[...] = _dot3(_silu(c_ref[...]), w_ref[...]) + b_ref[...]


def _ada(cond, w, b):
    rows, d = cond.shape
    n = w.shape[1]
    tn = 1536
    return pl.pallas_call(
        _ada_kernel,
        grid=(n // tn,),
        in_specs=[pl.BlockSpec((rows, d), lambda j: (0, 0)),
                  pl.BlockSpec((d, tn), lambda j: (0, j)),
                  pl.BlockSpec((1, tn), lambda j: (0, j))],
        out_specs=pl.BlockSpec((rows, tn), lambda j: (0, j)),
        out_shape=jax.ShapeDtypeStruct((rows, n), F32),
        compiler_params=_params(("parallel",)),
        name="ada",
    )(cond, w, b.reshape(1, n))


def _inproj_kernel(x_ref, mod_ref, w_ref, cos_ref, sin_ref,
                   q_ref, k_ref, v_ref, pb_ref, pc_ref):
    mod = mod_ref[...]
    h = (_ln(x_ref[...], LN_EPS) * (1.0 + mod[1:2]) + mod[0:1]).astype(BF16)
    cos, sin = cos_ref[...], sin_ref[...]
    scale = A_QK_DIM ** -0.5
    for j in range(A_QK_COLS // LANES):
        sl = slice(j * LANES, (j + 1) * LANES)
        qj = _dot(h, w_ref[:, sl])
        q_ref[sl, :] = (_rope(qj, cos, sin, A_QK_DIM // 4) * scale).T.astype(BF16)
        kj = _dot(h, w_ref[:, A_QK_COLS + j * LANES:A_QK_COLS + (j + 1) * LANES])
        k_ref[:, sl] = _rope(kj, cos, sin, A_QK_DIM // 4).astype(BF16)
        vj = _dot(h, w_ref[:, 2 * A_QK_COLS + j * LANES:2 * A_QK_COLS + (j + 1) * LANES])
        v_ref[sl, :] = vj.T.astype(BF16)
    pb_ref[...] = _dot(h, w_ref[:, N_A:N_A + N_B])
    pc_ref[...] = _dot(h, w_ref[:, N_A + N_B:])


def _mod_spec(ctx_len, d):
    ct = ctx_len // TM
    return pl.BlockSpec((None, None, 6, d), lambda b, i: (b, jnp.where(i >= ct, 1, 0), 0, 0))


def _inproj(xs, mods, w_p, cos_a, sin_a, ctx_len):
    bn, s, d = xs.shape
    n_tiles = s // TM
    n_w = w_p.shape[1]
    row = lambda width: pl.BlockSpec((None, TM, width), lambda b, i: (b, i, 0))
    col = lambda width: pl.BlockSpec((None, width, TM), lambda b, i: (b, 0, i))
    outs = [jax.ShapeDtypeStruct((bn, A_QK_COLS, s), BF16),
            jax.ShapeDtypeStruct((bn, s, A_QK_COLS), BF16),
            jax.ShapeDtypeStruct((bn, A_WIDTH, s), BF16),
            jax.ShapeDtypeStruct((bn, s, N_B), F32),
            jax.ShapeDtypeStruct((bn, s, N_C_PAD), F32)]
    return pl.pallas_call(
        _inproj_kernel,
        grid=(bn, n_tiles),
        in_specs=[row(d),
                  _mod_spec(ctx_len, d),
                  pl.BlockSpec((d, n_w), lambda b, i: (0, 0)),
                  pl.BlockSpec((TM, LANES), lambda b, i: (i, 0)),
                  pl.BlockSpec((TM, LANES), lambda b, i: (i, 0))],
        out_specs=[col(A_QK_COLS), row(A_QK_COLS), col(A_WIDTH), row(N_B), row(N_C_PAD)],
        out_shape=outs,
        compiler_params=_params(("parallel", "parallel")),
        name="inproj",
    )(xs, mods, w_p, cos_a, sin_a)


def _mla_prep_kernel(pc_ref, qg_ref, kvg_ref, wq_ref, wk_ref, wv_ref, cos_ref, sin_ref,
                     q_ref, k_ref, v_ref):
    pc = pc_ref[...]
    cq = pc[:, :C_Q_RANK]
    cq = cq * lax.rsqrt(jnp.mean(cq * cq, axis=-1, keepdims=True) + RMS_EPS) * qg_ref[...]
    ckv = pc[:, C_Q_RANK:C_Q_RANK + C_KV_RANK]
    ckv = ckv * lax.rsqrt(jnp.mean(ckv * ckv, axis=-1, keepdims=True) + RMS_EPS) * kvg_ref[...]
    cos, sin = cos_ref[...], sin_ref[...]
    kpe = _rope(pc[:, C_Q_RANK + C_KV_RANK:], cos, sin, C_ROPE // 4)
    cqb, ckvb = cq.astype(BF16), ckv.astype(BF16)
    scale = (C_NOPE + C_ROPE) ** -0.5
    for h in range(C_HEADS):
        sl = slice(h * LANES, (h + 1) * LANES)
        qh = _dot(cqb, wq_ref[:, sl])
        q_ref[sl, :] = (_rope(qh, cos, sin, C_ROPE // 4) * scale).T.astype(BF16)
        k_ref[:, sl] = (_dot(ckvb, wk_ref[:, sl]) + kpe).astype(BF16)
    for j in range(C_WIDTH // LANES):
        sl = slice(j * LANES, (j + 1) * LANES)
        v_ref[sl, :] = _dot(ckvb, wv_ref[:, sl]).T.astype(BF16)


def _mla_prep(pc, qg, kvg, wq_p, wk_p, wv_p, cos_c, sin_c):
    bn, s, _ = pc.shape
    row = lambda width: pl.BlockSpec((None, TM, width), lambda b, i: (b, i, 0))
    full = lambda a: pl.BlockSpec(a.shape, lambda b, i: (0,) * a.ndim)
    tab = pl.BlockSpec((TM, LANES), lambda b, i: (i, 0))
    col = lambda width: pl.BlockSpec((None, width, TM), lambda b, i: (b, 0, i))
    hw = C_HEADS * LANES
    return pl.pallas_call(
        _mla_prep_kernel,
        grid=(bn, s // TM),
        in_specs=[row(N_C_PAD), full(qg), full(kvg), full(wq_p), full(wk_p), full(wv_p), tab, tab],
        out_specs=[col(hw), row(hw), col(C_WIDTH)],
        out_shape=[jax.ShapeDtypeStruct((bn, hw, s), BF16),
                   jax.ShapeDtypeStruct((bn, s, hw), BF16),
                   jax.ShapeDtypeStruct((bn, C_WIDTH, s), BF16)],
        compiler_params=_params(("parallel", "parallel")),
        name="mla_prep",
    )(pc, qg, kvg, wq_p, wk_p, wv_p, cos_c, sin_c)


def _attn_kernel(lam_ref, g_ref, qt_ref, k_ref, vt_ref, o_ref, *, mode, ctx_tiles, ctx_len, lam_init):
    qi = pl.program_id(2)
    tq = qt_ref.shape[1]
    qw = LANES if mode == "diff" else 2 * LANES
    n_par = qt_ref.shape[0] // qw
    half = lax.broadcasted_iota(jnp.int32, (LANES, 1), 0) < (LANES // 2)
    q_ops = []
    for g in range(n_par):
        qt = qt_ref[g * qw:(g + 1) * qw, :]
        if mode == "diff":
            q_ops.append(jnp.concatenate([jnp.where(half, qt, jnp.zeros_like(qt)),
                                          jnp.where(half, jnp.zeros_like(qt), qt)], axis=1))
        else:
            q_ops.append((qt[:LANES], qt[LANES:]))

    def fold(x, reduce):
        rows = x.shape[0]
        while rows > 8:
            g = min(8, rows // 8)
            rows //= g
            x = reduce(x.reshape(g, rows, x.shape[1]), axis=0)
        return x

    def scores(kc, g):
        kg = kc[:, g * qw:(g + 1) * qw]
        if mode == "diff":
            return _dot(kg, q_ops[g])
        return jnp.concatenate([_dot(kg[:, :LANES], q_ops[g][0]), _dot(kg[:, LANES:], q_ops[g][1])], axis=1)

    def absorb(kc, vc, carry):
        ss = [scores(kc, g) for g in range(n_par)]
        m_new = [jnp.maximum(carry[g][0], jnp.max(fold(ss[g], jnp.max), axis=0, keepdims=True))
                 for g in range(n_par)]
        alpha = [jnp.exp(carry[g][0] - m_new[g]) for g in range(n_par)]
        ps = [jnp.exp(ss[g] - m_new[g]) for g in range(n_par)]
        l_new = [alpha[g] * carry[g][1] + jnp.sum(fold(ps[g], jnp.sum), axis=0, keepdims=True)
                 for g in range(n_par)]
        acc = [alpha[g] * carry[g][2] + _dot(vc[g * LANES:(g + 1) * LANES], ps[g].astype(BF16))
               for g in range(n_par)]
        return tuple((m_new[g], l_new[g], acc[g]) for g in range(n_par))

    n_latent_chunks = (k_ref.shape[0] - ctx_len) // ATT_TK

    def body(j, stats):
        off = pl.multiple_of(ctx_len + j * ATT_TK, LANES)
        return absorb(k_ref[pl.ds(off, ATT_TK), :], vt_ref[:, pl.ds(off, ATT_TK)], stats)

    neg = jnp.full((1, 2 * tq), -1e30, F32)
    zero1 = jnp.zeros((1, 2 * tq), F32)
    zacc = jnp.zeros((LANES, 2 * tq), F32)
    stats = absorb(k_ref[0:ctx_len, :], vt_ref[:, 0:ctx_len], ((neg, zero1, zacc),) * n_par)
    n_trips = jnp.where(qi < ctx_tiles, 0, n_latent_chunks)
    stats = lax.fori_loop(0, n_trips, body, stats)
    if mode == "diff":
        lp = lam_ref[...]
        lam = (jnp.exp(jnp.sum(lp[0:1] * lp[1:2], axis=-1, keepdims=True))
               - jnp.exp(jnp.sum(lp[2:3] * lp[3:4], axis=-1, keepdims=True)) + lam_init)
    for g in range(n_par):
        _, l_both, a_both = stats[g]
        o_both = a_both / l_both
        o0, o1 = o_both[:, :tq], o_both[:, tq:]
        if mode == "diff":
            o = o0 - lam * o1
            o = o * lax.rsqrt(jnp.mean(o * o, axis=0, keepdims=True) + RMS_EPS) * g_ref[...]
            o = o * (1.0 - lam_init)
        else:
            o = jnp.where(half, o0, o1)
        o_ref[:, g * LANES:(g + 1) * LANES] = o.T.astype(o_ref.dtype)


def _attention(qt, k, vt, lam_p, g, *, mode, ctx_len, lam_init=0.0):
    bn, s, _ = k.shape
    qw = LANES if mode == "diff" else 2 * LANES
    groups = k.shape[2] // qw
    par = math.gcd(groups, ATT_PAR)
    assert (s - ctx_len) % ATT_TK == 0 and ctx_len % LANES == 0
    kern = functools.partial(_attn_kernel, mode=mode, ctx_tiles=ctx_len // TM, ctx_len=ctx_len,
                             lam_init=lam_init)
    return pl.pallas_call(
        kern,
        grid=(bn, groups // par, s // TM),
        in_specs=[pl.BlockSpec(lam_p.shape, lambda b, h, i: (0, 0)),
                  pl.BlockSpec(g.shape, lambda b, h, i: (0, 0)),
                  pl.BlockSpec((None, par * qw, TM), lambda b, h, i: (b, h, i)),
                  pl.BlockSpec((None, s, par * qw), lambda b, h, i: (b, 0, h)),
                  pl.BlockSpec((None, par * LANES, s), lambda b, h, i: (b, h, 0))],
        out_specs=pl.BlockSpec((None, TM, par * LANES), lambda b, h, i: (b, i, h)),
        out_shape=jax.ShapeDtypeStruct((bn, s, groups * LANES), BF16),
        compiler_params=_params(("parallel", "parallel", "parallel")),
        name="attn_" + mode,
    )(lam_p, g, qt, k, vt)


def _rwkv_prep_kernel(pb_ref, prev_ref, next_ref, mu_ref, w0_ref, w2_ref, a0_ref, a2_ref, g2_ref,
                      kk_ref, ka_ref, rk_ref,
                      r_out, v_out, kkn_out, ld_out, kd_out, beta_out, g_out, bonus_out,
                      *, ctx_tiles, n_tiles):
    i = pl.program_id(1)
    x = pb_ref[...]
    row = lax.broadcasted_iota(jnp.int32, (TM, 1), 0)
    has_prev = jnp.logical_and(i != 0, i != ctx_tiles)
    has_next = jnp.logical_and(i != ctx_tiles - 1, i != n_tiles - 1)
    prev_edge = jnp.where(has_prev, prev_ref[7:8, :], 0.0)
    next_edge = jnp.where(has_next, next_ref[0:1, :], 0.0)
    xp = jnp.where(row == 0, prev_edge, pltpu.roll(x, 1, 0))
    xn = jnp.where(row == TM - 1, next_edge, pltpu.roll(x, TM - 1, 0))
    z = x + mu_ref[...] * (0.5 * (xp + xn) - x)

    r = z[:, :B_WIDTH]
    k = z[:, B_WIDTH:2 * B_WIDTH]
    v = z[:, 2 * B_WIDTH:3 * B_WIDTH]
    o = 3 * B_WIDTH
    wd = z[:, o:o + 2 * B_DECAY_RANK]
    o += 2 * B_DECAY_RANK
    ad = z[:, o:o + 2 * B_A_RANK]
    o += 2 * B_A_RANK
    gd = z[:, o:]

    u = w0_ref[...] + _dot1(jnp.tanh(wd), w2_ref[...])
    nu = -u
    w_raw = -(jnp.maximum(nu, 0.0) + jnp.log(1.0 + jnp.exp(-jnp.abs(nu)))) - 0.5
    ld = -jnp.exp(w_raw)
    lr = _sigmoid(a0_ref[...] + _dot1(ad, a2_ref[...]))
    g_out[...] = _dot1(_sigmoid(gd), g2_ref[...])

    ones = _group_ones(B_WIDTH, B_HEAD)
    kk = k * kk_ref[...]
    norm = jnp.sqrt(_dot_exact_rhs(kk * kk, ones))
    kkn = kk / jnp.maximum(norm, 1e-12)
    ka = ka_ref[...]
    kd_sum = jnp.zeros_like(k)
    for d in range(2):
        lr_d = lr[:, d * B_WIDTH:(d + 1) * B_WIDTH]
        kd = k * (1.0 + (lr_d - 1.0) * ka)
        kd_sum = kd_sum + kd
        ld_out[d] = ld[:, d * B_WIDTH:(d + 1) * B_WIDTH]
        kd_out[d] = kd
        beta_out[d] = kkn * lr_d
    bonus_out[...] = _dot_exact_rhs(r * kd_sum * rk_ref[...], ones) * v
    r_out[...] = r
    v_out[...] = v
    kkn_out[...] = kkn


def _rwkv_prep(pb, mu, w0, w2bd, a0, a2bd, g2, k_k, k_a, r_k, ctx_len):
    bn, s, _ = pb.shape
    n_tiles = s // TM
    eight = TM // 8
    row = lambda width: pl.BlockSpec((None, TM, width), lambda b, i: (b, i, 0))
    drow = pl.BlockSpec((2, None, TM, B_WIDTH), lambda b, i: (0, b, i, 0))
    full = lambda a: pl.BlockSpec(a.shape, lambda b, i: (0,) * a.ndim)
    one = jax.ShapeDtypeStruct((bn, s, B_WIDTH), F32)
    two = jax.ShapeDtypeStruct((2, bn, s, B_WIDTH), F32)
    kern = functools.partial(_rwkv_prep_kernel, ctx_tiles=ctx_len // TM, n_tiles=n_tiles)
    params = (mu, w0, w2bd, a0, a2bd, g2, k_k, k_a, r_k)
    return pl.pallas_call(
        kern,
        grid=(bn, n_tiles),
        in_specs=[row(N_B),
                  pl.BlockSpec((None, 8, N_B), lambda b, i: (b, jnp.maximum(i * eight - 1, 0), 0)),
                  pl.BlockSpec((None, 8, N_B), lambda b, i: (b, jnp.minimum((i + 1) * eight, s // 8 - 1), 0)),
                  ] + [full(p) for p in params],
        out_specs=[row(B_WIDTH), row(B_WIDTH), row(B_WIDTH), drow, drow, drow, row(B_WIDTH), row(B_WIDTH)],
        out_shape=[one, one, one, two, two, two, one, one],
        compiler_params=_params(("parallel", "parallel")),
        name="rwkv_prep",
    )(pb, pb, pb, *params)


def _pair_diag(x):
    lo = lax.broadcasted_iota(jnp.int32, (1, LANES), 1) < B_HEAD
    z = jnp.zeros_like(x)
    return jnp.concatenate([jnp.where(lo, x, z), jnp.where(lo, z, x)], axis=0)


def _pair_pick(x):
    lo = lax.broadcasted_iota(jnp.int32, (1, LANES), 1) < B_HEAD
    return jnp.where(lo, x[:B_HEAD], x[B_HEAD:])


def _rwkv_pair_kernel(rf_ref, vf_ref, kkf_ref, rb_ref, vb_ref, kkb_ref,
                      ldf_ref, kdf_ref, betaf_ref, ldb_ref, kdb_ref, betab_ref,
                      yf_ref, yb_ref, h_ref):
    c = pl.program_id(1)

    @pl.when(c == 0)
    def _():
        h_ref[...] = jnp.zeros_like(h_ref)

    n_pairs = B_WIDTH // LANES
    ti = lax.broadcasted_iota(jnp.int32, (CHUNK, LANES), 0)
    si = lax.broadcasted_iota(jnp.int32, (CHUNK, LANES), 1) % CHUNK
    t64 = lax.broadcasted_iota(jnp.int32, (CHUNK, CHUNK), 0)
    s64 = lax.broadcasted_iota(jnp.int32, (CHUNK, CHUNK), 1)
    eye = ti == si
    dirs = ((rf_ref, vf_ref, kkf_ref, ldf_ref, kdf_ref, betaf_ref, False),
            (rb_ref, vb_ref, kkb_ref, ldb_ref, kdb_ref, betab_ref, True))

    units = []
    n_rows = rf_ref.shape[0]
    for bi in range(n_rows):
      for d, (r_ref, v_ref, kk_ref, ld_ref, kd_ref, beta_ref, rev) in enumerate(dirs):
        strict = (si > ti) if rev else (ti > si)
        incl = (si >= ti) if rev else (ti >= si)
        tri = ((s64 >= t64) if rev else (t64 >= s64)).astype(BF16)
        ld_all = ld_ref[bi]
        cl_all = _dot_exact_lhs(tri, ld_all)
        for p in range(n_pairs):
            sl = slice(p * LANES, (p + 1) * LANES)
            ld, cl = ld_all[:, sl], cl_all[:, sl]
            total = jnp.sum(ld, axis=0, keepdims=True)
            inv_gam = jnp.exp(-cl)
            to_end = jnp.exp(total - cl)
            kk, kd, beta = kk_ref[bi, :, sl], kd_ref[bi, :, sl], beta_ref[bi, :, sl]
            units.append(dict(
                bi=bi, d=d, p=p, sl=sl, strict=strict, incl=incl, v=v_ref[bi, :, sl].astype(BF16),
                a_bar=-kk * jnp.exp(cl - ld), r_bar=r_ref[bi, :, sl] * jnp.exp(cl),
                b_til=(beta * inv_gam).astype(BF16), k_til=(kd * inv_gam).astype(BF16),
                b_hat=(beta * to_end).astype(BF16), k_hat=(kd * to_end).astype(BF16),
                gam_c=jnp.exp(total)))

    for u in units:
        x_mat = jnp.concatenate([u["a_bar"], u["r_bar"]], axis=0).astype(BF16)
        rhs = jnp.concatenate([_pair_diag(u["b_til"]), _pair_diag(u["k_til"])], axis=0)
        xbk = _dot(x_mat, rhs, NT)
        u["n_ab"] = jnp.where(u["strict"], xbk[:CHUNK, :LANES], 0.0)
        u["l_rb"] = jnp.where(u["incl"], xbk[CHUNK:, :LANES], 0.0)
        n_ak = jnp.where(u["strict"], xbk[:CHUNK, LANES:], 0.0)
        l_rk = jnp.where(u["incl"], xbk[CHUNK:, LANES:], 0.0)
        u["nl"] = jnp.concatenate([n_ak, l_rk], axis=0).astype(BF16)
    for u in units:
        nv = _dot(u["nl"], _pair_diag(u["v"]))
        u["w"], u["u0"], u["lrkv"] = u["a_bar"], nv[:CHUNK], nv[CHUNK:]
        u["npow"] = u["n_ab"].astype(BF16)

    steps = int(math.log2(CHUNK))
    for kstep in range(steps):
        for u in units:
            rhs = jnp.concatenate([_pair_diag(u["w"].astype(BF16)), _pair_diag(u["u0"].astype(BF16))], axis=1)
            upd = _dot(u["npow"], rhs)
            u["w"] = u["w"] + upd[:, :LANES]
            u["u0"] = u["u0"] + upd[:, LANES:]
        if kstep + 1 < steps:
            for u in units:
                u["npow"] = _dot(u["npow"], _pair_diag(u["npow"])).astype(BF16)

    for u in units:
        wb, ub = u["w"].astype(BF16), u["u0"].astype(BF16)
        lx = _dot(u["l_rb"].astype(BF16), jnp.concatenate([_pair_diag(wb), _pair_diag(ub)], axis=1))
        u["p_mat"] = u["r_bar"] + lx[:, :LANES]
        u["y0"] = u["lrkv"] + lx[:, LANES:]
        lhs = jnp.concatenate([u["b_hat"], u["k_hat"]], axis=0)
        rhs = jnp.concatenate([jnp.concatenate([wb, ub], axis=1),
                               jnp.concatenate([jnp.zeros_like(wb), u["v"]], axis=1)], axis=0)
        mg = _dot(lhs, rhs, TN)
        u["m_full"] = _pair_pick(mg[:, :LANES]) + jnp.where(eye, u["gam_c"], 0.0)
        u["g_mat"] = _pair_pick(mg[:, LANES:])

    for u in units:
        bi, d, p = u["bi"], u["d"], u["p"]
        h0 = h_ref[bi, d, p]
        a0, a1 = _split(jnp.concatenate([u["p_mat"], u["m_full"]], axis=0), 2)
        h_hi, h_lo = _split(h0, 2)
        bh, bl = _pair_diag(h_hi), _pair_diag(h_lo)
        out = _dot(a0, bh) + (_dot(a0, bl) + _dot(a1, bh))
        (yb_ref if d else yf_ref)[bi, :, u["sl"]] = out[:CHUNK] + u["y0"]
        h_ref[bi, d, p] = out[CHUNK:] + u["g_mat"]


def _rwkv_pairs(r, v, kk, ld, kd, beta, ctx_len):
    bn, s, _ = r.shape
    n_chunks = s // CHUNK
    ctx_chunks = ctx_len // CHUNK
    rows = RWKV_ROWS if bn % RWKV_ROWS == 0 else 1

    def back(c):
        return jnp.where(c < ctx_chunks, ctx_chunks - 1 - c, n_chunks - 1 + ctx_chunks - c)

    fwd = pl.BlockSpec((rows, CHUNK, B_WIDTH), lambda b, c: (b, c, 0))
    bwd = pl.BlockSpec((rows, CHUNK, B_WIDTH), lambda b, c: (b, back(c), 0))
    fwd_d = pl.BlockSpec((None, rows, CHUNK, B_WIDTH), lambda b, c: (0, b, c, 0))
    bwd_d = pl.BlockSpec((None, rows, CHUNK, B_WIDTH), lambda b, c: (1, b, back(c), 0))
    y = jax.ShapeDtypeStruct((bn, s, B_WIDTH), F32)
    return pl.pallas_call(
        _rwkv_pair_kernel,
        grid=(bn // rows, n_chunks),
        in_specs=[fwd, fwd, fwd, bwd, bwd, bwd, fwd_d, fwd_d, fwd_d, bwd_d, bwd_d, bwd_d],
        out_specs=[fwd, bwd],
        out_shape=[y, y],
        scratch_shapes=[pltpu.VMEM((rows, 2, B_WIDTH // LANES, B_HEAD, LANES), F32)],
        compiler_params=_params(("parallel", "arbitrary")),
        name="rwkv_scan",
    )(r, v, kk, r, v, kk, ld, kd, beta, ld, kd, beta)


def _outproj_kernel(x_ref, mod_ref, a_ref, yf_ref, yb_ref, bonus_ref, g_ref, c_ref,
                    wa_ref, wb_ref, wc_ref, lnxg_ref, lnxb_ref, ln1g_ref, ln1b_ref, o_ref, *, alpha):
    mod = mod_ref[...]
    y = yf_ref[...] + yb_ref[...] + bonus_ref[...]
    ones = _group_ones(B_WIDTH, B_HEAD)
    inv = 1.0 / B_HEAD
    mu = _dot_exact_rhs(y, ones) * inv
    yc = y - mu
    var = _dot_exact_rhs(yc * yc, ones) * inv
    yn = yc * lax.rsqrt(var + GN_EPS) * lnxg_ref[...] + lnxb_ref[...]
    bmix = (yn * g_ref[...]).astype(BF16)
    o = _dot(a_ref[...], wa_ref[...]) + _dot(bmix, wb_ref[...]) + _dot(c_ref[...], wc_ref[...])
    o_ref[...] = _ln(alpha * x_ref[...] + mod[2:3] * o, LN_EPS) * ln1g_ref[...] + ln1b_ref[...]


def _outproj(xs, mods, a_out, yf, yb, bonus, g, c_out, wa, wb, wc, lnxg, lnxb, ln1g, ln1b, ctx_len, alpha):
    bn, s, d = xs.shape
    row = lambda width: pl.BlockSpec((None, TM, width), lambda b, i: (b, i, 0))
    full = lambda a: pl.BlockSpec(a.shape, lambda b, i: (0,) * a.ndim)
    consts = (wa, wb, wc, lnxg, lnxb, ln1g, ln1b)
    return pl.pallas_call(
        functools.partial(_outproj_kernel, alpha=alpha),
        grid=(bn, s // TM),
        in_specs=[row(d), _mod_spec(ctx_len, d),
                  row(A_WIDTH), row(B_WIDTH), row(B_WIDTH), row(B_WIDTH), row(B_WIDTH), row(C_WIDTH)]
                 + [full(p) for p in consts],
        out_specs=row(d),
        out_shape=jax.ShapeDtypeStruct((bn, s, d), F32),
        compiler_params=_params(("parallel", "parallel")),
        name="outproj",
    )(xs, mods, a_out, yf, yb, bonus, g, c_out, *consts)


def _swiglu_rows(h, w1_ref, w3_ref, w2_ref):
    acc = jnp.zeros((h.shape[0], w2_ref.shape[-1]), F32)
    for j in range(w1_ref.shape[-1] // FF_CHUNK):
        sl = slice(j * FF_CHUNK, (j + 1) * FF_CHUNK)
        u = _dot(h, w1_ref[:, sl])
        t = _dot(h, w3_ref[:, sl])
        acc = acc + _dot((_silu(u) * t).astype(BF16), w2_ref[sl, :])
    return acc


def _ffn_kernel(x_ref, mod_ref, w1_ref, w3_ref, w2_ref, g_ref, b_ref, o_ref, *, alpha):
    mod = mod_ref[...]
    x = x_ref[...]
    h = (_ln(x, LN_EPS) * (1.0 + mod[4:5]) + mod[3:4]).astype(BF16)
    f = _swiglu_rows(h, w1_ref, w3_ref, w2_ref)
    o_ref[...] = _ln(alpha * x + mod[5:6] * f, LN_EPS) * g_ref[...] + b_ref[...]


def _ffn(x1, mods, w1, w3, w2, g, b, ctx_len, alpha):
    bn, s, d = x1.shape
    row = pl.BlockSpec((None, TM, d), lambda bb, i: (bb, i, 0))
    resident = lambda a: pl.BlockSpec(a.shape, lambda bb, i: (0,) * a.ndim, pipeline_mode=pl.Buffered(1))
    full = lambda a: pl.BlockSpec(a.shape, lambda bb, i: (0,) * a.ndim)
    return pl.pallas_call(
        functools.partial(_ffn_kernel, alpha=alpha),
        grid=(bn, s // TM),
        in_specs=[row, _mod_spec(ctx_len, d),
                  resident(w1), resident(w3), resident(w2), full(g), full(b)],
        out_specs=row,
        out_shape=jax.ShapeDtypeStruct((bn, s, d), F32),
        compiler_params=_params(("parallel", "parallel")),
        name="ffn",
    )(x1, mods, w1, w3, w2, g, b)


def _moe_pre_kernel(x_ref, mod_ref, router_ref, h_ref, logit_ref):
    mod = mod_ref[...]
    h = _ln(x_ref[...], LN_EPS) * (1.0 + mod[4:5]) + mod[3:4]
    h_ref[...] = h.astype(h_ref.dtype)
    logit_ref[...] = _dot6(h, router_ref[...])


def _moe_pre(x1, mods, router_p, ctx_len):
    bn, s, d = x1.shape
    ct = ctx_len // TM
    lt = (s - ctx_len) // TM
    return pl.pallas_call(
        _moe_pre_kernel,
        grid=(bn, lt),
        in_specs=[pl.BlockSpec((None, TM, d), lambda b, i: (b, i + ct, 0)),
                  pl.BlockSpec((None, None, 6, d), lambda b, i: (b, 1, 0, 0)),
                  pl.BlockSpec(router_p.shape, lambda b, i: (0, 0))],
        out_specs=[pl.BlockSpec((TM, d), lambda b, i: (b * lt + i, 0)),
                   pl.BlockSpec((TM, LANES), lambda b, i: (b * lt + i, 0))],
        out_shape=[jax.ShapeDtypeStruct((bn * lt * TM, d), BF16),
                   jax.ShapeDtypeStruct((bn * lt * TM, LANES), F32)],
        compiler_params=_params(("parallel", "parallel")),
        name="moe_pre",
    )(x1, mods, router_p)


def _slot_onehot(slots_ref, block):
    sl = slots_ref[...]
    s_iota = lax.broadcasted_iota(jnp.int32, (MOE_ROWS, TM), 0) + block * MOE_ROWS
    hit = jnp.logical_or(sl[0:1] == s_iota, sl[1:2] == s_iota)
    return jnp.where(hit, 1.0, 0.0).astype(BF16)


def _dispatch_kernel(wb_ref, wc_ref, wf_ref, wv_ref, slots_ref, h_ref, o_ref):
    w = pl.program_id(0)
    part = lambda: _dot(_slot_onehot(slots_ref, wb_ref[w]), h_ref[...])

    @pl.when(wf_ref[w] == 1)
    def _():
        o_ref[...] = part().astype(o_ref.dtype)

    @pl.when(jnp.logical_and(wf_ref[w] == 0, wv_ref[w] == 1))
    def _():
        o_ref[...] = (o_ref[...].astype(F32) + part()).astype(o_ref.dtype)


def _dispatch(h, slots, work):
    t, d = h.shape
    wb, wc, wf, wv = work
    n_slots = (t * TOP_K // MOE_ROWS + N_EXPERTS) * MOE_ROWS
    grid_spec = pltpu.PrefetchScalarGridSpec(
        num_scalar_prefetch=4,
        grid=(wb.shape[0],),
        in_specs=[pl.BlockSpec((TOP_K, TM), lambda w, b, c, f, v: (0, c[w])),
                  pl.BlockSpec((TM, d), lambda w, b, c, f, v: (c[w], 0))],
        out_specs=pl.BlockSpec((MOE_ROWS, d), lambda w, b, c, f, v: (b[w], 0)),
    )
    return pl.pallas_call(
        _dispatch_kernel,
        grid_spec=grid_spec,
        out_shape=jax.ShapeDtypeStruct((n_slots, d), BF16),
        compiler_params=_params(("arbitrary",)),
        name="moe_dispatch",
    )(wb, wc, wf, wv, slots, h)


def _collect_kernel(wb_ref, wc_ref, wf_ref, wv_ref, slots_ref, y_ref, o_ref):
    w = pl.program_id(0)
    part = lambda: _dot(_slot_onehot(slots_ref, wb_ref[w]), y_ref[...], TN)

    @pl.when(wf_ref[w] == 1)
    def _():
        o_ref[...] = part()

    @pl.when(jnp.logical_and(wf_ref[w] == 0, wv_ref[w] == 1))
    def _():
        o_ref[...] = o_ref[...] + part()


def _collect(y_buf, slots, work):
    n_slots, d = y_buf.shape
    t = slots.shape[1]
    wb, wc, wf, wv = work
    grid_spec = pltpu.PrefetchScalarGridSpec(
        num_scalar_prefetch=4,
        grid=(wb.shape[0],),
        in_specs=[pl.BlockSpec((TOP_K, TM), lambda w, b, c, f, v: (0, c[w])),
                  pl.BlockSpec((MOE_ROWS, d), lambda w, b, c, f, v: (b[w], 0))],
        out_specs=pl.BlockSpec((TM, d), lambda w, b, c, f, v: (c[w], 0)),
    )
    return pl.pallas_call(
        _collect_kernel,
        grid_spec=grid_spec,
        out_shape=jax.ShapeDtypeStruct((t, d), F32),
        compiler_params=_params(("arbitrary",)),
        name="moe_collect",
    )(wb, wc, wf, wv, slots, y_buf)


def _expert_kernel(be_ref, nb_ref, x_ref, gate_ref, w1_ref, w3_ref, w2_ref, o_ref):
    i = pl.program_id(0)

    @pl.when(i < nb_ref[0])
    def _():
        y = _swiglu_rows(x_ref[...], w1_ref, w3_ref, w2_ref)
        o_ref[...] = (y * gate_ref[...]).astype(o_ref.dtype)

    @pl.when(i >= nb_ref[0])
    def _():
        o_ref[...] = jnp.zeros_like(o_ref)


def _experts(buf, gate_slot, block_e, n_used, w1, w3, w2):
    n, d = buf.shape
    n_blocks = n // MOE_ROWS
    ff = w1.shape[-1]
    wspec = lambda shape: pl.BlockSpec((None,) + shape, lambda i, be, nb: (be[i], 0, 0),
                                       pipeline_mode=pl.Buffered(1))
    used = lambda i, be, nb: (jnp.minimum(i, nb[0] - 1), 0)
    grid_spec = pltpu.PrefetchScalarGridSpec(
        num_scalar_prefetch=2,
        grid=(n_blocks,),
        in_specs=[pl.BlockSpec((MOE_ROWS, d), used), pl.BlockSpec((MOE_ROWS, 1), used),
                  wspec((d, ff)), wspec((d, ff)), wspec((ff, d))],
        out_specs=pl.BlockSpec((MOE_ROWS, d), lambda i, be, nb: (i, 0)),
    )
    return pl.pallas_call(
        _expert_kernel,
        grid_spec=grid_spec,
        out_shape=jax.ShapeDtypeStruct((n, d), BF16),
        compiler_params=_params(("arbitrary",)),
        name="experts",
    )(block_e, n_used, buf, gate_slot, w1, w3, w2)


def _combine_kernel(x_ref, mod_ref, y_ref, g_ref, b_ref, o_ref, *, alpha):
    mod = mod_ref[...]
    o_ref[...] = _ln(alpha * x_ref[...] + mod[5:6] * y_ref[...], LN_EPS) * g_ref[...] + b_ref[...]


def _combine(x1, mods, y_tok, g, b, ctx_len, alpha):
    bn, s, d = x1.shape
    ct = ctx_len // TM
    lt = (s - ctx_len) // TM
    full = lambda a: pl.BlockSpec(a.shape, lambda bb, i: (0,) * a.ndim)
    return pl.pallas_call(
        functools.partial(_combine_kernel, alpha=alpha),
        grid=(bn, lt),
        in_specs=[pl.BlockSpec((None, TM, d), lambda bb, i: (bb, i + ct, 0)),
                  pl.BlockSpec((None, None, 6, d), lambda bb, i: (bb, 1, 0, 0)),
                  pl.BlockSpec((TM, d), lambda bb, i: (bb * lt + i, 0)),
                  full(g), full(b)],
        out_specs=pl.BlockSpec((None, TM, d), lambda bb, i: (bb, i, 0)),
        out_shape=jax.ShapeDtypeStruct((bn, lt * TM, d), F32),
        compiler_params=_params(("parallel", "parallel")),
        name="moe_combine",
    )(x1, mods, y_tok, g, b)


def _work_lists(rank_at_tile, counts, pstart, block_e, n_used, n_blocks, n_tiles):
    n_work = n_tiles * N_EXPERTS + n_blocks
    blocks = jnp.arange(n_blocks, dtype=jnp.int32)
    used = blocks < n_used
    r0 = blocks * MOE_ROWS - pstart[block_e]
    r_last = jnp.minimum(r0 + MOE_ROWS, counts[block_e]) - 1
    cols = rank_at_tile.T[block_e]
    find = jax.vmap(lambda col, val: jnp.searchsorted(col, val, side="right"))
    lo = jnp.clip(find(cols, r0) - 1, 0, n_tiles - 1)
    hi = jnp.clip(find(cols, r_last) - 1, 0, n_tiles - 1)
    n_b = jnp.where(used, hi - lo + 1, 0)
    ends = jnp.cumsum(n_b)
    starts = ends - n_b
    total = ends[-1]
    w = jnp.arange(n_work, dtype=jnp.int32)
    valid = w < total
    wl = jnp.minimum(w, total - 1)
    blk = jnp.minimum(jnp.searchsorted(ends, wl, side="right"), n_blocks - 1).astype(jnp.int32)
    tile = (lo[blk] + (wl - starts[blk])).astype(jnp.int32)
    first = jnp.logical_and(valid, w == starts[blk])
    as_i32 = lambda a: a.astype(jnp.int32)
    by_block = (blk, tile, as_i32(first), as_i32(valid))
    order = jnp.argsort(jnp.where(valid, tile, n_tiles), stable=True)
    order = order[jnp.minimum(w, total - 1)]
    tile2, blk2 = tile[order], blk[order]
    first2 = jnp.logical_and(valid, jnp.concatenate([jnp.ones((1,), bool), tile2[1:] != tile2[:-1]]))
    by_tile = (blk2, tile2, as_i32(first2), as_i32(valid))
    return by_block, by_tile


def _moe_layer(x1, mods, router, w1, w3, w2, g, b, ctx_len, alpha):
    d = x1.shape[-1]
    router_p = jnp.pad(router, ((0, 0), (0, LANES - N_EXPERTS)))
    h, logits = _moe_pre(x1, mods, router_p, ctx_len)
    t = h.shape[0]
    top_v, top_i = lax.top_k(logits[:, :N_EXPERTS], TOP_K)
    gates = jax.nn.softmax(top_v, axis=-1)
    e_flat = top_i.reshape(-1)
    onehot = (e_flat[:, None] == jnp.arange(N_EXPERTS)[None, :]).astype(jnp.int32)
    ranks = jnp.cumsum(onehot, axis=0) - onehot
    rank = jnp.sum(ranks * onehot, axis=1)
    counts = jnp.sum(onehot, axis=0)
    padded = (counts + MOE_ROWS - 1) // MOE_ROWS * MOE_ROWS
    pend = jnp.cumsum(padded)
    pstart = pend - padded
    slot = (pstart[e_flat] + rank).astype(jnp.int32)
    n_blocks = t * TOP_K // MOE_ROWS + N_EXPERTS
    n_tiles = t // TM
    block_e = jnp.minimum(jnp.searchsorted(pend, jnp.arange(n_blocks) * MOE_ROWS, side="right"),
                          N_EXPERTS - 1).astype(jnp.int32)
    n_used = (pend[-1:] // MOE_ROWS).astype(jnp.int32)
    rank_at_tile = jnp.concatenate([ranks[::TM * TOP_K], counts[None, :]], axis=0)
    by_block, by_tile = _work_lists(rank_at_tile, counts, pstart, block_e, n_used, n_blocks, n_tiles)
    slots = slot.reshape(t, TOP_K).T
    gate_slot = jnp.zeros((n_blocks * MOE_ROWS, 1), F32).at[slot, 0].set(gates.reshape(-1))
    buf = _dispatch(h, slots, by_block)
    y_buf = _experts(buf, gate_slot, block_e, n_used, w1, w3, w2)
    y_tok = _collect(y_buf, slots, by_tile)
    return _combine(x1, mods, y_tok, g, b, ctx_len, alpha)


def _rope_tables(n_rows, ctx_len, dim, lane_lo):
    quarter = dim // 4
    inv = ROPE_THETA ** (-jnp.arange(quarter, dtype=F32) / quarter)
    rows = jnp.repeat(jnp.arange(n_rows, dtype=F32), GRID_W)
    cols = jnp.tile(jnp.arange(GRID_W, dtype=F32), n_rows)
    ang = jnp.concatenate([rows[:, None] * inv, rows[:, None] * inv,
                           cols[:, None] * inv, cols[:, None] * inv], axis=-1)
    sign = jnp.tile(jnp.concatenate([-jnp.ones(quarter, F32), jnp.ones(quarter, F32)]), 2)
    cos, sin = jnp.cos(ang), jnp.sin(ang) * sign
    length = cos.shape[0]
    if lane_lo == 0:
        reps = LANES // dim
        cos, sin = jnp.tile(cos, (1, reps)), jnp.tile(sin, (1, reps))
    else:
        pad = ((0, 0), (lane_lo, LANES - lane_lo - dim))
        cos = jnp.pad(cos, pad, constant_values=1.0)
        sin = jnp.pad(sin, pad)
    cos = jnp.concatenate([jnp.ones((ctx_len, LANES), F32), cos], axis=0)
    sin = jnp.concatenate([jnp.zeros((ctx_len, LANES), F32), sin], axis=0)
    return cos, sin


def _block_diag2(w):
    z = jnp.zeros_like(w[0])
    return jnp.concatenate([jnp.concatenate([w[0], z], axis=1), jnp.concatenate([z, w[1]], axis=1)], axis=0)


def _lambda_init(layer):
    return 0.8 - 0.6 * math.exp(-0.3 * layer)


def kernel(x, c, ctx, c_ctx, ada_w, ada_b, w_in, w_out, ln1_g, ln1_b, ln2_g, ln2_b,
           lam_q1, lam_k1, lam_q2, lam_k2, diff_norm_g, shift_mu, w0, w2, a0, a2, g2,
           k_k, k_a, r_k, lnx_g, lnx_b, q_norm_g, w_uq, kv_norm_g, w_ukv,
           ff_w1, ff_w3, ff_w2, router, moe_w1, moe_w3, moe_w2):
    bn, seq, d = x.shape
    ctx_len = ctx.shape[1]
    depth = ada_w.shape[0]
    assert d == D_MODEL and seq % TM == 0 and ctx_len % TM == 0 and seq % GRID_W == 0
    alpha = (2.0 * depth) ** 0.25
    n_grid_rows = seq // GRID_W
    cos_a, sin_a = _rope_tables(n_grid_rows, ctx_len, A_QK_DIM, 0)
    cos_c, sin_c = _rope_tables(n_grid_rows, ctx_len, C_ROPE, C_NOPE)

    cond_rows = 8 * ((bn + 1 + 7) // 8)
    cond = jnp.zeros((cond_rows, d), F32).at[:bn].set(c).at[bn].set(c_ctx)
    xs = jnp.concatenate([ctx, x], axis=1)

    for i in range(depth):
        with_ctx = i < depth - 1
        m = _ada(cond, ada_w[i], ada_b[i]).reshape(cond_rows, 6, d)
        mods = jnp.stack([jnp.broadcast_to(m[bn], (bn, 6, d)), m[:bn]], axis=1)

        wc = w_in[i][:, N_A + N_B:]
        kpe_w = jnp.pad(wc[:, C_Q_RANK + C_KV_RANK:], ((0, 0), (C_NOPE, LANES - C_NOPE - C_ROPE)))
        w_p = jnp.concatenate([w_in[i][:, :N_A + N_B], wc[:, :C_Q_RANK + C_KV_RANK], kpe_w], axis=1).astype(BF16)
        qa, ka, va, pb, pc = _inproj(xs, mods, w_p, cos_a, sin_a, ctx_len)

        lam_p = jnp.stack([lam_q1[i], lam_k1[i], lam_q2[i], lam_k2[i]])
        g_col = jnp.broadcast_to(diff_norm_g[i][:, None], (A_V_DIM, TM))
        a_out = _attention(qa, ka, va, lam_p, g_col, mode="diff", ctx_len=ctx_len, lam_init=_lambda_init(i))

        wq = w_uq[i].reshape(C_Q_RANK, C_HEADS, C_NOPE + C_ROPE)
        wq_p = jnp.pad(wq, ((0, 0), (0, 0), (0, LANES - C_NOPE - C_ROPE))).reshape(C_Q_RANK, -1).astype(BF16)
        wkv = w_ukv[i].reshape(C_KV_RANK, C_HEADS, C_NOPE + C_V)
        wk_p = jnp.pad(wkv[:, :, :C_NOPE], ((0, 0), (0, 0), (0, LANES - C_NOPE))).reshape(C_KV_RANK, -1).astype(BF16)
        wv_p = wkv[:, :, C_NOPE:].reshape(C_KV_RANK, -1).astype(BF16)
        qc, kc, vc = _mla_prep(pc, q_norm_g[i].reshape(1, -1), kv_norm_g[i].reshape(1, -1),
                               wq_p, wk_p, wv_p, cos_c, sin_c)
        c_out = _attention(qc, kc, vc, lam_p, g_col, mode="mla", ctx_len=ctx_len)

        r, v, kkn, ld, kd, beta, gate, bonus = _rwkv_prep(
            pb, shift_mu[i].reshape(1, -1), w0[i].reshape(1, -1), _block_diag2(w2[i]).astype(BF16),
            a0[i].reshape(1, -1), _block_diag2(a2[i]).astype(BF16), g2[i].astype(BF16),
            k_k[i].reshape(1, -1), k_a[i].reshape(1, -1), r_k[i].reshape(1, -1), ctx_len)
        yf, yb = _rwkv_pairs(r, v, kkn, ld, kd, beta, ctx_len)

        wo = w_out[i].astype(BF16)
        x1 = _outproj(xs, mods, a_out, yf, yb, bonus, gate, c_out,
                      wo[:A_WIDTH], wo[A_WIDTH:A_WIDTH + B_WIDTH], wo[A_WIDTH + B_WIDTH:],
                      lnx_g[i].reshape(1, -1), lnx_b[i].reshape(1, -1),
                      ln1_g[i].reshape(1, -1), ln1_b[i].reshape(1, -1), ctx_len, alpha)

        j = i // 2
        g2n, b2n = ln2_g[i].reshape(1, -1), ln2_b[i].reshape(1, -1)
        if i % 2 == 0:
            xs = _ffn(x1, mods, ff_w1[j].astype(BF16), ff_w3[j].astype(BF16), ff_w2[j].astype(BF16),
                      g2n, b2n, ctx_len, alpha)
        else:
            if with_ctx:
                raise NotImplementedError("routed FFN on the context rows is not needed at this depth")
            return _moe_layer(x1, mods, router[j], moe_w1[j].astype(BF16), moe_w3[j].astype(BF16),
                              moe_w2[j].astype(BF16), g2n, b2n, ctx_len, alpha)
    return xs[:, ctx_len:]
```

```python
import functools
import math

import jax
import jax.numpy as jnp
from jax import lax
from jax.experimental import pallas as pl
from jax.experimental.pallas import tpu as pltpu

F32 = jnp.float32
BF16 = jnp.bfloat16

D_MODEL = 1024
GRID_W = 64
ROPE_THETA = 10000.0
A_HEADS, A_QK_DIM, A_V_DIM = 4, 64, 128
A_WIDTH = A_HEADS * A_V_DIM
A_QK_COLS = 2 * A_HEADS * A_QK_DIM
B_HEADS, B_HEAD = 4, 64
B_WIDTH = B_HEADS * B_HEAD
B_DECAY_RANK, B_A_RANK, B_GATE_RANK = 64, 64, 128
C_HEADS, C_NOPE, C_ROPE, C_V = 4, 64, 32, 64
C_WIDTH = C_HEADS * C_V
C_Q_RANK, C_KV_RANK = 256, 128
N_A = 2 * A_QK_COLS + A_WIDTH
N_B = 3 * B_WIDTH + 2 * B_DECAY_RANK + 2 * B_A_RANK + B_GATE_RANK
N_C = C_Q_RANK + C_KV_RANK + C_ROPE
N_C_PAD = C_Q_RANK + C_KV_RANK + 128
D_FF = 3584
N_EXPERTS = 8
TOP_K = 2
LN_EPS = 1e-6
RMS_EPS = 1e-6
GN_EPS = 64e-5

LANES = 128
TM = 256
CHUNK = 64
RWKV_ROWS = 2
ATT_TK = 512
ATT_PAR = 4
MOE_ROWS = 512
MOE_TILE = 512
FF_CHUNK = 512
VMEM_LIMIT = 56 * 1024 * 1024

NN = (((1,), (0,)), ((), ()))
NT = (((1,), (1,)), ((), ()))
TN = (((0,), (0,)), ((), ()))


def _params(sem, vmem=VMEM_LIMIT, flags=None):
    return pltpu.CompilerParams(dimension_semantics=sem, vmem_limit_bytes=vmem, flags=flags)


def _split(x, n):
    parts, r = [], x
    for _ in range(n):
        p = r.astype(BF16)
        parts.append(p)
        r = r - p.astype(F32)
    return parts


def _dot(a, b, dn=NN):
    return lax.dot_general(a, b, dn, preferred_element_type=F32)


def _dot1(a, b, dn=NN):
    return _dot(a.astype(BF16), b.astype(BF16), dn)


def _dot3(a, b, dn=NN):
    a0, a1 = _split(a, 2)
    b0, b1 = _split(b, 2)
    return _dot(a0, b0, dn) + (_dot(a0, b1, dn) + _dot(a1, b0, dn))


def _dot6(a, b, dn=NN):
    a0, a1, a2 = _split(a, 3)
    b0, b1, b2 = _split(b, 3)
    lo = _dot(a1, b1, dn) + (_dot(a0, b2, dn) + _dot(a2, b0, dn))
    return _dot(a0, b0, dn) + ((_dot(a0, b1, dn) + _dot(a1, b0, dn)) + lo)


def _dot_exact_lhs(a_bf16, b, dn=NN, n=3):
    parts = _split(b, n)
    out = _dot(a_bf16, parts[-1], dn)
    for p in parts[-2::-1]:
        out = out + _dot(a_bf16, p, dn)
    return out


def _dot_exact_rhs(a, b_bf16, dn=NN, n=3):
    parts = _split(a, n)
    out = _dot(parts[-1], b_bf16, dn)
    for p in parts[-2::-1]:
        out = out + _dot(p, b_bf16, dn)
    return out


def _ln(x, eps):
    mu = jnp.mean(x, axis=-1, keepdims=True)
    xc = x - mu
    return xc * lax.rsqrt(jnp.mean(xc * xc, axis=-1, keepdims=True) + eps)


def _sigmoid(x):
    return 1.0 / (1.0 + jnp.exp(-x))


def _silu(x):
    return x * _sigmoid(x)


def _group_ones(width, group):
    r = lax.broadcasted_iota(jnp.int32, (width, width), 0) // group
    c = lax.broadcasted_iota(jnp.int32, (width, width), 1) // group
    return (r == c).astype(BF16)


def _partner(x, half):
    lane = lax.broadcasted_iota(jnp.int32, x.shape, 1)
    up = pltpu.roll(x, LANES - half, 1)
    dn = pltpu.roll(x, half, 1)
    return jnp.where((lane % (2 * half)) < half, up, dn)


def _rope(x, cos, sin, half):
    return x * cos + _partner(x, half) * sin


def _ada_kernel(c_ref, w_ref, b_ref, o_ref):
    o_ref[...] = _dot3(_silu(c_ref[...]), w_ref[...]) + b_ref[...]


def _ada(cond, w, b):
    rows, d = cond.shape
    n = w.shape[1]
    tn = 1536
    return pl.pallas_call(
        _ada_kernel,
        grid=(n // tn,),
        in_specs=[pl.BlockSpec((rows, d), lambda j: (0, 0)),
                  pl.BlockSpec((d, tn), lambda j: (0, j)),
                  pl.BlockSpec((1, tn), lambda j: (0, j))],
        out_specs=pl.BlockSpec((rows, tn), lambda j: (0, j)),
        out_shape=jax.ShapeDtypeStruct((rows, n), F32),
        compiler_params=_params(("parallel",)),
        name="ada",
    )(cond, w, b.reshape(1, n))


def _inproj_kernel(x_ref, mod_ref, w_ref, cos_ref, sin_ref,
                   q_ref, k_ref, v_ref, pb_ref, pc_ref):
    mod = mod_ref[...]
    h = (_ln(x_ref[...], LN_EPS) * (1.0 + mod[1:2]) + mod[0:1]).astype(BF16)
    cos, sin = cos_ref[...], sin_ref[...]
    scale = A_QK_DIM ** -0.5
    for j in range(A_QK_COLS // LANES):
        sl = slice(j * LANES, (j + 1) * LANES)
        qj = _dot(h, w_ref[:, sl])
        q_ref[sl, :] = (_rope(qj, cos, sin, A_QK_DIM // 4) * scale).T.astype(BF16)
        kj = _dot(h, w_ref[:, A_QK_COLS + j * LANES:A_QK_COLS + (j + 1) * LANES])
        k_ref[:, sl] = _rope(kj, cos, sin, A_QK_DIM // 4).astype(BF16)
        vj = _dot(h, w_ref[:, 2 * A_QK_COLS + j * LANES:2 * A_QK_COLS + (j + 1) * LANES])
        v_ref[sl, :] = vj.T.astype(BF16)
    pb_ref[...] = _dot(h, w_ref[:, N_A:N_A + N_B])
    pc_ref[...] = _dot(h, w_ref[:, N_A + N_B:])


def _mod_spec(ctx_len, d):
    ct = ctx_len // TM
    return pl.BlockSpec((None, None, 6, d), lambda b, i: (b, jnp.where(i >= ct, 1, 0), 0, 0))


def _inproj(xs, mods, w_p, cos_a, sin_a, ctx_len):
    bn, s, d = xs.shape
    n_tiles = s // TM
    n_w = w_p.shape[1]
    row = lambda width: pl.BlockSpec((None, TM, width), lambda b, i: (b, i, 0))
    col = lambda width: pl.BlockSpec((None, width, TM), lambda b, i: (b, 0, i))
    outs = [jax.ShapeDtypeStruct((bn, A_QK_COLS, s), BF16),
            jax.ShapeDtypeStruct((bn, s, A_QK_COLS), BF16),
            jax.ShapeDtypeStruct((bn, A_WIDTH, s), BF16),
            jax.ShapeDtypeStruct((bn, s, N_B), F32),
            jax.ShapeDtypeStruct((bn, s, N_C_PAD), F32)]
    return pl.pallas_call(
        _inproj_kernel,
        grid=(bn, n_tiles),
        in_specs=[row(d),
                  _mod_spec(ctx_len, d),
                  pl.BlockSpec((d, n_w), lambda b, i: (0, 0)),
                  pl.BlockSpec((TM, LANES), lambda b, i: (i, 0)),
                  pl.BlockSpec((TM, LANES), lambda b, i: (i, 0))],
        out_specs=[col(A_QK_COLS), row(A_QK_COLS), col(A_WIDTH), row(N_B), row(N_C_PAD)],
        out_shape=outs,
        compiler_params=_params(("parallel", "parallel")),
        name="inproj",
    )(xs, mods, w_p, cos_a, sin_a)


def _mla_prep_kernel(pc_ref, qg_ref, kvg_ref, wq_ref, wk_ref, wv_ref, cos_ref, sin_ref,
                     q_ref, k_ref, v_ref):
    pc = pc_ref[...]
    cq = pc[:, :C_Q_RANK]
    cq = cq * lax.rsqrt(jnp.mean(cq * cq, axis=-1, keepdims=True) + RMS_EPS) * qg_ref[...]
    ckv = pc[:, C_Q_RANK:C_Q_RANK + C_KV_RANK]
    ckv = ckv * lax.rsqrt(jnp.mean(ckv * ckv, axis=-1, keepdims=True) + RMS_EPS) * kvg_ref[...]
    cos, sin = cos_ref[...], sin_ref[...]
    kpe = _rope(pc[:, C_Q_RANK + C_KV_RANK:], cos, sin, C_ROPE // 4)
    cqb, ckvb = cq.astype(BF16), ckv.astype(BF16)
    scale = (C_NOPE + C_ROPE) ** -0.5
    for h in range(C_HEADS):
        sl = slice(h * LANES, (h + 1) * LANES)
        qh = _dot(cqb, wq_ref[:, sl])
        q_ref[sl, :] = (_rope(qh, cos, sin, C_ROPE // 4) * scale).T.astype(BF16)
        k_ref[:, sl] = (_dot(ckvb, wk_ref[:, sl]) + kpe).astype(BF16)
    for j in range(C_WIDTH // LANES):
        sl = slice(j * LANES, (j + 1) * LANES)
        v_ref[sl, :] = _dot(ckvb, wv_ref[:, sl]).T.astype(BF16)


def _mla_prep(pc, qg, kvg, wq_p, wk_p, wv_p, cos_c, sin_c):
    bn, s, _ = pc.shape
    row = lambda width: pl.BlockSpec((None, TM, width), lambda b, i: (b, i, 0))
    full = lambda a: pl.BlockSpec(a.shape, lambda b, i: (0,) * a.ndim)
    tab = pl.BlockSpec((TM, LANES), lambda b, i: (i, 0))
    col = lambda width: pl.BlockSpec((None, width, TM), lambda b, i: (b, 0, i))
    hw = C_HEADS * LANES
    return pl.pallas_call(
        _mla_prep_kernel,
        grid=(bn, s // TM),
        in_specs=[row(N_C_PAD), full(qg), full(kvg), full(wq_p), full(wk_p), full(wv_p), tab, tab],
        out_specs=[col(hw), row(hw), col(C_WIDTH)],
        out_shape=[jax.ShapeDtypeStruct((bn, hw, s), BF16),
                   jax.ShapeDtypeStruct((bn, s, hw), BF16),
                   jax.ShapeDtypeStruct((bn, C_WIDTH, s), BF16)],
        compiler_params=_params(("parallel", "parallel")),
        name="mla_prep",
    )(pc, qg, kvg, wq_p, wk_p, wv_p, cos_c, sin_c)


def _attn_kernel(lam_ref, g_ref, qt_ref, k_ref, vt_ref, o_ref, *, mode, ctx_tiles, ctx_len, lam_init):
    qi = pl.program_id(2)
    tq = qt_ref.shape[1]
    n_par = qt_ref.shape[0] // LANES
    dv = vt_ref.shape[0] // n_par
    width = 2 * tq if mode == "diff" else tq
    half = lax.broadcasted_iota(jnp.int32, (LANES, 1), 0) < (LANES // 2)
    q_ops = []
    for g in range(n_par):
        qt = qt_ref[g * LANES:(g + 1) * LANES, :]
        if mode == "diff":
            qt = jnp.concatenate([jnp.where(half, qt, jnp.zeros_like(qt)),
                                  jnp.where(half, jnp.zeros_like(qt), qt)], axis=1)
        q_ops.append(qt)

    def fold(x, reduce):
        rows = x.shape[0]
        while rows > 8:
            g = min(8, rows // 8)
            rows //= g
            x = reduce(x.reshape(g, rows, x.shape[1]), axis=0)
        return x

    def scores(kc, g):
        return _dot(kc[:, g * LANES:(g + 1) * LANES], q_ops[g])

    def absorb(kc, vc, carry):
        ss = [scores(kc, g) for g in range(n_par)]
        m_new = [jnp.maximum(carry[g][0], jnp.max(fold(ss[g], jnp.max), axis=0, keepdims=True))
                 for g in range(n_par)]
        alpha = [jnp.exp(carry[g][0] - m_new[g]) for g in range(n_par)]
        ps = [jnp.exp(ss[g] - m_new[g]) for g in range(n_par)]
        l_new = [alpha[g] * carry[g][1] + jnp.sum(fold(ps[g], jnp.sum), axis=0, keepdims=True)
                 for g in range(n_par)]
        acc = [alpha[g] * carry[g][2] + _dot(vc[g * dv:(g + 1) * dv], ps[g].astype(BF16))
               for g in range(n_par)]
        return tuple((m_new[g], l_new[g], acc[g]) for g in range(n_par))

    n_latent_chunks = (k_ref.shape[0] - ctx_len) // ATT_TK

    def body(j, stats):
        off = pl.multiple_of(ctx_len + j * ATT_TK, LANES)
        return absorb(k_ref[pl.ds(off, ATT_TK), :], vt_ref[:, pl.ds(off, ATT_TK)], stats)

    neg = jnp.full((1, width), -1e30, F32)
    zero1 = jnp.zeros((1, width), F32)
    zacc = jnp.zeros((dv, width), F32)
    stats = absorb(k_ref[0:ctx_len, :], vt_ref[:, 0:ctx_len], ((neg, zero1, zacc),) * n_par)
    n_trips = jnp.where(qi < ctx_tiles, 0, n_latent_chunks)
    stats = lax.fori_loop(0, n_trips, body, stats)
    outs = [a / l for (_, l, a) in stats]
    if mode == "diff":
        lp = lam_ref[...]
        lam = (jnp.exp(jnp.sum(lp[0:1] * lp[1:2], axis=-1, keepdims=True))
               - jnp.exp(jnp.sum(lp[2:3] * lp[3:4], axis=-1, keepdims=True)) + lam_init)
        for g in range(n_par):
            o = outs[g][:, :tq] - lam * outs[g][:, tq:]
            o = o * lax.rsqrt(jnp.mean(o * o, axis=0, keepdims=True) + RMS_EPS) * g_ref[...]
            o = o * (1.0 - lam_init)
            o_ref[:, g * LANES:(g + 1) * LANES] = o.T.astype(o_ref.dtype)
    else:
        per_slab = LANES // dv
        for j in range(n_par // per_slab):
            o = jnp.concatenate(outs[j * per_slab:(j + 1) * per_slab], axis=0)
            o_ref[:, j * LANES:(j + 1) * LANES] = o.T.astype(o_ref.dtype)


def _attention(qt, k, vt, lam_p, g, *, mode, ctx_len, lam_init=0.0):
    bn, s, _ = k.shape
    heads = k.shape[2] // LANES
    dv = vt.shape[1] // heads
    par = math.gcd(heads, ATT_PAR)
    assert (s - ctx_len) % ATT_TK == 0 and ctx_len % LANES == 0 and (par * dv) % LANES == 0
    kern = functools.partial(_attn_kernel, mode=mode, ctx_tiles=ctx_len // TM, ctx_len=ctx_len,
                             lam_init=lam_init)
    return pl.pallas_call(
        kern,
        grid=(bn, heads // par, s // TM),
        in_specs=[pl.BlockSpec(lam_p.shape, lambda b, h, i: (0, 0)),
                  pl.BlockSpec(g.shape, lambda b, h, i: (0, 0)),
                  pl.BlockSpec((None, par * LANES, TM), lambda b, h, i: (b, h, i)),
                  pl.BlockSpec((None, s, par * LANES), lambda b, h, i: (b, 0, h)),
                  pl.BlockSpec((None, par * dv, s), lambda b, h, i: (b, h, 0))],
        out_specs=pl.BlockSpec((None, TM, par * dv), lambda b, h, i: (b, i, h)),
        out_shape=jax.ShapeDtypeStruct((bn, s, heads * dv), BF16),
        compiler_params=_params(("parallel", "parallel", "parallel")),
        name="attn_" + mode,
    )(lam_p, g, qt, k, vt)


def _rwkv_prep_kernel(pb_ref, prev_ref, next_ref, mu_ref, w0_ref, w2_ref, a0_ref, a2_ref, g2_ref,
                      kk_ref, ka_ref, rk_ref,
                      r_out, v_out, kkn_out, ld_out, kd_out, beta_out, g_out, bonus_out,
                      *, ctx_tiles, n_tiles):
    i = pl.program_id(1)
    x = pb_ref[...]
    row = lax.broadcasted_iota(jnp.int32, (TM, 1), 0)
    has_prev = jnp.logical_and(i != 0, i != ctx_tiles)
    has_next = jnp.logical_and(i != ctx_tiles - 1, i != n_tiles - 1)
    prev_edge = jnp.where(has_prev, prev_ref[7:8, :], 0.0)
    next_edge = jnp.where(has_next, next_ref[0:1, :], 0.0)
    xp = jnp.where(row == 0, prev_edge, pltpu.roll(x, 1, 0))
    xn = jnp.where(row == TM - 1, next_edge, pltpu.roll(x, TM - 1, 0))
    z = x + mu_ref[...] * (0.5 * (xp + xn) - x)

    r = z[:, :B_WIDTH]
    k = z[:, B_WIDTH:2 * B_WIDTH]
    v = z[:, 2 * B_WIDTH:3 * B_WIDTH]
    o = 3 * B_WIDTH
    wd = z[:, o:o + 2 * B_DECAY_RANK]
    o += 2 * B_DECAY_RANK
    ad = z[:, o:o + 2 * B_A_RANK]
    o += 2 * B_A_RANK
    gd = z[:, o:]

    u = w0_ref[...] + _dot1(jnp.tanh(wd), w2_ref[...])
    nu = -u
    w_raw = -(jnp.maximum(nu, 0.0) + jnp.log(1.0 + jnp.exp(-jnp.abs(nu)))) - 0.5
    ld = -jnp.exp(w_raw)
    lr = _sigmoid(a0_ref[...] + _dot1(ad, a2_ref[...]))
    g_out[...] = _dot1(_sigmoid(gd), g2_ref[...])

    ones = _group_ones(B_WIDTH, B_HEAD)
    kk = k * kk_ref[...]
    norm = jnp.sqrt(_dot_exact_rhs(kk * kk, ones))
    kkn = kk / jnp.maximum(norm, 1e-12)
    ka = ka_ref[...]
    kd_sum = jnp.zeros_like(k)
    for d in range(2):
        lr_d = lr[:, d * B_WIDTH:(d + 1) * B_WIDTH]
        kd = k * (1.0 + (lr_d - 1.0) * ka)
        kd_sum = kd_sum + kd
        ld_out[d] = ld[:, d * B_WIDTH:(d + 1) * B_WIDTH]
        kd_out[d] = kd
        beta_out[d] = kkn * lr_d
    bonus_out[...] = _dot_exact_rhs(r * kd_sum * rk_ref[...], ones) * v
    r_out[...] = r
    v_out[...] = v
    kkn_out[...] = kkn


def _rwkv_prep(pb, mu, w0, w2bd, a0, a2bd, g2, k_k, k_a, r_k, ctx_len):
    bn, s, _ = pb.shape
    n_tiles = s // TM
    eight = TM // 8
    row = lambda width: pl.BlockSpec((None, TM, width), lambda b, i: (b, i, 0))
    drow = pl.BlockSpec((2, None, TM, B_WIDTH), lambda b, i: (0, b, i, 0))
    full = lambda a: pl.BlockSpec(a.shape, lambda b, i: (0,) * a.ndim)
    one = jax.ShapeDtypeStruct((bn, s, B_WIDTH), F32)
    two = jax.ShapeDtypeStruct((2, bn, s, B_WIDTH), F32)
    kern = functools.partial(_rwkv_prep_kernel, ctx_tiles=ctx_len // TM, n_tiles=n_tiles)
    params = (mu, w0, w2bd, a0, a2bd, g2, k_k, k_a, r_k)
    return pl.pallas_call(
        kern,
        grid=(bn, n_tiles),
        in_specs=[row(N_B),
                  pl.BlockSpec((None, 8, N_B), lambda b, i: (b, jnp.maximum(i * eight - 1, 0), 0)),
                  pl.BlockSpec((None, 8, N_B), lambda b, i: (b, jnp.minimum((i + 1) * eight, s // 8 - 1), 0)),
                  ] + [full(p) for p in params],
        out_specs=[row(B_WIDTH), row(B_WIDTH), row(B_WIDTH), drow, drow, drow, row(B_WIDTH), row(B_WIDTH)],
        out_shape=[one, one, one, two, two, two, one, one],
        compiler_params=_params(("parallel", "parallel")),
        name="rwkv_prep",
    )(pb, pb, pb, *params)


def _pair_diag(x):
    lo = lax.broadcasted_iota(jnp.int32, (1, LANES), 1) < B_HEAD
    z = jnp.zeros_like(x)
    return jnp.concatenate([jnp.where(lo, x, z), jnp.where(lo, z, x)], axis=0)


def _pair_pick(x):
    lo = lax.broadcasted_iota(jnp.int32, (1, LANES), 1) < B_HEAD
    return jnp.where(lo, x[:B_HEAD], x[B_HEAD:])


def _rwkv_pair_kernel(rf_ref, vf_ref, kkf_ref, rb_ref, vb_ref, kkb_ref,
                      ldf_ref, kdf_ref, betaf_ref, ldb_ref, kdb_ref, betab_ref,
                      yf_ref, yb_ref, h_ref):
    c = pl.program_id(1)

    @pl.when(c == 0)
    def _():
        h_ref[...] = jnp.zeros_like(h_ref)

    n_pairs = B_WIDTH // LANES
    ti = lax.broadcasted_iota(jnp.int32, (CHUNK, LANES), 0)
    si = lax.broadcasted_iota(jnp.int32, (CHUNK, LANES), 1) % CHUNK
    t64 = lax.broadcasted_iota(jnp.int32, (CHUNK, CHUNK), 0)
    s64 = lax.broadcasted_iota(jnp.int32, (CHUNK, CHUNK), 1)
    eye = ti == si
    dirs = ((rf_ref, vf_ref, kkf_ref, ldf_ref, kdf_ref, betaf_ref, False),
            (rb_ref, vb_ref, kkb_ref, ldb_ref, kdb_ref, betab_ref, True))

    units = []
    n_rows = rf_ref.shape[0]
    for bi in range(n_rows):
      for d, (r_ref, v_ref, kk_ref, ld_ref, kd_ref, beta_ref, rev) in enumerate(dirs):
        strict = (si > ti) if rev else (ti > si)
        incl = (si >= ti) if rev else (ti >= si)
        tri = ((s64 >= t64) if rev else (t64 >= s64)).astype(BF16)
        ld_all = ld_ref[bi]
        cl_all = _dot_exact_lhs(tri, ld_all)
        for p in range(n_pairs):
            sl = slice(p * LANES, (p + 1) * LANES)
            ld, cl = ld_all[:, sl], cl_all[:, sl]
            total = jnp.sum(ld, axis=0, keepdims=True)
            inv_gam = jnp.exp(-cl)
            to_end = jnp.exp(total - cl)
            kk, kd, beta = kk_ref[bi, :, sl], kd_ref[bi, :, sl], beta_ref[bi, :, sl]
            units.append(dict(
                bi=bi, d=d, p=p, sl=sl, strict=strict, incl=incl, v=v_ref[bi, :, sl].astype(BF16),
                a_bar=-kk * jnp.exp(cl - ld), r_bar=r_ref[bi, :, sl] * jnp.exp(cl),
                b_til=(beta * inv_gam).astype(BF16), k_til=(kd * inv_gam).astype(BF16),
                b_hat=(beta * to_end).astype(BF16), k_hat=(kd * to_end).astype(BF16),
                gam_c=jnp.exp(total)))

    for u in units:
        x_mat = jnp.concatenate([u["a_bar"], u["r_bar"]], axis=0).astype(BF16)
        rhs = jnp.concatenate([_pair_diag(u["b_til"]), _pair_diag(u["k_til"])], axis=0)
        xbk = _dot(x_mat, rhs, NT)
        u["n_ab"] = jnp.where(u["strict"], xbk[:CHUNK, :LANES], 0.0)
        u["l_rb"] = jnp.where(u["incl"], xbk[CHUNK:, :LANES], 0.0)
        n_ak = jnp.where(u["strict"], xbk[:CHUNK, LANES:], 0.0)
        l_rk = jnp.where(u["incl"], xbk[CHUNK:, LANES:], 0.0)
        u["nl"] = jnp.concatenate([n_ak, l_rk], axis=0).astype(BF16)
    for u in units:
        nv = _dot(u["nl"], _pair_diag(u["v"]))
        u["w"], u["u0"], u["lrkv"] = u["a_bar"], nv[:CHUNK], nv[CHUNK:]
        u["npow"] = u["n_ab"].astype(BF16)

    steps = int(math.log2(CHUNK))
    for kstep in range(steps):
        for u in units:
            rhs = jnp.concatenate([_pair_diag(u["w"].astype(BF16)), _pair_diag(u["u0"].astype(BF16))], axis=1)
            upd = _dot(u["npow"], rhs)
            u["w"] = u["w"] + upd[:, :LANES]
            u["u0"] = u["u0"] + upd[:, LANES:]
        if kstep + 1 < steps:
            for u in units:
                u["npow"] = _dot(u["npow"], _pair_diag(u["npow"])).astype(BF16)

    for u in units:
        wb, ub = u["w"].astype(BF16), u["u0"].astype(BF16)
        lx = _dot(u["l_rb"].astype(BF16), jnp.concatenate([_pair_diag(wb), _pair_diag(ub)], axis=1))
        u["p_mat"] = u["r_bar"] + lx[:, :LANES]
        u["y0"] = u["lrkv"] + lx[:, LANES:]
        lhs = jnp.concatenate([u["b_hat"], u["k_hat"]], axis=0)
        rhs = jnp.concatenate([jnp.concatenate([wb, ub], axis=1),
                               jnp.concatenate([jnp.zeros_like(wb), u["v"]], axis=1)], axis=0)
        mg = _dot(lhs, rhs, TN)
        u["m_full"] = _pair_pick(mg[:, :LANES]) + jnp.where(eye, u["gam_c"], 0.0)
        u["g_mat"] = _pair_pick(mg[:, LANES:])

    for u in units:
        bi, d, p = u["bi"], u["d"], u["p"]
        h0 = h_ref[bi, d, p]
        a0, a1 = _split(jnp.concatenate([u["p_mat"], u["m_full"]], axis=0), 2)
        h_hi, h_lo = _split(h0, 2)
        bh, bl = _pair_diag(h_hi), _pair_diag(h_lo)
        out = _dot(a0, bh) + (_dot(a0, bl) + _dot(a1, bh))
        (yb_ref if d else yf_ref)[bi, :, u["sl"]] = out[:CHUNK] + u["y0"]
        h_ref[bi, d, p] = out[CHUNK:] + u["g_mat"]


def _rwkv_pairs(r, v, kk, ld, kd, beta, ctx_len):
    bn, s, _ = r.shape
    n_chunks = s // CHUNK
    ctx_chunks = ctx_len // CHUNK
    rows = RWKV_ROWS if bn % RWKV_ROWS == 0 else 1

    def back(c):
        return jnp.where(c < ctx_chunks, ctx_chunks - 1 - c, n_chunks - 1 + ctx_chunks - c)

    fwd = pl.BlockSpec((rows, CHUNK, B_WIDTH), lambda b, c: (b, c, 0))
    bwd = pl.BlockSpec((rows, CHUNK, B_WIDTH), lambda b, c: (b, back(c), 0))
    fwd_d = pl.BlockSpec((None, rows, CHUNK, B_WIDTH), lambda b, c: (0, b, c, 0))
    bwd_d = pl.BlockSpec((None, rows, CHUNK, B_WIDTH), lambda b, c: (1, b, back(c), 0))
    y = jax.ShapeDtypeStruct((bn, s, B_WIDTH), F32)
    return pl.pallas_call(
        _rwkv_pair_kernel,
        grid=(bn // rows, n_chunks),
        in_specs=[fwd, fwd, fwd, bwd, bwd, bwd, fwd_d, fwd_d, fwd_d, bwd_d, bwd_d, bwd_d],
        out_specs=[fwd, bwd],
        out_shape=[y, y],
        scratch_shapes=[pltpu.VMEM((rows, 2, B_WIDTH // LANES, B_HEAD, LANES), F32)],
        compiler_params=_params(("parallel", "arbitrary")),
        name="rwkv_scan",
    )(r, v, kk, r, v, kk, ld, kd, beta, ld, kd, beta)


def _outproj_kernel(x_ref, mod_ref, a_ref, yf_ref, yb_ref, bonus_ref, g_ref, c_ref,
                    wa_ref, wb_ref, wc_ref, lnxg_ref, lnxb_ref, ln1g_ref, ln1b_ref, o_ref, *, alpha):
    mod = mod_ref[...]
    y = yf_ref[...] + yb_ref[...] + bonus_ref[...]
    ones = _group_ones(B_WIDTH, B_HEAD)
    inv = 1.0 / B_HEAD
    mu = _dot_exact_rhs(y, ones) * inv
    yc = y - mu
    var = _dot_exact_rhs(yc * yc, ones) * inv
    yn = yc * lax.rsqrt(var + GN_EPS) * lnxg_ref[...] + lnxb_ref[...]
    bmix = (yn * g_ref[...]).astype(BF16)
    o = _dot(a_ref[...], wa_ref[...]) + _dot(bmix, wb_ref[...]) + _dot(c_ref[...], wc_ref[...])
    o_ref[...] = _ln(alpha * x_ref[...] + mod[2:3] * o, LN_EPS) * ln1g_ref[...] + ln1b_ref[...]


def _outproj(xs, mods, a_out, yf, yb, bonus, g, c_out, wa, wb, wc, lnxg, lnxb, ln1g, ln1b, ctx_len, alpha):
    bn, s, d = xs.shape
    row = lambda width: pl.BlockSpec((None, TM, width), lambda b, i: (b, i, 0))
    full = lambda a: pl.BlockSpec(a.shape, lambda b, i: (0,) * a.ndim)
    consts = (wa, wb, wc, lnxg, lnxb, ln1g, ln1b)
    return pl.pallas_call(
        functools.partial(_outproj_kernel, alpha=alpha),
        grid=(bn, s // TM),
        in_specs=[row(d), _mod_spec(ctx_len, d),
                  row(A_WIDTH), row(B_WIDTH), row(B_WIDTH), row(B_WIDTH), row(B_WIDTH), row(C_WIDTH)]
                 + [full(p) for p in consts],
        out_specs=row(d),
        out_shape=jax.ShapeDtypeStruct((bn, s, d), F32),
        compiler_params=_params(("parallel", "parallel")),
        name="outproj",
    )(xs, mods, a_out, yf, yb, bonus, g, c_out, *consts)


def _swiglu_rows(h, w1_ref, w3_ref, w2_ref):
    acc = jnp.zeros((h.shape[0], w2_ref.shape[-1]), F32)
    for j in range(w1_ref.shape[-1] // FF_CHUNK):
        sl = slice(j * FF_CHUNK, (j + 1) * FF_CHUNK)
        u = _dot(h, w1_ref[:, sl])
        t = _dot(h, w3_ref[:, sl])
        acc = acc + _dot((_silu(u) * t).astype(BF16), w2_ref[sl, :])
    return acc


def _ffn_kernel(x_ref, mod_ref, w1_ref, w3_ref, w2_ref, g_ref, b_ref, o_ref, *, alpha):
    mod = mod_ref[...]
    x = x_ref[...]
    h = (_ln(x, LN_EPS) * (1.0 + mod[4:5]) + mod[3:4]).astype(BF16)
    f = _swiglu_rows(h, w1_ref, w3_ref, w2_ref)
    o_ref[...] = _ln(alpha * x + mod[5:6] * f, LN_EPS) * g_ref[...] + b_ref[...]


def _ffn(x1, mods, w1, w3, w2, g, b, ctx_len, alpha):
    bn, s, d = x1.shape
    row = pl.BlockSpec((None, TM, d), lambda bb, i: (bb, i, 0))
    resident = lambda a: pl.BlockSpec(a.shape, lambda bb, i: (0,) * a.ndim, pipeline_mode=pl.Buffered(1))
    full = lambda a: pl.BlockSpec(a.shape, lambda bb, i: (0,) * a.ndim)
    return pl.pallas_call(
        functools.partial(_ffn_kernel, alpha=alpha),
        grid=(bn, s // TM),
        in_specs=[row, _mod_spec(ctx_len, d),
                  resident(w1), resident(w3), resident(w2), full(g), full(b)],
        out_specs=row,
        out_shape=jax.ShapeDtypeStruct((bn, s, d), F32),
        compiler_params=_params(("parallel", "parallel")),
        name="ffn",
    )(x1, mods, w1, w3, w2, g, b)


def _moe_pre_kernel(x_ref, mod_ref, router_ref, h_ref, logit_ref):
    mod = mod_ref[...]
    h = _ln(x_ref[...], LN_EPS) * (1.0 + mod[4:5]) + mod[3:4]
    h_ref[...] = h.astype(h_ref.dtype)
    logit_ref[...] = _dot6(h, router_ref[...])


def _moe_pre(x1, mods, router_p, ctx_len):
    bn, s, d = x1.shape
    ct = ctx_len // TM
    lt = (s - ctx_len) // TM
    return pl.pallas_call(
        _moe_pre_kernel,
        grid=(bn, lt),
        in_specs=[pl.BlockSpec((None, TM, d), lambda b, i: (b, i + ct, 0)),
                  pl.BlockSpec((None, None, 6, d), lambda b, i: (b, 1, 0, 0)),
                  pl.BlockSpec(router_p.shape, lambda b, i: (0, 0))],
        out_specs=[pl.BlockSpec((TM, d), lambda b, i: (b * lt + i, 0)),
                   pl.BlockSpec((TM, LANES), lambda b, i: (b * lt + i, 0))],
        out_shape=[jax.ShapeDtypeStruct((bn * lt * TM, d), BF16),
                   jax.ShapeDtypeStruct((bn * lt * TM, LANES), F32)],
        compiler_params=_params(("parallel", "parallel")),
        name="moe_pre",
    )(x1, mods, router_p)


def _slot_onehot(slots_ref, block):
    sl = slots_ref[...]
    s_iota = lax.broadcasted_iota(jnp.int32, (MOE_ROWS, MOE_TILE), 0) + block * MOE_ROWS
    hit = jnp.logical_or(sl[0:1] == s_iota, sl[1:2] == s_iota)
    return jnp.where(hit, 1.0, 0.0).astype(BF16)


def _dispatch_kernel(wb_ref, wc_ref, wf_ref, wv_ref, slots_ref, h_ref, o_ref):
    w = pl.program_id(0)
    part = lambda: _dot(_slot_onehot(slots_ref, wb_ref[w]), h_ref[...])

    @pl.when(wf_ref[w] == 1)
    def _():
        o_ref[...] = part().astype(o_ref.dtype)

    @pl.when(jnp.logical_and(wf_ref[w] == 0, wv_ref[w] == 1))
    def _():
        o_ref[...] = (o_ref[...].astype(F32) + part()).astype(o_ref.dtype)


def _dispatch(h, slots, work):
    t, d = h.shape
    wb, wc, wf, wv = work
    n_slots = (t * TOP_K // MOE_ROWS + N_EXPERTS) * MOE_ROWS
    grid_spec = pltpu.PrefetchScalarGridSpec(
        num_scalar_prefetch=4,
        grid=(wb.shape[0],),
        in_specs=[pl.BlockSpec((TOP_K, MOE_TILE), lambda w, b, c, f, v: (0, c[w])),
                  pl.BlockSpec((MOE_TILE, d), lambda w, b, c, f, v: (c[w], 0))],
        out_specs=pl.BlockSpec((MOE_ROWS, d), lambda w, b, c, f, v: (b[w], 0)),
    )
    return pl.pallas_call(
        _dispatch_kernel,
        grid_spec=grid_spec,
        out_shape=jax.ShapeDtypeStruct((n_slots, d), BF16),
        compiler_params=_params(("arbitrary",)),
        name="moe_dispatch",
    )(wb, wc, wf, wv, slots, h)


def _collect_kernel(wb_ref, wc_ref, wf_ref, wv_ref, slots_ref, scol_ref, gate_ref, y_ref, o_ref):
    w = pl.program_id(0)

    def part():
        base = wb_ref[w] * MOE_ROWS
        scol = scol_ref[...] - base
        gates = jnp.where(jnp.logical_and(scol >= 0, scol < MOE_ROWS), gate_ref[...], 0.0)
        gate = jnp.sum(gates, axis=1, keepdims=True)
        return _dot(_slot_onehot(slots_ref, wb_ref[w]), y_ref[...], TN) * gate

    @pl.when(wf_ref[w] == 1)
    def _():
        o_ref[...] = part()

    @pl.when(jnp.logical_and(wf_ref[w] == 0, wv_ref[w] == 1))
    def _():
        o_ref[...] = o_ref[...] + part()


def _collect(y_buf, slots, slots_col, gates, work):
    n_slots, d = y_buf.shape
    t = slots.shape[1]
    wb, wc, wf, wv = work
    tile = lambda w, b, c, f, v: (c[w], 0)
    grid_spec = pltpu.PrefetchScalarGridSpec(
        num_scalar_prefetch=4,
        grid=(wb.shape[0],),
        in_specs=[pl.BlockSpec((TOP_K, MOE_TILE), lambda w, b, c, f, v: (0, c[w])),
                  pl.BlockSpec((MOE_TILE, TOP_K), tile), pl.BlockSpec((MOE_TILE, TOP_K), tile),
                  pl.BlockSpec((MOE_ROWS, d), lambda w, b, c, f, v: (b[w], 0))],
        out_specs=pl.BlockSpec((MOE_TILE, d), tile),
    )
    return pl.pallas_call(
        _collect_kernel,
        grid_spec=grid_spec,
        out_shape=jax.ShapeDtypeStruct((t, d), F32),
        compiler_params=_params(("arbitrary",)),
        name="moe_collect",
    )(wb, wc, wf, wv, slots, slots_col, gates, y_buf)


def _expert_kernel(be_ref, nb_ref, x_ref, w1_ref, w3_ref, w2_ref, o_ref):
    i = pl.program_id(0)

    @pl.when(i < nb_ref[0])
    def _():
        o_ref[...] = _swiglu_rows(x_ref[...], w1_ref, w3_ref, w2_ref).astype(o_ref.dtype)

    @pl.when(i >= nb_ref[0])
    def _():
        o_ref[...] = jnp.zeros_like(o_ref)


def _experts(buf, block_e, n_used, w1, w3, w2):
    n, d = buf.shape
    n_blocks = n // MOE_ROWS
    ff = w1.shape[-1]
    wspec = lambda shape: pl.BlockSpec((None,) + shape, lambda i, be, nb: (be[i], 0, 0),
                                       pipeline_mode=pl.Buffered(1))
    used = lambda i, be, nb: (jnp.minimum(i, nb[0] - 1), 0)
    grid_spec = pltpu.PrefetchScalarGridSpec(
        num_scalar_prefetch=2,
        grid=(n_blocks,),
        in_specs=[pl.BlockSpec((MOE_ROWS, d), used), wspec((d, ff)), wspec((d, ff)), wspec((ff, d))],
        out_specs=pl.BlockSpec((MOE_ROWS, d), lambda i, be, nb: (i, 0)),
    )
    return pl.pallas_call(
        _expert_kernel,
        grid_spec=grid_spec,
        out_shape=jax.ShapeDtypeStruct((n, d), BF16),
        compiler_params=_params(("arbitrary",)),
        name="experts",
    )(block_e, n_used, buf, w1, w3, w2)


def _combine_kernel(x_ref, mod_ref, y_ref, g_ref, b_ref, o_ref, *, alpha):
    mod = mod_ref[...]
    o_ref[...] = _ln(alpha * x_ref[...] + mod[5:6] * y_ref[...], LN_EPS) * g_ref[...] + b_ref[...]


def _combine(x1, mods, y_tok, g, b, ctx_len, alpha):
    bn, s, d = x1.shape
    ct = ctx_len // TM
    lt = (s - ctx_len) // TM
    full = lambda a: pl.BlockSpec(a.shape, lambda bb, i: (0,) * a.ndim)
    return pl.pallas_call(
        functools.partial(_combine_kernel, alpha=alpha),
        grid=(bn, lt),
        in_specs=[pl.BlockSpec((None, TM, d), lambda bb, i: (bb, i + ct, 0)),
                  pl.BlockSpec((None, None, 6, d), lambda bb, i: (bb, 1, 0, 0)),
                  pl.BlockSpec((TM, d), lambda bb, i: (bb * lt + i, 0)),
                  full(g), full(b)],
        out_specs=pl.BlockSpec((None, TM, d), lambda bb, i: (bb, i, 0)),
        out_shape=jax.ShapeDtypeStruct((bn, lt * TM, d), F32),
        compiler_params=_params(("parallel", "parallel")),
        name="moe_combine",
    )(x1, mods, y_tok, g, b)


def _work_lists(rank_at_tile, counts, pstart, block_e, n_used, n_blocks, n_tiles):
    n_work = n_tiles * N_EXPERTS + n_blocks
    blocks = jnp.arange(n_blocks, dtype=jnp.int32)
    used = blocks < n_used
    r0 = blocks * MOE_ROWS - pstart[block_e]
    r_last = jnp.minimum(r0 + MOE_ROWS, counts[block_e]) - 1
    cols = rank_at_tile.T[block_e]
    find = jax.vmap(lambda col, val: jnp.searchsorted(col, val, side="right"))
    lo = jnp.clip(find(cols, r0) - 1, 0, n_tiles - 1)
    hi = jnp.clip(find(cols, r_last) - 1, 0, n_tiles - 1)
    n_b = jnp.where(used, hi - lo + 1, 0)
    ends = jnp.cumsum(n_b)
    starts = ends - n_b
    total = ends[-1]
    w = jnp.arange(n_work, dtype=jnp.int32)
    valid = w < total
    wl = jnp.minimum(w, total - 1)
    blk = jnp.minimum(jnp.searchsorted(ends, wl, side="right"), n_blocks - 1).astype(jnp.int32)
    tile = (lo[blk] + (wl - starts[blk])).astype(jnp.int32)
    first = jnp.logical_and(valid, w == starts[blk])
    as_i32 = lambda a: a.astype(jnp.int32)
    by_block = (blk, tile, as_i32(first), as_i32(valid))
    order = jnp.argsort(jnp.where(valid, tile, n_tiles), stable=True)
    order = order[jnp.minimum(w, total - 1)]
    tile2, blk2 = tile[order], blk[order]
    first2 = jnp.logical_and(valid, jnp.concatenate([jnp.ones((1,), bool), tile2[1:] != tile2[:-1]]))
    by_tile = (blk2, tile2, as_i32(first2), as_i32(valid))
    return by_block, by_tile


def _moe_layer(x1, mods, router, w1, w3, w2, g, b, ctx_len, alpha):
    d = x1.shape[-1]
    router_p = jnp.pad(router, ((0, 0), (0, LANES - N_EXPERTS)))
    h, logits = _moe_pre(x1, mods, router_p, ctx_len)
    t = h.shape[0]
    top_v, top_i = lax.top_k(logits[:, :N_EXPERTS], TOP_K)
    gates = jax.nn.softmax(top_v, axis=-1)
    e_flat = top_i.reshape(-1)
    onehot = (e_flat[:, None] == jnp.arange(N_EXPERTS)[None, :]).astype(jnp.int32)
    ranks = jnp.cumsum(onehot, axis=0) - onehot
    rank = jnp.sum(ranks * onehot, axis=1)
    counts = jnp.sum(onehot, axis=0)
    padded = (counts + MOE_ROWS - 1) // MOE_ROWS * MOE_ROWS
    pend = jnp.cumsum(padded)
    pstart = pend - padded
    slot = (pstart[e_flat] + rank).astype(jnp.int32)
    n_blocks = t * TOP_K // MOE_ROWS + N_EXPERTS
    assert t % MOE_TILE == 0
    n_tiles = t // MOE_TILE
    block_e = jnp.minimum(jnp.searchsorted(pend, jnp.arange(n_blocks) * MOE_ROWS, side="right"),
                          N_EXPERTS - 1).astype(jnp.int32)
    n_used = (pend[-1:] // MOE_ROWS).astype(jnp.int32)
    rank_at_tile = jnp.concatenate([ranks[::MOE_TILE * TOP_K], counts[None, :]], axis=0)
    by_block, by_tile = _work_lists(rank_at_tile, counts, pstart, block_e, n_used, n_blocks, n_tiles)
    slots_col = slot.reshape(t, TOP_K)
    slots = slots_col.T
    buf = _dispatch(h, slots, by_block)
    y_buf = _experts(buf, block_e, n_used, w1, w3, w2)
    y_tok = _collect(y_buf, slots, slots_col, gates, by_tile)
    return _combine(x1, mods, y_tok, g, b, ctx_len, alpha)


def _rope_tables(n_rows, ctx_len, dim, lane_lo):
    quarter = dim // 4
    inv = ROPE_THETA ** (-jnp.arange(quarter, dtype=F32) / quarter)
    rows = jnp.repeat(jnp.arange(n_rows, dtype=F32), GRID_W)
    cols = jnp.tile(jnp.arange(GRID_W, dtype=F32), n_rows)
    ang = jnp.concatenate([rows[:, None] * inv, rows[:, None] * inv,
                           cols[:, None] * inv, cols[:, None] * inv], axis=-1)
    sign = jnp.tile(jnp.concatenate([-jnp.ones(quarter, F32), jnp.ones(quarter, F32)]), 2)
    cos, sin = jnp.cos(ang), jnp.sin(ang) * sign
    length = cos.shape[0]
    if lane_lo == 0:
        reps = LANES // dim
        cos, sin = jnp.tile(cos, (1, reps)), jnp.tile(sin, (1, reps))
    else:
        pad = ((0, 0), (lane_lo, LANES - lane_lo - dim))
        cos = jnp.pad(cos, pad, constant_values=1.0)
        sin = jnp.pad(sin, pad)
    cos = jnp.concatenate([jnp.ones((ctx_len, LANES), F32), cos], axis=0)
    sin = jnp.concatenate([jnp.zeros((ctx_len, LANES), F32), sin], axis=0)
    return cos, sin


def _block_diag2(w):
    z = jnp.zeros_like(w[0])
    return jnp.concatenate([jnp.concatenate([w[0], z], axis=1), jnp.concatenate([z, w[1]], axis=1)], axis=0)


def _lambda_init(layer):
    return 0.8 - 0.6 * math.exp(-0.3 * layer)


def kernel(x, c, ctx, c_ctx, ada_w, ada_b, w_in, w_out, ln1_g, ln1_b, ln2_g, ln2_b,
           lam_q1, lam_k1, lam_q2, lam_k2, diff_norm_g, shift_mu, w0, w2, a0, a2, g2,
           k_k, k_a, r_k, lnx_g, lnx_b, q_norm_g, w_uq, kv_norm_g, w_ukv,
           ff_w1, ff_w3, ff_w2, router, moe_w1, moe_w3, moe_w2):
    bn, seq, d = x.shape
    ctx_len = ctx.shape[1]
    depth = ada_w.shape[0]
    assert d == D_MODEL and seq % TM == 0 and ctx_len % TM == 0 and seq % GRID_W == 0
    alpha = (2.0 * depth) ** 0.25
    n_grid_rows = seq // GRID_W
    cos_a, sin_a = _rope_tables(n_grid_rows, ctx_len, A_QK_DIM, 0)
    cos_c, sin_c = _rope_tables(n_grid_rows, ctx_len, C_ROPE, C_NOPE)

    cond_rows = 8 * ((bn + 1 + 7) // 8)
    cond = jnp.zeros((cond_rows, d), F32).at[:bn].set(c).at[bn].set(c_ctx)
    xs = jnp.concatenate([ctx, x], axis=1)

    for i in range(depth):
        with_ctx = i < depth - 1
        m = _ada(cond, ada_w[i], ada_b[i]).reshape(cond_rows, 6, d)
        mods = jnp.stack([jnp.broadcast_to(m[bn], (bn, 6, d)), m[:bn]], axis=1)

        wc = w_in[i][:, N_A + N_B:]
        kpe_w = jnp.pad(wc[:, C_Q_RANK + C_KV_RANK:], ((0, 0), (C_NOPE, LANES - C_NOPE - C_ROPE)))
        w_p = jnp.concatenate([w_in[i][:, :N_A + N_B], wc[:, :C_Q_RANK + C_KV_RANK], kpe_w], axis=1).astype(BF16)
        qa, ka, va, pb, pc = _inproj(xs, mods, w_p, cos_a, sin_a, ctx_len)

        lam_p = jnp.stack([lam_q1[i], lam_k1[i], lam_q2[i], lam_k2[i]])
        g_col = jnp.broadcast_to(diff_norm_g[i][:, None], (A_V_DIM, TM))
        a_out = _attention(qa, ka, va, lam_p, g_col, mode="diff", ctx_len=ctx_len, lam_init=_lambda_init(i))

        wq = w_uq[i].reshape(C_Q_RANK, C_HEADS, C_NOPE + C_ROPE)
        wq_p = jnp.pad(wq, ((0, 0), (0, 0), (0, LANES - C_NOPE - C_ROPE))).reshape(C_Q_RANK, -1).astype(BF16)
        wkv = w_ukv[i].reshape(C_KV_RANK, C_HEADS, C_NOPE + C_V)
        wk_p = jnp.pad(wkv[:, :, :C_NOPE], ((0, 0), (0, 0), (0, LANES - C_NOPE))).reshape(C_KV_RANK, -1).astype(BF16)
        wv_p = wkv[:, :, C_NOPE:].reshape(C_KV_RANK, -1).astype(BF16)
        qc, kc, vc = _mla_prep(pc, q_norm_g[i].reshape(1, -1), kv_norm_g[i].reshape(1, -1),
                               wq_p, wk_p, wv_p, cos_c, sin_c)
        c_out = _attention(qc, kc, vc, lam_p, g_col, mode="mla", ctx_len=ctx_len)

        r, v, kkn, ld, kd, beta, gate, bonus = _rwkv_prep(
            pb, shift_mu[i].reshape(1, -1), w0[i].reshape(1, -1), _block_diag2(w2[i]).astype(BF16),
            a0[i].reshape(1, -1), _block_diag2(a2[i]).astype(BF16), g2[i].astype(BF16),
            k_k[i].reshape(1, -1), k_a[i].reshape(1, -1), r_k[i].reshape(1, -1), ctx_len)
        yf, yb = _rwkv_pairs(r, v, kkn, ld, kd, beta, ctx_len)

        wo = w_out[i].astype(BF16)
        x1 = _outproj(xs, mods, a_out, yf, yb, bonus, gate, c_out,
                      wo[:A_WIDTH], wo[A_WIDTH:A_WIDTH + B_WIDTH], wo[A_WIDTH + B_WIDTH:],
                      lnx_g[i].reshape(1, -1), lnx_b[i].reshape(1, -1),
                      ln1_g[i].reshape(1, -1), ln1_b[i].reshape(1, -1), ctx_len, alpha)

        j = i // 2
        g2n, b2n = ln2_g[i].reshape(1, -1), ln2_b[i].reshape(1, -1)
        if i % 2 == 0:
            xs = _ffn(x1, mods, ff_w1[j].astype(BF16), ff_w3[j].astype(BF16), ff_w2[j].astype(BF16),
                      g2n, b2n, ctx_len, alpha)
        else:
            if with_ctx:
                raise NotImplementedError("routed FFN on the context rows is not needed at this depth")
            return _moe_layer(x1, mods, router[j], moe_w1[j].astype(BF16), moe_w3[j].astype(BF16),
                              moe_w2[j].astype(BF16), g2n, b2n, ctx_len, alpha)
    return xs[:, ctx_len:]
```

```python
import functools
import math

import jax
import jax.numpy as jnp
from jax import lax
from jax.experimental import pallas as pl
from jax.experimental.pallas import tpu as pltpu

F32 = jnp.float32
BF16 = jnp.bfloat16

D_MODEL = 1024
GRID_W = 64
ROPE_THETA = 10000.0
A_HEADS, A_QK_DIM, A_V_DIM = 4, 64, 128
A_WIDTH = A_HEADS * A_V_DIM
A_QK_COLS = 2 * A_HEADS * A_QK_DIM
B_HEADS, B_HEAD = 4, 64
B_WIDTH = B_HEADS * B_HEAD
B_DECAY_RANK, B_A_RANK, B_GATE_RANK = 64, 64, 128
C_HEADS, C_NOPE, C_ROPE, C_V = 4, 64, 32, 64
C_WIDTH = C_HEADS * C_V
C_Q_RANK, C_KV_RANK = 256, 128
N_A = 2 * A_QK_COLS + A_WIDTH
N_B = 3 * B_WIDTH + 2 * B_DECAY_RANK + 2 * B_A_RANK + B_GATE_RANK
N_C = C_Q_RANK + C_KV_RANK + C_ROPE
N_C_PAD = C_Q_RANK + C_KV_RANK + 128
D_FF = 3584
N_EXPERTS = 8
TOP_K = 2
LN_EPS = 1e-6
RMS_EPS = 1e-6
GN_EPS = 64e-5

LOG2E = math.log2(math.e)
LANES = 128
TM = 256
CHUNK = 64
ROWS_B = 2
RWKV_ROWS = 2
ATT_TK = 1024
ATT_PAR = 4
MOE_ROWS = 512
MOE_TILE = 512
FF_CHUNK = 512
VMEM_LIMIT = 56 * 1024 * 1024

NN = (((1,), (0,)), ((), ()))
NT = (((1,), (1,)), ((), ()))
TN = (((0,), (0,)), ((), ()))


def _params(sem, vmem=VMEM_LIMIT, flags=None):
    return pltpu.CompilerParams(dimension_semantics=sem, vmem_limit_bytes=vmem, flags=flags)


def _split(x, n):
    parts, r = [], x
    for _ in range(n):
        p = r.astype(BF16)
        parts.append(p)
        r = r - p.astype(F32)
    return parts


def _dot(a, b, dn=NN):
    return lax.dot_general(a, b, dn, preferred_element_type=F32)


def _dot1(a, b, dn=NN):
    return _dot(a.astype(BF16), b.astype(BF16), dn)


def _dot3(a, b, dn=NN):
    a0, a1 = _split(a, 2)
    b0, b1 = _split(b, 2)
    return _dot(a0, b0, dn) + (_dot(a0, b1, dn) + _dot(a1, b0, dn))


def _dot6(a, b, dn=NN):
    a0, a1, a2 = _split(a, 3)
    b0, b1, b2 = _split(b, 3)
    lo = _dot(a1, b1, dn) + (_dot(a0, b2, dn) + _dot(a2, b0, dn))
    return _dot(a0, b0, dn) + ((_dot(a0, b1, dn) + _dot(a1, b0, dn)) + lo)


def _dot_exact_lhs(a_bf16, b, dn=NN, n=3):
    parts = _split(b, n)
    out = _dot(a_bf16, parts[-1], dn)
    for p in parts[-2::-1]:
        out = out + _dot(a_bf16, p, dn)
    return out


def _dot_exact_rhs(a, b_bf16, dn=NN, n=3):
    parts = _split(a, n)
    out = _dot(parts[-1], b_bf16, dn)
    for p in parts[-2::-1]:
        out = out + _dot(p, b_bf16, dn)
    return out


def _ln(x, eps):
    mu = jnp.mean(x, axis=-1, keepdims=True)
    xc = x - mu
    return xc * lax.rsqrt(jnp.mean(xc * xc, axis=-1, keepdims=True) + eps)


def _sigmoid(x):
    return 1.0 / (1.0 + jnp.exp(-x))


def _silu(x):
    return x * _sigmoid(x)


def _group_ones(width, group):
    r = lax.broadcasted_iota(jnp.int32, (width, width), 0) // group
    c = lax.broadcasted_iota(jnp.int32, (width, width), 1) // group
    return (r == c).astype(BF16)


def _partner(x, half):
    lane = lax.broadcasted_iota(jnp.int32, x.shape, 1)
    up = pltpu.roll(x, LANES - half, 1)
    dn = pltpu.roll(x, half, 1)
    return jnp.where((lane % (2 * half)) < half, up, dn)


def _rope(x, cos, sin, half):
    return x * cos + _partner(x, half) * sin


def _ada_kernel(c_ref, w_ref, b_ref, o_ref):
    o_ref[...] = _dot3(_silu(c_ref[...]), w_ref[...]) + b_ref[...]


def _ada(cond, w, b):
    rows, d = cond.shape
    n = w.shape[1]
    tn = 1536
    return pl.pallas_call(
        _ada_kernel,
        grid=(n // tn,),
        in_specs=[pl.BlockSpec((rows, d), lambda j: (0, 0)),
                  pl.BlockSpec((d, tn), lambda j: (0, j)),
                  pl.BlockSpec((1, tn), lambda j: (0, j))],
        out_specs=pl.BlockSpec((rows, tn), lambda j: (0, j)),
        out_shape=jax.ShapeDtypeStruct((rows, n), F32),
        compiler_params=_params(("parallel",)),
        name="ada",
    )(cond, w, b.reshape(1, n))


def _inproj_kernel(x_ref, mod_ref, w_ref, cos_ref, sin_ref,
                   q_ref, k_ref, v_ref, pb_ref, pc_ref):
    nb = x_ref.shape[0]
    h = jnp.concatenate([(_ln(x_ref[r], LN_EPS) * (1.0 + mod_ref[r, 1:2]) + mod_ref[r, 0:1]).astype(BF16)
                         for r in range(nb)], axis=0)
    cos, sin = cos_ref[...], sin_ref[...]
    scale = A_QK_DIM ** -0.5 * LOG2E
    rows = lambda a, r: a[r * TM:(r + 1) * TM]
    for j in range(A_QK_COLS // LANES):
        sl = slice(j * LANES, (j + 1) * LANES)
        qj = _dot(h, w_ref[:, sl])
        kj = _dot(h, w_ref[:, A_QK_COLS + j * LANES:A_QK_COLS + (j + 1) * LANES])
        vj = _dot(h, w_ref[:, 2 * A_QK_COLS + j * LANES:2 * A_QK_COLS + (j + 1) * LANES])
        for r in range(nb):
            q_ref[r, sl, :] = (_rope(rows(qj, r), cos, sin, A_QK_DIM // 4) * scale).T.astype(BF16)
            k_ref[r, :, sl] = _rope(rows(kj, r), cos, sin, A_QK_DIM // 4).astype(BF16)
            v_ref[r, sl, :] = rows(vj, r).T.astype(BF16)
    pb = _dot(h, w_ref[:, N_A:N_A + N_B])
    pc = _dot(h, w_ref[:, N_A + N_B:])
    for r in range(nb):
        pb_ref[r] = rows(pb, r)
        pc_ref[r] = rows(pc, r)


def _batch_rows(bn):
    return ROWS_B if bn % ROWS_B == 0 else 1


def _mod_spec(ctx_len, d, nb=None):
    ct = ctx_len // TM
    return pl.BlockSpec((nb, None, 6, d), lambda b, i: (b, jnp.where(i >= ct, 1, 0), 0, 0))


def _inproj(xs, mods, w_p, cos_a, sin_a, ctx_len):
    bn, s, d = xs.shape
    n_tiles = s // TM
    n_w = w_p.shape[1]
    nb = _batch_rows(bn)
    row = lambda width: pl.BlockSpec((nb, TM, width), lambda b, i: (b, i, 0))
    col = lambda width: pl.BlockSpec((nb, width, TM), lambda b, i: (b, 0, i))
    outs = [jax.ShapeDtypeStruct((bn, A_QK_COLS, s), BF16),
            jax.ShapeDtypeStruct((bn, s, A_QK_COLS), BF16),
            jax.ShapeDtypeStruct((bn, A_WIDTH, s), BF16),
            jax.ShapeDtypeStruct((bn, s, N_B), F32),
            jax.ShapeDtypeStruct((bn, s, N_C_PAD), F32)]
    return pl.pallas_call(
        _inproj_kernel,
        grid=(bn // nb, n_tiles),
        in_specs=[row(d),
                  _mod_spec(ctx_len, d, nb),
                  pl.BlockSpec((d, n_w), lambda b, i: (0, 0)),
                  pl.BlockSpec((TM, LANES), lambda b, i: (i, 0)),
                  pl.BlockSpec((TM, LANES), lambda b, i: (i, 0))],
        out_specs=[col(A_QK_COLS), row(A_QK_COLS), col(A_WIDTH), row(N_B), row(N_C_PAD)],
        out_shape=outs,
        compiler_params=_params(("parallel", "parallel")),
        name="inproj",
    )(xs, mods, w_p, cos_a, sin_a)


def _mla_prep_kernel(pc_ref, qg_ref, kvg_ref, wq_ref, wk_ref, wv_ref, cos_ref, sin_ref,
                     q_ref, k_ref, v_ref):
    pc = pc_ref[...]
    cq = pc[:, :C_Q_RANK]
    cq = cq * lax.rsqrt(jnp.mean(cq * cq, axis=-1, keepdims=True) + RMS_EPS) * qg_ref[...]
    ckv = pc[:, C_Q_RANK:C_Q_RANK + C_KV_RANK]
    ckv = ckv * lax.rsqrt(jnp.mean(ckv * ckv, axis=-1, keepdims=True) + RMS_EPS) * kvg_ref[...]
    cos, sin = cos_ref[...], sin_ref[...]
    kpe = _rope(pc[:, C_Q_RANK + C_KV_RANK:], cos, sin, C_ROPE // 4)
    cqb, ckvb = cq.astype(BF16), ckv.astype(BF16)
    scale = (C_NOPE + C_ROPE) ** -0.5 * LOG2E
    for h in range(C_HEADS):
        sl = slice(h * LANES, (h + 1) * LANES)
        qh = _dot(cqb, wq_ref[:, sl])
        q_ref[sl, :] = (_rope(qh, cos, sin, C_ROPE // 4) * scale).T.astype(BF16)
        k_ref[:, sl] = (_dot(ckvb, wk_ref[:, sl]) + kpe).astype(BF16)
    for j in range(C_WIDTH // LANES):
        sl = slice(j * LANES, (j + 1) * LANES)
        v_ref[sl, :] = _dot(ckvb, wv_ref[:, sl]).T.astype(BF16)


def _mla_prep(pc, qg, kvg, wq_p, wk_p, wv_p, cos_c, sin_c):
    bn, s, _ = pc.shape
    row = lambda width: pl.BlockSpec((None, TM, width), lambda b, i: (b, i, 0))
    full = lambda a: pl.BlockSpec(a.shape, lambda b, i: (0,) * a.ndim)
    tab = pl.BlockSpec((TM, LANES), lambda b, i: (i, 0))
    col = lambda width: pl.BlockSpec((None, width, TM), lambda b, i: (b, 0, i))
    hw = C_HEADS * LANES
    return pl.pallas_call(
        _mla_prep_kernel,
        grid=(bn, s // TM),
        in_specs=[row(N_C_PAD), full(qg), full(kvg), full(wq_p), full(wk_p), full(wv_p), tab, tab],
        out_specs=[col(hw), row(hw), col(C_WIDTH)],
        out_shape=[jax.ShapeDtypeStruct((bn, hw, s), BF16),
                   jax.ShapeDtypeStruct((bn, s, hw), BF16),
                   jax.ShapeDtypeStruct((bn, C_WIDTH, s), BF16)],
        compiler_params=_params(("parallel", "parallel")),
        name="mla_prep",
    )(pc, qg, kvg, wq_p, wk_p, wv_p, cos_c, sin_c)


def _attn_kernel(lam_ref, g_ref, qt_ref, k_ref, vt_ref, o_ref, *, mode, ctx_tiles, ctx_len, tk, lam_init):
    qi = pl.program_id(2)
    tq = qt_ref.shape[1]
    n_par = qt_ref.shape[0] // LANES
    dv = vt_ref.shape[0] // n_par
    width = 2 * tq if mode == "diff" else tq
    half = lax.broadcasted_iota(jnp.int32, (LANES, 1), 0) < (LANES // 2)
    q_ops = []
    for g in range(n_par):
        qt = qt_ref[g * LANES:(g + 1) * LANES, :]
        if mode == "diff":
            qt = jnp.concatenate([jnp.where(half, qt, jnp.zeros_like(qt)),
                                  jnp.where(half, jnp.zeros_like(qt), qt)], axis=1)
        q_ops.append(qt)

    def fold(x, reduce):
        rows = x.shape[0]
        while rows > 8:
            g = min(8, rows // 8)
            rows //= g
            x = reduce(x.reshape(g, rows, x.shape[1]), axis=0)
        return x

    def scores(kc, g):
        return _dot(kc[:, g * LANES:(g + 1) * LANES], q_ops[g])

    def absorb(kc, vc, carry):
        ss = [scores(kc, g) for g in range(n_par)]
        m_new = [jnp.maximum(carry[g][0], jnp.max(fold(ss[g], jnp.max), axis=0, keepdims=True))
                 for g in range(n_par)]
        alpha = [jnp.exp2(carry[g][0] - m_new[g]) for g in range(n_par)]
        ps = [jnp.exp2(ss[g] - m_new[g]) for g in range(n_par)]
        l_new = [alpha[g] * carry[g][1] + jnp.sum(fold(ps[g], jnp.sum), axis=0, keepdims=True)
                 for g in range(n_par)]
        acc = [alpha[g] * carry[g][2] + _dot(vc[g * dv:(g + 1) * dv], ps[g].astype(BF16))
               for g in range(n_par)]
        return tuple((m_new[g], l_new[g], acc[g]) for g in range(n_par))

    n_latent_chunks = (k_ref.shape[0] - ctx_len) // tk

    def body(j, stats):
        off = pl.multiple_of(ctx_len + j * tk, LANES)
        return absorb(k_ref[pl.ds(off, tk), :], vt_ref[:, pl.ds(off, tk)], stats)

    neg = jnp.full((1, width), -1e30, F32)
    zero1 = jnp.zeros((1, width), F32)
    zacc = jnp.zeros((dv, width), F32)
    stats = absorb(k_ref[0:ctx_len, :], vt_ref[:, 0:ctx_len], ((neg, zero1, zacc),) * n_par)
    n_trips = jnp.where(qi < ctx_tiles, 0, n_latent_chunks)
    stats = lax.fori_loop(0, n_trips, body, stats)
    outs = [a / l for (_, l, a) in stats]
    if mode == "diff":
        lp = lam_ref[...]
        lam = (jnp.exp(jnp.sum(lp[0:1] * lp[1:2], axis=-1, keepdims=True))
               - jnp.exp(jnp.sum(lp[2:3] * lp[3:4], axis=-1, keepdims=True)) + lam_init)
        for g in range(n_par):
            o = outs[g][:, :tq] - lam * outs[g][:, tq:]
            o = o * lax.rsqrt(jnp.mean(o * o, axis=0, keepdims=True) + RMS_EPS) * g_ref[...]
            o = o * (1.0 - lam_init)
            o_ref[:, g * LANES:(g + 1) * LANES] = o.T.astype(o_ref.dtype)
    else:
        per_slab = LANES // dv
        for j in range(n_par // per_slab):
            o = jnp.concatenate(outs[j * per_slab:(j + 1) * per_slab], axis=0)
            o_ref[:, j * LANES:(j + 1) * LANES] = o.T.astype(o_ref.dtype)


def _attention(qt, k, vt, lam_p, g, *, mode, ctx_len, lam_init=0.0):
    bn, s, _ = k.shape
    heads = k.shape[2] // LANES
    dv = vt.shape[1] // heads
    par = math.gcd(heads, ATT_PAR)
    tk = math.gcd(s - ctx_len, ATT_TK)
    assert tk % LANES == 0 and ctx_len % LANES == 0 and (par * dv) % LANES == 0
    kern = functools.partial(_attn_kernel, mode=mode, ctx_tiles=ctx_len // TM, ctx_len=ctx_len, tk=tk,
                             lam_init=lam_init)
    return pl.pallas_call(
        kern,
        grid=(bn, heads // par, s // TM),
        in_specs=[pl.BlockSpec(lam_p.shape, lambda b, h, i: (0, 0)),
                  pl.BlockSpec(g.shape, lambda b, h, i: (0, 0)),
                  pl.BlockSpec((None, par * LANES, TM), lambda b, h, i: (b, h, i)),
                  pl.BlockSpec((None, s, par * LANES), lambda b, h, i: (b, 0, h)),
                  pl.BlockSpec((None, par * dv, s), lambda b, h, i: (b, h, 0))],
        out_specs=pl.BlockSpec((None, TM, par * dv), lambda b, h, i: (b, i, h)),
        out_shape=jax.ShapeDtypeStruct((bn, s, heads * dv), BF16),
        compiler_params=_params(("parallel", "parallel", "parallel")),
        name="attn_" + mode,
    )(lam_p, g, qt, k, vt)


def _rwkv_prep_kernel(pb_ref, prev_ref, next_ref, mu_ref, w0_ref, w2_ref, a0_ref, a2_ref, g2_ref,
                      kk_ref, ka_ref, rk_ref,
                      r_out, v_out, kkn_out, ld_out, kd_out, beta_out, g_out, bonus_out,
                      *, ctx_tiles, n_tiles):
    i = pl.program_id(1)
    x = pb_ref[...]
    row = lax.broadcasted_iota(jnp.int32, (TM, 1), 0)
    has_prev = jnp.logical_and(i != 0, i != ctx_tiles)
    has_next = jnp.logical_and(i != ctx_tiles - 1, i != n_tiles - 1)
    prev_edge = jnp.where(has_prev, prev_ref[7:8, :], 0.0)
    next_edge = jnp.where(has_next, next_ref[0:1, :], 0.0)
    xp = jnp.where(row == 0, prev_edge, pltpu.roll(x, 1, 0))
    xn = jnp.where(row == TM - 1, next_edge, pltpu.roll(x, TM - 1, 0))
    z = x + mu_ref[...] * (0.5 * (xp + xn) - x)

    r = z[:, :B_WIDTH]
    k = z[:, B_WIDTH:2 * B_WIDTH]
    v = z[:, 2 * B_WIDTH:3 * B_WIDTH]
    o = 3 * B_WIDTH
    wd = z[:, o:o + 2 * B_DECAY_RANK]
    o += 2 * B_DECAY_RANK
    ad = z[:, o:o + 2 * B_A_RANK]
    o += 2 * B_A_RANK
    gd = z[:, o:]

    u = w0_ref[...] + _dot1(jnp.tanh(wd), w2_ref[...])
    nu = -u
    w_raw = -(jnp.maximum(nu, 0.0) + jnp.log(1.0 + jnp.exp(-jnp.abs(nu)))) - 0.5
    ld = -jnp.exp(w_raw)
    lr = _sigmoid(a0_ref[...] + _dot1(ad, a2_ref[...]))
    g_out[...] = _dot1(_sigmoid(gd), g2_ref[...])

    ones = _group_ones(B_WIDTH, B_HEAD)
    kk = k * kk_ref[...]
    norm = jnp.sqrt(_dot_exact_rhs(kk * kk, ones))
    kkn = kk / jnp.maximum(norm, 1e-12)
    ka = ka_ref[...]
    kd_sum = jnp.zeros_like(k)
    for d in range(2):
        lr_d = lr[:, d * B_WIDTH:(d + 1) * B_WIDTH]
        kd = k * (1.0 + (lr_d - 1.0) * ka)
        kd_sum = kd_sum + kd
        ld_out[d] = ld[:, d * B_WIDTH:(d + 1) * B_WIDTH]
        kd_out[d] = kd
        beta_out[d] = kkn * lr_d
    bonus_out[...] = _dot_exact_rhs(r * kd_sum * rk_ref[...], ones) * v
    r_out[...] = r
    v_out[...] = v
    kkn_out[...] = kkn


def _rwkv_prep(pb, mu, w0, w2bd, a0, a2bd, g2, k_k, k_a, r_k, ctx_len):
    bn, s, _ = pb.shape
    n_tiles = s // TM
    eight = TM // 8
    row = lambda width: pl.BlockSpec((None, TM, width), lambda b, i: (b, i, 0))
    drow = pl.BlockSpec((2, None, TM, B_WIDTH), lambda b, i: (0, b, i, 0))
    full = lambda a: pl.BlockSpec(a.shape, lambda b, i: (0,) * a.ndim)
    one = jax.ShapeDtypeStruct((bn, s, B_WIDTH), F32)
    two = jax.ShapeDtypeStruct((2, bn, s, B_WIDTH), F32)
    kern = functools.partial(_rwkv_prep_kernel, ctx_tiles=ctx_len // TM, n_tiles=n_tiles)
    params = (mu, w0, w2bd, a0, a2bd, g2, k_k, k_a, r_k)
    return pl.pallas_call(
        kern,
        grid=(bn, n_tiles),
        in_specs=[row(N_B),
                  pl.BlockSpec((None, 8, N_B), lambda b, i: (b, jnp.maximum(i * eight - 1, 0), 0)),
                  pl.BlockSpec((None, 8, N_B), lambda b, i: (b, jnp.minimum((i + 1) * eight, s // 8 - 1), 0)),
                  ] + [full(p) for p in params],
        out_specs=[row(B_WIDTH), row(B_WIDTH), row(B_WIDTH), drow, drow, drow, row(B_WIDTH), row(B_WIDTH)],
        out_shape=[one, one, one, two, two, two, one, one],
        compiler_params=_params(("parallel", "parallel")),
        name="rwkv_prep",
    )(pb, pb, pb, *params)


def _pair_diag(x):
    lo = lax.broadcasted_iota(jnp.int32, (1, LANES), 1) < B_HEAD
    z = jnp.zeros_like(x)
    return jnp.concatenate([jnp.where(lo, x, z), jnp.where(lo, z, x)], axis=0)


def _pair_pick(x):
    lo = lax.broadcasted_iota(jnp.int32, (1, LANES), 1) < B_HEAD
    return jnp.where(lo, x[:B_HEAD], x[B_HEAD:])


def _rwkv_pair_kernel(rf_ref, vf_ref, kkf_ref, rb_ref, vb_ref, kkb_ref,
                      ldf_ref, kdf_ref, betaf_ref, ldb_ref, kdb_ref, betab_ref,
                      yf_ref, yb_ref, h_ref):
    c = pl.program_id(1)

    @pl.when(c == 0)
    def _():
        h_ref[...] = jnp.zeros_like(h_ref)

    n_pairs = B_WIDTH // LANES
    ti = lax.broadcasted_iota(jnp.int32, (CHUNK, LANES), 0)
    si = lax.broadcasted_iota(jnp.int32, (CHUNK, LANES), 1) % CHUNK
    t64 = lax.broadcasted_iota(jnp.int32, (CHUNK, CHUNK), 0)
    s64 = lax.broadcasted_iota(jnp.int32, (CHUNK, CHUNK), 1)
    eye = ti == si
    dirs = ((rf_ref, vf_ref, kkf_ref, ldf_ref, kdf_ref, betaf_ref, False),
            (rb_ref, vb_ref, kkb_ref, ldb_ref, kdb_ref, betab_ref, True))

    units = []
    n_rows = rf_ref.shape[0]
    for bi in range(n_rows):
      for d, (r_ref, v_ref, kk_ref, ld_ref, kd_ref, beta_ref, rev) in enumerate(dirs):
        strict = (si > ti) if rev else (ti > si)
        incl = (si >= ti) if rev else (ti >= si)
        tri = ((s64 >= t64) if rev else (t64 >= s64)).astype(BF16)
        ld_all = ld_ref[bi]
        cl_all = _dot_exact_lhs(tri, ld_all)
        for p in range(n_pairs):
            sl = slice(p * LANES, (p + 1) * LANES)
            ld, cl = ld_all[:, sl], cl_all[:, sl]
            total = jnp.sum(ld, axis=0, keepdims=True)
            inv_gam = jnp.exp(-cl)
            to_end = jnp.exp(total - cl)
            kk, kd, beta = kk_ref[bi, :, sl], kd_ref[bi, :, sl], beta_ref[bi, :, sl]
            units.append(dict(
                bi=bi, d=d, p=p, sl=sl, strict=strict, incl=incl, v=v_ref[bi, :, sl].astype(BF16),
                a_bar=-kk * jnp.exp(cl - ld), r_bar=r_ref[bi, :, sl] * jnp.exp(cl),
                b_til=(beta * inv_gam).astype(BF16), k_til=(kd * inv_gam).astype(BF16),
                b_hat=(beta * to_end).astype(BF16), k_hat=(kd * to_end).astype(BF16),
                gam_c=jnp.exp(total)))

    for u in units:
        x_mat = jnp.concatenate([u["a_bar"], u["r_bar"]], axis=0).astype(BF16)
        rhs = jnp.concatenate([_pair_diag(u["b_til"]), _pair_diag(u["k_til"])], axis=0)
        xbk = _dot(x_mat, rhs, NT)
        u["n_ab"] = jnp.where(u["strict"], xbk[:CHUNK, :LANES], 0.0)
        u["l_rb"] = jnp.where(u["incl"], xbk[CHUNK:, :LANES], 0.0)
        n_ak = jnp.where(u["strict"], xbk[:CHUNK, LANES:], 0.0)
        l_rk = jnp.where(u["incl"], xbk[CHUNK:, LANES:], 0.0)
        u["nl"] = jnp.concatenate([n_ak, l_rk], axis=0).astype(BF16)
    for u in units:
        nv = _dot(u["nl"], _pair_diag(u["v"]))
        u["w"], u["u0"], u["lrkv"] = u["a_bar"], nv[:CHUNK], nv[CHUNK:]
        u["npow"] = u["n_ab"].astype(BF16)

    steps = int(math.log2(CHUNK))
    for kstep in range(steps):
        for u in units:
            rhs = jnp.concatenate([_pair_diag(u["w"].astype(BF16)), _pair_diag(u["u0"].astype(BF16))], axis=1)
            upd = _dot(u["npow"], rhs)
            u["w"] = u["w"] + upd[:, :LANES]
            u["u0"] = u["u0"] + upd[:, LANES:]
        if kstep + 1 < steps:
            for u in units:
                u["npow"] = _dot(u["npow"], _pair_diag(u["npow"])).astype(BF16)

    for u in units:
        wb, ub = u["w"].astype(BF16), u["u0"].astype(BF16)
        lx = _dot(u["l_rb"].astype(BF16), jnp.concatenate([_pair_diag(wb), _pair_diag(ub)], axis=1))
        u["p_mat"] = u["r_bar"] + lx[:, :LANES]
        u["y0"] = u["lrkv"] + lx[:, LANES:]
        lhs = jnp.concatenate([u["b_hat"], u["k_hat"]], axis=0)
        rhs = jnp.concatenate([jnp.concatenate([wb, ub], axis=1),
                               jnp.concatenate([jnp.zeros_like(wb), u["v"]], axis=1)], axis=0)
        mg = _dot(lhs, rhs, TN)
        u["m_full"] = _pair_pick(mg[:, :LANES]) + jnp.where(eye, u["gam_c"], 0.0)
        u["g_mat"] = _pair_pick(mg[:, LANES:])

    for u in units:
        bi, d, p = u["bi"], u["d"], u["p"]
        h0 = h_ref[bi, d, p]
        a0, a1 = _split(jnp.concatenate([u["p_mat"], u["m_full"]], axis=0), 2)
        h_hi, h_lo = _split(h0, 2)
        bh, bl = _pair_diag(h_hi), _pair_diag(h_lo)
        out = _dot(a0, bh) + (_dot(a0, bl) + _dot(a1, bh))
        (yb_ref if d else yf_ref)[bi, :, u["sl"]] = out[:CHUNK] + u["y0"]
        h_ref[bi, d, p] = out[CHUNK:] + u["g_mat"]


def _rwkv_pairs(r, v, kk, ld, kd, beta, ctx_len):
    bn, s, _ = r.shape
    n_chunks = s // CHUNK
    ctx_chunks = ctx_len // CHUNK
    rows = RWKV_ROWS if bn % RWKV_ROWS == 0 else 1

    def back(c):
        return jnp.where(c < ctx_chunks, ctx_chunks - 1 - c, n_chunks - 1 + ctx_chunks - c)

    fwd = pl.BlockSpec((rows, CHUNK, B_WIDTH), lambda b, c: (b, c, 0))
    bwd = pl.BlockSpec((rows, CHUNK, B_WIDTH), lambda b, c: (b, back(c), 0))
    fwd_d = pl.BlockSpec((None, rows, CHUNK, B_WIDTH), lambda b, c: (0, b, c, 0))
    bwd_d = pl.BlockSpec((None, rows, CHUNK, B_WIDTH), lambda b, c: (1, b, back(c), 0))
    y = jax.ShapeDtypeStruct((bn, s, B_WIDTH), F32)
    return pl.pallas_call(
        _rwkv_pair_kernel,
        grid=(bn // rows, n_chunks),
        in_specs=[fwd, fwd, fwd, bwd, bwd, bwd, fwd_d, fwd_d, fwd_d, bwd_d, bwd_d, bwd_d],
        out_specs=[fwd, bwd],
        out_shape=[y, y],
        scratch_shapes=[pltpu.VMEM((rows, 2, B_WIDTH // LANES, B_HEAD, LANES), F32)],
        compiler_params=_params(("parallel", "arbitrary")),
        name="rwkv_scan",
    )(r, v, kk, r, v, kk, ld, kd, beta, ld, kd, beta)


def _outproj_kernel(x_ref, mod_ref, a_ref, yf_ref, yb_ref, bonus_ref, g_ref, c_ref,
                    wa_ref, wb_ref, wc_ref, lnxg_ref, lnxb_ref, ln1g_ref, ln1b_ref, o_ref, *, alpha):
    nb = x_ref.shape[0]
    stack = lambda ref: jnp.concatenate([ref[r] for r in range(nb)], axis=0)
    y = stack(yf_ref) + stack(yb_ref) + stack(bonus_ref)
    ones = _group_ones(B_WIDTH, B_HEAD)
    inv = 1.0 / B_HEAD
    mu = _dot_exact_rhs(y, ones) * inv
    yc = y - mu
    var = _dot_exact_rhs(yc * yc, ones) * inv
    yn = yc * lax.rsqrt(var + GN_EPS) * lnxg_ref[...] + lnxb_ref[...]
    bmix = (yn * stack(g_ref)).astype(BF16)
    o = _dot(stack(a_ref), wa_ref[...]) + _dot(bmix, wb_ref[...]) + _dot(stack(c_ref), wc_ref[...])
    for r in range(nb):
        o_ref[r] = (_ln(alpha * x_ref[r] + mod_ref[r, 2:3] * o[r * TM:(r + 1) * TM], LN_EPS) * ln1g_ref[...]
                    + ln1b_ref[...])


def _outproj(xs, mods, a_out, yf, yb, bonus, g, c_out, wa, wb, wc, lnxg, lnxb, ln1g, ln1b, ctx_len, alpha):
    bn, s, d = xs.shape
    nb = _batch_rows(bn)
    row = lambda width: pl.BlockSpec((nb, TM, width), lambda b, i: (b, i, 0))
    full = lambda a: pl.BlockSpec(a.shape, lambda b, i: (0,) * a.ndim)
    consts = (wa, wb, wc, lnxg, lnxb, ln1g, ln1b)
    return pl.pallas_call(
        functools.partial(_outproj_kernel, alpha=alpha),
        grid=(bn // nb, s // TM),
        in_specs=[row(d), _mod_spec(ctx_len, d, nb),
                  row(A_WIDTH), row(B_WIDTH), row(B_WIDTH), row(B_WIDTH), row(B_WIDTH), row(C_WIDTH)]
                 + [full(p) for p in consts],
        out_specs=row(d),
        out_shape=jax.ShapeDtypeStruct((bn, s, d), F32),
        compiler_params=_params(("parallel", "parallel")),
        name="outproj",
    )(xs, mods, a_out, yf, yb, bonus, g, c_out, *consts)


def _swiglu_rows(h, w1_ref, w3_ref, w2_ref):
    acc = jnp.zeros((h.shape[0], w2_ref.shape[-1]), F32)
    for j in range(w1_ref.shape[-1] // FF_CHUNK):
        sl = slice(j * FF_CHUNK, (j + 1) * FF_CHUNK)
        u = _dot(h, w1_ref[:, sl])
        t = _dot(h, w3_ref[:, sl])
        acc = acc + _dot((_silu(u) * t).astype(BF16), w2_ref[sl, :])
    return acc


def _ffn_kernel(x_ref, mod_ref, w1_ref, w3_ref, w2_ref, g_ref, b_ref, o_ref, *, alpha):
    nb = x_ref.shape[0]
    h = jnp.concatenate([(_ln(x_ref[r], LN_EPS) * (1.0 + mod_ref[r, 4:5]) + mod_ref[r, 3:4]).astype(BF16)
                         for r in range(nb)], axis=0)
    f = _swiglu_rows(h, w1_ref, w3_ref, w2_ref)
    for r in range(nb):
        o_ref[r] = (_ln(alpha * x_ref[r] + mod_ref[r, 5:6] * f[r * TM:(r + 1) * TM], LN_EPS) * g_ref[...]
                    + b_ref[...])


def _ffn(x1, mods, w1, w3, w2, g, b, ctx_len, alpha):
    bn, s, d = x1.shape
    nb = _batch_rows(bn)
    row = pl.BlockSpec((nb, TM, d), lambda bb, i: (bb, i, 0))
    resident = lambda a: pl.BlockSpec(a.shape, lambda bb, i: (0,) * a.ndim, pipeline_mode=pl.Buffered(1))
    full = lambda a: pl.BlockSpec(a.shape, lambda bb, i: (0,) * a.ndim)
    return pl.pallas_call(
        functools.partial(_ffn_kernel, alpha=alpha),
        grid=(bn // nb, s // TM),
        in_specs=[row, _mod_spec(ctx_len, d, nb),
                  resident(w1), resident(w3), resident(w2), full(g), full(b)],
        out_specs=row,
        out_shape=jax.ShapeDtypeStruct((bn, s, d), F32),
        compiler_params=_params(("parallel", "parallel")),
        name="ffn",
    )(x1, mods, w1, w3, w2, g, b)


def _moe_pre_kernel(x_ref, mod_ref, router_ref, h_ref, logit_ref):
    mod = mod_ref[...]
    h = _ln(x_ref[...], LN_EPS) * (1.0 + mod[4:5]) + mod[3:4]
    h_ref[...] = h.astype(h_ref.dtype)
    logit_ref[...] = _dot6(h, router_ref[...])


def _moe_pre(x1, mods, router_p, ctx_len):
    bn, s, d = x1.shape
    ct = ctx_len // TM
    lt = (s - ctx_len) // TM
    return pl.pallas_call(
        _moe_pre_kernel,
        grid=(bn, lt),
        in_specs=[pl.BlockSpec((None, TM, d), lambda b, i: (b, i + ct, 0)),
                  pl.BlockSpec((None, None, 6, d), lambda b, i: (b, 1, 0, 0)),
                  pl.BlockSpec(router_p.shape, lambda b, i: (0, 0))],
        out_specs=[pl.BlockSpec((TM, d), lambda b, i: (b * lt + i, 0)),
                   pl.BlockSpec((TM, LANES), lambda b, i: (b * lt + i, 0))],
        out_shape=[jax.ShapeDtypeStruct((bn * lt * TM, d), BF16),
                   jax.ShapeDtypeStruct((bn * lt * TM, LANES), F32)],
        compiler_params=_params(("parallel", "parallel")),
        name="moe_pre",
    )(x1, mods, router_p)


def _slot_onehot(slots_ref, block):
    sl = slots_ref[...]
    s_iota = lax.broadcasted_iota(jnp.int32, (MOE_ROWS, MOE_TILE), 0) + block * MOE_ROWS
    hit = jnp.logical_or(sl[0:1] == s_iota, sl[1:2] == s_iota)
    return jnp.where(hit, 1.0, 0.0).astype(BF16)


def _dispatch_kernel(wb_ref, wc_ref, wf_ref, wv_ref, slots_ref, h_ref, o_ref):
    w = pl.program_id(0)
    part = lambda: _dot(_slot_onehot(slots_ref, wb_ref[w]), h_ref[...])

    @pl.when(wf_ref[w] == 1)
    def _():
        o_ref[...] = part().astype(o_ref.dtype)

    @pl.when(jnp.logical_and(wf_ref[w] == 0, wv_ref[w] == 1))
    def _():
        o_ref[...] = (o_ref[...].astype(F32) + part()).astype(o_ref.dtype)


def _dispatch(h, slots, work):
    t, d = h.shape
    wb, wc, wf, wv = work
    n_slots = (t * TOP_K // MOE_ROWS + N_EXPERTS) * MOE_ROWS
    grid_spec = pltpu.PrefetchScalarGridSpec(
        num_scalar_prefetch=4,
        grid=(wb.shape[0],),
        in_specs=[pl.BlockSpec((TOP_K, MOE_TILE), lambda w, b, c, f, v: (0, c[w])),
                  pl.BlockSpec((MOE_TILE, d), lambda w, b, c, f, v: (c[w], 0))],
        out_specs=pl.BlockSpec((MOE_ROWS, d), lambda w, b, c, f, v: (b[w], 0)),
    )
    return pl.pallas_call(
        _dispatch_kernel,
        grid_spec=grid_spec,
        out_shape=jax.ShapeDtypeStruct((n_slots, d), BF16),
        compiler_params=_params(("arbitrary",)),
        name="moe_dispatch",
    )(wb, wc, wf, wv, slots, h)


def _collect_kernel(wb_ref, wc_ref, wf_ref, wv_ref, slots_ref, scol_ref, gate_ref, y_ref, o_ref):
    w = pl.program_id(0)

    def part():
        base = wb_ref[w] * MOE_ROWS
        scol = scol_ref[...] - base
        gates = jnp.where(jnp.logical_and(scol >= 0, scol < MOE_ROWS), gate_ref[...], 0.0)
        gate = jnp.sum(gates, axis=1, keepdims=True)
        return _dot(_slot_onehot(slots_ref, wb_ref[w]), y_ref[...], TN) * gate

    @pl.when(wf_ref[w] == 1)
    def _():
        o_ref[...] = part()

    @pl.when(jnp.logical_and(wf_ref[w] == 0, wv_ref[w] == 1))
    def _():
        o_ref[...] = o_ref[...] + part()


def _collect(y_buf, slots, slots_col, gates, work):
    n_slots, d = y_buf.shape
    t = slots.shape[1]
    wb, wc, wf, wv = work
    tile = lambda w, b, c, f, v: (c[w], 0)
    grid_spec = pltpu.PrefetchScalarGridSpec(
        num_scalar_prefetch=4,
        grid=(wb.shape[0],),
        in_specs=[pl.BlockSpec((TOP_K, MOE_TILE), lambda w, b, c, f, v: (0, c[w])),
                  pl.BlockSpec((MOE_TILE, TOP_K), tile), pl.BlockSpec((MOE_TILE, TOP_K), tile),
                  pl.BlockSpec((MOE_ROWS, d), lambda w, b, c, f, v: (b[w], 0))],
        out_specs=pl.BlockSpec((MOE_TILE, d), tile),
    )
    return pl.pallas_call(
        _collect_kernel,
        grid_spec=grid_spec,
        out_shape=jax.ShapeDtypeStruct((t, d), F32),
        compiler_params=_params(("arbitrary",)),
        name="moe_collect",
    )(wb, wc, wf, wv, slots, slots_col, gates, y_buf)


def _expert_kernel(be_ref, nb_ref, x_ref, w1_ref, w3_ref, w2_ref, o_ref):
    i = pl.program_id(0)

    @pl.when(i < nb_ref[0])
    def _():
        o_ref[...] = _swiglu_rows(x_ref[...], w1_ref, w3_ref, w2_ref).astype(o_ref.dtype)

    @pl.when(i >= nb_ref[0])
    def _():
        o_ref[...] = jnp.zeros_like(o_ref)


def _experts(buf, block_e, n_used, w1, w3, w2):
    n, d = buf.shape
    n_blocks = n // MOE_ROWS
    ff = w1.shape[-1]
    wspec = lambda shape: pl.BlockSpec((None,) + shape, lambda i, be, nb: (be[i], 0, 0),
                                       pipeline_mode=pl.Buffered(1))
    used = lambda i, be, nb: (jnp.minimum(i, nb[0] - 1), 0)
    grid_spec = pltpu.PrefetchScalarGridSpec(
        num_scalar_prefetch=2,
        grid=(n_blocks,),
        in_specs=[pl.BlockSpec((MOE_ROWS, d), used), wspec((d, ff)), wspec((d, ff)), wspec((ff, d))],
        out_specs=pl.BlockSpec((MOE_ROWS, d), lambda i, be, nb: (i, 0)),
    )
    return pl.pallas_call(
        _expert_kernel,
        grid_spec=grid_spec,
        out_shape=jax.ShapeDtypeStruct((n, d), BF16),
        compiler_params=_params(("arbitrary",)),
        name="experts",
    )(block_e, n_used, buf, w1, w3, w2)


def _combine_kernel(x_ref, mod_ref, y_ref, g_ref, b_ref, o_ref, *, alpha):
    mod = mod_ref[...]
    o_ref[...] = _ln(alpha * x_ref[...] + mod[5:6] * y_ref[...], LN_EPS) * g_ref[...] + b_ref[...]


def _combine(x1, mods, y_tok, g, b, ctx_len, alpha):
    bn, s, d = x1.shape
    ct = ctx_len // TM
    lt = (s - ctx_len) // TM
    full = lambda a: pl.BlockSpec(a.shape, lambda bb, i: (0,) * a.ndim)
    return pl.pallas_call(
        functools.partial(_combine_kernel, alpha=alpha),
        grid=(bn, lt),
        in_specs=[pl.BlockSpec((None, TM, d), lambda bb, i: (bb, i + ct, 0)),
                  pl.BlockSpec((None, None, 6, d), lambda bb, i: (bb, 1, 0, 0)),
                  pl.BlockSpec((TM, d), lambda bb, i: (bb * lt + i, 0)),
                  full(g), full(b)],
        out_specs=pl.BlockSpec((None, TM, d), lambda bb, i: (bb, i, 0)),
        out_shape=jax.ShapeDtypeStruct((bn, lt * TM, d), F32),
        compiler_params=_params(("parallel", "parallel")),
        name="moe_combine",
    )(x1, mods, y_tok, g, b)


def _work_lists(rank_at_tile, counts, pstart, block_e, n_used, n_blocks, n_tiles):
    n_work = n_tiles * N_EXPERTS + n_blocks
    blocks = jnp.arange(n_blocks, dtype=jnp.int32)
    used = blocks < n_used
    r0 = blocks * MOE_ROWS - pstart[block_e]
    r_last = jnp.minimum(r0 + MOE_ROWS, counts[block_e]) - 1
    cols = rank_at_tile.T[block_e]
    find = jax.vmap(lambda col, val: jnp.searchsorted(col, val, side="right"))
    lo = jnp.clip(find(cols, r0) - 1, 0, n_tiles - 1)
    hi = jnp.clip(find(cols, r_last) - 1, 0, n_tiles - 1)
    n_b = jnp.where(used, hi - lo + 1, 0)
    ends = jnp.cumsum(n_b)
    starts = ends - n_b
    total = ends[-1]
    w = jnp.arange(n_work, dtype=jnp.int32)
    valid = w < total
    wl = jnp.minimum(w, total - 1)
    blk = jnp.minimum(jnp.searchsorted(ends, wl, side="right"), n_blocks - 1).astype(jnp.int32)
    tile = (lo[blk] + (wl - starts[blk])).astype(jnp.int32)
    first = jnp.logical_and(valid, w == starts[blk])
    as_i32 = lambda a: a.astype(jnp.int32)
    by_block = (blk, tile, as_i32(first), as_i32(valid))
    order = jnp.argsort(jnp.where(valid, tile, n_tiles), stable=True)
    order = order[jnp.minimum(w, total - 1)]
    tile2, blk2 = tile[order], blk[order]
    first2 = jnp.logical_and(valid, jnp.concatenate([jnp.ones((1,), bool), tile2[1:] != tile2[:-1]]))
    by_tile = (blk2, tile2, as_i32(first2), as_i32(valid))
    return by_block, by_tile


def _moe_layer(x1, mods, router, w1, w3, w2, g, b, ctx_len, alpha):
    d = x1.shape[-1]
    router_p = jnp.pad(router, ((0, 0), (0, LANES - N_EXPERTS)))
    h, logits = _moe_pre(x1, mods, router_p, ctx_len)
    t = h.shape[0]
    top_v, top_i = lax.top_k(logits[:, :N_EXPERTS], TOP_K)
    gates = jax.nn.softmax(top_v, axis=-1)
    e_flat = top_i.reshape(-1)
    onehot = (e_flat[:, None] == jnp.arange(N_EXPERTS)[None, :]).astype(jnp.int32)
    ranks = jnp.cumsum(onehot, axis=0) - onehot
    rank = jnp.sum(ranks * onehot, axis=1)
    counts = jnp.sum(onehot, axis=0)
    padded = (counts + MOE_ROWS - 1) // MOE_ROWS * MOE_ROWS
    pend = jnp.cumsum(padded)
    pstart = pend - padded
    slot = (pstart[e_flat] + rank).astype(jnp.int32)
    n_blocks = t * TOP_K // MOE_ROWS + N_EXPERTS
    assert t % MOE_TILE == 0
    n_tiles = t // MOE_TILE
    block_e = jnp.minimum(jnp.searchsorted(pend, jnp.arange(n_blocks) * MOE_ROWS, side="right"),
                          N_EXPERTS - 1).astype(jnp.int32)
    n_used = (pend[-1:] // MOE_ROWS).astype(jnp.int32)
    rank_at_tile = jnp.concatenate([ranks[::MOE_TILE * TOP_K], counts[None, :]], axis=0)
    by_block, by_tile = _work_lists(rank_at_tile, counts, pstart, block_e, n_used, n_blocks, n_tiles)
    slots_col = slot.reshape(t, TOP_K)
    slots = slots_col.T
    buf = _dispatch(h, slots, by_block)
    y_buf = _experts(buf, block_e, n_used, w1, w3, w2)
    y_tok = _collect(y_buf, slots, slots_col, gates, by_tile)
    return _combine(x1, mods, y_tok, g, b, ctx_len, alpha)


def _rope_tables(n_rows, ctx_len, dim, lane_lo):
    quarter = dim // 4
    inv = ROPE_THETA ** (-jnp.arange(quarter, dtype=F32) / quarter)
    rows = jnp.repeat(jnp.arange(n_rows, dtype=F32), GRID_W)
    cols = jnp.tile(jnp.arange(GRID_W, dtype=F32), n_rows)
    ang = jnp.concatenate([rows[:, None] * inv, rows[:, None] * inv,
                           cols[:, None] * inv, cols[:, None] * inv], axis=-1)
    sign = jnp.tile(jnp.concatenate([-jnp.ones(quarter, F32), jnp.ones(quarter, F32)]), 2)
    cos, sin = jnp.cos(ang), jnp.sin(ang) * sign
    length = cos.shape[0]
    if lane_lo == 0:
        reps = LANES // dim
        cos, sin = jnp.tile(cos, (1, reps)), jnp.tile(sin, (1, reps))
    else:
        pad = ((0, 0), (lane_lo, LANES - lane_lo - dim))
        cos = jnp.pad(cos, pad, constant_values=1.0)
        sin = jnp.pad(sin, pad)
    cos = jnp.concatenate([jnp.ones((ctx_len, LANES), F32), cos], axis=0)
    sin = jnp.concatenate([jnp.zeros((ctx_len, LANES), F32), sin], axis=0)
    return cos, sin


def _block_diag2(w):
    z = jnp.zeros_like(w[0])
    return jnp.concatenate([jnp.concatenate([w[0], z], axis=1), jnp.concatenate([z, w[1]], axis=1)], axis=0)


def _lambda_init(layer):
    return 0.8 - 0.6 * math.exp(-0.3 * layer)


def kernel(x, c, ctx, c_ctx, ada_w, ada_b, w_in, w_out, ln1_g, ln1_b, ln2_g, ln2_b,
           lam_q1, lam_k1, lam_q2, lam_k2, diff_norm_g, shift_mu, w0, w2, a0, a2, g2,
           k_k, k_a, r_k, lnx_g, lnx_b, q_norm_g, w_uq, kv_norm_g, w_ukv,
           ff_w1, ff_w3, ff_w2, router, moe_w1, moe_w3, moe_w2):
    bn, seq, d = x.shape
    ctx_len = ctx.shape[1]
    depth = ada_w.shape[0]
    assert d == D_MODEL and seq % TM == 0 and ctx_len % TM == 0 and seq % GRID_W == 0
    alpha = (2.0 * depth) ** 0.25
    n_grid_rows = seq // GRID_W
    cos_a, sin_a = _rope_tables(n_grid_rows, ctx_len, A_QK_DIM, 0)
    cos_c, sin_c = _rope_tables(n_grid_rows, ctx_len, C_ROPE, C_NOPE)

    cond_rows = 8 * ((bn + 1 + 7) // 8)
    cond = jnp.zeros((cond_rows, d), F32).at[:bn].set(c).at[bn].set(c_ctx)
    xs = jnp.concatenate([ctx, x], axis=1)

    for i in range(depth):
        with_ctx = i < depth - 1
        m = _ada(cond, ada_w[i], ada_b[i]).reshape(cond_rows, 6, d)
        mods = jnp.stack([jnp.broadcast_to(m[bn], (bn, 6, d)), m[:bn]], axis=1)

        wc = w_in[i][:, N_A + N_B:]
        kpe_w = jnp.pad(wc[:, C_Q_RANK + C_KV_RANK:], ((0, 0), (C_NOPE, LANES - C_NOPE - C_ROPE)))
        w_p = jnp.concatenate([w_in[i][:, :N_A + N_B], wc[:, :C_Q_RANK + C_KV_RANK], kpe_w], axis=1).astype(BF16)
        qa, ka, va, pb, pc = _inproj(xs, mods, w_p, cos_a, sin_a, ctx_len)

        lam_p = jnp.stack([lam_q1[i], lam_k1[i], lam_q2[i], lam_k2[i]])
        g_col = jnp.broadcast_to(diff_norm_g[i][:, None], (A_V_DIM, TM))
        a_out = _attention(qa, ka, va, lam_p, g_col, mode="diff", ctx_len=ctx_len, lam_init=_lambda_init(i))

        wq = w_uq[i].reshape(C_Q_RANK, C_HEADS, C_NOPE + C_ROPE)
        wq_p = jnp.pad(wq, ((0, 0), (0, 0), (0, LANES - C_NOPE - C_ROPE))).reshape(C_Q_RANK, -1).astype(BF16)
        wkv = w_ukv[i].reshape(C_KV_RANK, C_HEADS, C_NOPE + C_V)
        wk_p = jnp.pad(wkv[:, :, :C_NOPE], ((0, 0), (0, 0), (0, LANES - C_NOPE))).reshape(C_KV_RANK, -1).astype(BF16)
        wv_p = wkv[:, :, C_NOPE:].reshape(C_KV_RANK, -1).astype(BF16)
        qc, kc, vc = _mla_prep(pc, q_norm_g[i].reshape(1, -1), kv_norm_g[i].reshape(1, -1),
                               wq_p, wk_p, wv_p, cos_c, sin_c)
        c_out = _attention(qc, kc, vc, lam_p, g_col, mode="mla", ctx_len=ctx_len)

        r, v, kkn, ld, kd, beta, gate, bonus = _rwkv_prep(
            pb, shift_mu[i].reshape(1, -1), w0[i].reshape(1, -1), _block_diag2(w2[i]).astype(BF16),
            a0[i].reshape(1, -1), _block_diag2(a2[i]).astype(BF16), g2[i].astype(BF16),
            k_k[i].reshape(1, -1), k_a[i].reshape(1, -1), r_k[i].reshape(1, -1), ctx_len)
        yf, yb = _rwkv_pairs(r, v, kkn, ld, kd, beta, ctx_len)

        wo = w_out[i].astype(BF16)
        x1 = _outproj(xs, mods, a_out, yf, yb, bonus, gate, c_out,
                      wo[:A_WIDTH], wo[A_WIDTH:A_WIDTH + B_WIDTH], wo[A_WIDTH + B_WIDTH:],
                      lnx_g[i].reshape(1, -1), lnx_b[i].reshape(1, -1),
                      ln1_g[i].reshape(1, -1), ln1_b[i].reshape(1, -1), ctx_len, alpha)

        j = i // 2
        g2n, b2n = ln2_g[i].reshape(1, -1), ln2_b[i].reshape(1, -1)
        if i % 2 == 0:
            xs = _ffn(x1, mods, ff_w1[j].astype(BF16), ff_w3[j].astype(BF16), ff_w2[j].astype(BF16),
                      g2n, b2n, ctx_len, alpha)
        else:
            if with_ctx:
                raise NotImplementedError("routed FFN on the context rows is not needed at this depth")
            return _moe_layer(x1, mods, router[j], moe_w1[j].astype(BF16), moe_w3[j].astype(BF16),
                              moe_w2[j].astype(BF16), g2n, b2n, ctx_len, alpha)
    return xs[:, ctx_len:]
```

```python
import functools
import math

import jax
import jax.numpy as jnp
from jax import lax
from jax.experimental import pallas as pl
from jax.experimental.pallas import tpu as pltpu

F32 = jnp.float32
BF16 = jnp.bfloat16

D_MODEL = 1024
GRID_W = 64
ROPE_THETA = 10000.0
A_HEADS, A_QK_DIM, A_V_DIM = 4, 64, 128
A_WIDTH = A_HEADS * A_V_DIM
A_QK_COLS = 2 * A_HEADS * A_QK_DIM
B_HEADS, B_HEAD = 4, 64
B_WIDTH = B_HEADS * B_HEAD
B_DECAY_RANK, B_A_RANK, B_GATE_RANK = 64, 64, 128
C_HEADS, C_NOPE, C_ROPE, C_V = 4, 64, 32, 64
C_WIDTH = C_HEADS * C_V
C_Q_RANK, C_KV_RANK = 256, 128
N_A = 2 * A_QK_COLS + A_WIDTH
N_B = 3 * B_WIDTH + 2 * B_DECAY_RANK + 2 * B_A_RANK + B_GATE_RANK
N_C = C_Q_RANK + C_KV_RANK + C_ROPE
N_C_PAD = C_Q_RANK + C_KV_RANK + 128
D_FF = 3584
N_EXPERTS = 8
TOP_K = 2
LN_EPS = 1e-6
RMS_EPS = 1e-6
GN_EPS = 64e-5

LOG2E = math.log2(math.e)
LANES = 128
TM = 256
CHUNK = 64
ROWS_B = 2
RWKV_ROWS = 2
ATT_TK = 1024
ONES_ROWS = 16
ATT_PAR = 4
MOE_ROWS = 512
MOE_TILE = 512
FF_CHUNK = 512
VMEM_LIMIT = 56 * 1024 * 1024

NN = (((1,), (0,)), ((), ()))
NT = (((1,), (1,)), ((), ()))
TN = (((0,), (0,)), ((), ()))


def _params(sem, vmem=VMEM_LIMIT, flags=None):
    return pltpu.CompilerParams(dimension_semantics=sem, vmem_limit_bytes=vmem, flags=flags)


def _split(x, n):
    parts, r = [], x
    for _ in range(n):
        p = r.astype(BF16)
        parts.append(p)
        r = r - p.astype(F32)
    return parts


def _dot(a, b, dn=NN):
    return lax.dot_general(a, b, dn, preferred_element_type=F32)


def _dot1(a, b, dn=NN):
    return _dot(a.astype(BF16), b.astype(BF16), dn)


def _dot3(a, b, dn=NN):
    a0, a1 = _split(a, 2)
    b0, b1 = _split(b, 2)
    return _dot(a0, b0, dn) + (_dot(a0, b1, dn) + _dot(a1, b0, dn))


def _dot6(a, b, dn=NN):
    a0, a1, a2 = _split(a, 3)
    b0, b1, b2 = _split(b, 3)
    lo = _dot(a1, b1, dn) + (_dot(a0, b2, dn) + _dot(a2, b0, dn))
    return _dot(a0, b0, dn) + ((_dot(a0, b1, dn) + _dot(a1, b0, dn)) + lo)


def _dot_exact_lhs(a_bf16, b, dn=NN, n=3):
    parts = _split(b, n)
    out = _dot(a_bf16, parts[-1], dn)
    for p in parts[-2::-1]:
        out = out + _dot(a_bf16, p, dn)
    return out


def _dot_exact_rhs(a, b_bf16, dn=NN, n=3):
    parts = _split(a, n)
    out = _dot(parts[-1], b_bf16, dn)
    for p in parts[-2::-1]:
        out = out + _dot(p, b_bf16, dn)
    return out


def _ln(x, eps):
    mu = jnp.mean(x, axis=-1, keepdims=True)
    xc = x - mu
    return xc * lax.rsqrt(jnp.mean(xc * xc, axis=-1, keepdims=True) + eps)


def _sigmoid(x):
    return 1.0 / (1.0 + jnp.exp(-x))


def _silu(x):
    return x * _sigmoid(x)


def _group_ones(width, group):
    r = lax.broadcasted_iota(jnp.int32, (width, width), 0) // group
    c = lax.broadcasted_iota(jnp.int32, (width, width), 1) // group
    return (r == c).astype(BF16)


def _partner(x, half):
    lane = lax.broadcasted_iota(jnp.int32, x.shape, 1)
    up = pltpu.roll(x, LANES - half, 1)
    dn = pltpu.roll(x, half, 1)
    return jnp.where((lane % (2 * half)) < half, up, dn)


def _rope(x, cos, sin, half):
    return x * cos + _partner(x, half) * sin


def _ada_kernel(c_ref, w_ref, b_ref, o_ref):
    o_ref[...] = _dot3(_silu(c_ref[...]), w_ref[...]) + b_ref[...]


def _ada(cond, w, b):
    rows, d = cond.shape
    n = w.shape[1]
    tn = 1536
    return pl.pallas_call(
        _ada_kernel,
        grid=(n // tn,),
        in_specs=[pl.BlockSpec((rows, d), lambda j: (0, 0)),
                  pl.BlockSpec((d, tn), lambda j: (0, j)),
                  pl.BlockSpec((1, tn), lambda j: (0, j))],
        out_specs=pl.BlockSpec((rows, tn), lambda j: (0, j)),
        out_shape=jax.ShapeDtypeStruct((rows, n), F32),
        compiler_params=_params(("parallel",)),
        name="ada",
    )(cond, w, b.reshape(1, n))


def _inproj_kernel(x_ref, mod_ref, w_ref, cos_ref, sin_ref,
                   q_ref, k_ref, v_ref, pb_ref, pc_ref):
    nb = x_ref.shape[0]
    h = jnp.concatenate([(_ln(x_ref[r], LN_EPS) * (1.0 + mod_ref[r, 1:2]) + mod_ref[r, 0:1]).astype(BF16)
                         for r in range(nb)], axis=0)
    cos, sin = cos_ref[...], sin_ref[...]
    scale = A_QK_DIM ** -0.5 * LOG2E
    rows = lambda a, r: a[r * TM:(r + 1) * TM]
    for j in range(A_QK_COLS // LANES):
        sl = slice(j * LANES, (j + 1) * LANES)
        qj = _dot(h, w_ref[:, sl])
        kj = _dot(h, w_ref[:, A_QK_COLS + j * LANES:A_QK_COLS + (j + 1) * LANES])
        vj = _dot(h, w_ref[:, 2 * A_QK_COLS + j * LANES:2 * A_QK_COLS + (j + 1) * LANES])
        for r in range(nb):
            q_ref[r, sl, :] = (_rope(rows(qj, r), cos, sin, A_QK_DIM // 4) * scale).T.astype(BF16)
            k_ref[r, :, sl] = _rope(rows(kj, r), cos, sin, A_QK_DIM // 4).astype(BF16)
            v_ref[r, sl, :] = rows(vj, r).T.astype(BF16)
    pb = _dot(h, w_ref[:, N_A:N_A + N_B])
    pc = _dot(h, w_ref[:, N_A + N_B:])
    for r in range(nb):
        pb_ref[r] = rows(pb, r)
        pc_ref[r] = rows(pc, r)


def _batch_rows(bn):
    return ROWS_B if bn % ROWS_B == 0 else 1


def _mod_spec(ctx_len, d, nb=None):
    ct = ctx_len // TM
    return pl.BlockSpec((nb, None, 6, d), lambda b, i: (b, jnp.where(i >= ct, 1, 0), 0, 0))


def _inproj(xs, mods, w_p, cos_a, sin_a, ctx_len):
    bn, s, d = xs.shape
    n_tiles = s // TM
    n_w = w_p.shape[1]
    nb = _batch_rows(bn)
    row = lambda width: pl.BlockSpec((nb, TM, width), lambda b, i: (b, i, 0))
    col = lambda width: pl.BlockSpec((nb, width, TM), lambda b, i: (b, 0, i))
    outs = [jax.ShapeDtypeStruct((bn, A_QK_COLS, s), BF16),
            jax.ShapeDtypeStruct((bn, s, A_QK_COLS), BF16),
            jax.ShapeDtypeStruct((bn, A_WIDTH, s), BF16),
            jax.ShapeDtypeStruct((bn, s, N_B), F32),
            jax.ShapeDtypeStruct((bn, s, N_C_PAD), F32)]
    return pl.pallas_call(
        _inproj_kernel,
        grid=(bn // nb, n_tiles),
        in_specs=[row(d),
                  _mod_spec(ctx_len, d, nb),
                  pl.BlockSpec((d, n_w), lambda b, i: (0, 0)),
                  pl.BlockSpec((TM, LANES), lambda b, i: (i, 0)),
                  pl.BlockSpec((TM, LANES), lambda b, i: (i, 0))],
        out_specs=[col(A_QK_COLS), row(A_QK_COLS), col(A_WIDTH), row(N_B), row(N_C_PAD)],
        out_shape=outs,
        compiler_params=_params(("parallel", "parallel")),
        name="inproj",
    )(xs, mods, w_p, cos_a, sin_a)


def _mla_prep_kernel(pc_ref, qg_ref, kvg_ref, wq_ref, wk_ref, wv_ref, cos_ref, sin_ref,
                     q_ref, k_ref, v_ref):
    pc = pc_ref[...]
    cq = pc[:, :C_Q_RANK]
    cq = cq * lax.rsqrt(jnp.mean(cq * cq, axis=-1, keepdims=True) + RMS_EPS) * qg_ref[...]
    ckv = pc[:, C_Q_RANK:C_Q_RANK + C_KV_RANK]
    ckv = ckv * lax.rsqrt(jnp.mean(ckv * ckv, axis=-1, keepdims=True) + RMS_EPS) * kvg_ref[...]
    cos, sin = cos_ref[...], sin_ref[...]
    kpe = _rope(pc[:, C_Q_RANK + C_KV_RANK:], cos, sin, C_ROPE // 4)
    cqb, ckvb = cq.astype(BF16), ckv.astype(BF16)
    scale = (C_NOPE + C_ROPE) ** -0.5 * LOG2E
    for h in range(C_HEADS):
        sl = slice(h * LANES, (h + 1) * LANES)
        qh = _dot(cqb, wq_ref[:, sl])
        q_ref[sl, :] = (_rope(qh, cos, sin, C_ROPE // 4) * scale).T.astype(BF16)
        k_ref[:, sl] = (_dot(ckvb, wk_ref[:, sl]) + kpe).astype(BF16)
    for j in range(C_WIDTH // LANES):
        sl = slice(j * LANES, (j + 1) * LANES)
        v_ref[sl, :] = _dot(ckvb, wv_ref[:, sl]).T.astype(BF16)


def _mla_prep(pc, qg, kvg, wq_p, wk_p, wv_p, cos_c, sin_c):
    bn, s, _ = pc.shape
    row = lambda width: pl.BlockSpec((None, TM, width), lambda b, i: (b, i, 0))
    full = lambda a: pl.BlockSpec(a.shape, lambda b, i: (0,) * a.ndim)
    tab = pl.BlockSpec((TM, LANES), lambda b, i: (i, 0))
    col = lambda width: pl.BlockSpec((None, width, TM), lambda b, i: (b, 0, i))
    hw = C_HEADS * LANES
    return pl.pallas_call(
        _mla_prep_kernel,
        grid=(bn, s // TM),
        in_specs=[row(N_C_PAD), full(qg), full(kvg), full(wq_p), full(wk_p), full(wv_p), tab, tab],
        out_specs=[col(hw), row(hw), col(C_WIDTH)],
        out_shape=[jax.ShapeDtypeStruct((bn, hw, s), BF16),
                   jax.ShapeDtypeStruct((bn, s, hw), BF16),
                   jax.ShapeDtypeStruct((bn, C_WIDTH, s), BF16)],
        compiler_params=_params(("parallel", "parallel")),
        name="mla_prep",
    )(pc, qg, kvg, wq_p, wk_p, wv_p, cos_c, sin_c)


def _attn_kernel(lam_ref, g_ref, qt_ref, k_ref, vt_ref, o_ref, *, mode, ctx_tiles, ctx_len, tk, lam_init):
    qi = pl.program_id(2)
    tq = qt_ref.shape[1]
    n_par = qt_ref.shape[0] // LANES
    dv = vt_ref.shape[0] // n_par
    width = 2 * tq if mode == "diff" else tq
    half = lax.broadcasted_iota(jnp.int32, (LANES, 1), 0) < (LANES // 2)
    q_ops = []
    for g in range(n_par):
        qt = qt_ref[g * LANES:(g + 1) * LANES, :]
        if mode == "diff":
            qt = jnp.concatenate([jnp.where(half, qt, jnp.zeros_like(qt)),
                                  jnp.where(half, jnp.zeros_like(qt), qt)], axis=1)
        q_ops.append(qt)

    def fold(x, reduce):
        rows = x.shape[0]
        while rows > 8:
            g = max(d for d in range(2, 9) if (rows // 8) % d == 0)
            rows //= g
            x = reduce(x.reshape(g, rows, x.shape[1]), axis=0)
        return x

    def scores(kc, g):
        return _dot(kc[:, g * LANES:(g + 1) * LANES], q_ops[g])

    def absorb(kc, vc, carry):
        ss = [scores(kc, g) for g in range(n_par)]
        m_new = [jnp.maximum(carry[g][0], jnp.max(fold(ss[g], jnp.max), axis=0, keepdims=True))
                 for g in range(n_par)]
        alpha = [jnp.exp2(carry[g][0] - m_new[g]) for g in range(n_par)]
        ps = [jnp.exp2(ss[g] - m_new[g]).astype(BF16) for g in range(n_par)]
        ones = jnp.ones((ONES_ROWS, vc.shape[1]), BF16)
        acc = [alpha[g] * carry[g][1] + _dot(jnp.concatenate([vc[g * dv:(g + 1) * dv], ones], axis=0), ps[g])
               for g in range(n_par)]
        return tuple((m_new[g], acc[g]) for g in range(n_par))

    n_latent_chunks = (k_ref.shape[0] - ctx_len) // tk

    def finish(stats):
        outs = [a[:dv] / a[dv:dv + 1] for (_, a) in stats]
        if mode == "diff":
            lp = lam_ref[...]
            lam = (jnp.exp(jnp.sum(lp[0:1] * lp[1:2], axis=-1, keepdims=True))
                   - jnp.exp(jnp.sum(lp[2:3] * lp[3:4], axis=-1, keepdims=True)) + lam_init)
            for g in range(n_par):
                o = outs[g][:, :tq] - lam * outs[g][:, tq:]
                o = o * lax.rsqrt(jnp.mean(o * o, axis=0, keepdims=True) + RMS_EPS) * g_ref[...]
                o = o * (1.0 - lam_init)
                o_ref[:, g * LANES:(g + 1) * LANES] = o.T.astype(o_ref.dtype)
        else:
            per_slab = LANES // dv
            for j in range(n_par // per_slab):
                o = jnp.concatenate(outs[j * per_slab:(j + 1) * per_slab], axis=0)
                o_ref[:, j * LANES:(j + 1) * LANES] = o.T.astype(o_ref.dtype)

    def run(first_keys, n_more):
        neg = jnp.full((1, width), -1e30, F32)
        zacc = jnp.zeros((dv + ONES_ROWS, width), F32)
        stats = absorb(k_ref[0:first_keys, :], vt_ref[:, 0:first_keys], ((neg, zacc),) * n_par)

        def body(j, stats):
            off = pl.multiple_of(first_keys + j * tk, LANES)
            return absorb(k_ref[pl.ds(off, tk), :], vt_ref[:, pl.ds(off, tk)], stats)

        finish(lax.fori_loop(0, n_more, body, stats))

    @pl.when(qi < ctx_tiles)
    def _():
        run(ctx_len, 0)

    @pl.when(qi >= ctx_tiles)
    def _():
        run(ctx_len + tk, n_latent_chunks - 1)


def _attention(qt, k, vt, lam_p, g, *, mode, ctx_len, lam_init=0.0):
    bn, s, _ = k.shape
    heads = k.shape[2] // LANES
    dv = vt.shape[1] // heads
    par = math.gcd(heads, ATT_PAR)
    tk = math.gcd(s - ctx_len, ATT_TK)
    assert tk % LANES == 0 and ctx_len % LANES == 0 and (par * dv) % LANES == 0
    kern = functools.partial(_attn_kernel, mode=mode, ctx_tiles=ctx_len // TM, ctx_len=ctx_len, tk=tk,
                             lam_init=lam_init)
    return pl.pallas_call(
        kern,
        grid=(bn, heads // par, s // TM),
        in_specs=[pl.BlockSpec(lam_p.shape, lambda b, h, i: (0, 0)),
                  pl.BlockSpec(g.shape, lambda b, h, i: (0, 0)),
                  pl.BlockSpec((None, par * LANES, TM), lambda b, h, i: (b, h, i)),
                  pl.BlockSpec((None, s, par * LANES), lambda b, h, i: (b, 0, h)),
                  pl.BlockSpec((None, par * dv, s), lambda b, h, i: (b, h, 0))],
        out_specs=pl.BlockSpec((None, TM, par * dv), lambda b, h, i: (b, i, h)),
        out_shape=jax.ShapeDtypeStruct((bn, s, heads * dv), BF16),
        compiler_params=_params(("parallel", "parallel", "parallel")),
        name="attn_" + mode,
    )(lam_p, g, qt, k, vt)


def _rwkv_prep_kernel(pb_ref, prev_ref, next_ref, mu_ref, w0_ref, w2_ref, a0_ref, a2_ref, g2_ref,
                      kk_ref, ka_ref, rk_ref,
                      r_out, v_out, kkn_out, ld_out, kd_out, beta_out, g_out, bonus_out,
                      *, ctx_tiles, n_tiles):
    i = pl.program_id(1)
    x = pb_ref[...]
    row = lax.broadcasted_iota(jnp.int32, (TM, 1), 0)
    has_prev = jnp.logical_and(i != 0, i != ctx_tiles)
    has_next = jnp.logical_and(i != ctx_tiles - 1, i != n_tiles - 1)
    prev_edge = jnp.where(has_prev, prev_ref[7:8, :], 0.0)
    next_edge = jnp.where(has_next, next_ref[0:1, :], 0.0)
    xp = jnp.where(row == 0, prev_edge, pltpu.roll(x, 1, 0))
    xn = jnp.where(row == TM - 1, next_edge, pltpu.roll(x, TM - 1, 0))
    z = x + mu_ref[...] * (0.5 * (xp + xn) - x)

    r = z[:, :B_WIDTH]
    k = z[:, B_WIDTH:2 * B_WIDTH]
    v = z[:, 2 * B_WIDTH:3 * B_WIDTH]
    o = 3 * B_WIDTH
    wd = z[:, o:o + 2 * B_DECAY_RANK]
    o += 2 * B_DECAY_RANK
    ad = z[:, o:o + 2 * B_A_RANK]
    o += 2 * B_A_RANK
    gd = z[:, o:]

    u = w0_ref[...] + _dot1(jnp.tanh(wd), w2_ref[...])
    nu = -u
    w_raw = -(jnp.maximum(nu, 0.0) + jnp.log(1.0 + jnp.exp(-jnp.abs(nu)))) - 0.5
    ld = -jnp.exp(w_raw)
    lr = _sigmoid(a0_ref[...] + _dot1(ad, a2_ref[...]))
    g_out[...] = _dot1(_sigmoid(gd), g2_ref[...])

    ones = _group_ones(B_WIDTH, B_HEAD)
    kk = k * kk_ref[...]
    norm = jnp.sqrt(_dot_exact_rhs(kk * kk, ones))
    kkn = kk / jnp.maximum(norm, 1e-12)
    ka = ka_ref[...]
    kd_sum = jnp.zeros_like(k)
    for d in range(2):
        lr_d = lr[:, d * B_WIDTH:(d + 1) * B_WIDTH]
        kd = k * (1.0 + (lr_d - 1.0) * ka)
        kd_sum = kd_sum + kd
        ld_out[d] = ld[:, d * B_WIDTH:(d + 1) * B_WIDTH]
        kd_out[d] = kd
        beta_out[d] = kkn * lr_d
    bonus_out[...] = _dot_exact_rhs(r * kd_sum * rk_ref[...], ones) * v
    r_out[...] = r
    v_out[...] = v
    kkn_out[...] = kkn


def _rwkv_prep(pb, mu, w0, w2bd, a0, a2bd, g2, k_k, k_a, r_k, ctx_len):
    bn, s, _ = pb.shape
    n_tiles = s // TM
    eight = TM // 8
    row = lambda width: pl.BlockSpec((None, TM, width), lambda b, i: (b, i, 0))
    drow = pl.BlockSpec((2, None, TM, B_WIDTH), lambda b, i: (0, b, i, 0))
    full = lambda a: pl.BlockSpec(a.shape, lambda b, i: (0,) * a.ndim)
    one = jax.ShapeDtypeStruct((bn, s, B_WIDTH), F32)
    two = jax.ShapeDtypeStruct((2, bn, s, B_WIDTH), F32)
    kern = functools.partial(_rwkv_prep_kernel, ctx_tiles=ctx_len // TM, n_tiles=n_tiles)
    params = (mu, w0, w2bd, a0, a2bd, g2, k_k, k_a, r_k)
    return pl.pallas_call(
        kern,
        grid=(bn, n_tiles),
        in_specs=[row(N_B),
                  pl.BlockSpec((None, 8, N_B), lambda b, i: (b, jnp.maximum(i * eight - 1, 0), 0)),
                  pl.BlockSpec((None, 8, N_B), lambda b, i: (b, jnp.minimum((i + 1) * eight, s // 8 - 1), 0)),
                  ] + [full(p) for p in params],
        out_specs=[row(B_WIDTH), row(B_WIDTH), row(B_WIDTH), drow, drow, drow, row(B_WIDTH), row(B_WIDTH)],
        out_shape=[one, one, one, two, two, two, one, one],
        compiler_params=_params(("parallel", "parallel")),
        name="rwkv_prep",
    )(pb, pb, pb, *params)


def _pair_diag(x):
    lo = lax.broadcasted_iota(jnp.int32, (1, LANES), 1) < B_HEAD
    z = jnp.zeros_like(x)
    return jnp.concatenate([jnp.where(lo, x, z), jnp.where(lo, z, x)], axis=0)


def _pair_pick(x):
    lo = lax.broadcasted_iota(jnp.int32, (1, LANES), 1) < B_HEAD
    return jnp.where(lo, x[:B_HEAD], x[B_HEAD:])


def _rwkv_pair_kernel(rf_ref, vf_ref, kkf_ref, rb_ref, vb_ref, kkb_ref,
                      ldf_ref, kdf_ref, betaf_ref, ldb_ref, kdb_ref, betab_ref,
                      yf_ref, yb_ref, h_ref):
    c = pl.program_id(1)

    @pl.when(c == 0)
    def _():
        h_ref[...] = jnp.zeros_like(h_ref)

    n_pairs = B_WIDTH // LANES
    ti = lax.broadcasted_iota(jnp.int32, (CHUNK, LANES), 0)
    si = lax.broadcasted_iota(jnp.int32, (CHUNK, LANES), 1) % CHUNK
    t64 = lax.broadcasted_iota(jnp.int32, (CHUNK, CHUNK), 0)
    s64 = lax.broadcasted_iota(jnp.int32, (CHUNK, CHUNK), 1)
    eye = ti == si
    dirs = ((rf_ref, vf_ref, kkf_ref, ldf_ref, kdf_ref, betaf_ref, False),
            (rb_ref, vb_ref, kkb_ref, ldb_ref, kdb_ref, betab_ref, True))

    units = []
    n_rows = rf_ref.shape[0]
    for bi in range(n_rows):
      for d, (r_ref, v_ref, kk_ref, ld_ref, kd_ref, beta_ref, rev) in enumerate(dirs):
        strict = (si > ti) if rev else (ti > si)
        incl = (si >= ti) if rev else (ti >= si)
        tri = ((s64 >= t64) if rev else (t64 >= s64)).astype(BF16)
        ld_all = ld_ref[bi]
        cl_all = _dot_exact_lhs(tri, ld_all)
        for p in range(n_pairs):
            sl = slice(p * LANES, (p + 1) * LANES)
            ld, cl = ld_all[:, sl], cl_all[:, sl]
            total = jnp.sum(ld, axis=0, keepdims=True)
            inv_gam = jnp.exp(-cl)
            to_end = jnp.exp(total - cl)
            kk, kd, beta = kk_ref[bi, :, sl], kd_ref[bi, :, sl], beta_ref[bi, :, sl]
            units.append(dict(
                bi=bi, d=d, p=p, sl=sl, strict=strict, incl=incl, v=v_ref[bi, :, sl].astype(BF16),
                a_bar=-kk * jnp.exp(cl - ld), r_bar=r_ref[bi, :, sl] * jnp.exp(cl),
                b_til=(beta * inv_gam).astype(BF16), k_til=(kd * inv_gam).astype(BF16),
                b_hat=(beta * to_end).astype(BF16), k_hat=(kd * to_end).astype(BF16),
                gam_c=jnp.exp(total)))

    for u in units:
        x_mat = jnp.concatenate([u["a_bar"], u["r_bar"]], axis=0).astype(BF16)
        rhs = jnp.concatenate([_pair_diag(u["b_til"]), _pair_diag(u["k_til"])], axis=0)
        xbk = _dot(x_mat, rhs, NT)
        u["n_ab"] = jnp.where(u["strict"], xbk[:CHUNK, :LANES], 0.0)
        u["l_rb"] = jnp.where(u["incl"], xbk[CHUNK:, :LANES], 0.0)
        n_ak = jnp.where(u["strict"], xbk[:CHUNK, LANES:], 0.0)
        l_rk = jnp.where(u["incl"], xbk[CHUNK:, LANES:], 0.0)
        u["nl"] = jnp.concatenate([n_ak, l_rk], axis=0).astype(BF16)
    for u in units:
        nv = _dot(u["nl"], _pair_diag(u["v"]))
        u["w"], u["u0"], u["lrkv"] = u["a_bar"], nv[:CHUNK], nv[CHUNK:]
        u["npow"] = u["n_ab"].astype(BF16)

    steps = int(math.log2(CHUNK))
    for kstep in range(steps):
        for u in units:
            rhs = jnp.concatenate([_pair_diag(u["w"].astype(BF16)), _pair_diag(u["u0"].astype(BF16))], axis=1)
            upd = _dot(u["npow"], rhs)
            u["w"] = u["w"] + upd[:, :LANES]
            u["u0"] = u["u0"] + upd[:, LANES:]
        if kstep + 1 < steps:
            for u in units:
                u["npow"] = _dot(u["npow"], _pair_diag(u["npow"])).astype(BF16)

    for u in units:
        wb, ub = u["w"].astype(BF16), u["u0"].astype(BF16)
        lx = _dot(u["l_rb"].astype(BF16), jnp.concatenate([_pair_diag(wb), _pair_diag(ub)], axis=1))
        u["p_mat"] = u["r_bar"] + lx[:, :LANES]
        u["y0"] = u["lrkv"] + lx[:, LANES:]
        lhs = jnp.concatenate([u["b_hat"], u["k_hat"]], axis=0)
        rhs = jnp.concatenate([jnp.concatenate([wb, ub], axis=1),
                               jnp.concatenate([jnp.zeros_like(wb), u["v"]], axis=1)], axis=0)
        mg = _dot(lhs, rhs, TN)
        u["m_full"] = _pair_pick(mg[:, :LANES]) + jnp.where(eye, u["gam_c"], 0.0)
        u["g_mat"] = _pair_pick(mg[:, LANES:])

    for u in units:
        bi, d, p = u["bi"], u["d"], u["p"]
        h0 = h_ref[bi, d, p]
        a0, a1 = _split(jnp.concatenate([u["p_mat"], u["m_full"]], axis=0), 2)
        h_hi, h_lo = _split(h0, 2)
        bh, bl = _pair_diag(h_hi), _pair_diag(h_lo)
        out = _dot(a0, bh) + (_dot(a0, bl) + _dot(a1, bh))
        (yb_ref if d else yf_ref)[bi, :, u["sl"]] = out[:CHUNK] + u["y0"]
        h_ref[bi, d, p] = out[CHUNK:] + u["g_mat"]


def _rwkv_pairs(r, v, kk, ld, kd, beta, ctx_len):
    bn, s, _ = r.shape
    n_chunks = s // CHUNK
    ctx_chunks = ctx_len // CHUNK
    rows = RWKV_ROWS if bn % RWKV_ROWS == 0 else 1

    def back(c):
        return jnp.where(c < ctx_chunks, ctx_chunks - 1 - c, n_chunks - 1 + ctx_chunks - c)

    fwd = pl.BlockSpec((rows, CHUNK, B_WIDTH), lambda b, c: (b, c, 0))
    bwd = pl.BlockSpec((rows, CHUNK, B_WIDTH), lambda b, c: (b, back(c), 0))
    fwd_d = pl.BlockSpec((None, rows, CHUNK, B_WIDTH), lambda b, c: (0, b, c, 0))
    bwd_d = pl.BlockSpec((None, rows, CHUNK, B_WIDTH), lambda b, c: (1, b, back(c), 0))
    y = jax.ShapeDtypeStruct((bn, s, B_WIDTH), F32)
    return pl.pallas_call(
        _rwkv_pair_kernel,
        grid=(bn // rows, n_chunks),
        in_specs=[fwd, fwd, fwd, bwd, bwd, bwd, fwd_d, fwd_d, fwd_d, bwd_d, bwd_d, bwd_d],
        out_specs=[fwd, bwd],
        out_shape=[y, y],
        scratch_shapes=[pltpu.VMEM((rows, 2, B_WIDTH // LANES, B_HEAD, LANES), F32)],
        compiler_params=_params(("parallel", "arbitrary")),
        name="rwkv_scan",
    )(r, v, kk, r, v, kk, ld, kd, beta, ld, kd, beta)


def _outproj_kernel(x_ref, mod_ref, a_ref, yf_ref, yb_ref, bonus_ref, g_ref, c_ref,
                    wa_ref, wb_ref, wc_ref, lnxg_ref, lnxb_ref, ln1g_ref, ln1b_ref, o_ref, *, alpha):
    nb = x_ref.shape[0]
    stack = lambda ref: jnp.concatenate([ref[r] for r in range(nb)], axis=0)
    y = stack(yf_ref) + stack(yb_ref) + stack(bonus_ref)
    ones = _group_ones(B_WIDTH, B_HEAD)
    inv = 1.0 / B_HEAD
    mu = _dot_exact_rhs(y, ones) * inv
    yc = y - mu
    var = _dot_exact_rhs(yc * yc, ones) * inv
    yn = yc * lax.rsqrt(var + GN_EPS) * lnxg_ref[...] + lnxb_ref[...]
    bmix = (yn * stack(g_ref)).astype(BF16)
    o = _dot(stack(a_ref), wa_ref[...]) + _dot(bmix, wb_ref[...]) + _dot(stack(c_ref), wc_ref[...])
    for r in range(nb):
        o_ref[r] = (_ln(alpha * x_ref[r] + mod_ref[r, 2:3] * o[r * TM:(r + 1) * TM], LN_EPS) * ln1g_ref[...]
                    + ln1b_ref[...])


def _outproj(xs, mods, a_out, yf, yb, bonus, g, c_out, wa, wb, wc, lnxg, lnxb, ln1g, ln1b, ctx_len, alpha):
    bn, s, d = xs.shape
    nb = _batch_rows(bn)
    row = lambda width: pl.BlockSpec((nb, TM, width), lambda b, i: (b, i, 0))
    full = lambda a: pl.BlockSpec(a.shape, lambda b, i: (0,) * a.ndim)
    consts = (wa, wb, wc, lnxg, lnxb, ln1g, ln1b)
    return pl.pallas_call(
        functools.partial(_outproj_kernel, alpha=alpha),
        grid=(bn // nb, s // TM),
        in_specs=[row(d), _mod_spec(ctx_len, d, nb),
                  row(A_WIDTH), row(B_WIDTH), row(B_WIDTH), row(B_WIDTH), row(B_WIDTH), row(C_WIDTH)]
                 + [full(p) for p in consts],
        out_specs=row(d),
        out_shape=jax.ShapeDtypeStruct((bn, s, d), F32),
        compiler_params=_params(("parallel", "parallel")),
        name="outproj",
    )(xs, mods, a_out, yf, yb, bonus, g, c_out, *consts)


def _swiglu_rows(h, w1_ref, w3_ref, w2_ref):
    acc = jnp.zeros((h.shape[0], w2_ref.shape[-1]), F32)
    for j in range(w1_ref.shape[-1] // FF_CHUNK):
        sl = slice(j * FF_CHUNK, (j + 1) * FF_CHUNK)
        u = _dot(h, w1_ref[:, sl])
        t = _dot(h, w3_ref[:, sl])
        acc = acc + _dot((_silu(u) * t).astype(BF16), w2_ref[sl, :])
    return acc


def _ffn_kernel(x_ref, mod_ref, w1_ref, w3_ref, w2_ref, g_ref, b_ref, o_ref, *, alpha):
    nb = x_ref.shape[0]
    h = jnp.concatenate([(_ln(x_ref[r], LN_EPS) * (1.0 + mod_ref[r, 4:5]) + mod_ref[r, 3:4]).astype(BF16)
                         for r in range(nb)], axis=0)
    f = _swiglu_rows(h, w1_ref, w3_ref, w2_ref)
    for r in range(nb):
        o_ref[r] = (_ln(alpha * x_ref[r] + mod_ref[r, 5:6] * f[r * TM:(r + 1) * TM], LN_EPS) * g_ref[...]
                    + b_ref[...])


def _ffn(x1, mods, w1, w3, w2, g, b, ctx_len, alpha):
    bn, s, d = x1.shape
    nb = _batch_rows(bn)
    row = pl.BlockSpec((nb, TM, d), lambda bb, i: (bb, i, 0))
    resident = lambda a: pl.BlockSpec(a.shape, lambda bb, i: (0,) * a.ndim, pipeline_mode=pl.Buffered(1))
    full = lambda a: pl.BlockSpec(a.shape, lambda bb, i: (0,) * a.ndim)
    return pl.pallas_call(
        functools.partial(_ffn_kernel, alpha=alpha),
        grid=(bn // nb, s // TM),
        in_specs=[row, _mod_spec(ctx_len, d, nb),
                  resident(w1), resident(w3), resident(w2), full(g), full(b)],
        out_specs=row,
        out_shape=jax.ShapeDtypeStruct((bn, s, d), F32),
        compiler_params=_params(("parallel", "parallel")),
        name="ffn",
    )(x1, mods, w1, w3, w2, g, b)


def _moe_pre_kernel(x_ref, mod_ref, router_ref, h_ref, logit_ref):
    mod = mod_ref[...]
    h = _ln(x_ref[...], LN_EPS) * (1.0 + mod[4:5]) + mod[3:4]
    h_ref[...] = h.astype(h_ref.dtype)
    logit_ref[...] = _dot6(h, router_ref[...])


def _moe_pre(x1, mods, router_p, ctx_len):
    bn, s, d = x1.shape
    ct = ctx_len // TM
    lt = (s - ctx_len) // TM
    return pl.pallas_call(
        _moe_pre_kernel,
        grid=(bn, lt),
        in_specs=[pl.BlockSpec((None, TM, d), lambda b, i: (b, i + ct, 0)),
                  pl.BlockSpec((None, None, 6, d), lambda b, i: (b, 1, 0, 0)),
                  pl.BlockSpec(router_p.shape, lambda b, i: (0, 0))],
        out_specs=[pl.BlockSpec((TM, d), lambda b, i: (b * lt + i, 0)),
                   pl.BlockSpec((TM, LANES), lambda b, i: (b * lt + i, 0))],
        out_shape=[jax.ShapeDtypeStruct((bn * lt * TM, d), BF16),
                   jax.ShapeDtypeStruct((bn * lt * TM, LANES), F32)],
        compiler_params=_params(("parallel", "parallel")),
        name="moe_pre",
    )(x1, mods, router_p)


def _slot_onehot(slots_ref, block):
    sl = slots_ref[...]
    s_iota = lax.broadcasted_iota(jnp.int32, (MOE_ROWS, MOE_TILE), 0) + block * MOE_ROWS
    hit = jnp.logical_or(sl[0:1] == s_iota, sl[1:2] == s_iota)
    return jnp.where(hit, 1.0, 0.0).astype(BF16)


def _dispatch_kernel(wb_ref, wc_ref, wf_ref, wv_ref, slots_ref, h_ref, o_ref):
    w = pl.program_id(0)
    part = lambda: _dot(_slot_onehot(slots_ref, wb_ref[w]), h_ref[...])

    @pl.when(wf_ref[w] == 1)
    def _():
        o_ref[...] = part().astype(o_ref.dtype)

    @pl.when(jnp.logical_and(wf_ref[w] == 0, wv_ref[w] == 1))
    def _():
        o_ref[...] = (o_ref[...].astype(F32) + part()).astype(o_ref.dtype)


def _dispatch(h, slots, work):
    t, d = h.shape
    wb, wc, wf, wv = work
    n_slots = (t * TOP_K // MOE_ROWS + N_EXPERTS) * MOE_ROWS
    grid_spec = pltpu.PrefetchScalarGridSpec(
        num_scalar_prefetch=4,
        grid=(wb.shape[0],),
        in_specs=[pl.BlockSpec((TOP_K, MOE_TILE), lambda w, b, c, f, v: (0, c[w])),
                  pl.BlockSpec((MOE_TILE, d), lambda w, b, c, f, v: (c[w], 0))],
        out_specs=pl.BlockSpec((MOE_ROWS, d), lambda w, b, c, f, v: (b[w], 0)),
    )
    return pl.pallas_call(
        _dispatch_kernel,
        grid_spec=grid_spec,
        out_shape=jax.ShapeDtypeStruct((n_slots, d), BF16),
        compiler_params=_params(("arbitrary",)),
        name="moe_dispatch",
    )(wb, wc, wf, wv, slots, h)


def _collect_kernel(wb_ref, wc_ref, wf_ref, wv_ref, slots_ref, scol_ref, gate_ref, y_ref, o_ref):
    w = pl.program_id(0)

    def part():
        base = wb_ref[w] * MOE_ROWS
        scol = scol_ref[...] - base
        gates = jnp.where(jnp.logical_and(scol >= 0, scol < MOE_ROWS), gate_ref[...], 0.0)
        gate = jnp.sum(gates, axis=1, keepdims=True)
        return _dot(_slot_onehot(slots_ref, wb_ref[w]), y_ref[...], TN) * gate

    @pl.when(wf_ref[w] == 1)
    def _():
        o_ref[...] = part()

    @pl.when(jnp.logical_and(wf_ref[w] == 0, wv_ref[w] == 1))
    def _():
        o_ref[...] = o_ref[...] + part()


def _collect(y_buf, slots, slots_col, gates, work):
    n_slots, d = y_buf.shape
    t = slots.shape[1]
    wb, wc, wf, wv = work
    tile = lambda w, b, c, f, v: (c[w], 0)
    grid_spec = pltpu.PrefetchScalarGridSpec(
        num_scalar_prefetch=4,
        grid=(wb.shape[0],),
        in_specs=[pl.BlockSpec((TOP_K, MOE_TILE), lambda w, b, c, f, v: (0, c[w])),
                  pl.BlockSpec((MOE_TILE, TOP_K), tile), pl.BlockSpec((MOE_TILE, TOP_K), tile),
                  pl.BlockSpec((MOE_ROWS, d), lambda w, b, c, f, v: (b[w], 0))],
        out_specs=pl.BlockSpec((MOE_TILE, d), tile),
    )
    return pl.pallas_call(
        _collect_kernel,
        grid_spec=grid_spec,
        out_shape=jax.ShapeDtypeStruct((t, d), F32),
        compiler_params=_params(("arbitrary",)),
        name="moe_collect",
    )(wb, wc, wf, wv, slots, slots_col, gates, y_buf)


def _expert_kernel(be_ref, nb_ref, x_ref, w1_ref, w3_ref, w2_ref, o_ref):
    i = pl.program_id(0)

    @pl.when(i < nb_ref[0])
    def _():
        o_ref[...] = _swiglu_rows(x_ref[...], w1_ref, w3_ref, w2_ref).astype(o_ref.dtype)

    @pl.when(i >= nb_ref[0])
    def _():
        o_ref[...] = jnp.zeros_like(o_ref)


def _experts(buf, block_e, n_used, w1, w3, w2):
    n, d = buf.shape
    n_blocks = n // MOE_ROWS
    ff = w1.shape[-1]
    wspec = lambda shape: pl.BlockSpec((None,) + shape, lambda i, be, nb: (be[i], 0, 0),
                                       pipeline_mode=pl.Buffered(1))
    used = lambda i, be, nb: (jnp.minimum(i, nb[0] - 1), 0)
    grid_spec = pltpu.PrefetchScalarGridSpec(
        num_scalar_prefetch=2,
        grid=(n_blocks,),
        in_specs=[pl.BlockSpec((MOE_ROWS, d), used), wspec((d, ff)), wspec((d, ff)), wspec((ff, d))],
        out_specs=pl.BlockSpec((MOE_ROWS, d), lambda i, be, nb: (i, 0)),
    )
    return pl.pallas_call(
        _expert_kernel,
        grid_spec=grid_spec,
        out_shape=jax.ShapeDtypeStruct((n, d), BF16),
        compiler_params=_params(("arbitrary",)),
        name="experts",
    )(block_e, n_used, buf, w1, w3, w2)


def _combine_kernel(x_ref, mod_ref, y_ref, g_ref, b_ref, o_ref, *, alpha):
    mod = mod_ref[...]
    o_ref[...] = _ln(alpha * x_ref[...] + mod[5:6] * y_ref[...], LN_EPS) * g_ref[...] + b_ref[...]


def _combine(x1, mods, y_tok, g, b, ctx_len, alpha):
    bn, s, d = x1.shape
    ct = ctx_len // TM
    lt = (s - ctx_len) // TM
    full = lambda a: pl.BlockSpec(a.shape, lambda bb, i: (0,) * a.ndim)
    return pl.pallas_call(
        functools.partial(_combine_kernel, alpha=alpha),
        grid=(bn, lt),
        in_specs=[pl.BlockSpec((None, TM, d), lambda bb, i: (bb, i + ct, 0)),
                  pl.BlockSpec((None, None, 6, d), lambda bb, i: (bb, 1, 0, 0)),
                  pl.BlockSpec((TM, d), lambda bb, i: (bb * lt + i, 0)),
                  full(g), full(b)],
        out_specs=pl.BlockSpec((None, TM, d), lambda bb, i: (bb, i, 0)),
        out_shape=jax.ShapeDtypeStruct((bn, lt * TM, d), F32),
        compiler_params=_params(("parallel", "parallel")),
        name="moe_combine",
    )(x1, mods, y_tok, g, b)


def _work_lists(rank_at_tile, counts, pstart, block_e, n_used, n_blocks, n_tiles):
    n_work = n_tiles * N_EXPERTS + n_blocks
    blocks = jnp.arange(n_blocks, dtype=jnp.int32)
    used = blocks < n_used
    r0 = blocks * MOE_ROWS - pstart[block_e]
    r_last = jnp.minimum(r0 + MOE_ROWS, counts[block_e]) - 1
    cols = rank_at_tile.T[block_e]
    find = jax.vmap(lambda col, val: jnp.searchsorted(col, val, side="right"))
    lo = jnp.clip(find(cols, r0) - 1, 0, n_tiles - 1)
    hi = jnp.clip(find(cols, r_last) - 1, 0, n_tiles - 1)
    n_b = jnp.where(used, hi - lo + 1, 0)
    ends = jnp.cumsum(n_b)
    starts = ends - n_b
    total = ends[-1]
    w = jnp.arange(n_work, dtype=jnp.int32)
    valid = w < total
    wl = jnp.minimum(w, total - 1)
    blk = jnp.minimum(jnp.searchsorted(ends, wl, side="right"), n_blocks - 1).astype(jnp.int32)
    tile = (lo[blk] + (wl - starts[blk])).astype(jnp.int32)
    first = jnp.logical_and(valid, w == starts[blk])
    as_i32 = lambda a: a.astype(jnp.int32)
    by_block = (blk, tile, as_i32(first), as_i32(valid))
    order = jnp.argsort(jnp.where(valid, tile, n_tiles), stable=True)
    order = order[jnp.minimum(w, total - 1)]
    tile2, blk2 = tile[order], blk[order]
    first2 = jnp.logical_and(valid, jnp.concatenate([jnp.ones((1,), bool), tile2[1:] != tile2[:-1]]))
    by_tile = (blk2, tile2, as_i32(first2), as_i32(valid))
    return by_block, by_tile


def _moe_layer(x1, mods, router, w1, w3, w2, g, b, ctx_len, alpha):
    d = x1.shape[-1]
    router_p = jnp.pad(router, ((0, 0), (0, LANES - N_EXPERTS)))
    h, logits = _moe_pre(x1, mods, router_p, ctx_len)
    t = h.shape[0]
    top_v, top_i = lax.top_k(logits[:, :N_EXPERTS], TOP_K)
    gates = jax.nn.softmax(top_v, axis=-1)
    e_flat = top_i.reshape(-1)
    onehot = (e_flat[:, None] == jnp.arange(N_EXPERTS)[None, :]).astype(jnp.int32)
    ranks = jnp.cumsum(onehot, axis=0) - onehot
    rank = jnp.sum(ranks * onehot, axis=1)
    counts = jnp.sum(onehot, axis=0)
    padded = (counts + MOE_ROWS - 1) // MOE_ROWS * MOE_ROWS
    pend = jnp.cumsum(padded)
    pstart = pend - padded
    slot = (pstart[e_flat] + rank).astype(jnp.int32)
    n_blocks = t * TOP_K // MOE_ROWS + N_EXPERTS
    assert t % MOE_TILE == 0
    n_tiles = t // MOE_TILE
    block_e = jnp.minimum(jnp.searchsorted(pend, jnp.arange(n_blocks) * MOE_ROWS, side="right"),
                          N_EXPERTS - 1).astype(jnp.int32)
    n_used = (pend[-1:] // MOE_ROWS).astype(jnp.int32)
    rank_at_tile = jnp.concatenate([ranks[::MOE_TILE * TOP_K], counts[None, :]], axis=0)
    by_block, by_tile = _work_lists(rank_at_tile, counts, pstart, block_e, n_used, n_blocks, n_tiles)
    slots_col = slot.reshape(t, TOP_K)
    slots = slots_col.T
    buf = _dispatch(h, slots, by_block)
    y_buf = _experts(buf, block_e, n_used, w1, w3, w2)
    y_tok = _collect(y_buf, slots, slots_col, gates, by_tile)
    return _combine(x1, mods, y_tok, g, b, ctx_len, alpha)


def _rope_tables(n_rows, ctx_len, dim, lane_lo):
    quarter = dim // 4
    inv = ROPE_THETA ** (-jnp.arange(quarter, dtype=F32) / quarter)
    rows = jnp.repeat(jnp.arange(n_rows, dtype=F32), GRID_W)
    cols = jnp.tile(jnp.arange(GRID_W, dtype=F32), n_rows)
    ang = jnp.concatenate([rows[:, None] * inv, rows[:, None] * inv,
                           cols[:, None] * inv, cols[:, None] * inv], axis=-1)
    sign = jnp.tile(jnp.concatenate([-jnp.ones(quarter, F32), jnp.ones(quarter, F32)]), 2)
    cos, sin = jnp.cos(ang), jnp.sin(ang) * sign
    length = cos.shape[0]
    if lane_lo == 0:
        reps = LANES // dim
        cos, sin = jnp.tile(cos, (1, reps)), jnp.tile(sin, (1, reps))
    else:
        pad = ((0, 0), (lane_lo, LANES - lane_lo - dim))
        cos = jnp.pad(cos, pad, constant_values=1.0)
        sin = jnp.pad(sin, pad)
    cos = jnp.concatenate([jnp.ones((ctx_len, LANES), F32), cos], axis=0)
    sin = jnp.concatenate([jnp.zeros((ctx_len, LANES), F32), sin], axis=0)
    return cos, sin


def _block_diag2(w):
    z = jnp.zeros_like(w[0])
    return jnp.concatenate([jnp.concatenate([w[0], z], axis=1), jnp.concatenate([z, w[1]], axis=1)], axis=0)


def _lambda_init(layer):
    return 0.8 - 0.6 * math.exp(-0.3 * layer)


def kernel(x, c, ctx, c_ctx, ada_w, ada_b, w_in, w_out, ln1_g, ln1_b, ln2_g, ln2_b,
           lam_q1, lam_k1, lam_q2, lam_k2, diff_norm_g, shift_mu, w0, w2, a0, a2, g2,
           k_k, k_a, r_k, lnx_g, lnx_b, q_norm_g, w_uq, kv_norm_g, w_ukv,
           ff_w1, ff_w3, ff_w2, router, moe_w1, moe_w3, moe_w2):
    bn, seq, d = x.shape
    ctx_len = ctx.shape[1]
    depth = ada_w.shape[0]
    assert d == D_MODEL and seq % TM == 0 and ctx_len % TM == 0 and seq % GRID_W == 0
    alpha = (2.0 * depth) ** 0.25
    n_grid_rows = seq // GRID_W
    cos_a, sin_a = _rope_tables(n_grid_rows, ctx_len, A_QK_DIM, 0)
    cos_c, sin_c = _rope_tables(n_grid_rows, ctx_len, C_ROPE, C_NOPE)

    cond_rows = 8 * ((bn + 1 + 7) // 8)
    cond = jnp.zeros((cond_rows, d), F32).at[:bn].set(c).at[bn].set(c_ctx)
    xs = jnp.concatenate([ctx, x], axis=1)

    for i in range(depth):
        with_ctx = i < depth - 1
        m = _ada(cond, ada_w[i], ada_b[i]).reshape(cond_rows, 6, d)
        mods = jnp.stack([jnp.broadcast_to(m[bn], (bn, 6, d)), m[:bn]], axis=1)

        wc = w_in[i][:, N_A + N_B:]
        kpe_w = jnp.pad(wc[:, C_Q_RANK + C_KV_RANK:], ((0, 0), (C_NOPE, LANES - C_NOPE - C_ROPE)))
        w_p = jnp.concatenate([w_in[i][:, :N_A + N_B], wc[:, :C_Q_RANK + C_KV_RANK], kpe_w], axis=1).astype(BF16)
        qa, ka, va, pb, pc = _inproj(xs, mods, w_p, cos_a, sin_a, ctx_len)

        lam_p = jnp.stack([lam_q1[i], lam_k1[i], lam_q2[i], lam_k2[i]])
        g_col = jnp.broadcast_to(diff_norm_g[i][:, None], (A_V_DIM, TM))
        a_out = _attention(qa, ka, va, lam_p, g_col, mode="diff", ctx_len=ctx_len, lam_init=_lambda_init(i))

        wq = w_uq[i].reshape(C_Q_RANK, C_HEADS, C_NOPE + C_ROPE)
        wq_p = jnp.pad(wq, ((0, 0), (0, 0), (0, LANES - C_NOPE - C_ROPE))).reshape(C_Q_RANK, -1).astype(BF16)
        wkv = w_ukv[i].reshape(C_KV_RANK, C_HEADS, C_NOPE + C_V)
        wk_p = jnp.pad(wkv[:, :, :C_NOPE], ((0, 0), (0, 0), (0, LANES - C_NOPE))).reshape(C_KV_RANK, -1).astype(BF16)
        wv_p = wkv[:, :, C_NOPE:].reshape(C_KV_RANK, -1).astype(BF16)
        qc, kc, vc = _mla_prep(pc, q_norm_g[i].reshape(1, -1), kv_norm_g[i].reshape(1, -1),
                               wq_p, wk_p, wv_p, cos_c, sin_c)
        c_out = _attention(qc, kc, vc, lam_p, g_col, mode="mla", ctx_len=ctx_len)

        r, v, kkn, ld, kd, beta, gate, bonus = _rwkv_prep(
            pb, shift_mu[i].reshape(1, -1), w0[i].reshape(1, -1), _block_diag2(w2[i]).astype(BF16),
            a0[i].reshape(1, -1), _block_diag2(a2[i]).astype(BF16), g2[i].astype(BF16),
            k_k[i].reshape(1, -1), k_a[i].reshape(1, -1), r_k[i].reshape(1, -1), ctx_len)
        yf, yb = _rwkv_pairs(r, v, kkn, ld, kd, beta, ctx_len)

        wo = w_out[i].astype(BF16)
        x1 = _outproj(xs, mods, a_out, yf, yb, bonus, gate, c_out,
                      wo[:A_WIDTH], wo[A_WIDTH:A_WIDTH + B_WIDTH], wo[A_WIDTH + B_WIDTH:],
                      lnx_g[i].reshape(1, -1), lnx_b[i].reshape(1, -1),
                      ln1_g[i].reshape(1, -1), ln1_b[i].reshape(1, -1), ctx_len, alpha)

        j = i // 2
        g2n, b2n = ln2_g[i].reshape(1, -1), ln2_b[i].reshape(1, -1)
        if i % 2 == 0:
            xs = _ffn(x1, mods, ff_w1[j].astype(BF16), ff_w3[j].astype(BF16), ff_w2[j].astype(BF16),
                      g2n, b2n, ctx_len, alpha)
        else:
            if with_ctx:
                raise NotImplementedError("routed FFN on the context rows is not needed at this depth")
            return _moe_layer(x1, mods, router[j], moe_w1[j].astype(BF16), moe_w3[j].astype(BF16),
                              moe_w2[j].astype(BF16), g2n, b2n, ctx_len, alpha)
    return xs[:, ctx_len:]
```

```python
import functools
import math

import jax
import jax.numpy as jnp
from jax import lax
from jax.experimental import pallas as pl
from jax.experimental.pallas import tpu as pltpu

F32 = jnp.float32
BF16 = jnp.bfloat16

D_MODEL = 1024
GRID_W = 64
ROPE_THETA = 10000.0
A_HEADS, A_QK_DIM, A_V_DIM = 4, 64, 128
A_WIDTH = A_HEADS * A_V_DIM
A_QK_COLS = 2 * A_HEADS * A_QK_DIM
B_HEADS, B_HEAD = 4, 64
B_WIDTH = B_HEADS * B_HEAD
B_DECAY_RANK, B_A_RANK, B_GATE_RANK = 64, 64, 128
C_HEADS, C_NOPE, C_ROPE, C_V = 4, 64, 32, 64
C_WIDTH = C_HEADS * C_V
C_Q_RANK, C_KV_RANK = 256, 128
N_A = 2 * A_QK_COLS + A_WIDTH
N_B = 3 * B_WIDTH + 2 * B_DECAY_RANK + 2 * B_A_RANK + B_GATE_RANK
N_C = C_Q_RANK + C_KV_RANK + C_ROPE
N_C_PAD = C_Q_RANK + C_KV_RANK + 128
D_FF = 3584
N_EXPERTS = 8
TOP_K = 2
LN_EPS = 1e-6
RMS_EPS = 1e-6
GN_EPS = 64e-5

LOG2E = math.log2(math.e)
LANES = 128
TM = 256
CHUNK = 64
ROWS_B = 2
RWKV_ROWS = 4
ATT_TK = 1024
ONES_ROWS = 16
ATT_PAR = 4
MOE_ROWS = 512
MOE_TILE = 512
FF_CHUNK = 512
VMEM_LIMIT = 56 * 1024 * 1024

NN = (((1,), (0,)), ((), ()))
NT = (((1,), (1,)), ((), ()))
TN = (((0,), (0,)), ((), ()))


def _params(sem, vmem=VMEM_LIMIT, flags=None):
    return pltpu.CompilerParams(dimension_semantics=sem, vmem_limit_bytes=vmem, flags=flags)


def _split(x, n):
    parts, r = [], x
    for _ in range(n):
        p = r.astype(BF16)
        parts.append(p)
        r = r - p.astype(F32)
    return parts


def _dot(a, b, dn=NN):
    return lax.dot_general(a, b, dn, preferred_element_type=F32)


def _dot1(a, b, dn=NN):
    return _dot(a.astype(BF16), b.astype(BF16), dn)


def _dot3(a, b, dn=NN):
    a0, a1 = _split(a, 2)
    b0, b1 = _split(b, 2)
    return _dot(a0, b0, dn) + (_dot(a0, b1, dn) + _dot(a1, b0, dn))


def _dot6(a, b, dn=NN):
    a0, a1, a2 = _split(a, 3)
    b0, b1, b2 = _split(b, 3)
    lo = _dot(a1, b1, dn) + (_dot(a0, b2, dn) + _dot(a2, b0, dn))
    return _dot(a0, b0, dn) + ((_dot(a0, b1, dn) + _dot(a1, b0, dn)) + lo)


def _dot_exact_lhs(a_bf16, b, dn=NN, n=3):
    parts = _split(b, n)
    out = _dot(a_bf16, parts[-1], dn)
    for p in parts[-2::-1]:
        out = out + _dot(a_bf16, p, dn)
    return out


def _dot_exact_rhs(a, b_bf16, dn=NN, n=3):
    parts = _split(a, n)
    out = _dot(parts[-1], b_bf16, dn)
    for p in parts[-2::-1]:
        out = out + _dot(p, b_bf16, dn)
    return out


def _ln(x, eps):
    mu = jnp.mean(x, axis=-1, keepdims=True)
    xc = x - mu
    return xc * lax.rsqrt(jnp.mean(xc * xc, axis=-1, keepdims=True) + eps)


def _sigmoid(x):
    return 1.0 / (1.0 + jnp.exp(-x))


def _silu(x):
    return x * _sigmoid(x)


def _group_ones(width, group):
    r = lax.broadcasted_iota(jnp.int32, (width, width), 0) // group
    c = lax.broadcasted_iota(jnp.int32, (width, width), 1) // group
    return (r == c).astype(BF16)


def _partner(x, half):
    lane = lax.broadcasted_iota(jnp.int32, x.shape, 1)
    up = pltpu.roll(x, LANES - half, 1)
    dn = pltpu.roll(x, half, 1)
    return jnp.where((lane % (2 * half)) < half, up, dn)


def _rope(x, cos, sin, half):
    return x * cos + _partner(x, half) * sin


def _ada_kernel(c_ref, w_ref, b_ref, o_ref):
    o_ref[...] = _dot3(_silu(c_ref[...]), w_ref[...]) + b_ref[...]


def _ada(cond, w, b):
    rows, d = cond.shape
    n = w.shape[1]
    tn = 1536
    return pl.pallas_call(
        _ada_kernel,
        grid=(n // tn,),
        in_specs=[pl.BlockSpec((rows, d), lambda j: (0, 0)),
                  pl.BlockSpec((d, tn), lambda j: (0, j)),
                  pl.BlockSpec((1, tn), lambda j: (0, j))],
        out_specs=pl.BlockSpec((rows, tn), lambda j: (0, j)),
        out_shape=jax.ShapeDtypeStruct((rows, n), F32),
        compiler_params=_params(("parallel",)),
        name="ada",
    )(cond, w, b.reshape(1, n))


def _inproj_kernel(x_ref, mod_ref, w_ref, cos_ref, sin_ref,
                   qg_ref, kvg_ref, wq_ref, wk_ref, wv_ref, cosc_ref, sinc_ref,
                   q_ref, k_ref, v_ref, pb_ref, cq_ref, ck_ref, cv_ref):
    nb = x_ref.shape[0]
    h = jnp.concatenate([(_ln(x_ref[r], LN_EPS) * (1.0 + mod_ref[r, 1:2]) + mod_ref[r, 0:1]).astype(BF16)
                         for r in range(nb)], axis=0)
    cos, sin = cos_ref[...], sin_ref[...]
    scale = A_QK_DIM ** -0.5 * LOG2E
    rows = lambda a, r: a[r * TM:(r + 1) * TM]
    for j in range(A_QK_COLS // LANES):
        sl = slice(j * LANES, (j + 1) * LANES)
        qj = _dot(h, w_ref[:, sl])
        kj = _dot(h, w_ref[:, A_QK_COLS + j * LANES:A_QK_COLS + (j + 1) * LANES])
        vj = _dot(h, w_ref[:, 2 * A_QK_COLS + j * LANES:2 * A_QK_COLS + (j + 1) * LANES])
        for r in range(nb):
            q_ref[r, sl, :] = (_rope(rows(qj, r), cos, sin, A_QK_DIM // 4) * scale).T.astype(BF16)
            k_ref[r, :, sl] = _rope(rows(kj, r), cos, sin, A_QK_DIM // 4).astype(BF16)
            v_ref[r, sl, :] = rows(vj, r).T.astype(BF16)
    pb = _dot(h, w_ref[:, N_A:N_A + N_B])
    for r in range(nb):
        pb_ref[r] = rows(pb, r)
    _mla_heads(_dot(h, w_ref[:, N_A + N_B:]), nb, qg_ref, kvg_ref, wq_ref, wk_ref, wv_ref,
               cosc_ref[...], sinc_ref[...], cq_ref, ck_ref, cv_ref)


def _batch_rows(bn):
    return ROWS_B if bn % ROWS_B == 0 else 1


def _mod_spec(ctx_len, d, nb=None):
    ct = ctx_len // TM
    return pl.BlockSpec((nb, None, 6, d), lambda b, i: (b, jnp.where(i >= ct, 1, 0), 0, 0))


def _inproj(xs, mods, w_p, cos_a, sin_a, mla_consts, cos_c, sin_c, ctx_len):
    bn, s, d = xs.shape
    n_tiles = s // TM
    n_w = w_p.shape[1]
    nb = _batch_rows(bn)
    row = lambda width: pl.BlockSpec((nb, TM, width), lambda b, i: (b, i, 0))
    col = lambda width: pl.BlockSpec((nb, width, TM), lambda b, i: (b, 0, i))
    full = lambda a: pl.BlockSpec(a.shape, lambda b, i: (0,) * a.ndim)
    tab = pl.BlockSpec((TM, LANES), lambda b, i: (i, 0))
    hw = C_HEADS * LANES
    outs = [jax.ShapeDtypeStruct((bn, A_QK_COLS, s), BF16),
            jax.ShapeDtypeStruct((bn, s, A_QK_COLS), BF16),
            jax.ShapeDtypeStruct((bn, A_WIDTH, s), BF16),
            jax.ShapeDtypeStruct((bn, s, N_B), F32),
            jax.ShapeDtypeStruct((bn, hw, s), BF16),
            jax.ShapeDtypeStruct((bn, s, hw), BF16),
            jax.ShapeDtypeStruct((bn, C_WIDTH, s), BF16)]
    return pl.pallas_call(
        _inproj_kernel,
        grid=(bn // nb, n_tiles),
        in_specs=[row(d), _mod_spec(ctx_len, d, nb), full(w_p), tab, tab]
                 + [full(a) for a in mla_consts] + [tab, tab],
        out_specs=[col(A_QK_COLS), row(A_QK_COLS), col(A_WIDTH), row(N_B), col(hw), row(hw), col(C_WIDTH)],
        out_shape=outs,
        compiler_params=_params(("parallel", "parallel")),
        name="inproj",
    )(xs, mods, w_p, cos_a, sin_a, *mla_consts, cos_c, sin_c)


def _mla_heads(pc, nb, qg_ref, kvg_ref, wq_ref, wk_ref, wv_ref, cos, sin, q_ref, k_ref, v_ref):
    cq = pc[:, :C_Q_RANK]
    cq = cq * lax.rsqrt(jnp.mean(cq * cq, axis=-1, keepdims=True) + RMS_EPS) * qg_ref[...]
    ckv = pc[:, C_Q_RANK:C_Q_RANK + C_KV_RANK]
    ckv = ckv * lax.rsqrt(jnp.mean(ckv * ckv, axis=-1, keepdims=True) + RMS_EPS) * kvg_ref[...]
    cqb, ckvb = cq.astype(BF16), ckv.astype(BF16)
    scale = (C_NOPE + C_ROPE) ** -0.5 * LOG2E
    rows = lambda a, r: a[r * TM:(r + 1) * TM]
    kpe = [_rope(rows(pc, r)[:, C_Q_RANK + C_KV_RANK:], cos, sin, C_ROPE // 4) for r in range(nb)]
    for h in range(C_HEADS):
        sl = slice(h * LANES, (h + 1) * LANES)
        qh = _dot(cqb, wq_ref[:, sl])
        kh = _dot(ckvb, wk_ref[:, sl])
        for r in range(nb):
            q_ref[r, sl, :] = (_rope(rows(qh, r), cos, sin, C_ROPE // 4) * scale).T.astype(BF16)
            k_ref[r, :, sl] = (rows(kh, r) + kpe[r]).astype(BF16)
    for j in range(C_WIDTH // LANES):
        sl = slice(j * LANES, (j + 1) * LANES)
        vj = _dot(ckvb, wv_ref[:, sl])
        for r in range(nb):
            v_ref[r, sl, :] = rows(vj, r).T.astype(BF16)


def _attn_kernel(lam_ref, g_ref, qt_ref, k_ref, vt_ref, o_ref, *, mode, ctx_tiles, ctx_len, tk, lam_init):
    qi = pl.program_id(2)
    tq = qt_ref.shape[1]
    n_par = qt_ref.shape[0] // LANES
    dv = vt_ref.shape[0] // n_par
    width = 2 * tq if mode == "diff" else tq
    half = lax.broadcasted_iota(jnp.int32, (LANES, 1), 0) < (LANES // 2)
    q_ops = []
    for g in range(n_par):
        qt = qt_ref[g * LANES:(g + 1) * LANES, :]
        if mode == "diff":
            qt = jnp.concatenate([jnp.where(half, qt, jnp.zeros_like(qt)),
                                  jnp.where(half, jnp.zeros_like(qt), qt)], axis=1)
        q_ops.append(qt)

    def fold(x, reduce):
        rows = x.shape[0]
        while rows > 8:
            g = max(d for d in range(2, 9) if (rows // 8) % d == 0)
            rows //= g
            x = reduce(x.reshape(g, rows, x.shape[1]), axis=0)
        return x

    def scores(kc, g):
        return _dot(kc[:, g * LANES:(g + 1) * LANES], q_ops[g])

    def absorb(kc, vc, carry):
        ss = [scores(kc, g) for g in range(n_par)]
        m_new = [jnp.maximum(carry[g][0], jnp.max(fold(ss[g], jnp.max), axis=0, keepdims=True))
                 for g in range(n_par)]
        alpha = [jnp.exp2(carry[g][0] - m_new[g]) for g in range(n_par)]
        ps = [jnp.exp2(ss[g] - m_new[g]).astype(BF16) for g in range(n_par)]
        ones = jnp.ones((ONES_ROWS, vc.shape[1]), BF16)
        acc = [alpha[g] * carry[g][1] + _dot(jnp.concatenate([vc[g * dv:(g + 1) * dv], ones], axis=0), ps[g])
               for g in range(n_par)]
        return tuple((m_new[g], acc[g]) for g in range(n_par))

    n_latent_chunks = (k_ref.shape[0] - ctx_len) // tk

    def finish(stats):
        outs = [a[:dv] / a[dv:dv + 1] for (_, a) in stats]
        if mode == "diff":
            lp = lam_ref[...]
            lam = (jnp.exp(jnp.sum(lp[0:1] * lp[1:2], axis=-1, keepdims=True))
                   - jnp.exp(jnp.sum(lp[2:3] * lp[3:4], axis=-1, keepdims=True)) + lam_init)
            for g in range(n_par):
                o = outs[g][:, :tq] - lam * outs[g][:, tq:]
                o = o * lax.rsqrt(jnp.mean(o * o, axis=0, keepdims=True) + RMS_EPS) * g_ref[...]
                o = o * (1.0 - lam_init)
                o_ref[:, g * LANES:(g + 1) * LANES] = o.T.astype(o_ref.dtype)
        else:
            per_slab = LANES // dv
            for j in range(n_par // per_slab):
                o = jnp.concatenate(outs[j * per_slab:(j + 1) * per_slab], axis=0)
                o_ref[:, j * LANES:(j + 1) * LANES] = o.T.astype(o_ref.dtype)

    def run(first_keys, n_more):
        neg = jnp.full((1, width), -1e30, F32)
        zacc = jnp.zeros((dv + ONES_ROWS, width), F32)
        stats = absorb(k_ref[0:first_keys, :], vt_ref[:, 0:first_keys], ((neg, zacc),) * n_par)

        def body(j, stats):
            off = pl.multiple_of(first_keys + j * tk, LANES)
            return absorb(k_ref[pl.ds(off, tk), :], vt_ref[:, pl.ds(off, tk)], stats)

        finish(lax.fori_loop(0, n_more, body, stats))

    @pl.when(qi < ctx_tiles)
    def _():
        run(ctx_len, 0)

    @pl.when(qi >= ctx_tiles)
    def _():
        run(ctx_len + tk, n_latent_chunks - 1)


def _attention(qt, k, vt, lam_p, g, *, mode, ctx_len, lam_init=0.0):
    bn, s, _ = k.shape
    heads = k.shape[2] // LANES
    dv = vt.shape[1] // heads
    par = math.gcd(heads, ATT_PAR)
    tk = math.gcd(s - ctx_len, ATT_TK)
    assert tk % LANES == 0 and ctx_len % LANES == 0 and (par * dv) % LANES == 0
    kern = functools.partial(_attn_kernel, mode=mode, ctx_tiles=ctx_len // TM, ctx_len=ctx_len, tk=tk,
                             lam_init=lam_init)
    return pl.pallas_call(
        kern,
        grid=(bn, heads // par, s // TM),
        in_specs=[pl.BlockSpec(lam_p.shape, lambda b, h, i: (0, 0)),
                  pl.BlockSpec(g.shape, lambda b, h, i: (0, 0)),
                  pl.BlockSpec((None, par * LANES, TM), lambda b, h, i: (b, h, i)),
                  pl.BlockSpec((None, s, par * LANES), lambda b, h, i: (b, 0, h)),
                  pl.BlockSpec((None, par * dv, s), lambda b, h, i: (b, h, 0))],
        out_specs=pl.BlockSpec((None, TM, par * dv), lambda b, h, i: (b, i, h)),
        out_shape=jax.ShapeDtypeStruct((bn, s, heads * dv), BF16),
        compiler_params=_params(("parallel", "parallel", "parallel")),
        name="attn_" + mode,
    )(lam_p, g, qt, k, vt)


def _rwkv_prep_kernel(pb_ref, prev_ref, next_ref, mu_ref, w0_ref, w2_ref, a0_ref, a2_ref, g2_ref,
                      kk_ref, ka_ref, rk_ref,
                      r_out, v_out, kkn_out, ld_out, kd_out, beta_out, g_out, bonus_out,
                      *, ctx_tiles, n_tiles):
    i = pl.program_id(1)
    x = pb_ref[...]
    row = lax.broadcasted_iota(jnp.int32, (TM, 1), 0)
    has_prev = jnp.logical_and(i != 0, i != ctx_tiles)
    has_next = jnp.logical_and(i != ctx_tiles - 1, i != n_tiles - 1)
    prev_edge = jnp.where(has_prev, prev_ref[7:8, :], 0.0)
    next_edge = jnp.where(has_next, next_ref[0:1, :], 0.0)
    xp = jnp.where(row == 0, prev_edge, pltpu.roll(x, 1, 0))
    xn = jnp.where(row == TM - 1, next_edge, pltpu.roll(x, TM - 1, 0))
    z = x + mu_ref[...] * (0.5 * (xp + xn) - x)

    r = z[:, :B_WIDTH]
    k = z[:, B_WIDTH:2 * B_WIDTH]
    v = z[:, 2 * B_WIDTH:3 * B_WIDTH]
    o = 3 * B_WIDTH
    wd = z[:, o:o + 2 * B_DECAY_RANK]
    o += 2 * B_DECAY_RANK
    ad = z[:, o:o + 2 * B_A_RANK]
    o += 2 * B_A_RANK
    gd = z[:, o:]

    u = w0_ref[...] + _dot1(jnp.tanh(wd), w2_ref[...])
    ld = -math.exp(-0.5) * _sigmoid(u)
    lr = _sigmoid(a0_ref[...] + _dot1(ad, a2_ref[...]))
    g_out[...] = _dot1(_sigmoid(gd), g2_ref[...])

    ones = _group_ones(B_WIDTH, B_HEAD)
    kk = k * kk_ref[...]
    norm = jnp.sqrt(_dot_exact_rhs(kk * kk, ones))
    kkn = kk / jnp.maximum(norm, 1e-12)
    ka = ka_ref[...]
    kd_sum = jnp.zeros_like(k)
    for d in range(2):
        lr_d = lr[:, d * B_WIDTH:(d + 1) * B_WIDTH]
        kd = k * (1.0 + (lr_d - 1.0) * ka)
        kd_sum = kd_sum + kd
        ld_out[d] = ld[:, d * B_WIDTH:(d + 1) * B_WIDTH]
        kd_out[d] = kd
        beta_out[d] = kkn * lr_d
    bonus_out[...] = _dot_exact_rhs(r * kd_sum * rk_ref[...], ones) * v
    r_out[...] = r
    v_out[...] = v
    kkn_out[...] = kkn


def _rwkv_prep(pb, mu, w0, w2bd, a0, a2bd, g2, k_k, k_a, r_k, ctx_len):
    bn, s, _ = pb.shape
    n_tiles = s // TM
    eight = TM // 8
    row = lambda width: pl.BlockSpec((None, TM, width), lambda b, i: (b, i, 0))
    drow = pl.BlockSpec((2, None, TM, B_WIDTH), lambda b, i: (0, b, i, 0))
    full = lambda a: pl.BlockSpec(a.shape, lambda b, i: (0,) * a.ndim)
    one = jax.ShapeDtypeStruct((bn, s, B_WIDTH), F32)
    two = jax.ShapeDtypeStruct((2, bn, s, B_WIDTH), F32)
    kern = functools.partial(_rwkv_prep_kernel, ctx_tiles=ctx_len // TM, n_tiles=n_tiles)
    params = (mu, w0, w2bd, a0, a2bd, g2, k_k, k_a, r_k)
    return pl.pallas_call(
        kern,
        grid=(bn, n_tiles),
        in_specs=[row(N_B),
                  pl.BlockSpec((None, 8, N_B), lambda b, i: (b, jnp.maximum(i * eight - 1, 0), 0)),
                  pl.BlockSpec((None, 8, N_B), lambda b, i: (b, jnp.minimum((i + 1) * eight, s // 8 - 1), 0)),
                  ] + [full(p) for p in params],
        out_specs=[row(B_WIDTH), row(B_WIDTH), row(B_WIDTH), drow, drow, drow, row(B_WIDTH), row(B_WIDTH)],
        out_shape=[one, one, one, two, two, two, one, one],
        compiler_params=_params(("parallel", "parallel")),
        name="rwkv_prep",
    )(pb, pb, pb, *params)


def _pair_diag(x):
    lo = lax.broadcasted_iota(jnp.int32, (1, LANES), 1) < B_HEAD
    z = jnp.zeros_like(x)
    return jnp.concatenate([jnp.where(lo, x, z), jnp.where(lo, z, x)], axis=0)


def _pair_pick(x):
    lo = lax.broadcasted_iota(jnp.int32, (1, LANES), 1) < B_HEAD
    return jnp.where(lo, x[:B_HEAD], x[B_HEAD:])


def _rwkv_pair_kernel(rf_ref, vf_ref, kkf_ref, rb_ref, vb_ref, kkb_ref,
                      ldf_ref, kdf_ref, betaf_ref, ldb_ref, kdb_ref, betab_ref,
                      yf_ref, yb_ref, h_ref):
    c = pl.program_id(1)

    @pl.when(c == 0)
    def _():
        h_ref[...] = jnp.zeros_like(h_ref)

    n_pairs = B_WIDTH // LANES
    ti = lax.broadcasted_iota(jnp.int32, (CHUNK, LANES), 0)
    si = lax.broadcasted_iota(jnp.int32, (CHUNK, LANES), 1) % CHUNK
    t64 = lax.broadcasted_iota(jnp.int32, (CHUNK, CHUNK), 0)
    s64 = lax.broadcasted_iota(jnp.int32, (CHUNK, CHUNK), 1)
    eye = ti == si
    dirs = ((rf_ref, vf_ref, kkf_ref, ldf_ref, kdf_ref, betaf_ref, False),
            (rb_ref, vb_ref, kkb_ref, ldb_ref, kdb_ref, betab_ref, True))

    units = []
    n_rows = rf_ref.shape[0]
    for bi in range(n_rows):
      for d, (r_ref, v_ref, kk_ref, ld_ref, kd_ref, beta_ref, rev) in enumerate(dirs):
        strict = (si > ti) if rev else (ti > si)
        incl = (si >= ti) if rev else (ti >= si)
        tri = ((s64 >= t64) if rev else (t64 >= s64)).astype(BF16)
        ld_all = ld_ref[bi]
        cl_all = _dot_exact_lhs(tri, ld_all)
        for p in range(n_pairs):
            sl = slice(p * LANES, (p + 1) * LANES)
            ld, cl = ld_all[:, sl], cl_all[:, sl]
            total = jnp.sum(ld, axis=0, keepdims=True)
            inv_gam = jnp.exp(-cl)
            to_end = jnp.exp(total - cl)
            kk, kd, beta = kk_ref[bi, :, sl], kd_ref[bi, :, sl], beta_ref[bi, :, sl]
            units.append(dict(
                bi=bi, d=d, p=p, sl=sl, strict=strict, incl=incl, v=v_ref[bi, :, sl].astype(BF16),
                a_bar=-kk * jnp.exp(cl - ld), r_bar=r_ref[bi, :, sl] * jnp.exp(cl),
                b_til=(beta * inv_gam).astype(BF16), k_til=(kd * inv_gam).astype(BF16),
                b_hat=(beta * to_end).astype(BF16), k_hat=(kd * to_end).astype(BF16),
                gam_c=jnp.exp(total)))

    for u in units:
        x_mat = jnp.concatenate([u["a_bar"], u["r_bar"]], axis=0).astype(BF16)
        rhs = jnp.concatenate([_pair_diag(u["b_til"]), _pair_diag(u["k_til"])], axis=0)
        xbk = _dot(x_mat, rhs, NT)
        u["n_ab"] = jnp.where(u["strict"], xbk[:CHUNK, :LANES], 0.0)
        u["l_rb"] = jnp.where(u["incl"], xbk[CHUNK:, :LANES], 0.0)
        n_ak = jnp.where(u["strict"], xbk[:CHUNK, LANES:], 0.0)
        l_rk = jnp.where(u["incl"], xbk[CHUNK:, LANES:], 0.0)
        u["nl"] = jnp.concatenate([n_ak, l_rk], axis=0).astype(BF16)
    for u in units:
        nv = _dot(u["nl"], _pair_diag(u["v"]))
        u["w"], u["u0"], u["lrkv"] = u["a_bar"], nv[:CHUNK], nv[CHUNK:]
        u["npow"] = u["n_ab"].astype(BF16)

    steps = int(math.log2(CHUNK))
    for kstep in range(steps):
        for u in units:
            rhs = jnp.concatenate([_pair_diag(u["w"].astype(BF16)), _pair_diag(u["u0"].astype(BF16))], axis=1)
            upd = _dot(u["npow"], rhs)
            u["w"] = u["w"] + upd[:, :LANES]
            u["u0"] = u["u0"] + upd[:, LANES:]
        if kstep + 1 < steps:
            for u in units:
                u["npow"] = _dot(u["npow"], _pair_diag(u["npow"])).astype(BF16)

    for u in units:
        wb, ub = u["w"].astype(BF16), u["u0"].astype(BF16)
        lx = _dot(u["l_rb"].astype(BF16), jnp.concatenate([_pair_diag(wb), _pair_diag(ub)], axis=1))
        u["p_mat"] = u["r_bar"] + lx[:, :LANES]
        u["y0"] = u["lrkv"] + lx[:, LANES:]
        lhs = jnp.concatenate([u["b_hat"], u["k_hat"]], axis=0)
        rhs = jnp.concatenate([jnp.concatenate([wb, ub], axis=1),
                               jnp.concatenate([jnp.zeros_like(wb), u["v"]], axis=1)], axis=0)
        mg = _dot(lhs, rhs, TN)
        u["m_full"] = _pair_pick(mg[:, :LANES]) + jnp.where(eye, u["gam_c"], 0.0)
        u["g_mat"] = _pair_pick(mg[:, LANES:])

    for u in units:
        bi, d, p = u["bi"], u["d"], u["p"]
        h0 = h_ref[bi, d, p]
        a0, a1 = _split(jnp.concatenate([u["p_mat"], u["m_full"]], axis=0), 2)
        h_hi, h_lo = _split(h0, 2)
        bh, bl = _pair_diag(h_hi), _pair_diag(h_lo)
        out = _dot(a0, bh) + (_dot(a0, bl) + _dot(a1, bh))
        (yb_ref if d else yf_ref)[bi, :, u["sl"]] = out[:CHUNK] + u["y0"]
        h_ref[bi, d, p] = out[CHUNK:] + u["g_mat"]


def _rwkv_pairs(r, v, kk, ld, kd, beta, ctx_len):
    bn, s, _ = r.shape
    n_chunks = s // CHUNK
    ctx_chunks = ctx_len // CHUNK
    rows = RWKV_ROWS if bn % RWKV_ROWS == 0 else 1

    def back(c):
        return jnp.where(c < ctx_chunks, ctx_chunks - 1 - c, n_chunks - 1 + ctx_chunks - c)

    fwd = pl.BlockSpec((rows, CHUNK, B_WIDTH), lambda b, c: (b, c, 0))
    bwd = pl.BlockSpec((rows, CHUNK, B_WIDTH), lambda b, c: (b, back(c), 0))
    fwd_d = pl.BlockSpec((None, rows, CHUNK, B_WIDTH), lambda b, c: (0, b, c, 0))
    bwd_d = pl.BlockSpec((None, rows, CHUNK, B_WIDTH), lambda b, c: (1, b, back(c), 0))
    y = jax.ShapeDtypeStruct((bn, s, B_WIDTH), F32)
    return pl.pallas_call(
        _rwkv_pair_kernel,
        grid=(bn // rows, n_chunks),
        in_specs=[fwd, fwd, fwd, bwd, bwd, bwd, fwd_d, fwd_d, fwd_d, bwd_d, bwd_d, bwd_d],
        out_specs=[fwd, bwd],
        out_shape=[y, y],
        scratch_shapes=[pltpu.VMEM((rows, 2, B_WIDTH // LANES, B_HEAD, LANES), F32)],
        compiler_params=_params(("parallel", "arbitrary")),
        name="rwkv_scan",
    )(r, v, kk, r, v, kk, ld, kd, beta, ld, kd, beta)


def _outproj_kernel(x_ref, mod_ref, a_ref, yf_ref, yb_ref, bonus_ref, g_ref, c_ref,
                    wa_ref, wb_ref, wc_ref, lnxg_ref, lnxb_ref, ln1g_ref, ln1b_ref, o_ref, *, alpha):
    nb = x_ref.shape[0]
    stack = lambda ref: jnp.concatenate([ref[r] for r in range(nb)], axis=0)
    y = stack(yf_ref) + stack(yb_ref) + stack(bonus_ref)
    ones = _group_ones(B_WIDTH, B_HEAD)
    inv = 1.0 / B_HEAD
    mu = _dot_exact_rhs(y, ones) * inv
    yc = y - mu
    var = _dot_exact_rhs(yc * yc, ones) * inv
    yn = yc * lax.rsqrt(var + GN_EPS) * lnxg_ref[...] + lnxb_ref[...]
    bmix = (yn * stack(g_ref)).astype(BF16)
    o = _dot(stack(a_ref), wa_ref[...]) + _dot(bmix, wb_ref[...]) + _dot(stack(c_ref), wc_ref[...])
    for r in range(nb):
        o_ref[r] = (_ln(alpha * x_ref[r] + mod_ref[r, 2:3] * o[r * TM:(r + 1) * TM], LN_EPS) * ln1g_ref[...]
                    + ln1b_ref[...])


def _outproj(xs, mods, a_out, yf, yb, bonus, g, c_out, wa, wb, wc, lnxg, lnxb, ln1g, ln1b, ctx_len, alpha):
    bn, s, d = xs.shape
    nb = _batch_rows(bn)
    row = lambda width: pl.BlockSpec((nb, TM, width), lambda b, i: (b, i, 0))
    full = lambda a: pl.BlockSpec(a.shape, lambda b, i: (0,) * a.ndim)
    consts = (wa, wb, wc, lnxg, lnxb, ln1g, ln1b)
    return pl.pallas_call(
        functools.partial(_outproj_kernel, alpha=alpha),
        grid=(bn // nb, s // TM),
        in_specs=[row(d), _mod_spec(ctx_len, d, nb),
                  row(A_WIDTH), row(B_WIDTH), row(B_WIDTH), row(B_WIDTH), row(B_WIDTH), row(C_WIDTH)]
                 + [full(p) for p in consts],
        out_specs=row(d),
        out_shape=jax.ShapeDtypeStruct((bn, s, d), F32),
        compiler_params=_params(("parallel", "parallel")),
        name="outproj",
    )(xs, mods, a_out, yf, yb, bonus, g, c_out, *consts)


def _swiglu_rows(h, w1_ref, w3_ref, w2_ref):
    acc = jnp.zeros((h.shape[0], w2_ref.shape[-1]), F32)
    for j in range(w1_ref.shape[-1] // FF_CHUNK):
        sl = slice(j * FF_CHUNK, (j + 1) * FF_CHUNK)
        u = _dot(h, w1_ref[:, sl])
        t = _dot(h, w3_ref[:, sl])
        acc = acc + _dot((_silu(u) * t).astype(BF16), w2_ref[sl, :])
    return acc


def _ffn_kernel(x_ref, mod_ref, w1_ref, w3_ref, w2_ref, g_ref, b_ref, o_ref, *, alpha):
    nb = x_ref.shape[0]
    h = jnp.concatenate([(_ln(x_ref[r], LN_EPS) * (1.0 + mod_ref[r, 4:5]) + mod_ref[r, 3:4]).astype(BF16)
                         for r in range(nb)], axis=0)
    f = _swiglu_rows(h, w1_ref, w3_ref, w2_ref)
    for r in range(nb):
        o_ref[r] = (_ln(alpha * x_ref[r] + mod_ref[r, 5:6] * f[r * TM:(r + 1) * TM], LN_EPS) * g_ref[...]
                    + b_ref[...])


def _ffn(x1, mods, w1, w3, w2, g, b, ctx_len, alpha):
    bn, s, d = x1.shape
    nb = _batch_rows(bn)
    row = pl.BlockSpec((nb, TM, d), lambda bb, i: (bb, i, 0))
    resident = lambda a: pl.BlockSpec(a.shape, lambda bb, i: (0,) * a.ndim, pipeline_mode=pl.Buffered(1))
    full = lambda a: pl.BlockSpec(a.shape, lambda bb, i: (0,) * a.ndim)
    return pl.pallas_call(
        functools.partial(_ffn_kernel, alpha=alpha),
        grid=(bn // nb, s // TM),
        in_specs=[row, _mod_spec(ctx_len, d, nb),
                  resident(w1), resident(w3), resident(w2), full(g), full(b)],
        out_specs=row,
        out_shape=jax.ShapeDtypeStruct((bn, s, d), F32),
        compiler_params=_params(("parallel", "parallel")),
        name="ffn",
    )(x1, mods, w1, w3, w2, g, b)


def _moe_pre_kernel(x_ref, mod_ref, router_ref, h_ref, logit_ref):
    mod = mod_ref[...]
    h = _ln(x_ref[...], LN_EPS) * (1.0 + mod[4:5]) + mod[3:4]
    h_ref[...] = h.astype(h_ref.dtype)
    logit_ref[...] = _dot6(h, router_ref[...])


def _moe_pre(x1, mods, router_p, ctx_len):
    bn, s, d = x1.shape
    ct = ctx_len // TM
    lt = (s - ctx_len) // TM
    return pl.pallas_call(
        _moe_pre_kernel,
        grid=(bn, lt),
        in_specs=[pl.BlockSpec((None, TM, d), lambda b, i: (b, i + ct, 0)),
                  pl.BlockSpec((None, None, 6, d), lambda b, i: (b, 1, 0, 0)),
                  pl.BlockSpec(router_p.shape, lambda b, i: (0, 0))],
        out_specs=[pl.BlockSpec((TM, d), lambda b, i: (b * lt + i, 0)),
                   pl.BlockSpec((TM, LANES), lambda b, i: (b * lt + i, 0))],
        out_shape=[jax.ShapeDtypeStruct((bn * lt * TM, d), BF16),
                   jax.ShapeDtypeStruct((bn * lt * TM, LANES), F32)],
        compiler_params=_params(("parallel", "parallel")),
        name="moe_pre",
    )(x1, mods, router_p)


def _slot_onehot(slots_ref, block):
    sl = slots_ref[...]
    s_iota = lax.broadcasted_iota(jnp.int32, (MOE_ROWS, MOE_TILE), 0) + block * MOE_ROWS
    hit = jnp.logical_or(sl[0:1] == s_iota, sl[1:2] == s_iota)
    return jnp.where(hit, 1.0, 0.0).astype(BF16)


def _dispatch_kernel(wb_ref, wc_ref, wf_ref, wv_ref, slots_ref, h_ref, o_ref):
    w = pl.program_id(0)
    part = lambda: _dot(_slot_onehot(slots_ref, wb_ref[w]), h_ref[...])

    @pl.when(wf_ref[w] == 1)
    def _():
        o_ref[...] = part().astype(o_ref.dtype)

    @pl.when(jnp.logical_and(wf_ref[w] == 0, wv_ref[w] == 1))
    def _():
        o_ref[...] = (o_ref[...].astype(F32) + part()).astype(o_ref.dtype)


def _dispatch(h, slots, work):
    t, d = h.shape
    wb, wc, wf, wv = work
    n_slots = (t * TOP_K // MOE_ROWS + N_EXPERTS) * MOE_ROWS
    grid_spec = pltpu.PrefetchScalarGridSpec(
        num_scalar_prefetch=4,
        grid=(wb.shape[0],),
        in_specs=[pl.BlockSpec((TOP_K, MOE_TILE), lambda w, b, c, f, v: (0, c[w])),
                  pl.BlockSpec((MOE_TILE, d), lambda w, b, c, f, v: (c[w], 0))],
        out_specs=pl.BlockSpec((MOE_ROWS, d), lambda w, b, c, f, v: (b[w], 0)),
    )
    return pl.pallas_call(
        _dispatch_kernel,
        grid_spec=grid_spec,
        out_shape=jax.ShapeDtypeStruct((n_slots, d), BF16),
        compiler_params=_params(("arbitrary",)),
        name="moe_dispatch",
    )(wb, wc, wf, wv, slots, h)


def _collect_kernel(wb_ref, wc_ref, wf_ref, wv_ref, slots_ref, scol_ref, gate_ref, y_ref, o_ref):
    w = pl.program_id(0)

    def part():
        base = wb_ref[w] * MOE_ROWS
        scol = scol_ref[...] - base
        gates = jnp.where(jnp.logical_and(scol >= 0, scol < MOE_ROWS), gate_ref[...], 0.0)
        gate = jnp.sum(gates, axis=1, keepdims=True)
        return _dot(_slot_onehot(slots_ref, wb_ref[w]), y_ref[...], TN) * gate

    @pl.when(wf_ref[w] == 1)
    def _():
        o_ref[...] = part()

    @pl.when(jnp.logical_and(wf_ref[w] == 0, wv_ref[w] == 1))
    def _():
        o_ref[...] = o_ref[...] + part()


def _collect(y_buf, slots, slots_col, gates, work):
    n_slots, d = y_buf.shape
    t = slots.shape[1]
    wb, wc, wf, wv = work
    tile = lambda w, b, c, f, v: (c[w], 0)
    grid_spec = pltpu.PrefetchScalarGridSpec(
        num_scalar_prefetch=4,
        grid=(wb.shape[0],),
        in_specs=[pl.BlockSpec((TOP_K, MOE_TILE), lambda w, b, c, f, v: (0, c[w])),
                  pl.BlockSpec((MOE_TILE, TOP_K), tile), pl.BlockSpec((MOE_TILE, TOP_K), tile),
                  pl.BlockSpec((MOE_ROWS, d), lambda w, b, c, f, v: (b[w], 0))],
        out_specs=pl.BlockSpec((MOE_TILE, d), tile),
    )
    return pl.pallas_call(
        _collect_kernel,
        grid_spec=grid_spec,
        out_shape=jax.ShapeDtypeStruct((t, d), F32),
        compiler_params=_params(("arbitrary",)),
        name="moe_collect",
    )(wb, wc, wf, wv, slots, slots_col, gates, y_buf)


def _expert_kernel(be_ref, nb_ref, x_ref, w1_ref, w3_ref, w2_ref, o_ref):
    i = pl.program_id(0)

    @pl.when(i < nb_ref[0])
    def _():
        o_ref[...] = _swiglu_rows(x_ref[...], w1_ref, w3_ref, w2_ref).astype(o_ref.dtype)

    @pl.when(i >= nb_ref[0])
    def _():
        o_ref[...] = jnp.zeros_like(o_ref)


def _experts(buf, block_e, n_used, w1, w3, w2):
    n, d = buf.shape
    n_blocks = n // MOE_ROWS
    ff = w1.shape[-1]
    wspec = lambda shape: pl.BlockSpec((None,) + shape, lambda i, be, nb: (be[i], 0, 0),
                                       pipeline_mode=pl.Buffered(1))
    used = lambda i, be, nb: (jnp.minimum(i, nb[0] - 1), 0)
    grid_spec = pltpu.PrefetchScalarGridSpec(
        num_scalar_prefetch=2,
        grid=(n_blocks,),
        in_specs=[pl.BlockSpec((MOE_ROWS, d), used), wspec((d, ff)), wspec((d, ff)), wspec((ff, d))],
        out_specs=pl.BlockSpec((MOE_ROWS, d), lambda i, be, nb: (i, 0)),
    )
    return pl.pallas_call(
        _expert_kernel,
        grid_spec=grid_spec,
        out_shape=jax.ShapeDtypeStruct((n, d), BF16),
        compiler_params=_params(("arbitrary",)),
        name="experts",
    )(block_e, n_used, buf, w1, w3, w2)


def _combine_kernel(x_ref, mod_ref, y_ref, g_ref, b_ref, o_ref, *, alpha):
    mod = mod_ref[...]
    o_ref[...] = _ln(alpha * x_ref[...] + mod[5:6] * y_ref[...], LN_EPS) * g_ref[...] + b_ref[...]


def _combine(x1, mods, y_tok, g, b, ctx_len, alpha):
    bn, s, d = x1.shape
    ct = ctx_len // TM
    lt = (s - ctx_len) // TM
    full = lambda a: pl.BlockSpec(a.shape, lambda bb, i: (0,) * a.ndim)
    return pl.pallas_call(
        functools.partial(_combine_kernel, alpha=alpha),
        grid=(bn, lt),
        in_specs=[pl.BlockSpec((None, TM, d), lambda bb, i: (bb, i + ct, 0)),
                  pl.BlockSpec((None, None, 6, d), lambda bb, i: (bb, 1, 0, 0)),
                  pl.BlockSpec((TM, d), lambda bb, i: (bb * lt + i, 0)),
                  full(g), full(b)],
        out_specs=pl.BlockSpec((None, TM, d), lambda bb, i: (bb, i, 0)),
        out_shape=jax.ShapeDtypeStruct((bn, lt * TM, d), F32),
        compiler_params=_params(("parallel", "parallel")),
        name="moe_combine",
    )(x1, mods, y_tok, g, b)


def _work_lists(rank_at_tile, counts, pstart, block_e, n_used, n_blocks, n_tiles):
    n_work = n_tiles * N_EXPERTS + n_blocks
    blocks = jnp.arange(n_blocks, dtype=jnp.int32)
    used = blocks < n_used
    r0 = blocks * MOE_ROWS - pstart[block_e]
    r_last = jnp.minimum(r0 + MOE_ROWS, counts[block_e]) - 1
    cols = rank_at_tile.T[block_e]
    find = jax.vmap(lambda col, val: jnp.searchsorted(col, val, side="right"))
    lo = jnp.clip(find(cols, r0) - 1, 0, n_tiles - 1)
    hi = jnp.clip(find(cols, r_last) - 1, 0, n_tiles - 1)
    n_b = jnp.where(used, hi - lo + 1, 0)
    ends = jnp.cumsum(n_b)
    starts = ends - n_b
    total = ends[-1]
    w = jnp.arange(n_work, dtype=jnp.int32)
    valid = w < total
    wl = jnp.minimum(w, total - 1)
    blk = jnp.minimum(jnp.searchsorted(ends, wl, side="right"), n_blocks - 1).astype(jnp.int32)
    tile = (lo[blk] + (wl - starts[blk])).astype(jnp.int32)
    first = jnp.logical_and(valid, w == starts[blk])
    as_i32 = lambda a: a.astype(jnp.int32)
    by_block = (blk, tile, as_i32(first), as_i32(valid))
    order = jnp.argsort(jnp.where(valid, tile, n_tiles), stable=True)
    order = order[jnp.minimum(w, total - 1)]
    tile2, blk2 = tile[order], blk[order]
    first2 = jnp.logical_and(valid, jnp.concatenate([jnp.ones((1,), bool), tile2[1:] != tile2[:-1]]))
    by_tile = (blk2, tile2, as_i32(first2), as_i32(valid))
    return by_block, by_tile


def _moe_layer(x1, mods, router, w1, w3, w2, g, b, ctx_len, alpha):
    d = x1.shape[-1]
    router_p = jnp.pad(router, ((0, 0), (0, LANES - N_EXPERTS)))
    h, logits = _moe_pre(x1, mods, router_p, ctx_len)
    t = h.shape[0]
    top_v, top_i = lax.top_k(logits[:, :N_EXPERTS], TOP_K)
    gates = jax.nn.softmax(top_v, axis=-1)
    e_flat = top_i.reshape(-1)
    onehot = (e_flat[:, None] == jnp.arange(N_EXPERTS)[None, :]).astype(jnp.int32)
    ranks = jnp.cumsum(onehot, axis=0) - onehot
    rank = jnp.sum(ranks * onehot, axis=1)
    counts = jnp.sum(onehot, axis=0)
    padded = (counts + MOE_ROWS - 1) // MOE_ROWS * MOE_ROWS
    pend = jnp.cumsum(padded)
    pstart = pend - padded
    slot = (pstart[e_flat] + rank).astype(jnp.int32)
    n_blocks = t * TOP_K // MOE_ROWS + N_EXPERTS
    assert t % MOE_TILE == 0
    n_tiles = t // MOE_TILE
    block_e = jnp.minimum(jnp.searchsorted(pend, jnp.arange(n_blocks) * MOE_ROWS, side="right"),
                          N_EXPERTS - 1).astype(jnp.int32)
    n_used = (pend[-1:] // MOE_ROWS).astype(jnp.int32)
    rank_at_tile = jnp.concatenate([ranks[::MOE_TILE * TOP_K], counts[None, :]], axis=0)
    by_block, by_tile = _work_lists(rank_at_tile, counts, pstart, block_e, n_used, n_blocks, n_tiles)
    slots_col = slot.reshape(t, TOP_K)
    slots = slots_col.T
    buf = _dispatch(h, slots, by_block)
    y_buf = _experts(buf, block_e, n_used, w1, w3, w2)
    y_tok = _collect(y_buf, slots, slots_col, gates, by_tile)
    return _combine(x1, mods, y_tok, g, b, ctx_len, alpha)


def _rope_tables(n_rows, ctx_len, dim, lane_lo):
    quarter = dim // 4
    inv = ROPE_THETA ** (-jnp.arange(quarter, dtype=F32) / quarter)
    rows = jnp.repeat(jnp.arange(n_rows, dtype=F32), GRID_W)
    cols = jnp.tile(jnp.arange(GRID_W, dtype=F32), n_rows)
    ang = jnp.concatenate([rows[:, None] * inv, rows[:, None] * inv,
                           cols[:, None] * inv, cols[:, None] * inv], axis=-1)
    sign = jnp.tile(jnp.concatenate([-jnp.ones(quarter, F32), jnp.ones(quarter, F32)]), 2)
    cos, sin = jnp.cos(ang), jnp.sin(ang) * sign
    length = cos.shape[0]
    if lane_lo == 0:
        reps = LANES // dim
        cos, sin = jnp.tile(cos, (1, reps)), jnp.tile(sin, (1, reps))
    else:
        pad = ((0, 0), (lane_lo, LANES - lane_lo - dim))
        cos = jnp.pad(cos, pad, constant_values=1.0)
        sin = jnp.pad(sin, pad)
    cos = jnp.concatenate([jnp.ones((ctx_len, LANES), F32), cos], axis=0)
    sin = jnp.concatenate([jnp.zeros((ctx_len, LANES), F32), sin], axis=0)
    return cos, sin


def _block_diag2(w):
    z = jnp.zeros_like(w[0])
    return jnp.concatenate([jnp.concatenate([w[0], z], axis=1), jnp.concatenate([z, w[1]], axis=1)], axis=0)


def _lambda_init(layer):
    return 0.8 - 0.6 * math.exp(-0.3 * layer)


def kernel(x, c, ctx, c_ctx, ada_w, ada_b, w_in, w_out, ln1_g, ln1_b, ln2_g, ln2_b,
           lam_q1, lam_k1, lam_q2, lam_k2, diff_norm_g, shift_mu, w0, w2, a0, a2, g2,
           k_k, k_a, r_k, lnx_g, lnx_b, q_norm_g, w_uq, kv_norm_g, w_ukv,
           ff_w1, ff_w3, ff_w2, router, moe_w1, moe_w3, moe_w2):
    bn, seq, d = x.shape
    ctx_len = ctx.shape[1]
    depth = ada_w.shape[0]
    assert d == D_MODEL and seq % TM == 0 and ctx_len % TM == 0 and seq % GRID_W == 0
    alpha = (2.0 * depth) ** 0.25
    n_grid_rows = seq // GRID_W
    cos_a, sin_a = _rope_tables(n_grid_rows, ctx_len, A_QK_DIM, 0)
    cos_c, sin_c = _rope_tables(n_grid_rows, ctx_len, C_ROPE, C_NOPE)

    cond_rows = 8 * ((bn + 1 + 7) // 8)
    cond = jnp.zeros((cond_rows, d), F32).at[:bn].set(c).at[bn].set(c_ctx)
    xs = jnp.concatenate([ctx, x], axis=1)

    for i in range(depth):
        with_ctx = i < depth - 1
        m = _ada(cond, ada_w[i], ada_b[i]).reshape(cond_rows, 6, d)
        mods = jnp.stack([jnp.broadcast_to(m[bn], (bn, 6, d)), m[:bn]], axis=1)

        wc = w_in[i][:, N_A + N_B:]
        kpe_w = jnp.pad(wc[:, C_Q_RANK + C_KV_RANK:], ((0, 0), (C_NOPE, LANES - C_NOPE - C_ROPE)))
        w_p = jnp.concatenate([w_in[i][:, :N_A + N_B], wc[:, :C_Q_RANK + C_KV_RANK], kpe_w], axis=1).astype(BF16)
        wq = w_uq[i].reshape(C_Q_RANK, C_HEADS, C_NOPE + C_ROPE)
        wq_p = jnp.pad(wq, ((0, 0), (0, 0), (0, LANES - C_NOPE - C_ROPE))).reshape(C_Q_RANK, -1).astype(BF16)
        wkv = w_ukv[i].reshape(C_KV_RANK, C_HEADS, C_NOPE + C_V)
        wk_p = jnp.pad(wkv[:, :, :C_NOPE], ((0, 0), (0, 0), (0, LANES - C_NOPE))).reshape(C_KV_RANK, -1).astype(BF16)
        wv_p = wkv[:, :, C_NOPE:].reshape(C_KV_RANK, -1).astype(BF16)
        mla_consts = (q_norm_g[i].reshape(1, -1), kv_norm_g[i].reshape(1, -1), wq_p, wk_p, wv_p)
        qa, ka, va, pb, qc, kc, vc = _inproj(xs, mods, w_p, cos_a, sin_a, mla_consts, cos_c, sin_c, ctx_len)

        lam_p = jnp.stack([lam_q1[i], lam_k1[i], lam_q2[i], lam_k2[i]])
        g_col = jnp.broadcast_to(diff_norm_g[i][:, None], (A_V_DIM, TM))
        a_out = _attention(qa, ka, va, lam_p, g_col, mode="diff", ctx_len=ctx_len, lam_init=_lambda_init(i))
        c_out = _attention(qc, kc, vc, lam_p, g_col, mode="mla", ctx_len=ctx_len)

        r, v, kkn, ld, kd, beta, gate, bonus = _rwkv_prep(
            pb, shift_mu[i].reshape(1, -1), w0[i].reshape(1, -1), _block_diag2(w2[i]).astype(BF16),
            a0[i].reshape(1, -1), _block_diag2(a2[i]).astype(BF16), g2[i].astype(BF16),
            k_k[i].reshape(1, -1), k_a[i].reshape(1, -1), r_k[i].reshape(1, -1), ctx_len)
        yf, yb = _rwkv_pairs(r, v, kkn, ld, kd, beta, ctx_len)

        wo = w_out[i].astype(BF16)
        x1 = _outproj(xs, mods, a_out, yf, yb, bonus, gate, c_out,
                      wo[:A_WIDTH], wo[A_WIDTH:A_WIDTH + B_WIDTH], wo[A_WIDTH + B_WIDTH:],
                      lnx_g[i].reshape(1, -1), lnx_b[i].reshape(1, -1),
                      ln1_g[i].reshape(1, -1), ln1_b[i].reshape(1, -1), ctx_len, alpha)

        j = i // 2
        g2n, b2n = ln2_g[i].reshape(1, -1), ln2_b[i].reshape(1, -1)
        if i % 2 == 0:
            xs = _ffn(x1, mods, ff_w1[j].astype(BF16), ff_w3[j].astype(BF16), ff_w2[j].astype(BF16),
                      g2n, b2n, ctx_len, alpha)
        else:
            if with_ctx:
                raise NotImplementedError("routed FFN on the context rows is not needed at this depth")
            return _moe_layer(x1, mods, router[j], moe_w1[j].astype(BF16), moe_w3[j].astype(BF16),
                              moe_w2[j].astype(BF16), g2n, b2n, ctx_len, alpha)
    return xs[:, ctx_len:]
```

```python
import functools
import math

import jax
import jax.numpy as jnp
from jax import lax
from jax.experimental import pallas as pl
from jax.experimental.pallas import tpu as pltpu

F32 = jnp.float32
BF16 = jnp.bfloat16

D_MODEL = 1024
GRID_W = 64
ROPE_THETA = 10000.0
A_HEADS, A_QK_DIM, A_V_DIM = 4, 64, 128
A_WIDTH = A_HEADS * A_V_DIM
A_QK_COLS = 2 * A_HEADS * A_QK_DIM
B_HEADS, B_HEAD = 4, 64
B_WIDTH = B_HEADS * B_HEAD
B_DECAY_RANK, B_A_RANK, B_GATE_RANK = 64, 64, 128
C_HEADS, C_NOPE, C_ROPE, C_V = 4, 64, 32, 64
C_WIDTH = C_HEADS * C_V
C_Q_RANK, C_KV_RANK = 256, 128
N_A = 2 * A_QK_COLS + A_WIDTH
N_B = 3 * B_WIDTH + 2 * B_DECAY_RANK + 2 * B_A_RANK + B_GATE_RANK
N_C = C_Q_RANK + C_KV_RANK + C_ROPE
N_C_PAD = C_Q_RANK + C_KV_RANK + 128
D_FF = 3584
N_EXPERTS = 8
TOP_K = 2
LN_EPS = 1e-6
RMS_EPS = 1e-6
GN_EPS = 64e-5

LOG2E = math.log2(math.e)
LANES = 128
TM = 256
CHUNK = 64
ROWS_B = 2
RWKV_ROWS = 4
ATT_TK = 1024
ONES_ROWS = 16
ATT_PAR = 4
MOE_ROWS = 512
MOE_TILE = 512
FF_CHUNK = 512
VMEM_LIMIT = 56 * 1024 * 1024

NN = (((1,), (0,)), ((), ()))
NT = (((1,), (1,)), ((), ()))
TN = (((0,), (0,)), ((), ()))


def _params(sem, vmem=VMEM_LIMIT, flags=None):
    return pltpu.CompilerParams(dimension_semantics=sem, vmem_limit_bytes=vmem, flags=flags)


def _split(x, n):
    parts, r = [], x
    for _ in range(n):
        p = r.astype(BF16)
        parts.append(p)
        r = r - p.astype(F32)
    return parts


def _dot(a, b, dn=NN):
    return lax.dot_general(a, b, dn, preferred_element_type=F32)


def _dot1(a, b, dn=NN):
    return _dot(a.astype(BF16), b.astype(BF16), dn)


def _dot3(a, b, dn=NN):
    a0, a1 = _split(a, 2)
    b0, b1 = _split(b, 2)
    return _dot(a0, b0, dn) + (_dot(a0, b1, dn) + _dot(a1, b0, dn))


def _dot6(a, b, dn=NN):
    a0, a1, a2 = _split(a, 3)
    b0, b1, b2 = _split(b, 3)
    lo = _dot(a1, b1, dn) + (_dot(a0, b2, dn) + _dot(a2, b0, dn))
    return _dot(a0, b0, dn) + ((_dot(a0, b1, dn) + _dot(a1, b0, dn)) + lo)


def _dot_exact_lhs(a_bf16, b, dn=NN, n=3):
    parts = _split(b, n)
    out = _dot(a_bf16, parts[-1], dn)
    for p in parts[-2::-1]:
        out = out + _dot(a_bf16, p, dn)
    return out


def _dot_exact_rhs(a, b_bf16, dn=NN, n=3):
    parts = _split(a, n)
    out = _dot(parts[-1], b_bf16, dn)
    for p in parts[-2::-1]:
        out = out + _dot(p, b_bf16, dn)
    return out


def _ln(x, eps):
    mu = jnp.mean(x, axis=-1, keepdims=True)
    xc = x - mu
    return xc * lax.rsqrt(jnp.mean(xc * xc, axis=-1, keepdims=True) + eps)


def _sigmoid(x):
    return 1.0 / (1.0 + jnp.exp(-x))


def _silu(x):
    return x * _sigmoid(x)


def _group_ones(width, group):
    r = lax.broadcasted_iota(jnp.int32, (width, width), 0) // group
    c = lax.broadcasted_iota(jnp.int32, (width, width), 1) // group
    return (r == c).astype(BF16)


def _partner(x, half):
    lane = lax.broadcasted_iota(jnp.int32, x.shape, 1)
    up = pltpu.roll(x, LANES - half, 1)
    dn = pltpu.roll(x, half, 1)
    return jnp.where((lane % (2 * half)) < half, up, dn)


def _rope(x, cos, sin, half):
    return x * cos + _partner(x, half) * sin


def _ada_kernel(c_ref, w_ref, b_ref, o_ref):
    o_ref[...] = _dot3(_silu(c_ref[...]), w_ref[...]) + b_ref[...]


def _ada(cond, w, b):
    rows, d = cond.shape
    n = w.shape[1]
    tn = 1536
    return pl.pallas_call(
        _ada_kernel,
        grid=(n // tn,),
        in_specs=[pl.BlockSpec((rows, d), lambda j: (0, 0)),
                  pl.BlockSpec((d, tn), lambda j: (0, j)),
                  pl.BlockSpec((1, tn), lambda j: (0, j))],
        out_specs=pl.BlockSpec((rows, tn), lambda j: (0, j)),
        out_shape=jax.ShapeDtypeStruct((rows, n), F32),
        compiler_params=_params(("parallel",)),
        name="ada",
    )(cond, w, b.reshape(1, n))


def _inproj_kernel(x_ref, mod_ref, w_ref, cos_ref, sin_ref,
                   qg_ref, kvg_ref, wq_ref, wk_ref, wv_ref, cosc_ref, sinc_ref,
                   q_ref, k_ref, v_ref, pb_ref, cq_ref, ck_ref, cv_ref):
    nb = x_ref.shape[0]
    h = jnp.concatenate([(_ln(x_ref[r], LN_EPS) * (1.0 + mod_ref[r, 1:2]) + mod_ref[r, 0:1]).astype(BF16)
                         for r in range(nb)], axis=0)
    cos, sin = cos_ref[...], sin_ref[...]
    scale = A_QK_DIM ** -0.5 * LOG2E
    rows = lambda a, r: a[r * TM:(r + 1) * TM]
    for j in range(A_QK_COLS // LANES):
        sl = slice(j * LANES, (j + 1) * LANES)
        qj = _dot(h, w_ref[:, sl])
        kj = _dot(h, w_ref[:, A_QK_COLS + j * LANES:A_QK_COLS + (j + 1) * LANES])
        vj = _dot(h, w_ref[:, 2 * A_QK_COLS + j * LANES:2 * A_QK_COLS + (j + 1) * LANES])
        for r in range(nb):
            q_ref[r, sl, :] = (_rope(rows(qj, r), cos, sin, A_QK_DIM // 4) * scale).T.astype(BF16)
            k_ref[r, :, sl] = _rope(rows(kj, r), cos, sin, A_QK_DIM // 4).astype(BF16)
            v_ref[r, sl, :] = rows(vj, r).T.astype(BF16)
    pb = _dot(h, w_ref[:, N_A:N_A + N_B])
    for r in range(nb):
        pb_ref[r] = rows(pb, r)
    _mla_heads(_dot(h, w_ref[:, N_A + N_B:]), nb, qg_ref, kvg_ref, wq_ref, wk_ref, wv_ref,
               cosc_ref[...], sinc_ref[...], cq_ref, ck_ref, cv_ref)


def _batch_rows(bn):
    return ROWS_B if bn % ROWS_B == 0 else 1


def _mod_spec(ctx_len, d, nb=None, first_tile=0):
    ct = ctx_len // TM
    return pl.BlockSpec((nb, None, 6, d), lambda b, i: (b, jnp.where(i + first_tile >= ct, 1, 0), 0, 0))


def _inproj(xs, mods, w_p, cos_a, sin_a, mla_consts, cos_c, sin_c, ctx_len):
    bn, s, d = xs.shape
    n_tiles = s // TM
    n_w = w_p.shape[1]
    nb = _batch_rows(bn)
    row = lambda width: pl.BlockSpec((nb, TM, width), lambda b, i: (b, i, 0))
    col = lambda width: pl.BlockSpec((nb, width, TM), lambda b, i: (b, 0, i))
    full = lambda a: pl.BlockSpec(a.shape, lambda b, i: (0,) * a.ndim)
    tab = pl.BlockSpec((TM, LANES), lambda b, i: (i, 0))
    hw = C_HEADS * LANES
    outs = [jax.ShapeDtypeStruct((bn, A_QK_COLS, s), BF16),
            jax.ShapeDtypeStruct((bn, s, A_QK_COLS), BF16),
            jax.ShapeDtypeStruct((bn, A_WIDTH, s), BF16),
            jax.ShapeDtypeStruct((bn, s, N_B), F32),
            jax.ShapeDtypeStruct((bn, hw, s), BF16),
            jax.ShapeDtypeStruct((bn, s, hw), BF16),
            jax.ShapeDtypeStruct((bn, C_WIDTH, s), BF16)]
    return pl.pallas_call(
        _inproj_kernel,
        grid=(bn // nb, n_tiles),
        in_specs=[row(d), _mod_spec(ctx_len, d, nb), full(w_p), tab, tab]
                 + [full(a) for a in mla_consts] + [tab, tab],
        out_specs=[col(A_QK_COLS), row(A_QK_COLS), col(A_WIDTH), row(N_B), col(hw), row(hw), col(C_WIDTH)],
        out_shape=outs,
        compiler_params=_params(("parallel", "parallel")),
        name="inproj",
    )(xs, mods, w_p, cos_a, sin_a, *mla_consts, cos_c, sin_c)


def _mla_heads(pc, nb, qg_ref, kvg_ref, wq_ref, wk_ref, wv_ref, cos, sin, q_ref, k_ref, v_ref):
    cq = pc[:, :C_Q_RANK]
    cq = cq * lax.rsqrt(jnp.mean(cq * cq, axis=-1, keepdims=True) + RMS_EPS) * qg_ref[...]
    ckv = pc[:, C_Q_RANK:C_Q_RANK + C_KV_RANK]
    ckv = ckv * lax.rsqrt(jnp.mean(ckv * ckv, axis=-1, keepdims=True) + RMS_EPS) * kvg_ref[...]
    cqb, ckvb = cq.astype(BF16), ckv.astype(BF16)
    scale = (C_NOPE + C_ROPE) ** -0.5 * LOG2E
    rows = lambda a, r: a[r * TM:(r + 1) * TM]
    kpe = [_rope(rows(pc, r)[:, C_Q_RANK + C_KV_RANK:], cos, sin, C_ROPE // 4) for r in range(nb)]
    for h in range(C_HEADS):
        sl = slice(h * LANES, (h + 1) * LANES)
        qh = _dot(cqb, wq_ref[:, sl])
        kh = _dot(ckvb, wk_ref[:, sl])
        for r in range(nb):
            q_ref[r, sl, :] = (_rope(rows(qh, r), cos, sin, C_ROPE // 4) * scale).T.astype(BF16)
            k_ref[r, :, sl] = (rows(kh, r) + kpe[r]).astype(BF16)
    for j in range(C_WIDTH // LANES):
        sl = slice(j * LANES, (j + 1) * LANES)
        vj = _dot(ckvb, wv_ref[:, sl])
        for r in range(nb):
            v_ref[r, sl, :] = rows(vj, r).T.astype(BF16)


def _attn_kernel(lam_ref, g_ref, qt_ref, k_ref, vt_ref, o_ref, *, mode, ctx_tiles, ctx_len, tk, first_tile,
                 lam_init):
    qi = pl.program_id(2) + first_tile
    tq = qt_ref.shape[1]
    n_par = qt_ref.shape[0] // LANES
    dv = vt_ref.shape[0] // n_par
    width = 2 * tq if mode == "diff" else tq
    half = lax.broadcasted_iota(jnp.int32, (LANES, 1), 0) < (LANES // 2)
    q_ops = []
    for g in range(n_par):
        qt = qt_ref[g * LANES:(g + 1) * LANES, :]
        if mode == "diff":
            qt = jnp.concatenate([jnp.where(half, qt, jnp.zeros_like(qt)),
                                  jnp.where(half, jnp.zeros_like(qt), qt)], axis=1)
        q_ops.append(qt)

    def fold(x, reduce):
        rows = x.shape[0]
        while rows > 8:
            g = max(d for d in range(2, 9) if (rows // 8) % d == 0)
            rows //= g
            x = reduce(x.reshape(g, rows, x.shape[1]), axis=0)
        return x

    def scores(kc, g):
        return _dot(kc[:, g * LANES:(g + 1) * LANES], q_ops[g])

    def absorb(kc, vc, carry):
        ss = [scores(kc, g) for g in range(n_par)]
        m_new = [jnp.maximum(carry[g][0], jnp.max(fold(ss[g], jnp.max), axis=0, keepdims=True))
                 for g in range(n_par)]
        alpha = [jnp.exp2(carry[g][0] - m_new[g]) for g in range(n_par)]
        ps = [jnp.exp2(ss[g] - m_new[g]).astype(BF16) for g in range(n_par)]
        ones = jnp.ones((ONES_ROWS, vc.shape[1]), BF16)
        acc = [alpha[g] * carry[g][1] + _dot(jnp.concatenate([vc[g * dv:(g + 1) * dv], ones], axis=0), ps[g])
               for g in range(n_par)]
        return tuple((m_new[g], acc[g]) for g in range(n_par))

    n_latent_chunks = (k_ref.shape[0] - ctx_len) // tk

    def finish(stats):
        outs = [a[:dv] / a[dv:dv + 1] for (_, a) in stats]
        if mode == "diff":
            lp = lam_ref[...]
            lam = (jnp.exp(jnp.sum(lp[0:1] * lp[1:2], axis=-1, keepdims=True))
                   - jnp.exp(jnp.sum(lp[2:3] * lp[3:4], axis=-1, keepdims=True)) + lam_init)
            for g in range(n_par):
                o = outs[g][:, :tq] - lam * outs[g][:, tq:]
                o = o * lax.rsqrt(jnp.mean(o * o, axis=0, keepdims=True) + RMS_EPS) * g_ref[...]
                o = o * (1.0 - lam_init)
                o_ref[:, g * LANES:(g + 1) * LANES] = o.T.astype(o_ref.dtype)
        else:
            per_slab = LANES // dv
            for j in range(n_par // per_slab):
                o = jnp.concatenate(outs[j * per_slab:(j + 1) * per_slab], axis=0)
                o_ref[:, j * LANES:(j + 1) * LANES] = o.T.astype(o_ref.dtype)

    def run(first_keys, n_more):
        neg = jnp.full((1, width), -1e30, F32)
        zacc = jnp.zeros((dv + ONES_ROWS, width), F32)
        stats = absorb(k_ref[0:first_keys, :], vt_ref[:, 0:first_keys], ((neg, zacc),) * n_par)

        def body(j, stats):
            off = pl.multiple_of(first_keys + j * tk, LANES)
            return absorb(k_ref[pl.ds(off, tk), :], vt_ref[:, pl.ds(off, tk)], stats)

        finish(lax.fori_loop(0, n_more, body, stats))

    @pl.when(qi < ctx_tiles)
    def _():
        run(ctx_len, 0)

    @pl.when(qi >= ctx_tiles)
    def _():
        run(ctx_len + tk, n_latent_chunks - 1)


def _attention(qt, k, vt, lam_p, g, *, mode, ctx_len, first_tile, lam_init=0.0):
    bn, s, _ = k.shape
    heads = k.shape[2] // LANES
    dv = vt.shape[1] // heads
    par = math.gcd(heads, ATT_PAR)
    tk = math.gcd(s - ctx_len, ATT_TK)
    assert tk % LANES == 0 and ctx_len % LANES == 0 and (par * dv) % LANES == 0
    kern = functools.partial(_attn_kernel, mode=mode, ctx_tiles=ctx_len // TM, ctx_len=ctx_len, tk=tk,
                             first_tile=first_tile, lam_init=lam_init)
    return pl.pallas_call(
        kern,
        grid=(bn, heads // par, s // TM - first_tile),
        in_specs=[pl.BlockSpec(lam_p.shape, lambda b, h, i: (0, 0)),
                  pl.BlockSpec(g.shape, lambda b, h, i: (0, 0)),
                  pl.BlockSpec((None, par * LANES, TM), lambda b, h, i: (b, h, i + first_tile)),
                  pl.BlockSpec((None, s, par * LANES), lambda b, h, i: (b, 0, h)),
                  pl.BlockSpec((None, par * dv, s), lambda b, h, i: (b, h, 0))],
        out_specs=pl.BlockSpec((None, TM, par * dv), lambda b, h, i: (b, i + first_tile, h)),
        out_shape=jax.ShapeDtypeStruct((bn, s, heads * dv), BF16),
        compiler_params=_params(("parallel", "parallel", "parallel")),
        name="attn_" + mode,
    )(lam_p, g, qt, k, vt)


def _rwkv_prep_kernel(pb_ref, prev_ref, next_ref, mu_ref, w0_ref, w2_ref, a0_ref, a2_ref, g2_ref,
                      kk_ref, ka_ref, rk_ref,
                      r_out, v_out, kkn_out, ld_out, kd_out, beta_out, g_out, bonus_out,
                      *, ctx_tiles, n_tiles):
    i = pl.program_id(1)
    x = pb_ref[...]
    row = lax.broadcasted_iota(jnp.int32, (TM, 1), 0)
    has_prev = jnp.logical_and(i != 0, i != ctx_tiles)
    has_next = jnp.logical_and(i != ctx_tiles - 1, i != n_tiles - 1)
    prev_edge = jnp.where(has_prev, prev_ref[7:8, :], 0.0)
    next_edge = jnp.where(has_next, next_ref[0:1, :], 0.0)
    xp = jnp.where(row == 0, prev_edge, pltpu.roll(x, 1, 0))
    xn = jnp.where(row == TM - 1, next_edge, pltpu.roll(x, TM - 1, 0))
    z = x + mu_ref[...] * (0.5 * (xp + xn) - x)

    r = z[:, :B_WIDTH]
    k = z[:, B_WIDTH:2 * B_WIDTH]
    v = z[:, 2 * B_WIDTH:3 * B_WIDTH]
    o = 3 * B_WIDTH
    wd = z[:, o:o + 2 * B_DECAY_RANK]
    o += 2 * B_DECAY_RANK
    ad = z[:, o:o + 2 * B_A_RANK]
    o += 2 * B_A_RANK
    gd = z[:, o:]

    u = w0_ref[...] + _dot1(jnp.tanh(wd), w2_ref[...])
    ld = -math.exp(-0.5) * _sigmoid(u)
    lr = _sigmoid(a0_ref[...] + _dot1(ad, a2_ref[...]))
    g_out[...] = _dot1(_sigmoid(gd), g2_ref[...])

    ones = _group_ones(B_WIDTH, B_HEAD)
    kk = k * kk_ref[...]
    norm = jnp.sqrt(_dot_exact_rhs(kk * kk, ones))
    kkn = kk / jnp.maximum(norm, 1e-12)
    ka = ka_ref[...]
    kd_sum = jnp.zeros_like(k)
    for d in range(2):
        lr_d = lr[:, d * B_WIDTH:(d + 1) * B_WIDTH]
        kd = k * (1.0 + (lr_d - 1.0) * ka)
        kd_sum = kd_sum + kd
        ld_out[d] = ld[:, d * B_WIDTH:(d + 1) * B_WIDTH]
        kd_out[d] = kd
        beta_out[d] = kkn * lr_d
    bonus_out[...] = _dot_exact_rhs(r * kd_sum * rk_ref[...], ones) * v
    r_out[...] = r
    v_out[...] = v
    kkn_out[...] = kkn


def _rwkv_prep(pb, mu, w0, w2bd, a0, a2bd, g2, k_k, k_a, r_k, ctx_len):
    bn, s, _ = pb.shape
    n_tiles = s // TM
    eight = TM // 8
    row = lambda width: pl.BlockSpec((None, TM, width), lambda b, i: (b, i, 0))
    drow = pl.BlockSpec((2, None, TM, B_WIDTH), lambda b, i: (0, b, i, 0))
    full = lambda a: pl.BlockSpec(a.shape, lambda b, i: (0,) * a.ndim)
    one = jax.ShapeDtypeStruct((bn, s, B_WIDTH), F32)
    two = jax.ShapeDtypeStruct((2, bn, s, B_WIDTH), F32)
    kern = functools.partial(_rwkv_prep_kernel, ctx_tiles=ctx_len // TM, n_tiles=n_tiles)
    params = (mu, w0, w2bd, a0, a2bd, g2, k_k, k_a, r_k)
    return pl.pallas_call(
        kern,
        grid=(bn, n_tiles),
        in_specs=[row(N_B),
                  pl.BlockSpec((None, 8, N_B), lambda b, i: (b, jnp.maximum(i * eight - 1, 0), 0)),
                  pl.BlockSpec((None, 8, N_B), lambda b, i: (b, jnp.minimum((i + 1) * eight, s // 8 - 1), 0)),
                  ] + [full(p) for p in params],
        out_specs=[row(B_WIDTH), row(B_WIDTH), row(B_WIDTH), drow, drow, drow, row(B_WIDTH), row(B_WIDTH)],
        out_shape=[one, one, one, two, two, two, one, one],
        compiler_params=_params(("parallel", "parallel")),
        name="rwkv_prep",
    )(pb, pb, pb, *params)


def _pair_diag(x):
    lo = lax.broadcasted_iota(jnp.int32, (1, LANES), 1) < B_HEAD
    z = jnp.zeros_like(x)
    return jnp.concatenate([jnp.where(lo, x, z), jnp.where(lo, z, x)], axis=0)


def _pair_pick(x):
    lo = lax.broadcasted_iota(jnp.int32, (1, LANES), 1) < B_HEAD
    return jnp.where(lo, x[:B_HEAD], x[B_HEAD:])


def _rwkv_pair_kernel(rf_ref, vf_ref, kkf_ref, rb_ref, vb_ref, kkb_ref,
                      ldf_ref, kdf_ref, betaf_ref, ldb_ref, kdb_ref, betab_ref,
                      yf_ref, yb_ref, h_ref):
    c = pl.program_id(1)

    @pl.when(c == 0)
    def _():
        h_ref[...] = jnp.zeros_like(h_ref)

    n_pairs = B_WIDTH // LANES
    ti = lax.broadcasted_iota(jnp.int32, (CHUNK, LANES), 0)
    si = lax.broadcasted_iota(jnp.int32, (CHUNK, LANES), 1) % CHUNK
    t64 = lax.broadcasted_iota(jnp.int32, (CHUNK, CHUNK), 0)
    s64 = lax.broadcasted_iota(jnp.int32, (CHUNK, CHUNK), 1)
    eye = ti == si
    dirs = ((rf_ref, vf_ref, kkf_ref, ldf_ref, kdf_ref, betaf_ref, False),
            (rb_ref, vb_ref, kkb_ref, ldb_ref, kdb_ref, betab_ref, True))

    units = []
    n_rows = rf_ref.shape[0]
    for bi in range(n_rows):
      for d, (r_ref, v_ref, kk_ref, ld_ref, kd_ref, beta_ref, rev) in enumerate(dirs):
        strict = (si > ti) if rev else (ti > si)
        incl = (si >= ti) if rev else (ti >= si)
        tri = ((s64 >= t64) if rev else (t64 >= s64)).astype(BF16)
        ld_all = ld_ref[bi]
        cl_all = _dot_exact_lhs(tri, ld_all)
        for p in range(n_pairs):
            sl = slice(p * LANES, (p + 1) * LANES)
            ld, cl = ld_all[:, sl], cl_all[:, sl]
            total = jnp.sum(ld, axis=0, keepdims=True)
            inv_gam = jnp.exp(-cl)
            to_end = jnp.exp(total - cl)
            kk, kd, beta = kk_ref[bi, :, sl], kd_ref[bi, :, sl], beta_ref[bi, :, sl]
            units.append(dict(
                bi=bi, d=d, p=p, sl=sl, strict=strict, incl=incl, v=v_ref[bi, :, sl].astype(BF16),
                a_bar=-kk * jnp.exp(cl - ld), r_bar=r_ref[bi, :, sl] * jnp.exp(cl),
                b_til=(beta * inv_gam).astype(BF16), k_til=(kd * inv_gam).astype(BF16),
                b_hat=(beta * to_end).astype(BF16), k_hat=(kd * to_end).astype(BF16),
                gam_c=jnp.exp(total)))

    for u in units:
        x_mat = jnp.concatenate([u["a_bar"], u["r_bar"]], axis=0).astype(BF16)
        rhs = jnp.concatenate([_pair_diag(u["b_til"]), _pair_diag(u["k_til"])], axis=0)
        xbk = _dot(x_mat, rhs, NT)
        u["n_ab"] = jnp.where(u["strict"], xbk[:CHUNK, :LANES], 0.0)
        u["l_rb"] = jnp.where(u["incl"], xbk[CHUNK:, :LANES], 0.0)
        n_ak = jnp.where(u["strict"], xbk[:CHUNK, LANES:], 0.0)
        l_rk = jnp.where(u["incl"], xbk[CHUNK:, LANES:], 0.0)
        u["nl"] = jnp.concatenate([n_ak, l_rk], axis=0).astype(BF16)
    for u in units:
        nv = _dot(u["nl"], _pair_diag(u["v"]))
        u["w"], u["u0"], u["lrkv"] = u["a_bar"], nv[:CHUNK], nv[CHUNK:]
        u["npow"] = u["n_ab"].astype(BF16)

    steps = int(math.log2(CHUNK))
    for kstep in range(steps):
        for u in units:
            rhs = jnp.concatenate([_pair_diag(u["w"].astype(BF16)), _pair_diag(u["u0"].astype(BF16))], axis=1)
            upd = _dot(u["npow"], rhs)
            u["w"] = u["w"] + upd[:, :LANES]
            u["u0"] = u["u0"] + upd[:, LANES:]
        if kstep + 1 < steps:
            for u in units:
                u["npow"] = _dot(u["npow"], _pair_diag(u["npow"])).astype(BF16)

    for u in units:
        wb, ub = u["w"].astype(BF16), u["u0"].astype(BF16)
        lx = _dot(u["l_rb"].astype(BF16), jnp.concatenate([_pair_diag(wb), _pair_diag(ub)], axis=1))
        u["p_mat"] = u["r_bar"] + lx[:, :LANES]
        u["y0"] = u["lrkv"] + lx[:, LANES:]
        lhs = jnp.concatenate([u["b_hat"], u["k_hat"]], axis=0)
        rhs = jnp.concatenate([jnp.concatenate([wb, ub], axis=1),
                               jnp.concatenate([jnp.zeros_like(wb), u["v"]], axis=1)], axis=0)
        mg = _dot(lhs, rhs, TN)
        u["m_full"] = _pair_pick(mg[:, :LANES]) + jnp.where(eye, u["gam_c"], 0.0)
        u["g_mat"] = _pair_pick(mg[:, LANES:])

    for u in units:
        bi, d, p = u["bi"], u["d"], u["p"]
        h0 = h_ref[bi, d, p]
        a0, a1 = _split(jnp.concatenate([u["p_mat"], u["m_full"]], axis=0), 2)
        h_hi, h_lo = _split(h0, 2)
        bh, bl = _pair_diag(h_hi), _pair_diag(h_lo)
        out = _dot(a0, bh) + (_dot(a0, bl) + _dot(a1, bh))
        (yb_ref if d else yf_ref)[bi, :, u["sl"]] = out[:CHUNK] + u["y0"]
        h_ref[bi, d, p] = out[CHUNK:] + u["g_mat"]


def _rwkv_pairs(r, v, kk, ld, kd, beta, ctx_len):
    bn, s, _ = r.shape
    n_chunks = s // CHUNK
    ctx_chunks = ctx_len // CHUNK
    rows = RWKV_ROWS if bn % RWKV_ROWS == 0 else 1

    def back(c):
        return jnp.where(c < ctx_chunks, ctx_chunks - 1 - c, n_chunks - 1 + ctx_chunks - c)

    fwd = pl.BlockSpec((rows, CHUNK, B_WIDTH), lambda b, c: (b, c, 0))
    bwd = pl.BlockSpec((rows, CHUNK, B_WIDTH), lambda b, c: (b, back(c), 0))
    fwd_d = pl.BlockSpec((None, rows, CHUNK, B_WIDTH), lambda b, c: (0, b, c, 0))
    bwd_d = pl.BlockSpec((None, rows, CHUNK, B_WIDTH), lambda b, c: (1, b, back(c), 0))
    y = jax.ShapeDtypeStruct((bn, s, B_WIDTH), F32)
    return pl.pallas_call(
        _rwkv_pair_kernel,
        grid=(bn // rows, n_chunks),
        in_specs=[fwd, fwd, fwd, bwd, bwd, bwd, fwd_d, fwd_d, fwd_d, bwd_d, bwd_d, bwd_d],
        out_specs=[fwd, bwd],
        out_shape=[y, y],
        scratch_shapes=[pltpu.VMEM((rows, 2, B_WIDTH // LANES, B_HEAD, LANES), F32)],
        compiler_params=_params(("parallel", "arbitrary")),
        name="rwkv_scan",
    )(r, v, kk, r, v, kk, ld, kd, beta, ld, kd, beta)


def _outproj_kernel(x_ref, mod_ref, a_ref, yf_ref, yb_ref, bonus_ref, g_ref, c_ref,
                    wa_ref, wb_ref, wc_ref, lnxg_ref, lnxb_ref, ln1g_ref, ln1b_ref, o_ref, *, alpha):
    nb = x_ref.shape[0]
    stack = lambda ref: jnp.concatenate([ref[r] for r in range(nb)], axis=0)
    y = stack(yf_ref) + stack(yb_ref) + stack(bonus_ref)
    ones = _group_ones(B_WIDTH, B_HEAD)
    inv = 1.0 / B_HEAD
    mu = _dot_exact_rhs(y, ones) * inv
    yc = y - mu
    var = _dot_exact_rhs(yc * yc, ones) * inv
    yn = yc * lax.rsqrt(var + GN_EPS) * lnxg_ref[...] + lnxb_ref[...]
    bmix = (yn * stack(g_ref)).astype(BF16)
    o = _dot(stack(a_ref), wa_ref[...]) + _dot(bmix, wb_ref[...]) + _dot(stack(c_ref), wc_ref[...])
    for r in range(nb):
        o_ref[r] = (_ln(alpha * x_ref[r] + mod_ref[r, 2:3] * o[r * TM:(r + 1) * TM], LN_EPS) * ln1g_ref[...]
                    + ln1b_ref[...])


def _outproj(xs, mods, a_out, yf, yb, bonus, g, c_out, wa, wb, wc, lnxg, lnxb, ln1g, ln1b, ctx_len, first_tile,
             alpha):
    bn, s, d = xs.shape
    nb = _batch_rows(bn)
    row = lambda width: pl.BlockSpec((nb, TM, width), lambda b, i: (b, i + first_tile, 0))
    full = lambda a: pl.BlockSpec(a.shape, lambda b, i: (0,) * a.ndim)
    consts = (wa, wb, wc, lnxg, lnxb, ln1g, ln1b)
    return pl.pallas_call(
        functools.partial(_outproj_kernel, alpha=alpha),
        grid=(bn // nb, s // TM - first_tile),
        in_specs=[row(d), _mod_spec(ctx_len, d, nb, first_tile),
                  row(A_WIDTH), row(B_WIDTH), row(B_WIDTH), row(B_WIDTH), row(B_WIDTH), row(C_WIDTH)]
                 + [full(p) for p in consts],
        out_specs=row(d),
        out_shape=jax.ShapeDtypeStruct((bn, s, d), F32),
        compiler_params=_params(("parallel", "parallel")),
        name="outproj",
    )(xs, mods, a_out, yf, yb, bonus, g, c_out, *consts)


def _swiglu_rows(h, w1_ref, w3_ref, w2_ref):
    acc = jnp.zeros((h.shape[0], w2_ref.shape[-1]), F32)
    for j in range(w1_ref.shape[-1] // FF_CHUNK):
        sl = slice(j * FF_CHUNK, (j + 1) * FF_CHUNK)
        u = _dot(h, w1_ref[:, sl])
        t = _dot(h, w3_ref[:, sl])
        acc = acc + _dot((_silu(u) * t).astype(BF16), w2_ref[sl, :])
    return acc


def _ffn_kernel(x_ref, mod_ref, w1_ref, w3_ref, w2_ref, g_ref, b_ref, o_ref, *, alpha):
    nb = x_ref.shape[0]
    h = jnp.concatenate([(_ln(x_ref[r], LN_EPS) * (1.0 + mod_ref[r, 4:5]) + mod_ref[r, 3:4]).astype(BF16)
                         for r in range(nb)], axis=0)
    f = _swiglu_rows(h, w1_ref, w3_ref, w2_ref)
    for r in range(nb):
        o_ref[r] = (_ln(alpha * x_ref[r] + mod_ref[r, 5:6] * f[r * TM:(r + 1) * TM], LN_EPS) * g_ref[...]
                    + b_ref[...])


def _ffn(x1, mods, w1, w3, w2, g, b, ctx_len, alpha):
    bn, s, d = x1.shape
    nb = _batch_rows(bn)
    row = pl.BlockSpec((nb, TM, d), lambda bb, i: (bb, i, 0))
    resident = lambda a: pl.BlockSpec(a.shape, lambda bb, i: (0,) * a.ndim, pipeline_mode=pl.Buffered(1))
    full = lambda a: pl.BlockSpec(a.shape, lambda bb, i: (0,) * a.ndim)
    return pl.pallas_call(
        functools.partial(_ffn_kernel, alpha=alpha),
        grid=(bn // nb, s // TM),
        in_specs=[row, _mod_spec(ctx_len, d, nb),
                  resident(w1), resident(w3), resident(w2), full(g), full(b)],
        out_specs=row,
        out_shape=jax.ShapeDtypeStruct((bn, s, d), F32),
        compiler_params=_params(("parallel", "parallel")),
        name="ffn",
    )(x1, mods, w1, w3, w2, g, b)


def _moe_pre_kernel(x_ref, mod_ref, router_ref, h_ref, logit_ref):
    mod = mod_ref[...]
    h = _ln(x_ref[...], LN_EPS) * (1.0 + mod[4:5]) + mod[3:4]
    h_ref[...] = h.astype(h_ref.dtype)
    logit_ref[...] = _dot6(h, router_ref[...])


def _moe_pre(x1, mods, router_p, ctx_len):
    bn, s, d = x1.shape
    ct = ctx_len // TM
    lt = (s - ctx_len) // TM
    return pl.pallas_call(
        _moe_pre_kernel,
        grid=(bn, lt),
        in_specs=[pl.BlockSpec((None, TM, d), lambda b, i: (b, i + ct, 0)),
                  pl.BlockSpec((None, None, 6, d), lambda b, i: (b, 1, 0, 0)),
                  pl.BlockSpec(router_p.shape, lambda b, i: (0, 0))],
        out_specs=[pl.BlockSpec((TM, d), lambda b, i: (b * lt + i, 0)),
                   pl.BlockSpec((TM, LANES), lambda b, i: (b * lt + i, 0))],
        out_shape=[jax.ShapeDtypeStruct((bn * lt * TM, d), BF16),
                   jax.ShapeDtypeStruct((bn * lt * TM, LANES), F32)],
        compiler_params=_params(("parallel", "parallel")),
        name="moe_pre",
    )(x1, mods, router_p)


def _slot_onehot(slots_ref, block):
    sl = slots_ref[...]
    s_iota = lax.broadcasted_iota(jnp.int32, (MOE_ROWS, MOE_TILE), 0) + block * MOE_ROWS
    hit = jnp.logical_or(sl[0:1] == s_iota, sl[1:2] == s_iota)
    return jnp.where(hit, 1.0, 0.0).astype(BF16)


def _dispatch_kernel(wb_ref, wc_ref, wf_ref, wv_ref, slots_ref, h_ref, o_ref):
    w = pl.program_id(0)
    part = lambda: _dot(_slot_onehot(slots_ref, wb_ref[w]), h_ref[...])

    @pl.when(wf_ref[w] == 1)
    def _():
        o_ref[...] = part().astype(o_ref.dtype)

    @pl.when(jnp.logical_and(wf_ref[w] == 0, wv_ref[w] == 1))
    def _():
        o_ref[...] = (o_ref[...].astype(F32) + part()).astype(o_ref.dtype)


def _dispatch(h, slots, work):
    t, d = h.shape
    wb, wc, wf, wv = work
    n_slots = (t * TOP_K // MOE_ROWS + N_EXPERTS) * MOE_ROWS
    grid_spec = pltpu.PrefetchScalarGridSpec(
        num_scalar_prefetch=4,
        grid=(wb.shape[0],),
        in_specs=[pl.BlockSpec((TOP_K, MOE_TILE), lambda w, b, c, f, v: (0, c[w])),
                  pl.BlockSpec((MOE_TILE, d), lambda w, b, c, f, v: (c[w], 0))],
        out_specs=pl.BlockSpec((MOE_ROWS, d), lambda w, b, c, f, v: (b[w], 0)),
    )
    return pl.pallas_call(
        _dispatch_kernel,
        grid_spec=grid_spec,
        out_shape=jax.ShapeDtypeStruct((n_slots, d), BF16),
        compiler_params=_params(("arbitrary",)),
        name="moe_dispatch",
    )(wb, wc, wf, wv, slots, h)


def _collect_kernel(wb_ref, wc_ref, wf_ref, wv_ref, slots_ref, scol_ref, gate_ref, y_ref, o_ref):
    w = pl.program_id(0)

    def part():
        base = wb_ref[w] * MOE_ROWS
        scol = scol_ref[...] - base
        gates = jnp.where(jnp.logical_and(scol >= 0, scol < MOE_ROWS), gate_ref[...], 0.0)
        gate = jnp.sum(gates, axis=1, keepdims=True)
        return _dot(_slot_onehot(slots_ref, wb_ref[w]), y_ref[...], TN) * gate

    @pl.when(wf_ref[w] == 1)
    def _():
        o_ref[...] = part()

    @pl.when(jnp.logical_and(wf_ref[w] == 0, wv_ref[w] == 1))
    def _():
        o_ref[...] = o_ref[...] + part()


def _collect(y_buf, slots, slots_col, gates, work):
    n_slots, d = y_buf.shape
    t = slots.shape[1]
    wb, wc, wf, wv = work
    tile = lambda w, b, c, f, v: (c[w], 0)
    grid_spec = pltpu.PrefetchScalarGridSpec(
        num_scalar_prefetch=4,
        grid=(wb.shape[0],),
        in_specs=[pl.BlockSpec((TOP_K, MOE_TILE), lambda w, b, c, f, v: (0, c[w])),
                  pl.BlockSpec((MOE_TILE, TOP_K), tile), pl.BlockSpec((MOE_TILE, TOP_K), tile),
                  pl.BlockSpec((MOE_ROWS, d), lambda w, b, c, f, v: (b[w], 0))],
        out_specs=pl.BlockSpec((MOE_TILE, d), tile),
    )
    return pl.pallas_call(
        _collect_kernel,
        grid_spec=grid_spec,
        out_shape=jax.ShapeDtypeStruct((t, d), F32),
        compiler_params=_params(("arbitrary",)),
        name="moe_collect",
    )(wb, wc, wf, wv, slots, slots_col, gates, y_buf)


def _expert_kernel(be_ref, nb_ref, x_ref, w1_ref, w3_ref, w2_ref, o_ref):
    i = pl.program_id(0)

    @pl.when(i < nb_ref[0])
    def _():
        o_ref[...] = _swiglu_rows(x_ref[...], w1_ref, w3_ref, w2_ref).astype(o_ref.dtype)

    @pl.when(i >= nb_ref[0])
    def _():
        o_ref[...] = jnp.zeros_like(o_ref)


def _experts(buf, block_e, n_used, w1, w3, w2):
    n, d = buf.shape
    n_blocks = n // MOE_ROWS
    ff = w1.shape[-1]
    wspec = lambda shape: pl.BlockSpec((None,) + shape, lambda i, be, nb: (be[i], 0, 0),
                                       pipeline_mode=pl.Buffered(1))
    used = lambda i, be, nb: (jnp.minimum(i, nb[0] - 1), 0)
    grid_spec = pltpu.PrefetchScalarGridSpec(
        num_scalar_prefetch=2,
        grid=(n_blocks,),
        in_specs=[pl.BlockSpec((MOE_ROWS, d), used), wspec((d, ff)), wspec((d, ff)), wspec((ff, d))],
        out_specs=pl.BlockSpec((MOE_ROWS, d), lambda i, be, nb: (i, 0)),
    )
    return pl.pallas_call(
        _expert_kernel,
        grid_spec=grid_spec,
        out_shape=jax.ShapeDtypeStruct((n, d), BF16),
        compiler_params=_params(("arbitrary",)),
        name="experts",
    )(block_e, n_used, buf, w1, w3, w2)


def _combine_kernel(x_ref, mod_ref, y_ref, g_ref, b_ref, o_ref, *, alpha):
    mod = mod_ref[...]
    o_ref[...] = _ln(alpha * x_ref[...] + mod[5:6] * y_ref[...], LN_EPS) * g_ref[...] + b_ref[...]


def _combine(x1, mods, y_tok, g, b, ctx_len, alpha):
    bn, s, d = x1.shape
    ct = ctx_len // TM
    lt = (s - ctx_len) // TM
    full = lambda a: pl.BlockSpec(a.shape, lambda bb, i: (0,) * a.ndim)
    return pl.pallas_call(
        functools.partial(_combine_kernel, alpha=alpha),
        grid=(bn, lt),
        in_specs=[pl.BlockSpec((None, TM, d), lambda bb, i: (bb, i + ct, 0)),
                  pl.BlockSpec((None, None, 6, d), lambda bb, i: (bb, 1, 0, 0)),
                  pl.BlockSpec((TM, d), lambda bb, i: (bb * lt + i, 0)),
                  full(g), full(b)],
        out_specs=pl.BlockSpec((None, TM, d), lambda bb, i: (bb, i, 0)),
        out_shape=jax.ShapeDtypeStruct((bn, lt * TM, d), F32),
        compiler_params=_params(("parallel", "parallel")),
        name="moe_combine",
    )(x1, mods, y_tok, g, b)


def _work_lists(rank_at_tile, counts, pstart, block_e, n_used, n_blocks, n_tiles):
    n_work = n_tiles * N_EXPERTS + n_blocks
    blocks = jnp.arange(n_blocks, dtype=jnp.int32)
    used = blocks < n_used
    r0 = blocks * MOE_ROWS - pstart[block_e]
    r_last = jnp.minimum(r0 + MOE_ROWS, counts[block_e]) - 1
    cols = rank_at_tile.T[block_e]
    find = jax.vmap(lambda col, val: jnp.searchsorted(col, val, side="right"))
    lo = jnp.clip(find(cols, r0) - 1, 0, n_tiles - 1)
    hi = jnp.clip(find(cols, r_last) - 1, 0, n_tiles - 1)
    n_b = jnp.where(used, hi - lo + 1, 0)
    ends = jnp.cumsum(n_b)
    starts = ends - n_b
    total = ends[-1]
    w = jnp.arange(n_work, dtype=jnp.int32)
    valid = w < total
    wl = jnp.minimum(w, total - 1)
    blk = jnp.minimum(jnp.searchsorted(ends, wl, side="right"), n_blocks - 1).astype(jnp.int32)
    tile = (lo[blk] + (wl - starts[blk])).astype(jnp.int32)
    first = jnp.logical_and(valid, w == starts[blk])
    as_i32 = lambda a: a.astype(jnp.int32)
    by_block = (blk, tile, as_i32(first), as_i32(valid))
    order = jnp.argsort(jnp.where(valid, tile, n_tiles), stable=True)
    order = order[jnp.minimum(w, total - 1)]
    tile2, blk2 = tile[order], blk[order]
    first2 = jnp.logical_and(valid, jnp.concatenate([jnp.ones((1,), bool), tile2[1:] != tile2[:-1]]))
    by_tile = (blk2, tile2, as_i32(first2), as_i32(valid))
    return by_block, by_tile


def _moe_layer(x1, mods, router, w1, w3, w2, g, b, ctx_len, alpha):
    d = x1.shape[-1]
    router_p = jnp.pad(router, ((0, 0), (0, LANES - N_EXPERTS)))
    h, logits = _moe_pre(x1, mods, router_p, ctx_len)
    t = h.shape[0]
    top_v, top_i = lax.top_k(logits[:, :N_EXPERTS], TOP_K)
    gates = jax.nn.softmax(top_v, axis=-1)
    e_flat = top_i.reshape(-1)
    onehot = (e_flat[:, None] == jnp.arange(N_EXPERTS)[None, :]).astype(jnp.int32)
    ranks = jnp.cumsum(onehot, axis=0) - onehot
    rank = jnp.sum(ranks * onehot, axis=1)
    counts = jnp.sum(onehot, axis=0)
    padded = (counts + MOE_ROWS - 1) // MOE_ROWS * MOE_ROWS
    pend = jnp.cumsum(padded)
    pstart = pend - padded
    slot = (pstart[e_flat] + rank).astype(jnp.int32)
    n_blocks = t * TOP_K // MOE_ROWS + N_EXPERTS
    assert t % MOE_TILE == 0
    n_tiles = t // MOE_TILE
    block_e = jnp.minimum(jnp.searchsorted(pend, jnp.arange(n_blocks) * MOE_ROWS, side="right"),
                          N_EXPERTS - 1).astype(jnp.int32)
    n_used = (pend[-1:] // MOE_ROWS).astype(jnp.int32)
    rank_at_tile = jnp.concatenate([ranks[::MOE_TILE * TOP_K], counts[None, :]], axis=0)
    by_block, by_tile = _work_lists(rank_at_tile, counts, pstart, block_e, n_used, n_blocks, n_tiles)
    slots_col = slot.reshape(t, TOP_K)
    slots = slots_col.T
    buf = _dispatch(h, slots, by_block)
    y_buf = _experts(buf, block_e, n_used, w1, w3, w2)
    y_tok = _collect(y_buf, slots, slots_col, gates, by_tile)
    return _combine(x1, mods, y_tok, g, b, ctx_len, alpha)


def _rope_tables(n_rows, ctx_len, dim, lane_lo):
    quarter = dim // 4
    inv = ROPE_THETA ** (-jnp.arange(quarter, dtype=F32) / quarter)
    rows = jnp.repeat(jnp.arange(n_rows, dtype=F32), GRID_W)
    cols = jnp.tile(jnp.arange(GRID_W, dtype=F32), n_rows)
    ang = jnp.concatenate([rows[:, None] * inv, rows[:, None] * inv,
                           cols[:, None] * inv, cols[:, None] * inv], axis=-1)
    sign = jnp.tile(jnp.concatenate([-jnp.ones(quarter, F32), jnp.ones(quarter, F32)]), 2)
    cos, sin = jnp.cos(ang), jnp.sin(ang) * sign
    length = cos.shape[0]
    if lane_lo == 0:
        reps = LANES // dim
        cos, sin = jnp.tile(cos, (1, reps)), jnp.tile(sin, (1, reps))
    else:
        pad = ((0, 0), (lane_lo, LANES - lane_lo - dim))
        cos = jnp.pad(cos, pad, constant_values=1.0)
        sin = jnp.pad(sin, pad)
    cos = jnp.concatenate([jnp.ones((ctx_len, LANES), F32), cos], axis=0)
    sin = jnp.concatenate([jnp.zeros((ctx_len, LANES), F32), sin], axis=0)
    return cos, sin


def _block_diag2(w):
    z = jnp.zeros_like(w[0])
    return jnp.concatenate([jnp.concatenate([w[0], z], axis=1), jnp.concatenate([z, w[1]], axis=1)], axis=0)


def _lambda_init(layer):
    return 0.8 - 0.6 * math.exp(-0.3 * layer)


def kernel(x, c, ctx, c_ctx, ada_w, ada_b, w_in, w_out, ln1_g, ln1_b, ln2_g, ln2_b,
           lam_q1, lam_k1, lam_q2, lam_k2, diff_norm_g, shift_mu, w0, w2, a0, a2, g2,
           k_k, k_a, r_k, lnx_g, lnx_b, q_norm_g, w_uq, kv_norm_g, w_ukv,
           ff_w1, ff_w3, ff_w2, router, moe_w1, moe_w3, moe_w2):
    bn, seq, d = x.shape
    ctx_len = ctx.shape[1]
    depth = ada_w.shape[0]
    assert d == D_MODEL and seq % TM == 0 and ctx_len % TM == 0 and seq % GRID_W == 0
    alpha = (2.0 * depth) ** 0.25
    n_grid_rows = seq // GRID_W
    cos_a, sin_a = _rope_tables(n_grid_rows, ctx_len, A_QK_DIM, 0)
    cos_c, sin_c = _rope_tables(n_grid_rows, ctx_len, C_ROPE, C_NOPE)

    cond_rows = 8 * ((bn + 1 + 7) // 8)
    cond = jnp.zeros((cond_rows, d), F32).at[:bn].set(c).at[bn].set(c_ctx)
    xs = jnp.concatenate([ctx, x], axis=1)

    for i in range(depth):
        with_ctx = i < depth - 1
        m = _ada(cond, ada_w[i], ada_b[i]).reshape(cond_rows, 6, d)
        mods = jnp.stack([jnp.broadcast_to(m[bn], (bn, 6, d)), m[:bn]], axis=1)

        wc = w_in[i][:, N_A + N_B:]
        kpe_w = jnp.pad(wc[:, C_Q_RANK + C_KV_RANK:], ((0, 0), (C_NOPE, LANES - C_NOPE - C_ROPE)))
        w_p = jnp.concatenate([w_in[i][:, :N_A + N_B], wc[:, :C_Q_RANK + C_KV_RANK], kpe_w], axis=1).astype(BF16)
        wq = w_uq[i].reshape(C_Q_RANK, C_HEADS, C_NOPE + C_ROPE)
        wq_p = jnp.pad(wq, ((0, 0), (0, 0), (0, LANES - C_NOPE - C_ROPE))).reshape(C_Q_RANK, -1).astype(BF16)
        wkv = w_ukv[i].reshape(C_KV_RANK, C_HEADS, C_NOPE + C_V)
        wk_p = jnp.pad(wkv[:, :, :C_NOPE], ((0, 0), (0, 0), (0, LANES - C_NOPE))).reshape(C_KV_RANK, -1).astype(BF16)
        wv_p = wkv[:, :, C_NOPE:].reshape(C_KV_RANK, -1).astype(BF16)
        mla_consts = (q_norm_g[i].reshape(1, -1), kv_norm_g[i].reshape(1, -1), wq_p, wk_p, wv_p)
        qa, ka, va, pb, qc, kc, vc = _inproj(xs, mods, w_p, cos_a, sin_a, mla_consts, cos_c, sin_c, ctx_len)

        lam_p = jnp.stack([lam_q1[i], lam_k1[i], lam_q2[i], lam_k2[i]])
        g_col = jnp.broadcast_to(diff_norm_g[i][:, None], (A_V_DIM, TM))
        first_tile = 0 if with_ctx else ctx_len // TM
        a_out = _attention(qa, ka, va, lam_p, g_col, mode="diff", ctx_len=ctx_len, first_tile=first_tile,
                           lam_init=_lambda_init(i))
        c_out = _attention(qc, kc, vc, lam_p, g_col, mode="mla", ctx_len=ctx_len, first_tile=first_tile)

        r, v, kkn, ld, kd, beta, gate, bonus = _rwkv_prep(
            pb, shift_mu[i].reshape(1, -1), w0[i].reshape(1, -1), _block_diag2(w2[i]).astype(BF16),
            a0[i].reshape(1, -1), _block_diag2(a2[i]).astype(BF16), g2[i].astype(BF16),
            k_k[i].reshape(1, -1), k_a[i].reshape(1, -1), r_k[i].reshape(1, -1), ctx_len)
        yf, yb = _rwkv_pairs(r, v, kkn, ld, kd, beta, ctx_len)

        wo = w_out[i].astype(BF16)
        x1 = _outproj(xs, mods, a_out, yf, yb, bonus, gate, c_out,
                      wo[:A_WIDTH], wo[A_WIDTH:A_WIDTH + B_WIDTH], wo[A_WIDTH + B_WIDTH:],
                      lnx_g[i].reshape(1, -1), lnx_b[i].reshape(1, -1),
                      ln1_g[i].reshape(1, -1), ln1_b[i].reshape(1, -1), ctx_len, first_tile, alpha)

        j = i // 2
        g2n, b2n = ln2_g[i].reshape(1, -1), ln2_b[i].reshape(1, -1)
        if i % 2 == 0:
            xs = _ffn(x1, mods, ff_w1[j].astype(BF16), ff_w3[j].astype(BF16), ff_w2[j].astype(BF16),
                      g2n, b2n, ctx_len, alpha)
        else:
            if with_ctx:
                raise NotImplementedError("routed FFN on the context rows is not needed at this depth")
            return _moe_layer(x1, mods, router[j], moe_w1[j].astype(BF16), moe_w3[j].astype(BF16),
                              moe_w2[j].astype(BF16), g2n, b2n, ctx_len, alpha)
    return xs[:, ctx_len:]
```

```python
import functools
import math

import jax
import jax.numpy as jnp
from jax import lax
from jax.experimental import pallas as pl
from jax.experimental.pallas import tpu as pltpu

F32 = jnp.float32
BF16 = jnp.bfloat16

D_MODEL = 1024
GRID_W = 64
ROPE_THETA = 10000.0
A_HEADS, A_QK_DIM, A_V_DIM = 4, 64, 128
A_WIDTH = A_HEADS * A_V_DIM
A_QK_COLS = 2 * A_HEADS * A_QK_DIM
B_HEADS, B_HEAD = 4, 64
B_WIDTH = B_HEADS * B_HEAD
B_DECAY_RANK, B_A_RANK, B_GATE_RANK = 64, 64, 128
C_HEADS, C_NOPE, C_ROPE, C_V = 4, 64, 32, 64
C_WIDTH = C_HEADS * C_V
C_Q_RANK, C_KV_RANK = 256, 128
N_A = 2 * A_QK_COLS + A_WIDTH
N_B = 3 * B_WIDTH + 2 * B_DECAY_RANK + 2 * B_A_RANK + B_GATE_RANK
N_C = C_Q_RANK + C_KV_RANK + C_ROPE
N_C_PAD = C_Q_RANK + C_KV_RANK + 128
D_FF = 3584
N_EXPERTS = 8
TOP_K = 2
LN_EPS = 1e-6
RMS_EPS = 1e-6
GN_EPS = 64e-5

LOG2E = math.log2(math.e)
LANES = 128
TM = 256
CHUNK = 64
ROWS_B = 2
RWKV_ROWS = 4
ATT_TK = 1024
ONES_ROWS = 16
ATT_PAR = 4
MOE_ROWS = 512
COLLECT_ROWS = 128
MOE_TILE = 512
FF_CHUNK = 512
VMEM_LIMIT = 56 * 1024 * 1024

NN = (((1,), (0,)), ((), ()))
NT = (((1,), (1,)), ((), ()))
TN = (((0,), (0,)), ((), ()))


def _params(sem, vmem=VMEM_LIMIT, flags=None):
    return pltpu.CompilerParams(dimension_semantics=sem, vmem_limit_bytes=vmem, flags=flags)


def _split(x, n):
    parts, r = [], x
    for _ in range(n):
        p = r.astype(BF16)
        parts.append(p)
        r = r - p.astype(F32)
    return parts


def _dot(a, b, dn=NN):
    return lax.dot_general(a, b, dn, preferred_element_type=F32)


def _dot1(a, b, dn=NN):
    return _dot(a.astype(BF16), b.astype(BF16), dn)


def _dot3(a, b, dn=NN):
    a0, a1 = _split(a, 2)
    b0, b1 = _split(b, 2)
    return _dot(a0, b0, dn) + (_dot(a0, b1, dn) + _dot(a1, b0, dn))


def _dot6(a, b, dn=NN):
    a0, a1, a2 = _split(a, 3)
    b0, b1, b2 = _split(b, 3)
    lo = _dot(a1, b1, dn) + (_dot(a0, b2, dn) + _dot(a2, b0, dn))
    return _dot(a0, b0, dn) + ((_dot(a0, b1, dn) + _dot(a1, b0, dn)) + lo)


def _dot_exact_lhs(a_bf16, b, dn=NN, n=3):
    parts = _split(b, n)
    out = _dot(a_bf16, parts[-1], dn)
    for p in parts[-2::-1]:
        out = out + _dot(a_bf16, p, dn)
    return out


def _dot_exact_rhs(a, b_bf16, dn=NN, n=3):
    parts = _split(a, n)
    out = _dot(parts[-1], b_bf16, dn)
    for p in parts[-2::-1]:
        out = out + _dot(p, b_bf16, dn)
    return out


def _ln(x, eps):
    mu = jnp.mean(x, axis=-1, keepdims=True)
    xc = x - mu
    return xc * lax.rsqrt(jnp.mean(xc * xc, axis=-1, keepdims=True) + eps)


def _sigmoid(x):
    return 1.0 / (1.0 + jnp.exp(-x))


def _silu(x):
    return x * _sigmoid(x)


def _group_ones(width, group):
    r = lax.broadcasted_iota(jnp.int32, (width, width), 0) // group
    c = lax.broadcasted_iota(jnp.int32, (width, width), 1) // group
    return (r == c).astype(BF16)


def _partner(x, half):
    lane = lax.broadcasted_iota(jnp.int32, x.shape, 1)
    up = pltpu.roll(x, LANES - half, 1)
    dn = pltpu.roll(x, half, 1)
    return jnp.where((lane % (2 * half)) < half, up, dn)


def _rope(x, cos, sin, half):
    return x * cos + _partner(x, half) * sin


def _ada_kernel(c_ref, w_ref, b_ref, o_ref):
    o_ref[...] = _dot3(_silu(c_ref[...]), w_ref[...]) + b_ref[...]


def _ada(cond, w, b):
    rows, d = cond.shape
    n = w.shape[1]
    tn = 1536
    return pl.pallas_call(
        _ada_kernel,
        grid=(n // tn,),
        in_specs=[pl.BlockSpec((rows, d), lambda j: (0, 0)),
                  pl.BlockSpec((d, tn), lambda j: (0, j)),
                  pl.BlockSpec((1, tn), lambda j: (0, j))],
        out_specs=pl.BlockSpec((rows, tn), lambda j: (0, j)),
        out_shape=jax.ShapeDtypeStruct((rows, n), F32),
        compiler_params=_params(("parallel",)),
        name="ada",
    )(cond, w, b.reshape(1, n))


def _inproj_kernel(x_ref, mod_ref, w_ref, cos_ref, sin_ref,
                   qg_ref, kvg_ref, wq_ref, wk_ref, wv_ref, cosc_ref, sinc_ref,
                   q_ref, k_ref, v_ref, pb_ref, cq_ref, ck_ref, cv_ref):
    nb = x_ref.shape[0]
    h = jnp.concatenate([(_ln(x_ref[r], LN_EPS) * (1.0 + mod_ref[r, 1:2]) + mod_ref[r, 0:1]).astype(BF16)
                         for r in range(nb)], axis=0)
    cos, sin = cos_ref[...], sin_ref[...]
    scale = A_QK_DIM ** -0.5 * LOG2E
    rows = lambda a, r: a[r * TM:(r + 1) * TM]
    for j in range(A_QK_COLS // LANES):
        sl = slice(j * LANES, (j + 1) * LANES)
        qj = _dot(h, w_ref[:, sl])
        kj = _dot(h, w_ref[:, A_QK_COLS + j * LANES:A_QK_COLS + (j + 1) * LANES])
        vj = _dot(h, w_ref[:, 2 * A_QK_COLS + j * LANES:2 * A_QK_COLS + (j + 1) * LANES])
        for r in range(nb):
            q_ref[r, sl, :] = (_rope(rows(qj, r), cos, sin, A_QK_DIM // 4) * scale).T.astype(BF16)
            k_ref[r, :, sl] = _rope(rows(kj, r), cos, sin, A_QK_DIM // 4).astype(BF16)
            v_ref[r, sl, :] = rows(vj, r).T.astype(BF16)
    pb = _dot(h, w_ref[:, N_A:N_A + N_B])
    for r in range(nb):
        pb_ref[r] = rows(pb, r)
    _mla_heads(_dot(h, w_ref[:, N_A + N_B:]), nb, qg_ref, kvg_ref, wq_ref, wk_ref, wv_ref,
               cosc_ref[...], sinc_ref[...], cq_ref, ck_ref, cv_ref)


def _batch_rows(bn):
    return ROWS_B if bn % ROWS_B == 0 else 1


def _mod_spec(ctx_len, d, nb=None, first_tile=0):
    ct = ctx_len // TM
    return pl.BlockSpec((nb, None, 6, d), lambda b, i: (b, jnp.where(i + first_tile >= ct, 1, 0), 0, 0))


def _inproj(xs, mods, w_p, cos_a, sin_a, mla_consts, cos_c, sin_c, ctx_len):
    bn, s, d = xs.shape
    n_tiles = s // TM
    n_w = w_p.shape[1]
    nb = _batch_rows(bn)
    row = lambda width: pl.BlockSpec((nb, TM, width), lambda b, i: (b, i, 0))
    col = lambda width: pl.BlockSpec((nb, width, TM), lambda b, i: (b, 0, i))
    full = lambda a: pl.BlockSpec(a.shape, lambda b, i: (0,) * a.ndim)
    tab = pl.BlockSpec((TM, LANES), lambda b, i: (i, 0))
    hw = C_HEADS * LANES
    outs = [jax.ShapeDtypeStruct((bn, A_QK_COLS, s), BF16),
            jax.ShapeDtypeStruct((bn, s, A_QK_COLS), BF16),
            jax.ShapeDtypeStruct((bn, A_WIDTH, s), BF16),
            jax.ShapeDtypeStruct((bn, s, N_B), F32),
            jax.ShapeDtypeStruct((bn, hw, s), BF16),
            jax.ShapeDtypeStruct((bn, s, hw), BF16),
            jax.ShapeDtypeStruct((bn, C_WIDTH, s), BF16)]
    return pl.pallas_call(
        _inproj_kernel,
        grid=(bn // nb, n_tiles),
        in_specs=[row(d), _mod_spec(ctx_len, d, nb), full(w_p), tab, tab]
                 + [full(a) for a in mla_consts] + [tab, tab],
        out_specs=[col(A_QK_COLS), row(A_QK_COLS), col(A_WIDTH), row(N_B), col(hw), row(hw), col(C_WIDTH)],
        out_shape=outs,
        compiler_params=_params(("parallel", "parallel")),
        name="inproj",
    )(xs, mods, w_p, cos_a, sin_a, *mla_consts, cos_c, sin_c)


def _mla_heads(pc, nb, qg_ref, kvg_ref, wq_ref, wk_ref, wv_ref, cos, sin, q_ref, k_ref, v_ref):
    cq = pc[:, :C_Q_RANK]
    cq = cq * lax.rsqrt(jnp.mean(cq * cq, axis=-1, keepdims=True) + RMS_EPS) * qg_ref[...]
    ckv = pc[:, C_Q_RANK:C_Q_RANK + C_KV_RANK]
    ckv = ckv * lax.rsqrt(jnp.mean(ckv * ckv, axis=-1, keepdims=True) + RMS_EPS) * kvg_ref[...]
    cqb, ckvb = cq.astype(BF16), ckv.astype(BF16)
    scale = (C_NOPE + C_ROPE) ** -0.5 * LOG2E
    rows = lambda a, r: a[r * TM:(r + 1) * TM]
    kpe = [_rope(rows(pc, r)[:, C_Q_RANK + C_KV_RANK:], cos, sin, C_ROPE // 4) for r in range(nb)]
    for h in range(C_HEADS):
        sl = slice(h * LANES, (h + 1) * LANES)
        qh = _dot(cqb, wq_ref[:, sl])
        kh = _dot(ckvb, wk_ref[:, sl])
        for r in range(nb):
            q_ref[r, sl, :] = (_rope(rows(qh, r), cos, sin, C_ROPE // 4) * scale).T.astype(BF16)
            k_ref[r, :, sl] = (rows(kh, r) + kpe[r]).astype(BF16)
    for j in range(C_WIDTH // LANES):
        sl = slice(j * LANES, (j + 1) * LANES)
        vj = _dot(ckvb, wv_ref[:, sl])
        for r in range(nb):
            v_ref[r, sl, :] = rows(vj, r).T.astype(BF16)


def _attn_kernel(lam_ref, g_ref, qt_ref, k_ref, vt_ref, o_ref, *, mode, ctx_tiles, ctx_len, tk, first_tile,
                 lam_init):
    qi = pl.program_id(2) + first_tile
    tq = qt_ref.shape[1]
    n_par = qt_ref.shape[0] // LANES
    dv = vt_ref.shape[0] // n_par
    width = 2 * tq if mode == "diff" else tq
    half = lax.broadcasted_iota(jnp.int32, (LANES, 1), 0) < (LANES // 2)
    q_ops = []
    for g in range(n_par):
        qt = qt_ref[g * LANES:(g + 1) * LANES, :]
        if mode == "diff":
            qt = jnp.concatenate([jnp.where(half, qt, jnp.zeros_like(qt)),
                                  jnp.where(half, jnp.zeros_like(qt), qt)], axis=1)
        q_ops.append(qt)

    def fold(x, reduce):
        rows = x.shape[0]
        while rows > 8:
            g = max(d for d in range(2, 9) if (rows // 8) % d == 0)
            rows //= g
            x = reduce(x.reshape(g, rows, x.shape[1]), axis=0)
        return x

    def scores(kc, g):
        return _dot(kc[:, g * LANES:(g + 1) * LANES], q_ops[g])

    def absorb(kc, vc, carry):
        ss = [scores(kc, g) for g in range(n_par)]
        m_new = [jnp.maximum(carry[g][0], jnp.max(fold(ss[g], jnp.max), axis=0, keepdims=True))
                 for g in range(n_par)]
        alpha = [jnp.exp2(carry[g][0] - m_new[g]) for g in range(n_par)]
        ps = [jnp.exp2(ss[g] - m_new[g]).astype(BF16) for g in range(n_par)]
        ones = jnp.ones((ONES_ROWS, vc.shape[1]), BF16)
        acc = [alpha[g] * carry[g][1] + _dot(jnp.concatenate([vc[g * dv:(g + 1) * dv], ones], axis=0), ps[g])
               for g in range(n_par)]
        return tuple((m_new[g], acc[g]) for g in range(n_par))

    n_latent_chunks = (k_ref.shape[0] - ctx_len) // tk

    def finish(stats):
        outs = [a[:dv] / a[dv:dv + 1] for (_, a) in stats]
        if mode == "diff":
            lp = lam_ref[...]
            lam = (jnp.exp(jnp.sum(lp[0:1] * lp[1:2], axis=-1, keepdims=True))
                   - jnp.exp(jnp.sum(lp[2:3] * lp[3:4], axis=-1, keepdims=True)) + lam_init)
            for g in range(n_par):
                o = outs[g][:, :tq] - lam * outs[g][:, tq:]
                o = o * lax.rsqrt(jnp.mean(o * o, axis=0, keepdims=True) + RMS_EPS) * g_ref[...]
                o = o * (1.0 - lam_init)
                o_ref[:, g * LANES:(g + 1) * LANES] = o.T.astype(o_ref.dtype)
        else:
            per_slab = LANES // dv
            for j in range(n_par // per_slab):
                o = jnp.concatenate(outs[j * per_slab:(j + 1) * per_slab], axis=0)
                o_ref[:, j * LANES:(j + 1) * LANES] = o.T.astype(o_ref.dtype)

    def run(first_keys, n_more):
        neg = jnp.full((1, width), -1e30, F32)
        zacc = jnp.zeros((dv + ONES_ROWS, width), F32)
        stats = absorb(k_ref[0:first_keys, :], vt_ref[:, 0:first_keys], ((neg, zacc),) * n_par)

        def body(j, stats):
            off = pl.multiple_of(first_keys + j * tk, LANES)
            return absorb(k_ref[pl.ds(off, tk), :], vt_ref[:, pl.ds(off, tk)], stats)

        finish(lax.fori_loop(0, n_more, body, stats))

    @pl.when(qi < ctx_tiles)
    def _():
        run(ctx_len, 0)

    @pl.when(qi >= ctx_tiles)
    def _():
        run(ctx_len + tk, n_latent_chunks - 1)


def _attention(qt, k, vt, lam_p, g, *, mode, ctx_len, first_tile, lam_init=0.0):
    bn, s, _ = k.shape
    heads = k.shape[2] // LANES
    dv = vt.shape[1] // heads
    par = math.gcd(heads, ATT_PAR)
    tk = math.gcd(s - ctx_len, ATT_TK)
    assert tk % LANES == 0 and ctx_len % LANES == 0 and (par * dv) % LANES == 0
    kern = functools.partial(_attn_kernel, mode=mode, ctx_tiles=ctx_len // TM, ctx_len=ctx_len, tk=tk,
                             first_tile=first_tile, lam_init=lam_init)
    return pl.pallas_call(
        kern,
        grid=(bn, heads // par, s // TM - first_tile),
        in_specs=[pl.BlockSpec(lam_p.shape, lambda b, h, i: (0, 0)),
                  pl.BlockSpec(g.shape, lambda b, h, i: (0, 0)),
                  pl.BlockSpec((None, par * LANES, TM), lambda b, h, i: (b, h, i + first_tile)),
                  pl.BlockSpec((None, s, par * LANES), lambda b, h, i: (b, 0, h)),
                  pl.BlockSpec((None, par * dv, s), lambda b, h, i: (b, h, 0))],
        out_specs=pl.BlockSpec((None, TM, par * dv), lambda b, h, i: (b, i + first_tile, h)),
        out_shape=jax.ShapeDtypeStruct((bn, s, heads * dv), BF16),
        compiler_params=_params(("parallel", "parallel", "parallel")),
        name="attn_" + mode,
    )(lam_p, g, qt, k, vt)


def _rwkv_prep_kernel(pb_ref, prev_ref, next_ref, mu_ref, w0_ref, w2_ref, a0_ref, a2_ref, g2_ref,
                      kk_ref, ka_ref, rk_ref,
                      r_out, v_out, kkn_out, ld_out, kd_out, beta_out, g_out, bonus_out,
                      *, ctx_tiles, n_tiles):
    i = pl.program_id(1)
    x = pb_ref[...]
    row = lax.broadcasted_iota(jnp.int32, (TM, 1), 0)
    has_prev = jnp.logical_and(i != 0, i != ctx_tiles)
    has_next = jnp.logical_and(i != ctx_tiles - 1, i != n_tiles - 1)
    prev_edge = jnp.where(has_prev, prev_ref[7:8, :], 0.0)
    next_edge = jnp.where(has_next, next_ref[0:1, :], 0.0)
    xp = jnp.where(row == 0, prev_edge, pltpu.roll(x, 1, 0))
    xn = jnp.where(row == TM - 1, next_edge, pltpu.roll(x, TM - 1, 0))
    z = x + mu_ref[...] * (0.5 * (xp + xn) - x)

    r = z[:, :B_WIDTH]
    k = z[:, B_WIDTH:2 * B_WIDTH]
    v = z[:, 2 * B_WIDTH:3 * B_WIDTH]
    o = 3 * B_WIDTH
    wd = z[:, o:o + 2 * B_DECAY_RANK]
    o += 2 * B_DECAY_RANK
    ad = z[:, o:o + 2 * B_A_RANK]
    o += 2 * B_A_RANK
    gd = z[:, o:]

    u = w0_ref[...] + _dot1(jnp.tanh(wd), w2_ref[...])
    ld = -math.exp(-0.5) * _sigmoid(u)
    lr = _sigmoid(a0_ref[...] + _dot1(ad, a2_ref[...]))
    g_out[...] = _dot1(_sigmoid(gd), g2_ref[...])

    ones = _group_ones(B_WIDTH, B_HEAD)
    kk = k * kk_ref[...]
    norm = jnp.sqrt(_dot_exact_rhs(kk * kk, ones))
    kkn = kk / jnp.maximum(norm, 1e-12)
    ka = ka_ref[...]
    kd_sum = jnp.zeros_like(k)
    for d in range(2):
        lr_d = lr[:, d * B_WIDTH:(d + 1) * B_WIDTH]
        kd = k * (1.0 + (lr_d - 1.0) * ka)
        kd_sum = kd_sum + kd
        ld_out[d] = ld[:, d * B_WIDTH:(d + 1) * B_WIDTH]
        kd_out[d] = kd
        beta_out[d] = kkn * lr_d
    bonus_out[...] = _dot_exact_rhs(r * kd_sum * rk_ref[...], ones) * v
    r_out[...] = r
    v_out[...] = v
    kkn_out[...] = kkn


def _rwkv_prep(pb, mu, w0, w2bd, a0, a2bd, g2, k_k, k_a, r_k, ctx_len):
    bn, s, _ = pb.shape
    n_tiles = s // TM
    eight = TM // 8
    row = lambda width: pl.BlockSpec((None, TM, width), lambda b, i: (b, i, 0))
    drow = pl.BlockSpec((2, None, TM, B_WIDTH), lambda b, i: (0, b, i, 0))
    full = lambda a: pl.BlockSpec(a.shape, lambda b, i: (0,) * a.ndim)
    one = jax.ShapeDtypeStruct((bn, s, B_WIDTH), F32)
    two = jax.ShapeDtypeStruct((2, bn, s, B_WIDTH), F32)
    kern = functools.partial(_rwkv_prep_kernel, ctx_tiles=ctx_len // TM, n_tiles=n_tiles)
    params = (mu, w0, w2bd, a0, a2bd, g2, k_k, k_a, r_k)
    return pl.pallas_call(
        kern,
        grid=(bn, n_tiles),
        in_specs=[row(N_B),
                  pl.BlockSpec((None, 8, N_B), lambda b, i: (b, jnp.maximum(i * eight - 1, 0), 0)),
                  pl.BlockSpec((None, 8, N_B), lambda b, i: (b, jnp.minimum((i + 1) * eight, s // 8 - 1), 0)),
                  ] + [full(p) for p in params],
        out_specs=[row(B_WIDTH), row(B_WIDTH), row(B_WIDTH), drow, drow, drow, row(B_WIDTH), row(B_WIDTH)],
        out_shape=[one, one, one, two, two, two, one, one],
        compiler_params=_params(("parallel", "parallel")),
        name="rwkv_prep",
    )(pb, pb, pb, *params)


def _pair_diag(x):
    lo = lax.broadcasted_iota(jnp.int32, (1, LANES), 1) < B_HEAD
    z = jnp.zeros_like(x)
    return jnp.concatenate([jnp.where(lo, x, z), jnp.where(lo, z, x)], axis=0)


def _pair_pick(x):
    lo = lax.broadcasted_iota(jnp.int32, (1, LANES), 1) < B_HEAD
    return jnp.where(lo, x[:B_HEAD], x[B_HEAD:])


def _rwkv_pair_kernel(rf_ref, vf_ref, kkf_ref, rb_ref, vb_ref, kkb_ref,
                      ldf_ref, kdf_ref, betaf_ref, ldb_ref, kdb_ref, betab_ref,
                      yf_ref, yb_ref, h_ref):
    c = pl.program_id(1)

    @pl.when(c == 0)
    def _():
        h_ref[...] = jnp.zeros_like(h_ref)

    n_pairs = B_WIDTH // LANES
    ti = lax.broadcasted_iota(jnp.int32, (CHUNK, LANES), 0)
    si = lax.broadcasted_iota(jnp.int32, (CHUNK, LANES), 1) % CHUNK
    t64 = lax.broadcasted_iota(jnp.int32, (CHUNK, CHUNK), 0)
    s64 = lax.broadcasted_iota(jnp.int32, (CHUNK, CHUNK), 1)
    eye = ti == si
    dirs = ((rf_ref, vf_ref, kkf_ref, ldf_ref, kdf_ref, betaf_ref, False),
            (rb_ref, vb_ref, kkb_ref, ldb_ref, kdb_ref, betab_ref, True))

    units = []
    n_rows = rf_ref.shape[0]
    for bi in range(n_rows):
      for d, (r_ref, v_ref, kk_ref, ld_ref, kd_ref, beta_ref, rev) in enumerate(dirs):
        strict = (si > ti) if rev else (ti > si)
        incl = (si >= ti) if rev else (ti >= si)
        tri = ((s64 >= t64) if rev else (t64 >= s64)).astype(BF16)
        ld_all = ld_ref[bi]
        cl_all = _dot_exact_lhs(tri, ld_all)
        for p in range(n_pairs):
            sl = slice(p * LANES, (p + 1) * LANES)
            ld, cl = ld_all[:, sl], cl_all[:, sl]
            total = jnp.sum(ld, axis=0, keepdims=True)
            inv_gam = jnp.exp(-cl)
            to_end = jnp.exp(total - cl)
            kk, kd, beta = kk_ref[bi, :, sl], kd_ref[bi, :, sl], beta_ref[bi, :, sl]
            units.append(dict(
                bi=bi, d=d, p=p, sl=sl, strict=strict, incl=incl, v=v_ref[bi, :, sl].astype(BF16),
                a_bar=-kk * jnp.exp(cl - ld), r_bar=r_ref[bi, :, sl] * jnp.exp(cl),
                b_til=(beta * inv_gam).astype(BF16), k_til=(kd * inv_gam).astype(BF16),
                b_hat=(beta * to_end).astype(BF16), k_hat=(kd * to_end).astype(BF16),
                gam_c=jnp.exp(total)))

    for u in units:
        x_mat = jnp.concatenate([u["a_bar"], u["r_bar"]], axis=0).astype(BF16)
        rhs = jnp.concatenate([_pair_diag(u["b_til"]), _pair_diag(u["k_til"])], axis=0)
        xbk = _dot(x_mat, rhs, NT)
        u["n_ab"] = jnp.where(u["strict"], xbk[:CHUNK, :LANES], 0.0)
        u["l_rb"] = jnp.where(u["incl"], xbk[CHUNK:, :LANES], 0.0)
        n_ak = jnp.where(u["strict"], xbk[:CHUNK, LANES:], 0.0)
        l_rk = jnp.where(u["incl"], xbk[CHUNK:, LANES:], 0.0)
        u["nl"] = jnp.concatenate([n_ak, l_rk], axis=0).astype(BF16)
    for u in units:
        nv = _dot(u["nl"], _pair_diag(u["v"]))
        u["w"], u["u0"], u["lrkv"] = u["a_bar"], nv[:CHUNK], nv[CHUNK:]
        u["npow"] = u["n_ab"].astype(BF16)

    steps = int(math.log2(CHUNK))
    for kstep in range(steps):
        for u in units:
            rhs = jnp.concatenate([_pair_diag(u["w"].astype(BF16)), _pair_diag(u["u0"].astype(BF16))], axis=1)
            upd = _dot(u["npow"], rhs)
            u["w"] = u["w"] + upd[:, :LANES]
            u["u0"] = u["u0"] + upd[:, LANES:]
        if kstep + 1 < steps:
            for u in units:
                u["npow"] = _dot(u["npow"], _pair_diag(u["npow"])).astype(BF16)

    for u in units:
        wb, ub = u["w"].astype(BF16), u["u0"].astype(BF16)
        lx = _dot(u["l_rb"].astype(BF16), jnp.concatenate([_pair_diag(wb), _pair_diag(ub)], axis=1))
        u["p_mat"] = u["r_bar"] + lx[:, :LANES]
        u["y0"] = u["lrkv"] + lx[:, LANES:]
        lhs = jnp.concatenate([u["b_hat"], u["k_hat"]], axis=0)
        rhs = jnp.concatenate([jnp.concatenate([wb, ub], axis=1),
                               jnp.concatenate([jnp.zeros_like(wb), u["v"]], axis=1)], axis=0)
        mg = _dot(lhs, rhs, TN)
        u["m_full"] = _pair_pick(mg[:, :LANES]) + jnp.where(eye, u["gam_c"], 0.0)
        u["g_mat"] = _pair_pick(mg[:, LANES:])

    for u in units:
        bi, d, p = u["bi"], u["d"], u["p"]
        h0 = h_ref[bi, d, p]
        a0, a1 = _split(jnp.concatenate([u["p_mat"], u["m_full"]], axis=0), 2)
        h_hi, h_lo = _split(h0, 2)
        bh, bl = _pair_diag(h_hi), _pair_diag(h_lo)
        out = _dot(a0, bh) + (_dot(a0, bl) + _dot(a1, bh))
        (yb_ref if d else yf_ref)[bi, :, u["sl"]] = out[:CHUNK] + u["y0"]
        h_ref[bi, d, p] = out[CHUNK:] + u["g_mat"]


def _rwkv_pairs(r, v, kk, ld, kd, beta, ctx_len):
    bn, s, _ = r.shape
    n_chunks = s // CHUNK
    ctx_chunks = ctx_len // CHUNK
    rows = RWKV_ROWS if bn % RWKV_ROWS == 0 else 1

    def back(c):
        return jnp.where(c < ctx_chunks, ctx_chunks - 1 - c, n_chunks - 1 + ctx_chunks - c)

    fwd = pl.BlockSpec((rows, CHUNK, B_WIDTH), lambda b, c: (b, c, 0))
    bwd = pl.BlockSpec((rows, CHUNK, B_WIDTH), lambda b, c: (b, back(c), 0))
    fwd_d = pl.BlockSpec((None, rows, CHUNK, B_WIDTH), lambda b, c: (0, b, c, 0))
    bwd_d = pl.BlockSpec((None, rows, CHUNK, B_WIDTH), lambda b, c: (1, b, back(c), 0))
    y = jax.ShapeDtypeStruct((bn, s, B_WIDTH), F32)
    return pl.pallas_call(
        _rwkv_pair_kernel,
        grid=(bn // rows, n_chunks),
        in_specs=[fwd, fwd, fwd, bwd, bwd, bwd, fwd_d, fwd_d, fwd_d, bwd_d, bwd_d, bwd_d],
        out_specs=[fwd, bwd],
        out_shape=[y, y],
        scratch_shapes=[pltpu.VMEM((rows, 2, B_WIDTH // LANES, B_HEAD, LANES), F32)],
        compiler_params=_params(("parallel", "arbitrary")),
        name="rwkv_scan",
    )(r, v, kk, r, v, kk, ld, kd, beta, ld, kd, beta)


def _outproj_kernel(x_ref, mod_ref, a_ref, yf_ref, yb_ref, bonus_ref, g_ref, c_ref,
                    wa_ref, wb_ref, wc_ref, lnxg_ref, lnxb_ref, ln1g_ref, ln1b_ref, o_ref, *, alpha):
    nb = x_ref.shape[0]
    stack = lambda ref: jnp.concatenate([ref[r] for r in range(nb)], axis=0)
    y = stack(yf_ref) + stack(yb_ref) + stack(bonus_ref)
    ones = _group_ones(B_WIDTH, B_HEAD)
    inv = 1.0 / B_HEAD
    mu = _dot_exact_rhs(y, ones) * inv
    yc = y - mu
    var = _dot_exact_rhs(yc * yc, ones) * inv
    yn = yc * lax.rsqrt(var + GN_EPS) * lnxg_ref[...] + lnxb_ref[...]
    bmix = (yn * stack(g_ref)).astype(BF16)
    o = _dot(stack(a_ref), wa_ref[...]) + _dot(bmix, wb_ref[...]) + _dot(stack(c_ref), wc_ref[...])
    for r in range(nb):
        o_ref[r] = (_ln(alpha * x_ref[r] + mod_ref[r, 2:3] * o[r * TM:(r + 1) * TM], LN_EPS) * ln1g_ref[...]
                    + ln1b_ref[...])


def _outproj(xs, mods, a_out, yf, yb, bonus, g, c_out, wa, wb, wc, lnxg, lnxb, ln1g, ln1b, ctx_len, first_tile,
             alpha):
    bn, s, d = xs.shape
    nb = _batch_rows(bn)
    row = lambda width: pl.BlockSpec((nb, TM, width), lambda b, i: (b, i + first_tile, 0))
    full = lambda a: pl.BlockSpec(a.shape, lambda b, i: (0,) * a.ndim)
    consts = (wa, wb, wc, lnxg, lnxb, ln1g, ln1b)
    return pl.pallas_call(
        functools.partial(_outproj_kernel, alpha=alpha),
        grid=(bn // nb, s // TM - first_tile),
        in_specs=[row(d), _mod_spec(ctx_len, d, nb, first_tile),
                  row(A_WIDTH), row(B_WIDTH), row(B_WIDTH), row(B_WIDTH), row(B_WIDTH), row(C_WIDTH)]
                 + [full(p) for p in consts],
        out_specs=row(d),
        out_shape=jax.ShapeDtypeStruct((bn, s, d), F32),
        compiler_params=_params(("parallel", "parallel")),
        name="outproj",
    )(xs, mods, a_out, yf, yb, bonus, g, c_out, *consts)


def _swiglu_rows(h, w1_ref, w3_ref, w2_ref):
    acc = jnp.zeros((h.shape[0], w2_ref.shape[-1]), F32)
    for j in range(w1_ref.shape[-1] // FF_CHUNK):
        sl = slice(j * FF_CHUNK, (j + 1) * FF_CHUNK)
        u = _dot(h, w1_ref[:, sl])
        t = _dot(h, w3_ref[:, sl])
        acc = acc + _dot((_silu(u) * t).astype(BF16), w2_ref[sl, :])
    return acc


def _ffn_kernel(x_ref, mod_ref, w1_ref, w3_ref, w2_ref, g_ref, b_ref, o_ref, *, alpha):
    nb = x_ref.shape[0]
    h = jnp.concatenate([(_ln(x_ref[r], LN_EPS) * (1.0 + mod_ref[r, 4:5]) + mod_ref[r, 3:4]).astype(BF16)
                         for r in range(nb)], axis=0)
    f = _swiglu_rows(h, w1_ref, w3_ref, w2_ref)
    for r in range(nb):
        o_ref[r] = (_ln(alpha * x_ref[r] + mod_ref[r, 5:6] * f[r * TM:(r + 1) * TM], LN_EPS) * g_ref[...]
                    + b_ref[...])


def _ffn(x1, mods, w1, w3, w2, g, b, ctx_len, alpha):
    bn, s, d = x1.shape
    nb = _batch_rows(bn)
    row = pl.BlockSpec((nb, TM, d), lambda bb, i: (bb, i, 0))
    resident = lambda a: pl.BlockSpec(a.shape, lambda bb, i: (0,) * a.ndim, pipeline_mode=pl.Buffered(1))
    full = lambda a: pl.BlockSpec(a.shape, lambda bb, i: (0,) * a.ndim)
    return pl.pallas_call(
        functools.partial(_ffn_kernel, alpha=alpha),
        grid=(bn // nb, s // TM),
        in_specs=[row, _mod_spec(ctx_len, d, nb),
                  resident(w1), resident(w3), resident(w2), full(g), full(b)],
        out_specs=row,
        out_shape=jax.ShapeDtypeStruct((bn, s, d), F32),
        compiler_params=_params(("parallel", "parallel")),
        name="ffn",
    )(x1, mods, w1, w3, w2, g, b)


def _moe_pre_kernel(x_ref, mod_ref, router_ref, h_ref, logit_ref):
    mod = mod_ref[...]
    h = _ln(x_ref[...], LN_EPS) * (1.0 + mod[4:5]) + mod[3:4]
    h_ref[...] = h.astype(h_ref.dtype)
    logit_ref[...] = _dot6(h, router_ref[...])


def _moe_pre(x1, mods, router_p, ctx_len):
    bn, s, d = x1.shape
    ct = ctx_len // TM
    lt = (s - ctx_len) // TM
    return pl.pallas_call(
        _moe_pre_kernel,
        grid=(bn, lt),
        in_specs=[pl.BlockSpec((None, TM, d), lambda b, i: (b, i + ct, 0)),
                  pl.BlockSpec((None, None, 6, d), lambda b, i: (b, 1, 0, 0)),
                  pl.BlockSpec(router_p.shape, lambda b, i: (0, 0))],
        out_specs=[pl.BlockSpec((TM, d), lambda b, i: (b * lt + i, 0)),
                   pl.BlockSpec((TM, LANES), lambda b, i: (b * lt + i, 0))],
        out_shape=[jax.ShapeDtypeStruct((bn * lt * TM, d), BF16),
                   jax.ShapeDtypeStruct((bn * lt * TM, LANES), F32)],
        compiler_params=_params(("parallel", "parallel")),
        name="moe_pre",
    )(x1, mods, router_p)


def _slot_onehot(slots_ref, block, rows):
    sl = slots_ref[...]
    s_iota = lax.broadcasted_iota(jnp.int32, (rows, MOE_TILE), 0) + block * rows
    hit = jnp.logical_or(sl[0:1] == s_iota, sl[1:2] == s_iota)
    return jnp.where(hit, 1.0, 0.0).astype(BF16)


def _dispatch_kernel(wb_ref, wc_ref, wf_ref, wv_ref, slots_ref, h_ref, o_ref):
    w = pl.program_id(0)
    part = lambda: _dot(_slot_onehot(slots_ref, wb_ref[w], MOE_ROWS), h_ref[...])

    @pl.when(wf_ref[w] == 1)
    def _():
        o_ref[...] = part().astype(o_ref.dtype)

    @pl.when(jnp.logical_and(wf_ref[w] == 0, wv_ref[w] == 1))
    def _():
        o_ref[...] = (o_ref[...].astype(F32) + part()).astype(o_ref.dtype)


def _dispatch(h, slots, work):
    t, d = h.shape
    wb, wc, wf, wv = work
    n_slots = (t * TOP_K // MOE_ROWS + N_EXPERTS) * MOE_ROWS
    grid_spec = pltpu.PrefetchScalarGridSpec(
        num_scalar_prefetch=4,
        grid=(wb.shape[0],),
        in_specs=[pl.BlockSpec((TOP_K, MOE_TILE), lambda w, b, c, f, v: (0, c[w])),
                  pl.BlockSpec((MOE_TILE, d), lambda w, b, c, f, v: (c[w], 0))],
        out_specs=pl.BlockSpec((MOE_ROWS, d), lambda w, b, c, f, v: (b[w], 0)),
    )
    return pl.pallas_call(
        _dispatch_kernel,
        grid_spec=grid_spec,
        out_shape=jax.ShapeDtypeStruct((n_slots, d), BF16),
        compiler_params=_params(("arbitrary",)),
        name="moe_dispatch",
    )(wb, wc, wf, wv, slots, h)


def _collect_kernel(wb_ref, wc_ref, wf_ref, wv_ref, slots_ref, scol_ref, gate_ref, y_ref, o_ref):
    w = pl.program_id(0)

    def part():
        base = wb_ref[w] * COLLECT_ROWS
        scol = scol_ref[...] - base
        gates = jnp.where(jnp.logical_and(scol >= 0, scol < COLLECT_ROWS), gate_ref[...], 0.0)
        gate = jnp.sum(gates, axis=1, keepdims=True)
        return _dot(_slot_onehot(slots_ref, wb_ref[w], COLLECT_ROWS), y_ref[...], TN) * gate

    @pl.when(wf_ref[w] == 1)
    def _():
        o_ref[...] = part()

    @pl.when(jnp.logical_and(wf_ref[w] == 0, wv_ref[w] == 1))
    def _():
        o_ref[...] = o_ref[...] + part()


def _collect(y_buf, slots, slots_col, gates, work):
    n_slots, d = y_buf.shape
    t = slots.shape[1]
    wb, wc, wf, wv = work
    tile = lambda w, b, c, f, v: (c[w], 0)
    grid_spec = pltpu.PrefetchScalarGridSpec(
        num_scalar_prefetch=4,
        grid=(wb.shape[0],),
        in_specs=[pl.BlockSpec((TOP_K, MOE_TILE), lambda w, b, c, f, v: (0, c[w])),
                  pl.BlockSpec((MOE_TILE, TOP_K), tile), pl.BlockSpec((MOE_TILE, TOP_K), tile),
                  pl.BlockSpec((COLLECT_ROWS, d), lambda w, b, c, f, v: (b[w], 0))],
        out_specs=pl.BlockSpec((MOE_TILE, d), tile),
    )
    return pl.pallas_call(
        _collect_kernel,
        grid_spec=grid_spec,
        out_shape=jax.ShapeDtypeStruct((t, d), F32),
        compiler_params=_params(("arbitrary",)),
        name="moe_collect",
    )(wb, wc, wf, wv, slots, slots_col, gates, y_buf)


def _expert_kernel(be_ref, nb_ref, x_ref, w1_ref, w3_ref, w2_ref, o_ref):
    i = pl.program_id(0)

    @pl.when(i < nb_ref[0])
    def _():
        o_ref[...] = _swiglu_rows(x_ref[...], w1_ref, w3_ref, w2_ref).astype(o_ref.dtype)

    @pl.when(i >= nb_ref[0])
    def _():
        o_ref[...] = jnp.zeros_like(o_ref)


def _experts(buf, block_e, n_used, w1, w3, w2):
    n, d = buf.shape
    n_blocks = n // MOE_ROWS
    ff = w1.shape[-1]
    wspec = lambda shape: pl.BlockSpec((None,) + shape, lambda i, be, nb: (be[i], 0, 0),
                                       pipeline_mode=pl.Buffered(1))
    used = lambda i, be, nb: (jnp.minimum(i, nb[0] - 1), 0)
    grid_spec = pltpu.PrefetchScalarGridSpec(
        num_scalar_prefetch=2,
        grid=(n_blocks,),
        in_specs=[pl.BlockSpec((MOE_ROWS, d), used), wspec((d, ff)), wspec((d, ff)), wspec((ff, d))],
        out_specs=pl.BlockSpec((MOE_ROWS, d), lambda i, be, nb: (i, 0)),
    )
    return pl.pallas_call(
        _expert_kernel,
        grid_spec=grid_spec,
        out_shape=jax.ShapeDtypeStruct((n, d), BF16),
        compiler_params=_params(("arbitrary",)),
        name="experts",
    )(block_e, n_used, buf, w1, w3, w2)


def _combine_kernel(x_ref, mod_ref, y_ref, g_ref, b_ref, o_ref, *, alpha):
    mod = mod_ref[...]
    o_ref[...] = _ln(alpha * x_ref[...] + mod[5:6] * y_ref[...], LN_EPS) * g_ref[...] + b_ref[...]


def _combine(x1, mods, y_tok, g, b, ctx_len, alpha):
    bn, s, d = x1.shape
    ct = ctx_len // TM
    lt = (s - ctx_len) // TM
    full = lambda a: pl.BlockSpec(a.shape, lambda bb, i: (0,) * a.ndim)
    return pl.pallas_call(
        functools.partial(_combine_kernel, alpha=alpha),
        grid=(bn, lt),
        in_specs=[pl.BlockSpec((None, TM, d), lambda bb, i: (bb, i + ct, 0)),
                  pl.BlockSpec((None, None, 6, d), lambda bb, i: (bb, 1, 0, 0)),
                  pl.BlockSpec((TM, d), lambda bb, i: (bb * lt + i, 0)),
                  full(g), full(b)],
        out_specs=pl.BlockSpec((None, TM, d), lambda bb, i: (bb, i, 0)),
        out_shape=jax.ShapeDtypeStruct((bn, lt * TM, d), F32),
        compiler_params=_params(("parallel", "parallel")),
        name="moe_combine",
    )(x1, mods, y_tok, g, b)


def _work_lists(rank_at_tile, counts, pstart, pend, rows, n_slots, n_tiles):
    n_blocks = n_slots // rows
    n_work = n_tiles * N_EXPERTS + n_blocks
    blocks = jnp.arange(n_blocks, dtype=jnp.int32)
    block_e = jnp.minimum(jnp.searchsorted(pend, blocks * rows, side="right"), N_EXPERTS - 1).astype(jnp.int32)
    r0 = blocks * rows - pstart[block_e]
    used = jnp.logical_and(blocks * rows < pend[-1], r0 < counts[block_e])
    r_last = jnp.minimum(r0 + rows, counts[block_e]) - 1
    cols = rank_at_tile.T[block_e]
    find = jax.vmap(lambda col, val: jnp.searchsorted(col, val, side="right"))
    lo = jnp.clip(find(cols, r0) - 1, 0, n_tiles - 1)
    hi = jnp.clip(find(cols, r_last) - 1, 0, n_tiles - 1)
    n_b = jnp.where(used, hi - lo + 1, 0)
    ends = jnp.cumsum(n_b)
    starts = ends - n_b
    total = ends[-1]
    w = jnp.arange(n_work, dtype=jnp.int32)
    valid = w < total
    wl = jnp.minimum(w, total - 1)
    blk = jnp.minimum(jnp.searchsorted(ends, wl, side="right"), n_blocks - 1).astype(jnp.int32)
    tile = (lo[blk] + (wl - starts[blk])).astype(jnp.int32)
    first = jnp.logical_and(valid, w == starts[blk])
    as_i32 = lambda a: a.astype(jnp.int32)
    by_block = (blk, tile, as_i32(first), as_i32(valid))
    order = jnp.argsort(jnp.where(valid, tile, n_tiles), stable=True)
    order = order[jnp.minimum(w, total - 1)]
    tile2, blk2 = tile[order], blk[order]
    first2 = jnp.logical_and(valid, jnp.concatenate([jnp.ones((1,), bool), tile2[1:] != tile2[:-1]]))
    by_tile = (blk2, tile2, as_i32(first2), as_i32(valid))
    return by_block, by_tile


def _moe_layer(x1, mods, router, w1, w3, w2, g, b, ctx_len, alpha):
    d = x1.shape[-1]
    router_p = jnp.pad(router, ((0, 0), (0, LANES - N_EXPERTS)))
    h, logits = _moe_pre(x1, mods, router_p, ctx_len)
    t = h.shape[0]
    top_v, top_i = lax.top_k(logits[:, :N_EXPERTS], TOP_K)
    gates = jax.nn.softmax(top_v, axis=-1)
    e_flat = top_i.reshape(-1)
    onehot = (e_flat[:, None] == jnp.arange(N_EXPERTS)[None, :]).astype(jnp.int32)
    ranks = jnp.cumsum(onehot, axis=0) - onehot
    rank = jnp.sum(ranks * onehot, axis=1)
    counts = jnp.sum(onehot, axis=0)
    padded = (counts + MOE_ROWS - 1) // MOE_ROWS * MOE_ROWS
    pend = jnp.cumsum(padded)
    pstart = pend - padded
    slot = (pstart[e_flat] + rank).astype(jnp.int32)
    n_blocks = t * TOP_K // MOE_ROWS + N_EXPERTS
    assert t % MOE_TILE == 0
    n_tiles = t // MOE_TILE
    block_e = jnp.minimum(jnp.searchsorted(pend, jnp.arange(n_blocks) * MOE_ROWS, side="right"),
                          N_EXPERTS - 1).astype(jnp.int32)
    n_used = (pend[-1:] // MOE_ROWS).astype(jnp.int32)
    rank_at_tile = jnp.concatenate([ranks[::MOE_TILE * TOP_K], counts[None, :]], axis=0)
    n_slots = n_blocks * MOE_ROWS
    by_block, _ = _work_lists(rank_at_tile, counts, pstart, pend, MOE_ROWS, n_slots, n_tiles)
    _, by_tile = _work_lists(rank_at_tile, counts, pstart, pend, COLLECT_ROWS, n_slots, n_tiles)
    slots_col = slot.reshape(t, TOP_K)
    slots = slots_col.T
    buf = _dispatch(h, slots, by_block)
    y_buf = _experts(buf, block_e, n_used, w1, w3, w2)
    y_tok = _collect(y_buf, slots, slots_col, gates, by_tile)
    return _combine(x1, mods, y_tok, g, b, ctx_len, alpha)


def _rope_tables(n_rows, ctx_len, dim, lane_lo):
    quarter = dim // 4
    inv = ROPE_THETA ** (-jnp.arange(quarter, dtype=F32) / quarter)
    rows = jnp.repeat(jnp.arange(n_rows, dtype=F32), GRID_W)
    cols = jnp.tile(jnp.arange(GRID_W, dtype=F32), n_rows)
    ang = jnp.concatenate([rows[:, None] * inv, rows[:, None] * inv,
                           cols[:, None] * inv, cols[:, None] * inv], axis=-1)
    sign = jnp.tile(jnp.concatenate([-jnp.ones(quarter, F32), jnp.ones(quarter, F32)]), 2)
    cos, sin = jnp.cos(ang), jnp.sin(ang) * sign
    length = cos.shape[0]
    if lane_lo == 0:
        reps = LANES // dim
        cos, sin = jnp.tile(cos, (1, reps)), jnp.tile(sin, (1, reps))
    else:
        pad = ((0, 0), (lane_lo, LANES - lane_lo - dim))
        cos = jnp.pad(cos, pad, constant_values=1.0)
        sin = jnp.pad(sin, pad)
    cos = jnp.concatenate([jnp.ones((ctx_len, LANES), F32), cos], axis=0)
    sin = jnp.concatenate([jnp.zeros((ctx_len, LANES), F32), sin], axis=0)
    return cos, sin


def _block_diag2(w):
    z = jnp.zeros_like(w[0])
    return jnp.concatenate([jnp.concatenate([w[0], z], axis=1), jnp.concatenate([z, w[1]], axis=1)], axis=0)


def _lambda_init(layer):
    return 0.8 - 0.6 * math.exp(-0.3 * layer)


def kernel(x, c, ctx, c_ctx, ada_w, ada_b, w_in, w_out, ln1_g, ln1_b, ln2_g, ln2_b,
           lam_q1, lam_k1, lam_q2, lam_k2, diff_norm_g, shift_mu, w0, w2, a0, a2, g2,
           k_k, k_a, r_k, lnx_g, lnx_b, q_norm_g, w_uq, kv_norm_g, w_ukv,
           ff_w1, ff_w3, ff_w2, router, moe_w1, moe_w3, moe_w2):
    bn, seq, d = x.shape
    ctx_len = ctx.shape[1]
    depth = ada_w.shape[0]
    assert d == D_MODEL and seq % TM == 0 and ctx_len % TM == 0 and seq % GRID_W == 0
    alpha = (2.0 * depth) ** 0.25
    n_grid_rows = seq // GRID_W
    cos_a, sin_a = _rope_tables(n_grid_rows, ctx_len, A_QK_DIM, 0)
    cos_c, sin_c = _rope_tables(n_grid_rows, ctx_len, C_ROPE, C_NOPE)

    cond_rows = 8 * ((bn + 1 + 7) // 8)
    cond = jnp.zeros((cond_rows, d), F32).at[:bn].set(c).at[bn].set(c_ctx)
    xs = jnp.concatenate([ctx, x], axis=1)

    for i in range(depth):
        with_ctx = i < depth - 1
        m = _ada(cond, ada_w[i], ada_b[i]).reshape(cond_rows, 6, d)
        mods = jnp.stack([jnp.broadcast_to(m[bn], (bn, 6, d)), m[:bn]], axis=1)

        wc = w_in[i][:, N_A + N_B:]
        kpe_w = jnp.pad(wc[:, C_Q_RANK + C_KV_RANK:], ((0, 0), (C_NOPE, LANES - C_NOPE - C_ROPE)))
        w_p = jnp.concatenate([w_in[i][:, :N_A + N_B], wc[:, :C_Q_RANK + C_KV_RANK], kpe_w], axis=1).astype(BF16)
        wq = w_uq[i].reshape(C_Q_RANK, C_HEADS, C_NOPE + C_ROPE)
        wq_p = jnp.pad(wq, ((0, 0), (0, 0), (0, LANES - C_NOPE - C_ROPE))).reshape(C_Q_RANK, -1).astype(BF16)
        wkv = w_ukv[i].reshape(C_KV_RANK, C_HEADS, C_NOPE + C_V)
        wk_p = jnp.pad(wkv[:, :, :C_NOPE], ((0, 0), (0, 0), (0, LANES - C_NOPE))).reshape(C_KV_RANK, -1).astype(BF16)
        wv_p = wkv[:, :, C_NOPE:].reshape(C_KV_RANK, -1).astype(BF16)
        mla_consts = (q_norm_g[i].reshape(1, -1), kv_norm_g[i].reshape(1, -1), wq_p, wk_p, wv_p)
        qa, ka, va, pb, qc, kc, vc = _inproj(xs, mods, w_p, cos_a, sin_a, mla_consts, cos_c, sin_c, ctx_len)

        lam_p = jnp.stack([lam_q1[i], lam_k1[i], lam_q2[i], lam_k2[i]])
        g_col = jnp.broadcast_to(diff_norm_g[i][:, None], (A_V_DIM, TM))
        first_tile = 0 if with_ctx else ctx_len // TM
        a_out = _attention(qa, ka, va, lam_p, g_col, mode="diff", ctx_len=ctx_len, first_tile=first_tile,
                           lam_init=_lambda_init(i))
        c_out = _attention(qc, kc, vc, lam_p, g_col, mode="mla", ctx_len=ctx_len, first_tile=first_tile)

        r, v, kkn, ld, kd, beta, gate, bonus = _rwkv_prep(
            pb, shift_mu[i].reshape(1, -1), w0[i].reshape(1, -1), _block_diag2(w2[i]).astype(BF16),
            a0[i].reshape(1, -1), _block_diag2(a2[i]).astype(BF16), g2[i].astype(BF16),
            k_k[i].reshape(1, -1), k_a[i].reshape(1, -1), r_k[i].reshape(1, -1), ctx_len)
        yf, yb = _rwkv_pairs(r, v, kkn, ld, kd, beta, ctx_len)

        wo = w_out[i].astype(BF16)
        x1 = _outproj(xs, mods, a_out, yf, yb, bonus, gate, c_out,
                      wo[:A_WIDTH], wo[A_WIDTH:A_WIDTH + B_WIDTH], wo[A_WIDTH + B_WIDTH:],
                      lnx_g[i].reshape(1, -1), lnx_b[i].reshape(1, -1),
                      ln1_g[i].reshape(1, -1), ln1_b[i].reshape(1, -1), ctx_len, first_tile, alpha)

        j = i // 2
        g2n, b2n = ln2_g[i].reshape(1, -1), ln2_b[i].reshape(1, -1)
        if i % 2 == 0:
            xs = _ffn(x1, mods, ff_w1[j].astype(BF16), ff_w3[j].astype(BF16), ff_w2[j].astype(BF16),
                      g2n, b2n, ctx_len, alpha)
        else:
            if with_ctx:
                raise NotImplementedError("routed FFN on the context rows is not needed at this depth")
            return _moe_layer(x1, mods, router[j], moe_w1[j].astype(BF16), moe_w3[j].astype(BF16),
                              moe_w2[j].astype(BF16), g2n, b2n, ctx_len, alpha)
    return xs[:, ctx_len:]
```

```python
import functools
import math

import jax
import jax.numpy as jnp
from jax import lax
from jax.experimental import pallas as pl
from jax.experimental.pallas import tpu as pltpu

F32 = jnp.float32
BF16 = jnp.bfloat16

D_MODEL = 1024
GRID_W = 64
ROPE_THETA = 10000.0
A_HEADS, A_QK_DIM, A_V_DIM = 4, 64, 128
A_WIDTH = A_HEADS * A_V_DIM
A_QK_COLS = 2 * A_HEADS * A_QK_DIM
B_HEADS, B_HEAD = 4, 64
B_WIDTH = B_HEADS * B_HEAD
B_DECAY_RANK, B_A_RANK, B_GATE_RANK = 64, 64, 128
C_HEADS, C_NOPE, C_ROPE, C_V = 4, 64, 32, 64
C_WIDTH = C_HEADS * C_V
C_Q_RANK, C_KV_RANK = 256, 128
N_A = 2 * A_QK_COLS + A_WIDTH
N_B = 3 * B_WIDTH + 2 * B_DECAY_RANK + 2 * B_A_RANK + B_GATE_RANK
N_C = C_Q_RANK + C_KV_RANK + C_ROPE
N_C_PAD = C_Q_RANK + C_KV_RANK + 128
D_FF = 3584
N_EXPERTS = 8
TOP_K = 2
LN_EPS = 1e-6
RMS_EPS = 1e-6
GN_EPS = 64e-5

LOG2E = math.log2(math.e)
LANES = 128
TM = 256
CHUNK = 64
ROWS_B = 2
RWKV_ROWS = 4
ATT_TK = 1024
ONES_ROWS = 16
MOE_ROWS = 512
MOE_TILE = 512
FF_CHUNK = 512
VMEM_LIMIT = 56 * 1024 * 1024

NN = (((1,), (0,)), ((), ()))
NT = (((1,), (1,)), ((), ()))
TN = (((0,), (0,)), ((), ()))


def _params(sem, vmem=VMEM_LIMIT, flags=None):
    return pltpu.CompilerParams(dimension_semantics=sem, vmem_limit_bytes=vmem, flags=flags)


def _split(x, n):
    parts, r = [], x
    for _ in range(n):
        p = r.astype(BF16)
        parts.append(p)
        r = r - p.astype(F32)
    return parts


def _dot(a, b, dn=NN):
    return lax.dot_general(a, b, dn, preferred_element_type=F32)


def _dot1(a, b, dn=NN):
    return _dot(a.astype(BF16), b.astype(BF16), dn)


def _dot3(a, b, dn=NN):
    a0, a1 = _split(a, 2)
    b0, b1 = _split(b, 2)
    return _dot(a0, b0, dn) + (_dot(a0, b1, dn) + _dot(a1, b0, dn))


def _dot6(a, b, dn=NN):
    a0, a1, a2 = _split(a, 3)
    b0, b1, b2 = _split(b, 3)
    lo = _dot(a1, b1, dn) + (_dot(a0, b2, dn) + _dot(a2, b0, dn))
    return _dot(a0, b0, dn) + ((_dot(a0, b1, dn) + _dot(a1, b0, dn)) + lo)


def _dot_exact_lhs(a_bf16, b, dn=NN, n=3):
    parts = _split(b, n)
    out = _dot(a_bf16, parts[-1], dn)
    for p in parts[-2::-1]:
        out = out + _dot(a_bf16, p, dn)
    return out


def _dot_exact_rhs(a, b_bf16, dn=NN, n=3):
    parts = _split(a, n)
    out = _dot(parts[-1], b_bf16, dn)
    for p in parts[-2::-1]:
        out = out + _dot(p, b_bf16, dn)
    return out


def _ln(x, eps):
    mu = jnp.mean(x, axis=-1, keepdims=True)
    xc = x - mu
    return xc * lax.rsqrt(jnp.mean(xc * xc, axis=-1, keepdims=True) + eps)


def _sigmoid(x):
    return 1.0 / (1.0 + jnp.exp(-x))


def _silu(x):
    return x * _sigmoid(x)


def _group_ones(width, group):
    r = lax.broadcasted_iota(jnp.int32, (width, width), 0) // group
    c = lax.broadcasted_iota(jnp.int32, (width, width), 1) // group
    return (r == c).astype(BF16)


def _partner(x, half):
    lane = lax.broadcasted_iota(jnp.int32, x.shape, 1)
    up = pltpu.roll(x, LANES - half, 1)
    dn = pltpu.roll(x, half, 1)
    return jnp.where((lane % (2 * half)) < half, up, dn)


def _rope(x, cos, sin, half):
    return x * cos + _partner(x, half) * sin


def _ada_kernel(c_ref, w_ref, b_ref, o_ref):
    o_ref[...] = _dot3(_silu(c_ref[...]), w_ref[...]) + b_ref[...]


def _ada(cond, w, b):
    rows, d = cond.shape
    n = w.shape[1]
    tn = 1536
    return pl.pallas_call(
        _ada_kernel,
        grid=(n // tn,),
        in_specs=[pl.BlockSpec((rows, d), lambda j: (0, 0)),
                  pl.BlockSpec((d, tn), lambda j: (0, j)),
                  pl.BlockSpec((1, tn), lambda j: (0, j))],
        out_specs=pl.BlockSpec((rows, tn), lambda j: (0, j)),
        out_shape=jax.ShapeDtypeStruct((rows, n), F32),
        compiler_params=_params(("parallel",)),
        name="ada",
    )(cond, w, b.reshape(1, n))


def _inproj_kernel(x_ref, mod_ref, w_ref, cos_ref, sin_ref,
                   qg_ref, kvg_ref, wq_ref, wk_ref, wv_ref, cosc_ref, sinc_ref,
                   q_ref, k_ref, v_ref, pb_ref, cq_ref, ck_ref, cv_ref):
    nb = x_ref.shape[0]
    h = jnp.concatenate([(_ln(x_ref[r], LN_EPS) * (1.0 + mod_ref[r, 1:2]) + mod_ref[r, 0:1]).astype(BF16)
                         for r in range(nb)], axis=0)
    cos, sin = cos_ref[...], sin_ref[...]
    scale = A_QK_DIM ** -0.5 * LOG2E
    rows = lambda a, r: a[r * TM:(r + 1) * TM]
    for j in range(A_QK_COLS // LANES):
        sl = slice(j * LANES, (j + 1) * LANES)
        qj = _dot(h, w_ref[:, sl])
        kj = _dot(h, w_ref[:, A_QK_COLS + j * LANES:A_QK_COLS + (j + 1) * LANES])
        vj = _dot(h, w_ref[:, 2 * A_QK_COLS + j * LANES:2 * A_QK_COLS + (j + 1) * LANES])
        for r in range(nb):
            q_ref[r, sl, :] = (_rope(rows(qj, r), cos, sin, A_QK_DIM // 4) * scale).T.astype(BF16)
            k_ref[r, :, sl] = _rope(rows(kj, r), cos, sin, A_QK_DIM // 4).astype(BF16)
            v_ref[r, sl, :] = rows(vj, r).T.astype(BF16)
    pb = _dot(h, w_ref[:, N_A:N_A + N_B])
    for r in range(nb):
        pb_ref[r] = rows(pb, r)
    _mla_heads(_dot(h, w_ref[:, N_A + N_B:]), nb, qg_ref, kvg_ref, wq_ref, wk_ref, wv_ref,
               cosc_ref[...], sinc_ref[...], cq_ref, ck_ref, cv_ref)


def _batch_rows(bn):
    return ROWS_B if bn % ROWS_B == 0 else 1


def _mod_spec(ctx_len, d, nb=None, first_tile=0):
    ct = ctx_len // TM
    return pl.BlockSpec((nb, None, 6, d), lambda b, i: (b, jnp.where(i + first_tile >= ct, 1, 0), 0, 0))


def _inproj(xs, mods, w_p, cos_a, sin_a, mla_consts, cos_c, sin_c, ctx_len):
    bn, s, d = xs.shape
    n_tiles = s // TM
    n_w = w_p.shape[1]
    nb = _batch_rows(bn)
    row = lambda width: pl.BlockSpec((nb, TM, width), lambda b, i: (b, i, 0))
    col = lambda width: pl.BlockSpec((nb, width, TM), lambda b, i: (b, 0, i))
    full = lambda a: pl.BlockSpec(a.shape, lambda b, i: (0,) * a.ndim)
    tab = pl.BlockSpec((TM, LANES), lambda b, i: (i, 0))
    hw = C_HEADS * LANES
    outs = [jax.ShapeDtypeStruct((bn, A_QK_COLS, s), BF16),
            jax.ShapeDtypeStruct((bn, s, A_QK_COLS), BF16),
            jax.ShapeDtypeStruct((bn, A_WIDTH, s), BF16),
            jax.ShapeDtypeStruct((bn, s, N_B), F32),
            jax.ShapeDtypeStruct((bn, hw, s), BF16),
            jax.ShapeDtypeStruct((bn, s, hw), BF16),
            jax.ShapeDtypeStruct((bn, C_WIDTH, s), BF16)]
    return pl.pallas_call(
        _inproj_kernel,
        grid=(bn // nb, n_tiles),
        in_specs=[row(d), _mod_spec(ctx_len, d, nb), full(w_p), tab, tab]
                 + [full(a) for a in mla_consts] + [tab, tab],
        out_specs=[col(A_QK_COLS), row(A_QK_COLS), col(A_WIDTH), row(N_B), col(hw), row(hw), col(C_WIDTH)],
        out_shape=outs,
        compiler_params=_params(("parallel", "parallel")),
        name="inproj",
    )(xs, mods, w_p, cos_a, sin_a, *mla_consts, cos_c, sin_c)


def _mla_heads(pc, nb, qg_ref, kvg_ref, wq_ref, wk_ref, wv_ref, cos, sin, q_ref, k_ref, v_ref):
    cq = pc[:, :C_Q_RANK]
    cq = cq * lax.rsqrt(jnp.mean(cq * cq, axis=-1, keepdims=True) + RMS_EPS) * qg_ref[...]
    ckv = pc[:, C_Q_RANK:C_Q_RANK + C_KV_RANK]
    ckv = ckv * lax.rsqrt(jnp.mean(ckv * ckv, axis=-1, keepdims=True) + RMS_EPS) * kvg_ref[...]
    cqb, ckvb = cq.astype(BF16), ckv.astype(BF16)
    scale = (C_NOPE + C_ROPE) ** -0.5 * LOG2E
    rows = lambda a, r: a[r * TM:(r + 1) * TM]
    kpe = [_rope(rows(pc, r)[:, C_Q_RANK + C_KV_RANK:], cos, sin, C_ROPE // 4) for r in range(nb)]
    for h in range(C_HEADS):
        sl = slice(h * LANES, (h + 1) * LANES)
        qh = _dot(cqb, wq_ref[:, sl])
        kh = _dot(ckvb, wk_ref[:, sl])
        for r in range(nb):
            q_ref[r, sl, :] = (_rope(rows(qh, r), cos, sin, C_ROPE // 4) * scale).T.astype(BF16)
            k_ref[r, :, sl] = (rows(kh, r) + kpe[r]).astype(BF16)
    for j in range(C_WIDTH // LANES):
        sl = slice(j * LANES, (j + 1) * LANES)
        vj = _dot(ckvb, wv_ref[:, sl])
        for r in range(nb):
            v_ref[r, sl, :] = rows(vj, r).T.astype(BF16)


def _attn_kernel(lam_ref, g_ref, *refs, modes, ctx_tiles, ctx_len, tk, first_tile, lam_init):
    n_sets = len(modes)
    in_refs = [refs[3 * t:3 * t + 3] for t in range(n_sets)]
    o_refs = refs[3 * n_sets:]
    qi = pl.program_id(1) + first_tile
    tq = in_refs[0][0].shape[1]
    half = lax.broadcasted_iota(jnp.int32, (LANES, 1), 0) < (LANES // 2)
    chains = []
    for t, (qt_ref, k_ref, vt_ref) in enumerate(in_refs):
        heads = qt_ref.shape[0] // LANES
        for g in range(heads):
            qt = qt_ref[g * LANES:(g + 1) * LANES, :]
            if modes[t] == "diff":
                qt = jnp.concatenate([jnp.where(half, qt, jnp.zeros_like(qt)),
                                      jnp.where(half, jnp.zeros_like(qt), qt)], axis=1)
            chains.append((t, g, qt, vt_ref.shape[0] // heads, qt.shape[1]))
    s_len = in_refs[0][1].shape[0]

    def fold(x, reduce):
        rows = x.shape[0]
        while rows > 8:
            g = max(d for d in range(2, 9) if (rows // 8) % d == 0)
            rows //= g
            x = reduce(x.reshape(g, rows, x.shape[1]), axis=0)
        return x

    def absorb(kcs, vcs, carry):
        n = len(chains)
        ss = [_dot(kcs[t][:, g * LANES:(g + 1) * LANES], q) for (t, g, q, _, _) in chains]
        m_new = [jnp.maximum(carry[c][0], jnp.max(fold(ss[c], jnp.max), axis=0, keepdims=True)) for c in range(n)]
        alpha = [jnp.exp2(carry[c][0] - m_new[c]) for c in range(n)]
        ps = [jnp.exp2(ss[c] - m_new[c]).astype(BF16) for c in range(n)]
        ones = jnp.ones((ONES_ROWS, kcs[0].shape[0]), BF16)
        acc = [alpha[c] * carry[c][1]
               + _dot(jnp.concatenate([vcs[t][g * dv:(g + 1) * dv], ones], axis=0), ps[c])
               for c, (t, g, _, dv, _) in enumerate(chains)]
        return tuple((m_new[c], acc[c]) for c in range(n))

    def finish(stats):
        outs = [a[:dv] / a[dv:dv + 1] for (_, a), (_, _, _, dv, _) in zip(stats, chains)]
        for t in range(n_sets):
            mine = [outs[c] for c, ch in enumerate(chains) if ch[0] == t]
            if modes[t] == "diff":
                lp = lam_ref[...]
                lam = (jnp.exp(jnp.sum(lp[0:1] * lp[1:2], axis=-1, keepdims=True))
                       - jnp.exp(jnp.sum(lp[2:3] * lp[3:4], axis=-1, keepdims=True)) + lam_init)
                for g, og in enumerate(mine):
                    o = og[:, :tq] - lam * og[:, tq:]
                    o = o * lax.rsqrt(jnp.mean(o * o, axis=0, keepdims=True) + RMS_EPS) * g_ref[...]
                    o = o * (1.0 - lam_init)
                    o_refs[t][:, g * LANES:(g + 1) * LANES] = o.T.astype(o_refs[t].dtype)
            else:
                per_slab = LANES // mine[0].shape[0]
                for j in range(len(mine) // per_slab):
                    o = jnp.concatenate(mine[j * per_slab:(j + 1) * per_slab], axis=0)
                    o_refs[t][:, j * LANES:(j + 1) * LANES] = o.T.astype(o_refs[t].dtype)

    def run(first_keys, n_more):
        init = tuple((jnp.full((1, w), -1e30, F32), jnp.zeros((dv + ONES_ROWS, w), F32))
                     for (_, _, _, dv, w) in chains)
        stats = absorb([r[1][0:first_keys, :] for r in in_refs], [r[2][:, 0:first_keys] for r in in_refs], init)

        def body(j, stats):
            off = pl.multiple_of(first_keys + j * tk, LANES)
            return absorb([r[1][pl.ds(off, tk), :] for r in in_refs],
                          [r[2][:, pl.ds(off, tk)] for r in in_refs], stats)

        finish(lax.fori_loop(0, n_more, body, stats))

    @pl.when(qi < ctx_tiles)
    def _():
        run(ctx_len, 0)

    @pl.when(qi >= ctx_tiles)
    def _():
        run(ctx_len + tk, (s_len - ctx_len) // tk - 1)


def _attention(sets, lam_p, g, *, ctx_len, first_tile, lam_init):
    bn, s, _ = sets[0][2].shape
    tk = math.gcd(s - ctx_len, ATT_TK)
    assert tk % LANES == 0 and ctx_len % LANES == 0
    kern = functools.partial(_attn_kernel, modes=tuple(m for m, _, _, _ in sets), ctx_tiles=ctx_len // TM,
                             ctx_len=ctx_len, tk=tk, first_tile=first_tile, lam_init=lam_init)
    in_specs = [pl.BlockSpec(lam_p.shape, lambda b, i: (0, 0)), pl.BlockSpec(g.shape, lambda b, i: (0, 0))]
    args, out_specs, out_shapes = [lam_p, g], [], []
    for _, qt, k, vt in sets:
        in_specs += [pl.BlockSpec((None, qt.shape[1], TM), lambda b, i: (b, 0, i + first_tile)),
                     pl.BlockSpec((None, s, k.shape[2]), lambda b, i: (b, 0, 0), pipeline_mode=pl.Buffered(1)),
                     pl.BlockSpec((None, vt.shape[1], s), lambda b, i: (b, 0, 0), pipeline_mode=pl.Buffered(1))]
        args += [qt, k, vt]
        out_specs.append(pl.BlockSpec((None, TM, vt.shape[1]), lambda b, i: (b, i + first_tile, 0)))
        out_shapes.append(jax.ShapeDtypeStruct((bn, s, vt.shape[1]), BF16))
    return pl.pallas_call(
        kern,
        grid=(bn, s // TM - first_tile),
        in_specs=in_specs,
        out_specs=out_specs,
        out_shape=out_shapes,
        compiler_params=_params(("parallel", "parallel")),
        name="attention",
    )(*args)


def _rwkv_prep_kernel(pb_ref, prev_ref, next_ref, mu_ref, w0_ref, w2_ref, a0_ref, a2_ref, g2_ref,
                      kk_ref, ka_ref, rk_ref,
                      r_out, v_out, kkn_out, ld_out, kd_out, beta_out, g_out, bonus_out,
                      *, ctx_tiles, n_tiles):
    i = pl.program_id(1)
    x = pb_ref[...]
    row = lax.broadcasted_iota(jnp.int32, (TM, 1), 0)
    has_prev = jnp.logical_and(i != 0, i != ctx_tiles)
    has_next = jnp.logical_and(i != ctx_tiles - 1, i != n_tiles - 1)
    prev_edge = jnp.where(has_prev, prev_ref[7:8, :], 0.0)
    next_edge = jnp.where(has_next, next_ref[0:1, :], 0.0)
    xp = jnp.where(row == 0, prev_edge, pltpu.roll(x, 1, 0))
    xn = jnp.where(row == TM - 1, next_edge, pltpu.roll(x, TM - 1, 0))
    z = x + mu_ref[...] * (0.5 * (xp + xn) - x)

    r = z[:, :B_WIDTH]
    k = z[:, B_WIDTH:2 * B_WIDTH]
    v = z[:, 2 * B_WIDTH:3 * B_WIDTH]
    o = 3 * B_WIDTH
    wd = z[:, o:o + 2 * B_DECAY_RANK]
    o += 2 * B_DECAY_RANK
    ad = z[:, o:o + 2 * B_A_RANK]
    o += 2 * B_A_RANK
    gd = z[:, o:]

    u = w0_ref[...] + _dot1(jnp.tanh(wd), w2_ref[...])
    ld = -math.exp(-0.5) * _sigmoid(u)
    lr = _sigmoid(a0_ref[...] + _dot1(ad, a2_ref[...]))
    g_out[...] = _dot1(_sigmoid(gd), g2_ref[...])

    ones = _group_ones(B_WIDTH, B_HEAD)
    kk = k * kk_ref[...]
    norm = jnp.sqrt(_dot_exact_rhs(kk * kk, ones))
    kkn = kk / jnp.maximum(norm, 1e-12)
    ka = ka_ref[...]
    kd_sum = jnp.zeros_like(k)
    for d in range(2):
        lr_d = lr[:, d * B_WIDTH:(d + 1) * B_WIDTH]
        kd = k * (1.0 + (lr_d - 1.0) * ka)
        kd_sum = kd_sum + kd
        ld_out[d] = ld[:, d * B_WIDTH:(d + 1) * B_WIDTH]
        kd_out[d] = kd
        beta_out[d] = kkn * lr_d
    bonus_out[...] = _dot_exact_rhs(r * kd_sum * rk_ref[...], ones) * v
    r_out[...] = r
    v_out[...] = v
    kkn_out[...] = kkn


def _rwkv_prep(pb, mu, w0, w2bd, a0, a2bd, g2, k_k, k_a, r_k, ctx_len):
    bn, s, _ = pb.shape
    n_tiles = s // TM
    eight = TM // 8
    row = lambda width: pl.BlockSpec((None, TM, width), lambda b, i: (b, i, 0))
    drow = pl.BlockSpec((2, None, TM, B_WIDTH), lambda b, i: (0, b, i, 0))
    full = lambda a: pl.BlockSpec(a.shape, lambda b, i: (0,) * a.ndim)
    one = jax.ShapeDtypeStruct((bn, s, B_WIDTH), F32)
    two = jax.ShapeDtypeStruct((2, bn, s, B_WIDTH), F32)
    kern = functools.partial(_rwkv_prep_kernel, ctx_tiles=ctx_len // TM, n_tiles=n_tiles)
    params = (mu, w0, w2bd, a0, a2bd, g2, k_k, k_a, r_k)
    return pl.pallas_call(
        kern,
        grid=(bn, n_tiles),
        in_specs=[row(N_B),
                  pl.BlockSpec((None, 8, N_B), lambda b, i: (b, jnp.maximum(i * eight - 1, 0), 0)),
                  pl.BlockSpec((None, 8, N_B), lambda b, i: (b, jnp.minimum((i + 1) * eight, s // 8 - 1), 0)),
                  ] + [full(p) for p in params],
        out_specs=[row(B_WIDTH), row(B_WIDTH), row(B_WIDTH), drow, drow, drow, row(B_WIDTH), row(B_WIDTH)],
        out_shape=[one, one, one, two, two, two, one, one],
        compiler_params=_params(("parallel", "parallel")),
        name="rwkv_prep",
    )(pb, pb, pb, *params)


def _pair_diag(x):
    lo = lax.broadcasted_iota(jnp.int32, (1, LANES), 1) < B_HEAD
    z = jnp.zeros_like(x)
    return jnp.concatenate([jnp.where(lo, x, z), jnp.where(lo, z, x)], axis=0)


def _pair_pick(x):
    lo = lax.broadcasted_iota(jnp.int32, (1, LANES), 1) < B_HEAD
    return jnp.where(lo, x[:B_HEAD], x[B_HEAD:])


def _rwkv_pair_kernel(rf_ref, vf_ref, kkf_ref, rb_ref, vb_ref, kkb_ref,
                      ldf_ref, kdf_ref, betaf_ref, ldb_ref, kdb_ref, betab_ref,
                      yf_ref, yb_ref, h_ref):
    c = pl.program_id(1)

    @pl.when(c == 0)
    def _():
        h_ref[...] = jnp.zeros_like(h_ref)

    n_pairs = B_WIDTH // LANES
    ti = lax.broadcasted_iota(jnp.int32, (CHUNK, LANES), 0)
    si = lax.broadcasted_iota(jnp.int32, (CHUNK, LANES), 1) % CHUNK
    t64 = lax.broadcasted_iota(jnp.int32, (CHUNK, CHUNK), 0)
    s64 = lax.broadcasted_iota(jnp.int32, (CHUNK, CHUNK), 1)
    eye = ti == si
    dirs = ((rf_ref, vf_ref, kkf_ref, ldf_ref, kdf_ref, betaf_ref, False),
            (rb_ref, vb_ref, kkb_ref, ldb_ref, kdb_ref, betab_ref, True))

    units = []
    n_rows = rf_ref.shape[0]
    for bi in range(n_rows):
      for d, (r_ref, v_ref, kk_ref, ld_ref, kd_ref, beta_ref, rev) in enumerate(dirs):
        strict = (si > ti) if rev else (ti > si)
        incl = (si >= ti) if rev else (ti >= si)
        tri = ((s64 >= t64) if rev else (t64 >= s64)).astype(BF16)
        ld_all = ld_ref[bi]
        cl_all = _dot_exact_lhs(tri, ld_all)
        for p in range(n_pairs):
            sl = slice(p * LANES, (p + 1) * LANES)
            ld, cl = ld_all[:, sl], cl_all[:, sl]
            total = jnp.sum(ld, axis=0, keepdims=True)
            inv_gam = jnp.exp(-cl)
            to_end = jnp.exp(total - cl)
            kk, kd, beta = kk_ref[bi, :, sl], kd_ref[bi, :, sl], beta_ref[bi, :, sl]
            units.append(dict(
                bi=bi, d=d, p=p, sl=sl, strict=strict, incl=incl, v=v_ref[bi, :, sl].astype(BF16),
                a_bar=-kk * jnp.exp(cl - ld), r_bar=r_ref[bi, :, sl] * jnp.exp(cl),
                b_til=(beta * inv_gam).astype(BF16), k_til=(kd * inv_gam).astype(BF16),
                b_hat=(beta * to_end).astype(BF16), k_hat=(kd * to_end).astype(BF16),
                gam_c=jnp.exp(total)))

    for u in units:
        x_mat = jnp.concatenate([u["a_bar"], u["r_bar"]], axis=0).astype(BF16)
        rhs = jnp.concatenate([_pair_diag(u["b_til"]), _pair_diag(u["k_til"])], axis=0)
        xbk = _dot(x_mat, rhs, NT)
        u["n_ab"] = jnp.where(u["strict"], xbk[:CHUNK, :LANES], 0.0)
        u["l_rb"] = jnp.where(u["incl"], xbk[CHUNK:, :LANES], 0.0)
        n_ak = jnp.where(u["strict"], xbk[:CHUNK, LANES:], 0.0)
        l_rk = jnp.where(u["incl"], xbk[CHUNK:, LANES:], 0.0)
        u["nl"] = jnp.concatenate([n_ak, l_rk], axis=0).astype(BF16)
    for u in units:
        nv = _dot(u["nl"], _pair_diag(u["v"]))
        u["w"], u["u0"], u["lrkv"] = u["a_bar"], nv[:CHUNK], nv[CHUNK:]
        u["npow"] = u["n_ab"].astype(BF16)

    steps = int(math.log2(CHUNK))
    for kstep in range(steps):
        for u in units:
            rhs = jnp.concatenate([_pair_diag(u["w"].astype(BF16)), _pair_diag(u["u0"].astype(BF16))], axis=1)
            upd = _dot(u["npow"], rhs)
            u["w"] = u["w"] + upd[:, :LANES]
            u["u0"] = u["u0"] + upd[:, LANES:]
        if kstep + 1 < steps:
            for u in units:
                u["npow"] = _dot(u["npow"], _pair_diag(u["npow"])).astype(BF16)

    for u in units:
        wb, ub = u["w"].astype(BF16), u["u0"].astype(BF16)
        lx = _dot(u["l_rb"].astype(BF16), jnp.concatenate([_pair_diag(wb), _pair_diag(ub)], axis=1))
        u["p_mat"] = u["r_bar"] + lx[:, :LANES]
        u["y0"] = u["lrkv"] + lx[:, LANES:]
        lhs = jnp.concatenate([u["b_hat"], u["k_hat"]], axis=0)
        rhs = jnp.concatenate([jnp.concatenate([wb, ub], axis=1),
                               jnp.concatenate([jnp.zeros_like(wb), u["v"]], axis=1)], axis=0)
        mg = _dot(lhs, rhs, TN)
        u["m_full"] = _pair_pick(mg[:, :LANES]) + jnp.where(eye, u["gam_c"], 0.0)
        u["g_mat"] = _pair_pick(mg[:, LANES:])

    for u in units:
        bi, d, p = u["bi"], u["d"], u["p"]
        h0 = h_ref[bi, d, p]
        a0, a1 = _split(jnp.concatenate([u["p_mat"], u["m_full"]], axis=0), 2)
        h_hi, h_lo = _split(h0, 2)
        bh, bl = _pair_diag(h_hi), _pair_diag(h_lo)
        out = _dot(a0, bh) + (_dot(a0, bl) + _dot(a1, bh))
        (yb_ref if d else yf_ref)[bi, :, u["sl"]] = out[:CHUNK] + u["y0"]
        h_ref[bi, d, p] = out[CHUNK:] + u["g_mat"]


def _rwkv_pairs(r, v, kk, ld, kd, beta, ctx_len):
    bn, s, _ = r.shape
    n_chunks = s // CHUNK
    ctx_chunks = ctx_len // CHUNK
    rows = RWKV_ROWS if bn % RWKV_ROWS == 0 else 1

    def back(c):
        return jnp.where(c < ctx_chunks, ctx_chunks - 1 - c, n_chunks - 1 + ctx_chunks - c)

    fwd = pl.BlockSpec((rows, CHUNK, B_WIDTH), lambda b, c: (b, c, 0))
    bwd = pl.BlockSpec((rows, CHUNK, B_WIDTH), lambda b, c: (b, back(c), 0))
    fwd_d = pl.BlockSpec((None, rows, CHUNK, B_WIDTH), lambda b, c: (0, b, c, 0))
    bwd_d = pl.BlockSpec((None, rows, CHUNK, B_WIDTH), lambda b, c: (1, b, back(c), 0))
    y = jax.ShapeDtypeStruct((bn, s, B_WIDTH), F32)
    return pl.pallas_call(
        _rwkv_pair_kernel,
        grid=(bn // rows, n_chunks),
        in_specs=[fwd, fwd, fwd, bwd, bwd, bwd, fwd_d, fwd_d, fwd_d, bwd_d, bwd_d, bwd_d],
        out_specs=[fwd, bwd],
        out_shape=[y, y],
        scratch_shapes=[pltpu.VMEM((rows, 2, B_WIDTH // LANES, B_HEAD, LANES), F32)],
        compiler_params=_params(("parallel", "arbitrary")),
        name="rwkv_scan",
    )(r, v, kk, r, v, kk, ld, kd, beta, ld, kd, beta)


def _outproj_kernel(x_ref, mod_ref, a_ref, yf_ref, yb_ref, bonus_ref, g_ref, c_ref,
                    wa_ref, wb_ref, wc_ref, lnxg_ref, lnxb_ref, ln1g_ref, ln1b_ref, o_ref, *, alpha):
    nb = x_ref.shape[0]
    stack = lambda ref: jnp.concatenate([ref[r] for r in range(nb)], axis=0)
    y = stack(yf_ref) + stack(yb_ref) + stack(bonus_ref)
    ones = _group_ones(B_WIDTH, B_HEAD)
    inv = 1.0 / B_HEAD
    mu = _dot_exact_rhs(y, ones) * inv
    yc = y - mu
    var = _dot_exact_rhs(yc * yc, ones) * inv
    yn = yc * lax.rsqrt(var + GN_EPS) * lnxg_ref[...] + lnxb_ref[...]
    bmix = (yn * stack(g_ref)).astype(BF16)
    o = _dot(stack(a_ref), wa_ref[...]) + _dot(bmix, wb_ref[...]) + _dot(stack(c_ref), wc_ref[...])
    for r in range(nb):
        o_ref[r] = (_ln(alpha * x_ref[r] + mod_ref[r, 2:3] * o[r * TM:(r + 1) * TM], LN_EPS) * ln1g_ref[...]
                    + ln1b_ref[...])


def _outproj(xs, mods, a_out, yf, yb, bonus, g, c_out, wa, wb, wc, lnxg, lnxb, ln1g, ln1b, ctx_len, first_tile,
             alpha):
    bn, s, d = xs.shape
    nb = _batch_rows(bn)
    row = lambda width: pl.BlockSpec((nb, TM, width), lambda b, i: (b, i + first_tile, 0))
    full = lambda a: pl.BlockSpec(a.shape, lambda b, i: (0,) * a.ndim)
    consts = (wa, wb, wc, lnxg, lnxb, ln1g, ln1b)
    return pl.pallas_call(
        functools.partial(_outproj_kernel, alpha=alpha),
        grid=(bn // nb, s // TM - first_tile),
        in_specs=[row(d), _mod_spec(ctx_len, d, nb, first_tile),
                  row(A_WIDTH), row(B_WIDTH), row(B_WIDTH), row(B_WIDTH), row(B_WIDTH), row(C_WIDTH)]
                 + [full(p) for p in consts],
        out_specs=row(d),
        out_shape=jax.ShapeDtypeStruct((bn, s, d), F32),
        compiler_params=_params(("parallel", "parallel")),
        name="outproj",
    )(xs, mods, a_out, yf, yb, bonus, g, c_out, *consts)


def _swiglu_rows(h, w1_ref, w3_ref, w2_ref):
    acc = jnp.zeros((h.shape[0], w2_ref.shape[-1]), F32)
    for j in range(w1_ref.shape[-1] // FF_CHUNK):
        sl = slice(j * FF_CHUNK, (j + 1) * FF_CHUNK)
        u = _dot(h, w1_ref[:, sl])
        t = _dot(h, w3_ref[:, sl])
        acc = acc + _dot((_silu(u) * t).astype(BF16), w2_ref[sl, :])
    return acc


def _ffn_kernel(x_ref, mod_ref, w1_ref, w3_ref, w2_ref, g_ref, b_ref, o_ref, *, alpha):
    nb = x_ref.shape[0]
    h = jnp.concatenate([(_ln(x_ref[r], LN_EPS) * (1.0 + mod_ref[r, 4:5]) + mod_ref[r, 3:4]).astype(BF16)
                         for r in range(nb)], axis=0)
    f = _swiglu_rows(h, w1_ref, w3_ref, w2_ref)
    for r in range(nb):
        o_ref[r] = (_ln(alpha * x_ref[r] + mod_ref[r, 5:6] * f[r * TM:(r + 1) * TM], LN_EPS) * g_ref[...]
                    + b_ref[...])


def _ffn(x1, mods, w1, w3, w2, g, b, ctx_len, alpha):
    bn, s, d = x1.shape
    nb = _batch_rows(bn)
    row = pl.BlockSpec((nb, TM, d), lambda bb, i: (bb, i, 0))
    resident = lambda a: pl.BlockSpec(a.shape, lambda bb, i: (0,) * a.ndim, pipeline_mode=pl.Buffered(1))
    full = lambda a: pl.BlockSpec(a.shape, lambda bb, i: (0,) * a.ndim)
    return pl.pallas_call(
        functools.partial(_ffn_kernel, alpha=alpha),
        grid=(bn // nb, s // TM),
        in_specs=[row, _mod_spec(ctx_len, d, nb),
                  resident(w1), resident(w3), resident(w2), full(g), full(b)],
        out_specs=row,
        out_shape=jax.ShapeDtypeStruct((bn, s, d), F32),
        compiler_params=_params(("parallel", "parallel")),
        name="ffn",
    )(x1, mods, w1, w3, w2, g, b)


def _moe_pre_kernel(x_ref, mod_ref, router_ref, h_ref, logit_ref):
    mod = mod_ref[...]
    h = _ln(x_ref[...], LN_EPS) * (1.0 + mod[4:5]) + mod[3:4]
    h_ref[...] = h.astype(h_ref.dtype)
    logit_ref[...] = _dot6(h, router_ref[...])


def _moe_pre(x1, mods, router_p, ctx_len):
    bn, s, d = x1.shape
    ct = ctx_len // TM
    lt = (s - ctx_len) // TM
    return pl.pallas_call(
        _moe_pre_kernel,
        grid=(bn, lt),
        in_specs=[pl.BlockSpec((None, TM, d), lambda b, i: (b, i + ct, 0)),
                  pl.BlockSpec((None, None, 6, d), lambda b, i: (b, 1, 0, 0)),
                  pl.BlockSpec(router_p.shape, lambda b, i: (0, 0))],
        out_specs=[pl.BlockSpec((TM, d), lambda b, i: (b * lt + i, 0)),
                   pl.BlockSpec((TM, LANES), lambda b, i: (b * lt + i, 0))],
        out_shape=[jax.ShapeDtypeStruct((bn * lt * TM, d), BF16),
                   jax.ShapeDtypeStruct((bn * lt * TM, LANES), F32)],
        compiler_params=_params(("parallel", "parallel")),
        name="moe_pre",
    )(x1, mods, router_p)


def _slot_onehot(slots_ref, block):
    sl = slots_ref[...]
    s_iota = lax.broadcasted_iota(jnp.int32, (MOE_ROWS, MOE_TILE), 0) + block * MOE_ROWS
    hit = jnp.logical_or(sl[0:1] == s_iota, sl[1:2] == s_iota)
    return jnp.where(hit, 1.0, 0.0).astype(BF16)


def _dispatch_kernel(wb_ref, wc_ref, wf_ref, wv_ref, slots_ref, h_ref, o_ref):
    w = pl.program_id(0)
    part = lambda: _dot(_slot_onehot(slots_ref, wb_ref[w]), h_ref[...])

    @pl.when(wf_ref[w] == 1)
    def _():
        o_ref[...] = part().astype(o_ref.dtype)

    @pl.when(jnp.logical_and(wf_ref[w] == 0, wv_ref[w] == 1))
    def _():
        o_ref[...] = (o_ref[...].astype(F32) + part()).astype(o_ref.dtype)


def _dispatch(h, slots, work):
    t, d = h.shape
    wb, wc, wf, wv = work
    n_slots = (t * TOP_K // MOE_ROWS + N_EXPERTS) * MOE_ROWS
    grid_spec = pltpu.PrefetchScalarGridSpec(
        num_scalar_prefetch=4,
        grid=(wb.shape[0],),
        in_specs=[pl.BlockSpec((TOP_K, MOE_TILE), lambda w, b, c, f, v: (0, c[w])),
                  pl.BlockSpec((MOE_TILE, d), lambda w, b, c, f, v: (c[w], 0))],
        out_specs=pl.BlockSpec((MOE_ROWS, d), lambda w, b, c, f, v: (b[w], 0)),
    )
    return pl.pallas_call(
        _dispatch_kernel,
        grid_spec=grid_spec,
        out_shape=jax.ShapeDtypeStruct((n_slots, d), BF16),
        compiler_params=_params(("arbitrary",)),
        name="moe_dispatch",
    )(wb, wc, wf, wv, slots, h)


def _collect_kernel(wb_ref, wc_ref, wf_ref, wv_ref, slots_ref, scol_ref, gate_ref, y_ref, o_ref):
    w = pl.program_id(0)

    def part():
        base = wb_ref[w] * MOE_ROWS
        scol = scol_ref[...] - base
        gates = jnp.where(jnp.logical_and(scol >= 0, scol < MOE_ROWS), gate_ref[...], 0.0)
        gate = jnp.sum(gates, axis=1, keepdims=True)
        return _dot(_slot_onehot(slots_ref, wb_ref[w]), y_ref[...], TN) * gate

    @pl.when(wf_ref[w] == 1)
    def _():
        o_ref[...] = part()

    @pl.when(jnp.logical_and(wf_ref[w] == 0, wv_ref[w] == 1))
    def _():
        o_ref[...] = o_ref[...] + part()


def _collect(y_buf, slots, slots_col, gates, work):
    n_slots, d = y_buf.shape
    t = slots.shape[1]
    wb, wc, wf, wv = work
    tile = lambda w, b, c, f, v: (c[w], 0)
    grid_spec = pltpu.PrefetchScalarGridSpec(
        num_scalar_prefetch=4,
        grid=(wb.shape[0],),
        in_specs=[pl.BlockSpec((TOP_K, MOE_TILE), lambda w, b, c, f, v: (0, c[w])),
                  pl.BlockSpec((MOE_TILE, TOP_K), tile), pl.BlockSpec((MOE_TILE, TOP_K), tile),
                  pl.BlockSpec((MOE_ROWS, d), lambda w, b, c, f, v: (b[w], 0))],
        out_specs=pl.BlockSpec((MOE_TILE, d), tile),
    )
    return pl.pallas_call(
        _collect_kernel,
        grid_spec=grid_spec,
        out_shape=jax.ShapeDtypeStruct((t, d), F32),
        compiler_params=_params(("arbitrary",)),
        name="moe_collect",
    )(wb, wc, wf, wv, slots, slots_col, gates, y_buf)


def _expert_kernel(be_ref, nb_ref, x_ref, w1_ref, w3_ref, w2_ref, o_ref):
    i = pl.program_id(0)

    @pl.when(i < nb_ref[0])
    def _():
        o_ref[...] = _swiglu_rows(x_ref[...], w1_ref, w3_ref, w2_ref).astype(o_ref.dtype)

    @pl.when(i >= nb_ref[0])
    def _():
        o_ref[...] = jnp.zeros_like(o_ref)


def _experts(buf, block_e, n_used, w1, w3, w2):
    n, d = buf.shape
    n_blocks = n // MOE_ROWS
    ff = w1.shape[-1]
    wspec = lambda shape: pl.BlockSpec((None,) + shape, lambda i, be, nb: (be[i], 0, 0),
                                       pipeline_mode=pl.Buffered(1))
    used = lambda i, be, nb: (jnp.minimum(i, nb[0] - 1), 0)
    grid_spec = pltpu.PrefetchScalarGridSpec(
        num_scalar_prefetch=2,
        grid=(n_blocks,),
        in_specs=[pl.BlockSpec((MOE_ROWS, d), used), wspec((d, ff)), wspec((d, ff)), wspec((ff, d))],
        out_specs=pl.BlockSpec((MOE_ROWS, d), lambda i, be, nb: (i, 0)),
    )
    return pl.pallas_call(
        _expert_kernel,
        grid_spec=grid_spec,
        out_shape=jax.ShapeDtypeStruct((n, d), BF16),
        compiler_params=_params(("arbitrary",)),
        name="experts",
    )(block_e, n_used, buf, w1, w3, w2)


def _combine_kernel(x_ref, mod_ref, y_ref, g_ref, b_ref, o_ref, *, alpha):
    mod = mod_ref[...]
    o_ref[...] = _ln(alpha * x_ref[...] + mod[5:6] * y_ref[...], LN_EPS) * g_ref[...] + b_ref[...]


def _combine(x1, mods, y_tok, g, b, ctx_len, alpha):
    bn, s, d = x1.shape
    ct = ctx_len // TM
    lt = (s - ctx_len) // TM
    full = lambda a: pl.BlockSpec(a.shape, lambda bb, i: (0,) * a.ndim)
    return pl.pallas_call(
        functools.partial(_combine_kernel, alpha=alpha),
        grid=(bn, lt),
        in_specs=[pl.BlockSpec((None, TM, d), lambda bb, i: (bb, i + ct, 0)),
                  pl.BlockSpec((None, None, 6, d), lambda bb, i: (bb, 1, 0, 0)),
                  pl.BlockSpec((TM, d), lambda bb, i: (bb * lt + i, 0)),
                  full(g), full(b)],
        out_specs=pl.BlockSpec((None, TM, d), lambda bb, i: (bb, i, 0)),
        out_shape=jax.ShapeDtypeStruct((bn, lt * TM, d), F32),
        compiler_params=_params(("parallel", "parallel")),
        name="moe_combine",
    )(x1, mods, y_tok, g, b)


def _work_lists(rank_at_tile, counts, pstart, block_e, n_used, n_blocks, n_tiles):
    n_work = n_tiles * N_EXPERTS + n_blocks
    blocks = jnp.arange(n_blocks, dtype=jnp.int32)
    used = blocks < n_used
    r0 = blocks * MOE_ROWS - pstart[block_e]
    r_last = jnp.minimum(r0 + MOE_ROWS, counts[block_e]) - 1
    cols = rank_at_tile.T[block_e]
    find = jax.vmap(lambda col, val: jnp.searchsorted(col, val, side="right"))
    lo = jnp.clip(find(cols, r0) - 1, 0, n_tiles - 1)
    hi = jnp.clip(find(cols, r_last) - 1, 0, n_tiles - 1)
    n_b = jnp.where(used, hi - lo + 1, 0)
    ends = jnp.cumsum(n_b)
    starts = ends - n_b
    total = ends[-1]
    w = jnp.arange(n_work, dtype=jnp.int32)
    valid = w < total
    wl = jnp.minimum(w, total - 1)
    blk = jnp.minimum(jnp.searchsorted(ends, wl, side="right"), n_blocks - 1).astype(jnp.int32)
    tile = (lo[blk] + (wl - starts[blk])).astype(jnp.int32)
    first = jnp.logical_and(valid, w == starts[blk])
    as_i32 = lambda a: a.astype(jnp.int32)
    by_block = (blk, tile, as_i32(first), as_i32(valid))
    order = jnp.argsort(jnp.where(valid, tile, n_tiles), stable=True)
    order = order[jnp.minimum(w, total - 1)]
    tile2, blk2 = tile[order], blk[order]
    first2 = jnp.logical_and(valid, jnp.concatenate([jnp.ones((1,), bool), tile2[1:] != tile2[:-1]]))
    by_tile = (blk2, tile2, as_i32(first2), as_i32(valid))
    return by_block, by_tile


def _moe_layer(x1, mods, router, w1, w3, w2, g, b, ctx_len, alpha):
    d = x1.shape[-1]
    router_p = jnp.pad(router, ((0, 0), (0, LANES - N_EXPERTS)))
    h, logits = _moe_pre(x1, mods, router_p, ctx_len)
    t = h.shape[0]
    top_v, top_i = lax.top_k(logits[:, :N_EXPERTS], TOP_K)
    gates = jax.nn.softmax(top_v, axis=-1)
    e_flat = top_i.reshape(-1)
    onehot = (e_flat[:, None] == jnp.arange(N_EXPERTS)[None, :]).astype(jnp.int32)
    ranks = jnp.cumsum(onehot, axis=0) - onehot
    rank = jnp.sum(ranks * onehot, axis=1)
    counts = jnp.sum(onehot, axis=0)
    padded = (counts + MOE_ROWS - 1) // MOE_ROWS * MOE_ROWS
    pend = jnp.cumsum(padded)
    pstart = pend - padded
    slot = (pstart[e_flat] + rank).astype(jnp.int32)
    n_blocks = t * TOP_K // MOE_ROWS + N_EXPERTS
    assert t % MOE_TILE == 0
    n_tiles = t // MOE_TILE
    block_e = jnp.minimum(jnp.searchsorted(pend, jnp.arange(n_blocks) * MOE_ROWS, side="right"),
                          N_EXPERTS - 1).astype(jnp.int32)
    n_used = (pend[-1:] // MOE_ROWS).astype(jnp.int32)
    rank_at_tile = jnp.concatenate([ranks[::MOE_TILE * TOP_K], counts[None, :]], axis=0)
    by_block, by_tile = _work_lists(rank_at_tile, counts, pstart, block_e, n_used, n_blocks, n_tiles)
    slots_col = slot.reshape(t, TOP_K)
    slots = slots_col.T
    buf = _dispatch(h, slots, by_block)
    y_buf = _experts(buf, block_e, n_used, w1, w3, w2)
    y_tok = _collect(y_buf, slots, slots_col, gates, by_tile)
    return _combine(x1, mods, y_tok, g, b, ctx_len, alpha)


def _rope_tables(n_rows, ctx_len, dim, lane_lo):
    quarter = dim // 4
    inv = ROPE_THETA ** (-jnp.arange(quarter, dtype=F32) / quarter)
    rows = jnp.repeat(jnp.arange(n_rows, dtype=F32), GRID_W)
    cols = jnp.tile(jnp.arange(GRID_W, dtype=F32), n_rows)
    ang = jnp.concatenate([rows[:, None] * inv, rows[:, None] * inv,
                           cols[:, None] * inv, cols[:, None] * inv], axis=-1)
    sign = jnp.tile(jnp.concatenate([-jnp.ones(quarter, F32), jnp.ones(quarter, F32)]), 2)
    cos, sin = jnp.cos(ang), jnp.sin(ang) * sign
    length = cos.shape[0]
    if lane_lo == 0:
        reps = LANES // dim
        cos, sin = jnp.tile(cos, (1, reps)), jnp.tile(sin, (1, reps))
    else:
        pad = ((0, 0), (lane_lo, LANES - lane_lo - dim))
        cos = jnp.pad(cos, pad, constant_values=1.0)
        sin = jnp.pad(sin, pad)
    cos = jnp.concatenate([jnp.ones((ctx_len, LANES), F32), cos], axis=0)
    sin = jnp.concatenate([jnp.zeros((ctx_len, LANES), F32), sin], axis=0)
    return cos, sin


def _block_diag2(w):
    z = jnp.zeros_like(w[0])
    return jnp.concatenate([jnp.concatenate([w[0], z], axis=1), jnp.concatenate([z, w[1]], axis=1)], axis=0)


def _lambda_init(layer):
    return 0.8 - 0.6 * math.exp(-0.3 * layer)


def kernel(x, c, ctx, c_ctx, ada_w, ada_b, w_in, w_out, ln1_g, ln1_b, ln2_g, ln2_b,
           lam_q1, lam_k1, lam_q2, lam_k2, diff_norm_g, shift_mu, w0, w2, a0, a2, g2,
           k_k, k_a, r_k, lnx_g, lnx_b, q_norm_g, w_uq, kv_norm_g, w_ukv,
           ff_w1, ff_w3, ff_w2, router, moe_w1, moe_w3, moe_w2):
    bn, seq, d = x.shape
    ctx_len = ctx.shape[1]
    depth = ada_w.shape[0]
    assert d == D_MODEL and seq % TM == 0 and ctx_len % TM == 0 and seq % GRID_W == 0
    alpha = (2.0 * depth) ** 0.25
    n_grid_rows = seq // GRID_W
    cos_a, sin_a = _rope_tables(n_grid_rows, ctx_len, A_QK_DIM, 0)
    cos_c, sin_c = _rope_tables(n_grid_rows, ctx_len, C_ROPE, C_NOPE)

    cond_rows = 8 * ((bn + 1 + 7) // 8)
    cond = jnp.zeros((cond_rows, d), F32).at[:bn].set(c).at[bn].set(c_ctx)
    xs = jnp.concatenate([ctx, x], axis=1)

    for i in range(depth):
        with_ctx = i < depth - 1
        m = _ada(cond, ada_w[i], ada_b[i]).reshape(cond_rows, 6, d)
        mods = jnp.stack([jnp.broadcast_to(m[bn], (bn, 6, d)), m[:bn]], axis=1)

        wc = w_in[i][:, N_A + N_B:]
        kpe_w = jnp.pad(wc[:, C_Q_RANK + C_KV_RANK:], ((0, 0), (C_NOPE, LANES - C_NOPE - C_ROPE)))
        w_p = jnp.concatenate([w_in[i][:, :N_A + N_B], wc[:, :C_Q_RANK + C_KV_RANK], kpe_w], axis=1).astype(BF16)
        wq = w_uq[i].reshape(C_Q_RANK, C_HEADS, C_NOPE + C_ROPE)
        wq_p = jnp.pad(wq, ((0, 0), (0, 0), (0, LANES - C_NOPE - C_ROPE))).reshape(C_Q_RANK, -1).astype(BF16)
        wkv = w_ukv[i].reshape(C_KV_RANK, C_HEADS, C_NOPE + C_V)
        wk_p = jnp.pad(wkv[:, :, :C_NOPE], ((0, 0), (0, 0), (0, LANES - C_NOPE))).reshape(C_KV_RANK, -1).astype(BF16)
        wv_p = wkv[:, :, C_NOPE:].reshape(C_KV_RANK, -1).astype(BF16)
        mla_consts = (q_norm_g[i].reshape(1, -1), kv_norm_g[i].reshape(1, -1), wq_p, wk_p, wv_p)
        qa, ka, va, pb, qc, kc, vc = _inproj(xs, mods, w_p, cos_a, sin_a, mla_consts, cos_c, sin_c, ctx_len)

        lam_p = jnp.stack([lam_q1[i], lam_k1[i], lam_q2[i], lam_k2[i]])
        g_col = jnp.broadcast_to(diff_norm_g[i][:, None], (A_V_DIM, TM))
        first_tile = 0 if with_ctx else ctx_len // TM
        a_out, c_out = _attention([("diff", qa, ka, va), ("mla", qc, kc, vc)], lam_p, g_col, ctx_len=ctx_len,
                                  first_tile=first_tile, lam_init=_lambda_init(i))
        r, v, kkn, ld, kd, beta, gate, bonus = _rwkv_prep(
            pb, shift_mu[i].reshape(1, -1), w0[i].reshape(1, -1), _block_diag2(w2[i]).astype(BF16),
            a0[i].reshape(1, -1), _block_diag2(a2[i]).astype(BF16), g2[i].astype(BF16),
            k_k[i].reshape(1, -1), k_a[i].reshape(1, -1), r_k[i].reshape(1, -1), ctx_len)
        yf, yb = _rwkv_pairs(r, v, kkn, ld, kd, beta, ctx_len)

        wo = w_out[i].astype(BF16)
        x1 = _outproj(xs, mods, a_out, yf, yb, bonus, gate, c_out,
                      wo[:A_WIDTH], wo[A_WIDTH:A_WIDTH + B_WIDTH], wo[A_WIDTH + B_WIDTH:],
                      lnx_g[i].reshape(1, -1), lnx_b[i].reshape(1, -1),
                      ln1_g[i].reshape(1, -1), ln1_b[i].reshape(1, -1), ctx_len, first_tile, alpha)

        j = i // 2
        g2n, b2n = ln2_g[i].reshape(1, -1), ln2_b[i].reshape(1, -1)
        if i % 2 == 0:
            xs = _ffn(x1, mods, ff_w1[j].astype(BF16), ff_w3[j].astype(BF16), ff_w2[j].astype(BF16),
                      g2n, b2n, ctx_len, alpha)
        else:
            if with_ctx:
                raise NotImplementedError("routed FFN on the context rows is not needed at this depth")
            return _moe_layer(x1, mods, router[j], moe_w1[j].astype(BF16), moe_w3[j].astype(BF16),
                              moe_w2[j].astype(BF16), g2n, b2n, ctx_len, alpha)
    return xs[:, ctx_len:]
```

```python
import functools
import math

import jax
import jax.numpy as jnp
from jax import lax
from jax.experimental import pallas as pl
from jax.experimental.pallas import tpu as pltpu

F32 = jnp.float32
BF16 = jnp.bfloat16

D_MODEL = 1024
GRID_W = 64
ROPE_THETA = 10000.0
A_HEADS, A_QK_DIM, A_V_DIM = 4, 64, 128
A_WIDTH = A_HEADS * A_V_DIM
A_QK_COLS = 2 * A_HEADS * A_QK_DIM
B_HEADS, B_HEAD = 4, 64
B_WIDTH = B_HEADS * B_HEAD
B_DECAY_RANK, B_A_RANK, B_GATE_RANK = 64, 64, 128
C_HEADS, C_NOPE, C_ROPE, C_V = 4, 64, 32, 64
C_WIDTH = C_HEADS * C_V
C_Q_RANK, C_KV_RANK = 256, 128
N_A = 2 * A_QK_COLS + A_WIDTH
N_B = 3 * B_WIDTH + 2 * B_DECAY_RANK + 2 * B_A_RANK + B_GATE_RANK
N_C = C_Q_RANK + C_KV_RANK + C_ROPE
N_C_PAD = C_Q_RANK + C_KV_RANK + 128
D_FF = 3584
N_EXPERTS = 8
TOP_K = 2
LN_EPS = 1e-6
RMS_EPS = 1e-6
GN_EPS = 64e-5

LOG2E = math.log2(math.e)
LANES = 128
TM = 256
CHUNK = 64
ROWS_B = 2
RWKV_ROWS = 4
ATT_TK = 2048
ONES_ROWS = 16
MOE_ROWS = 512
MOE_TILE = 512
FF_CHUNK = 512
VMEM_LIMIT = 56 * 1024 * 1024

NN = (((1,), (0,)), ((), ()))
NT = (((1,), (1,)), ((), ()))
TN = (((0,), (0,)), ((), ()))


def _params(sem, vmem=VMEM_LIMIT, flags=None):
    return pltpu.CompilerParams(dimension_semantics=sem, vmem_limit_bytes=vmem, flags=flags)


def _split(x, n):
    parts, r = [], x
    for _ in range(n):
        p = r.astype(BF16)
        parts.append(p)
        r = r - p.astype(F32)
    return parts


def _dot(a, b, dn=NN):
    return lax.dot_general(a, b, dn, preferred_element_type=F32)


def _dot1(a, b, dn=NN):
    return _dot(a.astype(BF16), b.astype(BF16), dn)


def _dot3(a, b, dn=NN):
    a0, a1 = _split(a, 2)
    b0, b1 = _split(b, 2)
    return _dot(a0, b0, dn) + (_dot(a0, b1, dn) + _dot(a1, b0, dn))


def _dot6(a, b, dn=NN):
    a0, a1, a2 = _split(a, 3)
    b0, b1, b2 = _split(b, 3)
    lo = _dot(a1, b1, dn) + (_dot(a0, b2, dn) + _dot(a2, b0, dn))
    return _dot(a0, b0, dn) + ((_dot(a0, b1, dn) + _dot(a1, b0, dn)) + lo)


def _dot_exact_lhs(a_bf16, b, dn=NN, n=3):
    parts = _split(b, n)
    out = _dot(a_bf16, parts[-1], dn)
    for p in parts[-2::-1]:
        out = out + _dot(a_bf16, p, dn)
    return out


def _dot_exact_rhs(a, b_bf16, dn=NN, n=3):
    parts = _split(a, n)
    out = _dot(parts[-1], b_bf16, dn)
    for p in parts[-2::-1]:
        out = out + _dot(p, b_bf16, dn)
    return out


def _ln(x, eps):
    mu = jnp.mean(x, axis=-1, keepdims=True)
    xc = x - mu
    return xc * lax.rsqrt(jnp.mean(xc * xc, axis=-1, keepdims=True) + eps)


def _sigmoid(x):
    return 1.0 / (1.0 + jnp.exp(-x))


def _silu(x):
    return x * _sigmoid(x)


def _group_ones(width, group):
    r = lax.broadcasted_iota(jnp.int32, (width, width), 0) // group
    c = lax.broadcasted_iota(jnp.int32, (width, width), 1) // group
    return (r == c).astype(BF16)


def _partner(x, half):
    lane = lax.broadcasted_iota(jnp.int32, x.shape, 1)
    up = pltpu.roll(x, LANES - half, 1)
    dn = pltpu.roll(x, half, 1)
    return jnp.where((lane % (2 * half)) < half, up, dn)


def _rope(x, cos, sin, half):
    return x * cos + _partner(x, half) * sin


def _ada_kernel(c_ref, w_ref, b_ref, o_ref):
    o_ref[...] = _dot3(_silu(c_ref[...]), w_ref[...]) + b_ref[...]


def _ada(cond, w, b):
    rows, d = cond.shape
    n = w.shape[1]
    tn = 1536
    return pl.pallas_call(
        _ada_kernel,
        grid=(n // tn,),
        in_specs=[pl.BlockSpec((rows, d), lambda j: (0, 0)),
                  pl.BlockSpec((d, tn), lambda j: (0, j)),
                  pl.BlockSpec((1, tn), lambda j: (0, j))],
        out_specs=pl.BlockSpec((rows, tn), lambda j: (0, j)),
        out_shape=jax.ShapeDtypeStruct((rows, n), F32),
        compiler_params=_params(("parallel",)),
        name="ada",
    )(cond, w, b.reshape(1, n))


def _inproj_kernel(x_ref, mod_ref, w_ref, cos_ref, sin_ref,
                   qg_ref, kvg_ref, wq_ref, wk_ref, wv_ref, cosc_ref, sinc_ref,
                   q_ref, k_ref, v_ref, pb_ref, cq_ref, ck_ref, cv_ref):
    nb = x_ref.shape[0]
    h = jnp.concatenate([(_ln(x_ref[r], LN_EPS) * (1.0 + mod_ref[r, 1:2]) + mod_ref[r, 0:1]).astype(BF16)
                         for r in range(nb)], axis=0)
    cos, sin = cos_ref[...], sin_ref[...]
    scale = A_QK_DIM ** -0.5 * LOG2E
    rows = lambda a, r: a[r * TM:(r + 1) * TM]
    for j in range(A_QK_COLS // LANES):
        sl = slice(j * LANES, (j + 1) * LANES)
        qj = _dot(h, w_ref[:, sl])
        kj = _dot(h, w_ref[:, A_QK_COLS + j * LANES:A_QK_COLS + (j + 1) * LANES])
        vj = _dot(h, w_ref[:, 2 * A_QK_COLS + j * LANES:2 * A_QK_COLS + (j + 1) * LANES])
        for r in range(nb):
            q_ref[r, sl, :] = (_rope(rows(qj, r), cos, sin, A_QK_DIM // 4) * scale).T.astype(BF16)
            k_ref[r, :, sl] = _rope(rows(kj, r), cos, sin, A_QK_DIM // 4).astype(BF16)
            v_ref[r, sl, :] = rows(vj, r).T.astype(BF16)
    pb = _dot(h, w_ref[:, N_A:N_A + N_B])
    for r in range(nb):
        pb_ref[r] = rows(pb, r)
    _mla_heads(_dot(h, w_ref[:, N_A + N_B:]), nb, qg_ref, kvg_ref, wq_ref, wk_ref, wv_ref,
               cosc_ref[...], sinc_ref[...], cq_ref, ck_ref, cv_ref)


def _batch_rows(bn):
    return ROWS_B if bn % ROWS_B == 0 else 1


def _mod_spec(ctx_len, d, nb=None, first_tile=0):
    ct = ctx_len // TM
    return pl.BlockSpec((nb, None, 6, d), lambda b, i: (b, jnp.where(i + first_tile >= ct, 1, 0), 0, 0))


def _inproj(xs, mods, w_p, cos_a, sin_a, mla_consts, cos_c, sin_c, ctx_len):
    bn, s, d = xs.shape
    n_tiles = s // TM
    n_w = w_p.shape[1]
    nb = _batch_rows(bn)
    row = lambda width: pl.BlockSpec((nb, TM, width), lambda b, i: (b, i, 0))
    col = lambda width: pl.BlockSpec((nb, width, TM), lambda b, i: (b, 0, i))
    full = lambda a: pl.BlockSpec(a.shape, lambda b, i: (0,) * a.ndim)
    tab = pl.BlockSpec((TM, LANES), lambda b, i: (i, 0))
    hw = C_HEADS * LANES
    outs = [jax.ShapeDtypeStruct((bn, A_QK_COLS, s), BF16),
            jax.ShapeDtypeStruct((bn, s, A_QK_COLS), BF16),
            jax.ShapeDtypeStruct((bn, A_WIDTH, s), BF16),
            jax.ShapeDtypeStruct((bn, s, N_B), F32),
            jax.ShapeDtypeStruct((bn, hw, s), BF16),
            jax.ShapeDtypeStruct((bn, s, hw), BF16),
            jax.ShapeDtypeStruct((bn, C_WIDTH, s), BF16)]
    return pl.pallas_call(
        _inproj_kernel,
        grid=(bn // nb, n_tiles),
        in_specs=[row(d), _mod_spec(ctx_len, d, nb), full(w_p), tab, tab]
                 + [full(a) for a in mla_consts] + [tab, tab],
        out_specs=[col(A_QK_COLS), row(A_QK_COLS), col(A_WIDTH), row(N_B), col(hw), row(hw), col(C_WIDTH)],
        out_shape=outs,
        compiler_params=_params(("parallel", "parallel")),
        name="inproj",
    )(xs, mods, w_p, cos_a, sin_a, *mla_consts, cos_c, sin_c)


def _mla_heads(pc, nb, qg_ref, kvg_ref, wq_ref, wk_ref, wv_ref, cos, sin, q_ref, k_ref, v_ref):
    cq = pc[:, :C_Q_RANK]
    cq = cq * lax.rsqrt(jnp.mean(cq * cq, axis=-1, keepdims=True) + RMS_EPS) * qg_ref[...]
    ckv = pc[:, C_Q_RANK:C_Q_RANK + C_KV_RANK]
    ckv = ckv * lax.rsqrt(jnp.mean(ckv * ckv, axis=-1, keepdims=True) + RMS_EPS) * kvg_ref[...]
    cqb, ckvb = cq.astype(BF16), ckv.astype(BF16)
    scale = (C_NOPE + C_ROPE) ** -0.5 * LOG2E
    rows = lambda a, r: a[r * TM:(r + 1) * TM]
    kpe = [_rope(rows(pc, r)[:, C_Q_RANK + C_KV_RANK:], cos, sin, C_ROPE // 4) for r in range(nb)]
    for h in range(C_HEADS):
        sl = slice(h * LANES, (h + 1) * LANES)
        qh = _dot(cqb, wq_ref[:, sl])
        kh = _dot(ckvb, wk_ref[:, sl])
        for r in range(nb):
            q_ref[r, sl, :] = (_rope(rows(qh, r), cos, sin, C_ROPE // 4) * scale).T.astype(BF16)
            k_ref[r, :, sl] = (rows(kh, r) + kpe[r]).astype(BF16)
    for j in range(C_WIDTH // LANES):
        sl = slice(j * LANES, (j + 1) * LANES)
        vj = _dot(ckvb, wv_ref[:, sl])
        for r in range(nb):
            v_ref[r, sl, :] = rows(vj, r).T.astype(BF16)


def _attn_kernel(lam_ref, g_ref, *refs, modes, ctx_tiles, ctx_len, tk, first_tile, lam_init):
    n_sets = len(modes)
    in_refs = [refs[3 * t:3 * t + 3] for t in range(n_sets)]
    o_refs = refs[3 * n_sets:]
    qi = pl.program_id(1) + first_tile
    tq = in_refs[0][0].shape[1]
    half = lax.broadcasted_iota(jnp.int32, (LANES, 1), 0) < (LANES // 2)
    chains = []
    for t, (qt_ref, k_ref, vt_ref) in enumerate(in_refs):
        heads = qt_ref.shape[0] // LANES
        for g in range(heads):
            qt = qt_ref[g * LANES:(g + 1) * LANES, :]
            if modes[t] == "diff":
                qt = jnp.concatenate([jnp.where(half, qt, jnp.zeros_like(qt)),
                                      jnp.where(half, jnp.zeros_like(qt), qt)], axis=1)
            chains.append((t, g, qt, vt_ref.shape[0] // heads, qt.shape[1]))
    s_len = in_refs[0][1].shape[0]

    def fold(x, reduce):
        rows = x.shape[0]
        while rows > 8:
            g = max(d for d in range(2, 9) if (rows // 8) % d == 0)
            rows //= g
            x = reduce(x.reshape(g, rows, x.shape[1]), axis=0)
        return x

    def absorb(kcs, vcs, carry):
        n = len(chains)
        ss = [_dot(kcs[t][:, g * LANES:(g + 1) * LANES], q) for (t, g, q, _, _) in chains]
        m_new = [jnp.maximum(carry[c][0], jnp.max(fold(ss[c], jnp.max), axis=0, keepdims=True)) for c in range(n)]
        alpha = [jnp.exp2(carry[c][0] - m_new[c]) for c in range(n)]
        ps = [jnp.exp2(ss[c] - m_new[c]).astype(BF16) for c in range(n)]
        ones = jnp.ones((ONES_ROWS, kcs[0].shape[0]), BF16)
        acc = [alpha[c] * carry[c][1]
               + _dot(jnp.concatenate([vcs[t][g * dv:(g + 1) * dv], ones], axis=0), ps[c])
               for c, (t, g, _, dv, _) in enumerate(chains)]
        return tuple((m_new[c], acc[c]) for c in range(n))

    def finish(stats):
        outs = [a[:dv] / a[dv:dv + 1] for (_, a), (_, _, _, dv, _) in zip(stats, chains)]
        for t in range(n_sets):
            mine = [outs[c] for c, ch in enumerate(chains) if ch[0] == t]
            if modes[t] == "diff":
                lp = lam_ref[...]
                lam = (jnp.exp(jnp.sum(lp[0:1] * lp[1:2], axis=-1, keepdims=True))
                       - jnp.exp(jnp.sum(lp[2:3] * lp[3:4], axis=-1, keepdims=True)) + lam_init)
                for g, og in enumerate(mine):
                    o = og[:, :tq] - lam * og[:, tq:]
                    o = o * lax.rsqrt(jnp.mean(o * o, axis=0, keepdims=True) + RMS_EPS) * g_ref[...]
                    o = o * (1.0 - lam_init)
                    o_refs[t][:, g * LANES:(g + 1) * LANES] = o.T.astype(o_refs[t].dtype)
            else:
                per_slab = LANES // mine[0].shape[0]
                for j in range(len(mine) // per_slab):
                    o = jnp.concatenate(mine[j * per_slab:(j + 1) * per_slab], axis=0)
                    o_refs[t][:, j * LANES:(j + 1) * LANES] = o.T.astype(o_refs[t].dtype)

    def run(first_keys, n_more):
        init = tuple((jnp.full((1, w), -1e30, F32), jnp.zeros((dv + ONES_ROWS, w), F32))
                     for (_, _, _, dv, w) in chains)
        stats = absorb([r[1][0:first_keys, :] for r in in_refs], [r[2][:, 0:first_keys] for r in in_refs], init)

        def body(j, stats):
            off = pl.multiple_of(first_keys + j * tk, LANES)
            return absorb([r[1][pl.ds(off, tk), :] for r in in_refs],
                          [r[2][:, pl.ds(off, tk)] for r in in_refs], stats)

        finish(lax.fori_loop(0, n_more, body, stats))

    @pl.when(qi < ctx_tiles)
    def _():
        run(ctx_len, 0)

    @pl.when(qi >= ctx_tiles)
    def _():
        run(ctx_len + tk, (s_len - ctx_len) // tk - 1)


def _attention(sets, lam_p, g, *, ctx_len, first_tile, lam_init):
    bn, s, _ = sets[0][2].shape
    tk = math.gcd(s - ctx_len, ATT_TK)
    assert tk % LANES == 0 and ctx_len % LANES == 0
    kern = functools.partial(_attn_kernel, modes=tuple(m for m, _, _, _ in sets), ctx_tiles=ctx_len // TM,
                             ctx_len=ctx_len, tk=tk, first_tile=first_tile, lam_init=lam_init)
    in_specs = [pl.BlockSpec(lam_p.shape, lambda b, i: (0, 0)), pl.BlockSpec(g.shape, lambda b, i: (0, 0))]
    args, out_specs, out_shapes = [lam_p, g], [], []
    for _, qt, k, vt in sets:
        in_specs += [pl.BlockSpec((None, qt.shape[1], TM), lambda b, i: (b, 0, i + first_tile)),
                     pl.BlockSpec((None, s, k.shape[2]), lambda b, i: (b, 0, 0), pipeline_mode=pl.Buffered(1)),
                     pl.BlockSpec((None, vt.shape[1], s), lambda b, i: (b, 0, 0), pipeline_mode=pl.Buffered(1))]
        args += [qt, k, vt]
        out_specs.append(pl.BlockSpec((None, TM, vt.shape[1]), lambda b, i: (b, i + first_tile, 0)))
        out_shapes.append(jax.ShapeDtypeStruct((bn, s, vt.shape[1]), BF16))
    return pl.pallas_call(
        kern,
        grid=(bn, s // TM - first_tile),
        in_specs=in_specs,
        out_specs=out_specs,
        out_shape=out_shapes,
        compiler_params=_params(("parallel", "parallel")),
        name="attention",
    )(*args)


def _rwkv_prep_kernel(pb_ref, prev_ref, next_ref, mu_ref, w0_ref, w2_ref, a0_ref, a2_ref, g2_ref,
                      kk_ref, ka_ref, rk_ref,
                      r_out, v_out, kkn_out, ld_out, kd_out, beta_out, g_out, bonus_out,
                      *, ctx_tiles, n_tiles):
    i = pl.program_id(1)
    x = pb_ref[...]
    row = lax.broadcasted_iota(jnp.int32, (TM, 1), 0)
    has_prev = jnp.logical_and(i != 0, i != ctx_tiles)
    has_next = jnp.logical_and(i != ctx_tiles - 1, i != n_tiles - 1)
    prev_edge = jnp.where(has_prev, prev_ref[7:8, :], 0.0)
    next_edge = jnp.where(has_next, next_ref[0:1, :], 0.0)
    xp = jnp.where(row == 0, prev_edge, pltpu.roll(x, 1, 0))
    xn = jnp.where(row == TM - 1, next_edge, pltpu.roll(x, TM - 1, 0))
    z = x + mu_ref[...] * (0.5 * (xp + xn) - x)

    r = z[:, :B_WIDTH]
    k = z[:, B_WIDTH:2 * B_WIDTH]
    v = z[:, 2 * B_WIDTH:3 * B_WIDTH]
    o = 3 * B_WIDTH
    wd = z[:, o:o + 2 * B_DECAY_RANK]
    o += 2 * B_DECAY_RANK
    ad = z[:, o:o + 2 * B_A_RANK]
    o += 2 * B_A_RANK
    gd = z[:, o:]

    u = w0_ref[...] + _dot1(jnp.tanh(wd), w2_ref[...])
    ld = -math.exp(-0.5) * _sigmoid(u)
    lr = _sigmoid(a0_ref[...] + _dot1(ad, a2_ref[...]))
    g_out[...] = _dot1(_sigmoid(gd), g2_ref[...])

    ones = _group_ones(B_WIDTH, B_HEAD)
    kk = k * kk_ref[...]
    norm = jnp.sqrt(_dot_exact_rhs(kk * kk, ones))
    kkn = kk / jnp.maximum(norm, 1e-12)
    ka = ka_ref[...]
    kd_sum = jnp.zeros_like(k)
    for d in range(2):
        lr_d = lr[:, d * B_WIDTH:(d + 1) * B_WIDTH]
        kd = k * (1.0 + (lr_d - 1.0) * ka)
        kd_sum = kd_sum + kd
        ld_out[d] = ld[:, d * B_WIDTH:(d + 1) * B_WIDTH]
        kd_out[d] = kd
        beta_out[d] = kkn * lr_d
    bonus_out[...] = _dot_exact_rhs(r * kd_sum * rk_ref[...], ones) * v
    r_out[...] = r
    v_out[...] = v
    kkn_out[...] = kkn


def _rwkv_prep(pb, mu, w0, w2bd, a0, a2bd, g2, k_k, k_a, r_k, ctx_len):
    bn, s, _ = pb.shape
    n_tiles = s // TM
    eight = TM // 8
    row = lambda width: pl.BlockSpec((None, TM, width), lambda b, i: (b, i, 0))
    drow = pl.BlockSpec((2, None, TM, B_WIDTH), lambda b, i: (0, b, i, 0))
    full = lambda a: pl.BlockSpec(a.shape, lambda b, i: (0,) * a.ndim)
    one = jax.ShapeDtypeStruct((bn, s, B_WIDTH), F32)
    two = jax.ShapeDtypeStruct((2, bn, s, B_WIDTH), F32)
    kern = functools.partial(_rwkv_prep_kernel, ctx_tiles=ctx_len // TM, n_tiles=n_tiles)
    params = (mu, w0, w2bd, a0, a2bd, g2, k_k, k_a, r_k)
    return pl.pallas_call(
        kern,
        grid=(bn, n_tiles),
        in_specs=[row(N_B),
                  pl.BlockSpec((None, 8, N_B), lambda b, i: (b, jnp.maximum(i * eight - 1, 0), 0)),
                  pl.BlockSpec((None, 8, N_B), lambda b, i: (b, jnp.minimum((i + 1) * eight, s // 8 - 1), 0)),
                  ] + [full(p) for p in params],
        out_specs=[row(B_WIDTH), row(B_WIDTH), row(B_WIDTH), drow, drow, drow, row(B_WIDTH), row(B_WIDTH)],
        out_shape=[one, one, one, two, two, two, one, one],
        compiler_params=_params(("parallel", "parallel")),
        name="rwkv_prep",
    )(pb, pb, pb, *params)


def _pair_diag(x):
    lo = lax.broadcasted_iota(jnp.int32, (1, LANES), 1) < B_HEAD
    z = jnp.zeros_like(x)
    return jnp.concatenate([jnp.where(lo, x, z), jnp.where(lo, z, x)], axis=0)


def _pair_pick(x):
    lo = lax.broadcasted_iota(jnp.int32, (1, LANES), 1) < B_HEAD
    return jnp.where(lo, x[:B_HEAD], x[B_HEAD:])


def _rwkv_pair_kernel(rf_ref, vf_ref, kkf_ref, rb_ref, vb_ref, kkb_ref,
                      ldf_ref, kdf_ref, betaf_ref, ldb_ref, kdb_ref, betab_ref,
                      yf_ref, yb_ref, h_ref):
    c = pl.program_id(1)

    @pl.when(c == 0)
    def _():
        h_ref[...] = jnp.zeros_like(h_ref)

    n_pairs = B_WIDTH // LANES
    ti = lax.broadcasted_iota(jnp.int32, (CHUNK, LANES), 0)
    si = lax.broadcasted_iota(jnp.int32, (CHUNK, LANES), 1) % CHUNK
    t64 = lax.broadcasted_iota(jnp.int32, (CHUNK, CHUNK), 0)
    s64 = lax.broadcasted_iota(jnp.int32, (CHUNK, CHUNK), 1)
    eye = ti == si
    dirs = ((rf_ref, vf_ref, kkf_ref, ldf_ref, kdf_ref, betaf_ref, False),
            (rb_ref, vb_ref, kkb_ref, ldb_ref, kdb_ref, betab_ref, True))

    units = []
    n_rows = rf_ref.shape[0]
    for bi in range(n_rows):
      for d, (r_ref, v_ref, kk_ref, ld_ref, kd_ref, beta_ref, rev) in enumerate(dirs):
        strict = (si > ti) if rev else (ti > si)
        incl = (si >= ti) if rev else (ti >= si)
        tri = ((s64 >= t64) if rev else (t64 >= s64)).astype(BF16)
        ld_all = ld_ref[bi]
        cl_all = _dot_exact_lhs(tri, ld_all)
        for p in range(n_pairs):
            sl = slice(p * LANES, (p + 1) * LANES)
            ld, cl = ld_all[:, sl], cl_all[:, sl]
            total = jnp.sum(ld, axis=0, keepdims=True)
            inv_gam = jnp.exp(-cl)
            to_end = jnp.exp(total - cl)
            kk, kd, beta = kk_ref[bi, :, sl], kd_ref[bi, :, sl], beta_ref[bi, :, sl]
            units.append(dict(
                bi=bi, d=d, p=p, sl=sl, strict=strict, incl=incl, v=v_ref[bi, :, sl].astype(BF16),
                a_bar=-kk * jnp.exp(cl - ld), r_bar=r_ref[bi, :, sl] * jnp.exp(cl),
                b_til=(beta * inv_gam).astype(BF16), k_til=(kd * inv_gam).astype(BF16),
                b_hat=(beta * to_end).astype(BF16), k_hat=(kd * to_end).astype(BF16),
                gam_c=jnp.exp(total)))

    for u in units:
        x_mat = jnp.concatenate([u["a_bar"], u["r_bar"]], axis=0).astype(BF16)
        rhs = jnp.concatenate([_pair_diag(u["b_til"]), _pair_diag(u["k_til"])], axis=0)
        xbk = _dot(x_mat, rhs, NT)
        u["n_ab"] = jnp.where(u["strict"], xbk[:CHUNK, :LANES], 0.0)
        u["l_rb"] = jnp.where(u["incl"], xbk[CHUNK:, :LANES], 0.0)
        n_ak = jnp.where(u["strict"], xbk[:CHUNK, LANES:], 0.0)
        l_rk = jnp.where(u["incl"], xbk[CHUNK:, LANES:], 0.0)
        u["nl"] = jnp.concatenate([n_ak, l_rk], axis=0).astype(BF16)
    for u in units:
        nv = _dot(u["nl"], _pair_diag(u["v"]))
        u["w"], u["u0"], u["lrkv"] = u["a_bar"], nv[:CHUNK], nv[CHUNK:]
        u["npow"] = u["n_ab"].astype(BF16)

    steps = int(math.log2(CHUNK))
    for kstep in range(steps):
        for u in units:
            rhs = jnp.concatenate([_pair_diag(u["w"].astype(BF16)), _pair_diag(u["u0"].astype(BF16))], axis=1)
            upd = _dot(u["npow"], rhs)
            u["w"] = u["w"] + upd[:, :LANES]
            u["u0"] = u["u0"] + upd[:, LANES:]
        if kstep + 1 < steps:
            for u in units:
                u["npow"] = _dot(u["npow"], _pair_diag(u["npow"])).astype(BF16)

    for u in units:
        wb, ub = u["w"].astype(BF16), u["u0"].astype(BF16)
        lx = _dot(u["l_rb"].astype(BF16), jnp.concatenate([_pair_diag(wb), _pair_diag(ub)], axis=1))
        u["p_mat"] = u["r_bar"] + lx[:, :LANES]
        u["y0"] = u["lrkv"] + lx[:, LANES:]
        lhs = jnp.concatenate([u["b_hat"], u["k_hat"]], axis=0)
        rhs = jnp.concatenate([jnp.concatenate([wb, ub], axis=1),
                               jnp.concatenate([jnp.zeros_like(wb), u["v"]], axis=1)], axis=0)
        mg = _dot(lhs, rhs, TN)
        u["m_full"] = _pair_pick(mg[:, :LANES]) + jnp.where(eye, u["gam_c"], 0.0)
        u["g_mat"] = _pair_pick(mg[:, LANES:])

    for u in units:
        bi, d, p = u["bi"], u["d"], u["p"]
        h0 = h_ref[bi, d, p]
        a0, a1 = _split(jnp.concatenate([u["p_mat"], u["m_full"]], axis=0), 2)
        h_hi, h_lo = _split(h0, 2)
        bh, bl = _pair_diag(h_hi), _pair_diag(h_lo)
        out = _dot(a0, bh) + (_dot(a0, bl) + _dot(a1, bh))
        (yb_ref if d else yf_ref)[bi, :, u["sl"]] = out[:CHUNK] + u["y0"]
        h_ref[bi, d, p] = out[CHUNK:] + u["g_mat"]


def _rwkv_pairs(r, v, kk, ld, kd, beta, ctx_len):
    bn, s, _ = r.shape
    n_chunks = s // CHUNK
    ctx_chunks = ctx_len // CHUNK
    rows = RWKV_ROWS if bn % RWKV_ROWS == 0 else 1

    def back(c):
        return jnp.where(c < ctx_chunks, ctx_chunks - 1 - c, n_chunks - 1 + ctx_chunks - c)

    fwd = pl.BlockSpec((rows, CHUNK, B_WIDTH), lambda b, c: (b, c, 0))
    bwd = pl.BlockSpec((rows, CHUNK, B_WIDTH), lambda b, c: (b, back(c), 0))
    fwd_d = pl.BlockSpec((None, rows, CHUNK, B_WIDTH), lambda b, c: (0, b, c, 0))
    bwd_d = pl.BlockSpec((None, rows, CHUNK, B_WIDTH), lambda b, c: (1, b, back(c), 0))
    y = jax.ShapeDtypeStruct((bn, s, B_WIDTH), F32)
    return pl.pallas_call(
        _rwkv_pair_kernel,
        grid=(bn // rows, n_chunks),
        in_specs=[fwd, fwd, fwd, bwd, bwd, bwd, fwd_d, fwd_d, fwd_d, bwd_d, bwd_d, bwd_d],
        out_specs=[fwd, bwd],
        out_shape=[y, y],
        scratch_shapes=[pltpu.VMEM((rows, 2, B_WIDTH // LANES, B_HEAD, LANES), F32)],
        compiler_params=_params(("parallel", "arbitrary")),
        name="rwkv_scan",
    )(r, v, kk, r, v, kk, ld, kd, beta, ld, kd, beta)


def _outproj_kernel(x_ref, mod_ref, a_ref, yf_ref, yb_ref, bonus_ref, g_ref, c_ref,
                    wa_ref, wb_ref, wc_ref, lnxg_ref, lnxb_ref, ln1g_ref, ln1b_ref, o_ref, *, alpha):
    nb = x_ref.shape[0]
    stack = lambda ref: jnp.concatenate([ref[r] for r in range(nb)], axis=0)
    y = stack(yf_ref) + stack(yb_ref) + stack(bonus_ref)
    ones = _group_ones(B_WIDTH, B_HEAD)
    inv = 1.0 / B_HEAD
    mu = _dot_exact_rhs(y, ones) * inv
    yc = y - mu
    var = _dot_exact_rhs(yc * yc, ones) * inv
    yn = yc * lax.rsqrt(var + GN_EPS) * lnxg_ref[...] + lnxb_ref[...]
    bmix = (yn * stack(g_ref)).astype(BF16)
    o = _dot(stack(a_ref), wa_ref[...]) + _dot(bmix, wb_ref[...]) + _dot(stack(c_ref), wc_ref[...])
    for r in range(nb):
        o_ref[r] = (_ln(alpha * x_ref[r] + mod_ref[r, 2:3] * o[r * TM:(r + 1) * TM], LN_EPS) * ln1g_ref[...]
                    + ln1b_ref[...])


def _outproj(xs, mods, a_out, yf, yb, bonus, g, c_out, wa, wb, wc, lnxg, lnxb, ln1g, ln1b, ctx_len, first_tile,
             alpha):
    bn, s, d = xs.shape
    nb = _batch_rows(bn)
    row = lambda width: pl.BlockSpec((nb, TM, width), lambda b, i: (b, i + first_tile, 0))
    full = lambda a: pl.BlockSpec(a.shape, lambda b, i: (0,) * a.ndim)
    consts = (wa, wb, wc, lnxg, lnxb, ln1g, ln1b)
    return pl.pallas_call(
        functools.partial(_outproj_kernel, alpha=alpha),
        grid=(bn // nb, s // TM - first_tile),
        in_specs=[row(d), _mod_spec(ctx_len, d, nb, first_tile),
                  row(A_WIDTH), row(B_WIDTH), row(B_WIDTH), row(B_WIDTH), row(B_WIDTH), row(C_WIDTH)]
                 + [full(p) for p in consts],
        out_specs=row(d),
        out_shape=jax.ShapeDtypeStruct((bn, s, d), F32),
        compiler_params=_params(("parallel", "parallel")),
        name="outproj",
    )(xs, mods, a_out, yf, yb, bonus, g, c_out, *consts)


def _swiglu_rows(h, w1_ref, w3_ref, w2_ref):
    acc = jnp.zeros((h.shape[0], w2_ref.shape[-1]), F32)
    for j in range(w1_ref.shape[-1] // FF_CHUNK):
        sl = slice(j * FF_CHUNK, (j + 1) * FF_CHUNK)
        u = _dot(h, w1_ref[:, sl])
        t = _dot(h, w3_ref[:, sl])
        acc = acc + _dot((_silu(u) * t).astype(BF16), w2_ref[sl, :])
    return acc


def _ffn_kernel(x_ref, mod_ref, w1_ref, w3_ref, w2_ref, g_ref, b_ref, o_ref, *, alpha):
    nb = x_ref.shape[0]
    h = jnp.concatenate([(_ln(x_ref[r], LN_EPS) * (1.0 + mod_ref[r, 4:5]) + mod_ref[r, 3:4]).astype(BF16)
                         for r in range(nb)], axis=0)
    f = _swiglu_rows(h, w1_ref, w3_ref, w2_ref)
    for r in range(nb):
        o_ref[r] = (_ln(alpha * x_ref[r] + mod_ref[r, 5:6] * f[r * TM:(r + 1) * TM], LN_EPS) * g_ref[...]
                    + b_ref[...])


def _ffn(x1, mods, w1, w3, w2, g, b, ctx_len, alpha):
    bn, s, d = x1.shape
    nb = _batch_rows(bn)
    row = pl.BlockSpec((nb, TM, d), lambda bb, i: (bb, i, 0))
    resident = lambda a: pl.BlockSpec(a.shape, lambda bb, i: (0,) * a.ndim, pipeline_mode=pl.Buffered(1))
    full = lambda a: pl.BlockSpec(a.shape, lambda bb, i: (0,) * a.ndim)
    return pl.pallas_call(
        functools.partial(_ffn_kernel, alpha=alpha),
        grid=(bn // nb, s // TM),
        in_specs=[row, _mod_spec(ctx_len, d, nb),
                  resident(w1), resident(w3), resident(w2), full(g), full(b)],
        out_specs=row,
        out_shape=jax.ShapeDtypeStruct((bn, s, d), F32),
        compiler_params=_params(("parallel", "parallel")),
        name="ffn",
    )(x1, mods, w1, w3, w2, g, b)


def _moe_pre_kernel(x_ref, mod_ref, router_ref, h_ref, logit_ref):
    mod = mod_ref[...]
    h = _ln(x_ref[...], LN_EPS) * (1.0 + mod[4:5]) + mod[3:4]
    h_ref[...] = h.astype(h_ref.dtype)
    logit_ref[...] = _dot6(h, router_ref[...])


def _moe_pre(x1, mods, router_p, ctx_len):
    bn, s, d = x1.shape
    ct = ctx_len // TM
    lt = (s - ctx_len) // TM
    return pl.pallas_call(
        _moe_pre_kernel,
        grid=(bn, lt),
        in_specs=[pl.BlockSpec((None, TM, d), lambda b, i: (b, i + ct, 0)),
                  pl.BlockSpec((None, None, 6, d), lambda b, i: (b, 1, 0, 0)),
                  pl.BlockSpec(router_p.shape, lambda b, i: (0, 0))],
        out_specs=[pl.BlockSpec((TM, d), lambda b, i: (b * lt + i, 0)),
                   pl.BlockSpec((TM, LANES), lambda b, i: (b * lt + i, 0))],
        out_shape=[jax.ShapeDtypeStruct((bn * lt * TM, d), BF16),
                   jax.ShapeDtypeStruct((bn * lt * TM, LANES), F32)],
        compiler_params=_params(("parallel", "parallel")),
        name="moe_pre",
    )(x1, mods, router_p)


def _slot_onehot(slots_ref, block):
    sl = slots_ref[...]
    s_iota = lax.broadcasted_iota(jnp.int32, (MOE_ROWS, MOE_TILE), 0) + block * MOE_ROWS
    hit = jnp.logical_or(sl[0:1] == s_iota, sl[1:2] == s_iota)
    return jnp.where(hit, 1.0, 0.0).astype(BF16)


def _dispatch_kernel(wb_ref, wc_ref, wf_ref, wv_ref, slots_ref, h_ref, o_ref):
    w = pl.program_id(0)
    part = lambda: _dot(_slot_onehot(slots_ref, wb_ref[w]), h_ref[...])

    @pl.when(wf_ref[w] == 1)
    def _():
        o_ref[...] = part().astype(o_ref.dtype)

    @pl.when(jnp.logical_and(wf_ref[w] == 0, wv_ref[w] == 1))
    def _():
        o_ref[...] = (o_ref[...].astype(F32) + part()).astype(o_ref.dtype)


def _dispatch(h, slots, work):
    t, d = h.shape
    wb, wc, wf, wv = work
    n_slots = (t * TOP_K // MOE_ROWS + N_EXPERTS) * MOE_ROWS
    grid_spec = pltpu.PrefetchScalarGridSpec(
        num_scalar_prefetch=4,
        grid=(wb.shape[0],),
        in_specs=[pl.BlockSpec((TOP_K, MOE_TILE), lambda w, b, c, f, v: (0, c[w])),
                  pl.BlockSpec((MOE_TILE, d), lambda w, b, c, f, v: (c[w], 0))],
        out_specs=pl.BlockSpec((MOE_ROWS, d), lambda w, b, c, f, v: (b[w], 0)),
    )
    return pl.pallas_call(
        _dispatch_kernel,
        grid_spec=grid_spec,
        out_shape=jax.ShapeDtypeStruct((n_slots, d), BF16),
        compiler_params=_params(("arbitrary",)),
        name="moe_dispatch",
    )(wb, wc, wf, wv, slots, h)


def _collect_kernel(wb_ref, wc_ref, wf_ref, wv_ref, slots_ref, scol_ref, gate_ref, y_ref, o_ref):
    w = pl.program_id(0)

    def part():
        base = wb_ref[w] * MOE_ROWS
        scol = scol_ref[...] - base
        gates = jnp.where(jnp.logical_and(scol >= 0, scol < MOE_ROWS), gate_ref[...], 0.0)
        gate = jnp.sum(gates, axis=1, keepdims=True)
        return _dot(_slot_onehot(slots_ref, wb_ref[w]), y_ref[...], TN) * gate

    @pl.when(wf_ref[w] == 1)
    def _():
        o_ref[...] = part()

    @pl.when(jnp.logical_and(wf_ref[w] == 0, wv_ref[w] == 1))
    def _():
        o_ref[...] = o_ref[...] + part()


def _collect(y_buf, slots, slots_col, gates, work):
    n_slots, d = y_buf.shape
    t = slots.shape[1]
    wb, wc, wf, wv = work
    tile = lambda w, b, c, f, v: (c[w], 0)
    grid_spec = pltpu.PrefetchScalarGridSpec(
        num_scalar_prefetch=4,
        grid=(wb.shape[0],),
        in_specs=[pl.BlockSpec((TOP_K, MOE_TILE), lambda w, b, c, f, v: (0, c[w])),
                  pl.BlockSpec((MOE_TILE, TOP_K), tile), pl.BlockSpec((MOE_TILE, TOP_K), tile),
                  pl.BlockSpec((MOE_ROWS, d), lambda w, b, c, f, v: (b[w], 0))],
        out_specs=pl.BlockSpec((MOE_TILE, d), tile),
    )
    return pl.pallas_call(
        _collect_kernel,
        grid_spec=grid_spec,
        out_shape=jax.ShapeDtypeStruct((t, d), F32),
        compiler_params=_params(("arbitrary",)),
        name="moe_collect",
    )(wb, wc, wf, wv, slots, slots_col, gates, y_buf)


def _expert_kernel(be_ref, nb_ref, x_ref, w1_ref, w3_ref, w2_ref, o_ref):
    i = pl.program_id(0)

    @pl.when(i < nb_ref[0])
    def _():
        o_ref[...] = _swiglu_rows(x_ref[...], w1_ref, w3_ref, w2_ref).astype(o_ref.dtype)

    @pl.when(i >= nb_ref[0])
    def _():
        o_ref[...] = jnp.zeros_like(o_ref)


def _experts(buf, block_e, n_used, w1, w3, w2):
    n, d = buf.shape
    n_blocks = n // MOE_ROWS
    ff = w1.shape[-1]
    wspec = lambda shape: pl.BlockSpec((None,) + shape, lambda i, be, nb: (be[i], 0, 0),
                                       pipeline_mode=pl.Buffered(1))
    used = lambda i, be, nb: (jnp.minimum(i, nb[0] - 1), 0)
    grid_spec = pltpu.PrefetchScalarGridSpec(
        num_scalar_prefetch=2,
        grid=(n_blocks,),
        in_specs=[pl.BlockSpec((MOE_ROWS, d), used), wspec((d, ff)), wspec((d, ff)), wspec((ff, d))],
        out_specs=pl.BlockSpec((MOE_ROWS, d), lambda i, be, nb: (i, 0)),
    )
    return pl.pallas_call(
        _expert_kernel,
        grid_spec=grid_spec,
        out_shape=jax.ShapeDtypeStruct((n, d), BF16),
        compiler_params=_params(("arbitrary",)),
        name="experts",
    )(block_e, n_used, buf, w1, w3, w2)


def _combine_kernel(x_ref, mod_ref, y_ref, g_ref, b_ref, o_ref, *, alpha):
    mod = mod_ref[...]
    o_ref[...] = _ln(alpha * x_ref[...] + mod[5:6] * y_ref[...], LN_EPS) * g_ref[...] + b_ref[...]


def _combine(x1, mods, y_tok, g, b, ctx_len, alpha):
    bn, s, d = x1.shape
    ct = ctx_len // TM
    lt = (s - ctx_len) // TM
    full = lambda a: pl.BlockSpec(a.shape, lambda bb, i: (0,) * a.ndim)
    return pl.pallas_call(
        functools.partial(_combine_kernel, alpha=alpha),
        grid=(bn, lt),
        in_specs=[pl.BlockSpec((None, TM, d), lambda bb, i: (bb, i + ct, 0)),
                  pl.BlockSpec((None, None, 6, d), lambda bb, i: (bb, 1, 0, 0)),
                  pl.BlockSpec((TM, d), lambda bb, i: (bb * lt + i, 0)),
                  full(g), full(b)],
        out_specs=pl.BlockSpec((None, TM, d), lambda bb, i: (bb, i, 0)),
        out_shape=jax.ShapeDtypeStruct((bn, lt * TM, d), F32),
        compiler_params=_params(("parallel", "parallel")),
        name="moe_combine",
    )(x1, mods, y_tok, g, b)


def _work_lists(rank_at_tile, counts, pstart, block_e, n_used, n_blocks, n_tiles):
    n_work = n_tiles * N_EXPERTS + n_blocks
    blocks = jnp.arange(n_blocks, dtype=jnp.int32)
    used = blocks < n_used
    r0 = blocks * MOE_ROWS - pstart[block_e]
    r_last = jnp.minimum(r0 + MOE_ROWS, counts[block_e]) - 1
    cols = rank_at_tile.T[block_e]
    find = jax.vmap(lambda col, val: jnp.searchsorted(col, val, side="right"))
    lo = jnp.clip(find(cols, r0) - 1, 0, n_tiles - 1)
    hi = jnp.clip(find(cols, r_last) - 1, 0, n_tiles - 1)
    n_b = jnp.where(used, hi - lo + 1, 0)
    ends = jnp.cumsum(n_b)
    starts = ends - n_b
    total = ends[-1]
    w = jnp.arange(n_work, dtype=jnp.int32)
    valid = w < total
    wl = jnp.minimum(w, total - 1)
    blk = jnp.minimum(jnp.searchsorted(ends, wl, side="right"), n_blocks - 1).astype(jnp.int32)
    tile = (lo[blk] + (wl - starts[blk])).astype(jnp.int32)
    first = jnp.logical_and(valid, w == starts[blk])
    as_i32 = lambda a: a.astype(jnp.int32)
    by_block = (blk, tile, as_i32(first), as_i32(valid))
    order = jnp.argsort(jnp.where(valid, tile, n_tiles), stable=True)
    order = order[jnp.minimum(w, total - 1)]
    tile2, blk2 = tile[order], blk[order]
    first2 = jnp.logical_and(valid, jnp.concatenate([jnp.ones((1,), bool), tile2[1:] != tile2[:-1]]))
    by_tile = (blk2, tile2, as_i32(first2), as_i32(valid))
    return by_block, by_tile


def _moe_layer(x1, mods, router, w1, w3, w2, g, b, ctx_len, alpha):
    d = x1.shape[-1]
    router_p = jnp.pad(router, ((0, 0), (0, LANES - N_EXPERTS)))
    h, logits = _moe_pre(x1, mods, router_p, ctx_len)
    t = h.shape[0]
    top_v, top_i = lax.top_k(logits[:, :N_EXPERTS], TOP_K)
    gates = jax.nn.softmax(top_v, axis=-1)
    e_flat = top_i.reshape(-1)
    onehot = (e_flat[:, None] == jnp.arange(N_EXPERTS)[None, :]).astype(jnp.int32)
    ranks = jnp.cumsum(onehot, axis=0) - onehot
    rank = jnp.sum(ranks * onehot, axis=1)
    counts = jnp.sum(onehot, axis=0)
    padded = (counts + MOE_ROWS - 1) // MOE_ROWS * MOE_ROWS
    pend = jnp.cumsum(padded)
    pstart = pend - padded
    slot = (pstart[e_flat] + rank).astype(jnp.int32)
    n_blocks = t * TOP_K // MOE_ROWS + N_EXPERTS
    assert t % MOE_TILE == 0
    n_tiles = t // MOE_TILE
    block_e = jnp.minimum(jnp.searchsorted(pend, jnp.arange(n_blocks) * MOE_ROWS, side="right"),
                          N_EXPERTS - 1).astype(jnp.int32)
    n_used = (pend[-1:] // MOE_ROWS).astype(jnp.int32)
    rank_at_tile = jnp.concatenate([ranks[::MOE_TILE * TOP_K], counts[None, :]], axis=0)
    by_block, by_tile = _work_lists(rank_at_tile, counts, pstart, block_e, n_used, n_blocks, n_tiles)
    slots_col = slot.reshape(t, TOP_K)
    slots = slots_col.T
    buf = _dispatch(h, slots, by_block)
    y_buf = _experts(buf, block_e, n_used, w1, w3, w2)
    y_tok = _collect(y_buf, slots, slots_col, gates, by_tile)
    return _combine(x1, mods, y_tok, g, b, ctx_len, alpha)


def _rope_tables(n_rows, ctx_len, dim, lane_lo):
    quarter = dim // 4
    inv = ROPE_THETA ** (-jnp.arange(quarter, dtype=F32) / quarter)
    rows = jnp.repeat(jnp.arange(n_rows, dtype=F32), GRID_W)
    cols = jnp.tile(jnp.arange(GRID_W, dtype=F32), n_rows)
    ang = jnp.concatenate([rows[:, None] * inv, rows[:, None] * inv,
                           cols[:, None] * inv, cols[:, None] * inv], axis=-1)
    sign = jnp.tile(jnp.concatenate([-jnp.ones(quarter, F32), jnp.ones(quarter, F32)]), 2)
    cos, sin = jnp.cos(ang), jnp.sin(ang) * sign
    length = cos.shape[0]
    if lane_lo == 0:
        reps = LANES // dim
        cos, sin = jnp.tile(cos, (1, reps)), jnp.tile(sin, (1, reps))
    else:
        pad = ((0, 0), (lane_lo, LANES - lane_lo - dim))
        cos = jnp.pad(cos, pad, constant_values=1.0)
        sin = jnp.pad(sin, pad)
    cos = jnp.concatenate([jnp.ones((ctx_len, LANES), F32), cos], axis=0)
    sin = jnp.concatenate([jnp.zeros((ctx_len, LANES), F32), sin], axis=0)
    return cos, sin


def _block_diag2(w):
    z = jnp.zeros_like(w[0])
    return jnp.concatenate([jnp.concatenate([w[0], z], axis=1), jnp.concatenate([z, w[1]], axis=1)], axis=0)


def _lambda_init(layer):
    return 0.8 - 0.6 * math.exp(-0.3 * layer)


def kernel(x, c, ctx, c_ctx, ada_w, ada_b, w_in, w_out, ln1_g, ln1_b, ln2_g, ln2_b,
           lam_q1, lam_k1, lam_q2, lam_k2, diff_norm_g, shift_mu, w0, w2, a0, a2, g2,
           k_k, k_a, r_k, lnx_g, lnx_b, q_norm_g, w_uq, kv_norm_g, w_ukv,
           ff_w1, ff_w3, ff_w2, router, moe_w1, moe_w3, moe_w2):
    bn, seq, d = x.shape
    ctx_len = ctx.shape[1]
    depth = ada_w.shape[0]
    assert d == D_MODEL and seq % TM == 0 and ctx_len % TM == 0 and seq % GRID_W == 0
    alpha = (2.0 * depth) ** 0.25
    n_grid_rows = seq // GRID_W
    cos_a, sin_a = _rope_tables(n_grid_rows, ctx_len, A_QK_DIM, 0)
    cos_c, sin_c = _rope_tables(n_grid_rows, ctx_len, C_ROPE, C_NOPE)

    cond_rows = 8 * ((bn + 1 + 7) // 8)
    cond = jnp.zeros((cond_rows, d), F32).at[:bn].set(c).at[bn].set(c_ctx)
    xs = jnp.concatenate([ctx, x], axis=1)

    for i in range(depth):
        with_ctx = i < depth - 1
        m = _ada(cond, ada_w[i], ada_b[i]).reshape(cond_rows, 6, d)
        mods = jnp.stack([jnp.broadcast_to(m[bn], (bn, 6, d)), m[:bn]], axis=1)

        wc = w_in[i][:, N_A + N_B:]
        kpe_w = jnp.pad(wc[:, C_Q_RANK + C_KV_RANK:], ((0, 0), (C_NOPE, LANES - C_NOPE - C_ROPE)))
        w_p = jnp.concatenate([w_in[i][:, :N_A + N_B], wc[:, :C_Q_RANK + C_KV_RANK], kpe_w], axis=1).astype(BF16)
        wq = w_uq[i].reshape(C_Q_RANK, C_HEADS, C_NOPE + C_ROPE)
        wq_p = jnp.pad(wq, ((0, 0), (0, 0), (0, LANES - C_NOPE - C_ROPE))).reshape(C_Q_RANK, -1).astype(BF16)
        wkv = w_ukv[i].reshape(C_KV_RANK, C_HEADS, C_NOPE + C_V)
        wk_p = jnp.pad(wkv[:, :, :C_NOPE], ((0, 0), (0, 0), (0, LANES - C_NOPE))).reshape(C_KV_RANK, -1).astype(BF16)
        wv_p = wkv[:, :, C_NOPE:].reshape(C_KV_RANK, -1).astype(BF16)
        mla_consts = (q_norm_g[i].reshape(1, -1), kv_norm_g[i].reshape(1, -1), wq_p, wk_p, wv_p)
        qa, ka, va, pb, qc, kc, vc = _inproj(xs, mods, w_p, cos_a, sin_a, mla_consts, cos_c, sin_c, ctx_len)

        lam_p = jnp.stack([lam_q1[i], lam_k1[i], lam_q2[i], lam_k2[i]])
        g_col = jnp.broadcast_to(diff_norm_g[i][:, None], (A_V_DIM, TM))
        first_tile = 0 if with_ctx else ctx_len // TM
        a_out, c_out = _attention([("diff", qa, ka, va), ("mla", qc, kc, vc)], lam_p, g_col, ctx_len=ctx_len,
                                  first_tile=first_tile, lam_init=_lambda_init(i))
        r, v, kkn, ld, kd, beta, gate, bonus = _rwkv_prep(
            pb, shift_mu[i].reshape(1, -1), w0[i].reshape(1, -1), _block_diag2(w2[i]).astype(BF16),
            a0[i].reshape(1, -1), _block_diag2(a2[i]).astype(BF16), g2[i].astype(BF16),
            k_k[i].reshape(1, -1), k_a[i].reshape(1, -1), r_k[i].reshape(1, -1), ctx_len)
        yf, yb = _rwkv_pairs(r, v, kkn, ld, kd, beta, ctx_len)

        wo = w_out[i].astype(BF16)
        x1 = _outproj(xs, mods, a_out, yf, yb, bonus, gate, c_out,
                      wo[:A_WIDTH], wo[A_WIDTH:A_WIDTH + B_WIDTH], wo[A_WIDTH + B_WIDTH:],
                      lnx_g[i].reshape(1, -1), lnx_b[i].reshape(1, -1),
                      ln1_g[i].reshape(1, -1), ln1_b[i].reshape(1, -1), ctx_len, first_tile, alpha)

        j = i // 2
        g2n, b2n = ln2_g[i].reshape(1, -1), ln2_b[i].reshape(1, -1)
        if i % 2 == 0:
            xs = _ffn(x1, mods, ff_w1[j].astype(BF16), ff_w3[j].astype(BF16), ff_w2[j].astype(BF16),
                      g2n, b2n, ctx_len, alpha)
        else:
            if with_ctx:
                raise NotImplementedError("routed FFN on the context rows is not needed at this depth")
            return _moe_layer(x1, mods, router[j], moe_w1[j].astype(BF16), moe_w3[j].astype(BF16),
                              moe_w2[j].astype(BF16), g2n, b2n, ctx_len, alpha)
    return xs[:, ctx_len:]
```

```python
import functools
import math

import jax
import jax.numpy as jnp
from jax import lax
from jax.experimental import pallas as pl
from jax.experimental.pallas import tpu as pltpu

F32 = jnp.float32
BF16 = jnp.bfloat16

D_MODEL = 1024
GRID_W = 64
ROPE_THETA = 10000.0
A_HEADS, A_QK_DIM, A_V_DIM = 4, 64, 128
A_WIDTH = A_HEADS * A_V_DIM
A_QK_COLS = 2 * A_HEADS * A_QK_DIM
B_HEADS, B_HEAD = 4, 64
B_WIDTH = B_HEADS * B_HEAD
B_DECAY_RANK, B_A_RANK, B_GATE_RANK = 64, 64, 128
C_HEADS, C_NOPE, C_ROPE, C_V = 4, 64, 32, 64
C_WIDTH = C_HEADS * C_V
C_Q_RANK, C_KV_RANK = 256, 128
N_A = 2 * A_QK_COLS + A_WIDTH
N_B = 3 * B_WIDTH + 2 * B_DECAY_RANK + 2 * B_A_RANK + B_GATE_RANK
N_C = C_Q_RANK + C_KV_RANK + C_ROPE
N_C_PAD = C_Q_RANK + C_KV_RANK + 128
D_FF = 3584
N_EXPERTS = 8
TOP_K = 2
LN_EPS = 1e-6
RMS_EPS = 1e-6
GN_EPS = 64e-5

LOG2E = math.log2(math.e)
LANES = 128
TM = 256
CHUNK = 64
ROWS_B = 2
RWKV_ROWS = 4
ATT_TK = 2048
ONES_ROWS = 16
MOE_ROWS = 512
MOE_TILE = 1024
FF_CHUNK = 512
VMEM_LIMIT = 56 * 1024 * 1024

NN = (((1,), (0,)), ((), ()))
NT = (((1,), (1,)), ((), ()))
TN = (((0,), (0,)), ((), ()))


def _params(sem, vmem=VMEM_LIMIT, flags=None):
    return pltpu.CompilerParams(dimension_semantics=sem, vmem_limit_bytes=vmem, flags=flags)


def _split(x, n):
    parts, r = [], x
    for _ in range(n):
        p = r.astype(BF16)
        parts.append(p)
        r = r - p.astype(F32)
    return parts


def _dot(a, b, dn=NN):
    return lax.dot_general(a, b, dn, preferred_element_type=F32)


def _dot1(a, b, dn=NN):
    return _dot(a.astype(BF16), b.astype(BF16), dn)


def _dot3(a, b, dn=NN):
    a0, a1 = _split(a, 2)
    b0, b1 = _split(b, 2)
    return _dot(a0, b0, dn) + (_dot(a0, b1, dn) + _dot(a1, b0, dn))


def _dot6(a, b, dn=NN):
    a0, a1, a2 = _split(a, 3)
    b0, b1, b2 = _split(b, 3)
    lo = _dot(a1, b1, dn) + (_dot(a0, b2, dn) + _dot(a2, b0, dn))
    return _dot(a0, b0, dn) + ((_dot(a0, b1, dn) + _dot(a1, b0, dn)) + lo)


def _dot_exact_lhs(a_bf16, b, dn=NN, n=3):
    parts = _split(b, n)
    out = _dot(a_bf16, parts[-1], dn)
    for p in parts[-2::-1]:
        out = out + _dot(a_bf16, p, dn)
    return out


def _dot_exact_rhs(a, b_bf16, dn=NN, n=3):
    parts = _split(a, n)
    out = _dot(parts[-1], b_bf16, dn)
    for p in parts[-2::-1]:
        out = out + _dot(p, b_bf16, dn)
    return out


def _ln(x, eps):
    mu = jnp.mean(x, axis=-1, keepdims=True)
    xc = x - mu
    return xc * lax.rsqrt(jnp.mean(xc * xc, axis=-1, keepdims=True) + eps)


def _sigmoid(x):
    return 1.0 / (1.0 + jnp.exp(-x))


def _silu(x):
    return x * _sigmoid(x)


def _group_ones(width, group):
    r = lax.broadcasted_iota(jnp.int32, (width, width), 0) // group
    c = lax.broadcasted_iota(jnp.int32, (width, width), 1) // group
    return (r == c).astype(BF16)


def _partner(x, half):
    lane = lax.broadcasted_iota(jnp.int32, x.shape, 1)
    up = pltpu.roll(x, LANES - half, 1)
    dn = pltpu.roll(x, half, 1)
    return jnp.where((lane % (2 * half)) < half, up, dn)


def _rope(x, cos, sin, half):
    return x * cos + _partner(x, half) * sin


def _ada_kernel(c_ref, w_ref, b_ref, o_ref):
    o_ref[...] = _dot3(_silu(c_ref[...]), w_ref[...]) + b_ref[...]


def _ada(cond, w, b):
    rows, d = cond.shape
    n = w.shape[1]
    tn = 1536
    return pl.pallas_call(
        _ada_kernel,
        grid=(n // tn,),
        in_specs=[pl.BlockSpec((rows, d), lambda j: (0, 0)),
                  pl.BlockSpec((d, tn), lambda j: (0, j)),
                  pl.BlockSpec((1, tn), lambda j: (0, j))],
        out_specs=pl.BlockSpec((rows, tn), lambda j: (0, j)),
        out_shape=jax.ShapeDtypeStruct((rows, n), F32),
        compiler_params=_params(("parallel",)),
        name="ada",
    )(cond, w, b.reshape(1, n))


def _inproj_kernel(x_ref, mod_ref, w_ref, cos_ref, sin_ref,
                   qg_ref, kvg_ref, wq_ref, wk_ref, wv_ref, cosc_ref, sinc_ref,
                   q_ref, k_ref, v_ref, pb_ref, cq_ref, ck_ref, cv_ref):
    nb = x_ref.shape[0]
    h = jnp.concatenate([(_ln(x_ref[r], LN_EPS) * (1.0 + mod_ref[r, 1:2]) + mod_ref[r, 0:1]).astype(BF16)
                         for r in range(nb)], axis=0)
    cos, sin = cos_ref[...], sin_ref[...]
    scale = A_QK_DIM ** -0.5 * LOG2E
    rows = lambda a, r: a[r * TM:(r + 1) * TM]
    for j in range(A_QK_COLS // LANES):
        sl = slice(j * LANES, (j + 1) * LANES)
        qj = _dot(h, w_ref[:, sl])
        kj = _dot(h, w_ref[:, A_QK_COLS + j * LANES:A_QK_COLS + (j + 1) * LANES])
        vj = _dot(h, w_ref[:, 2 * A_QK_COLS + j * LANES:2 * A_QK_COLS + (j + 1) * LANES])
        for r in range(nb):
            q_ref[r, sl, :] = (_rope(rows(qj, r), cos, sin, A_QK_DIM // 4) * scale).T.astype(BF16)
            k_ref[r, :, sl] = _rope(rows(kj, r), cos, sin, A_QK_DIM // 4).astype(BF16)
            v_ref[r, sl, :] = rows(vj, r).T.astype(BF16)
    pb = _dot(h, w_ref[:, N_A:N_A + N_B])
    for r in range(nb):
        pb_ref[r] = rows(pb, r)
    _mla_heads(_dot(h, w_ref[:, N_A + N_B:]), nb, qg_ref, kvg_ref, wq_ref, wk_ref, wv_ref,
               cosc_ref[...], sinc_ref[...], cq_ref, ck_ref, cv_ref)


def _batch_rows(bn):
    return ROWS_B if bn % ROWS_B == 0 else 1


def _mod_spec(ctx_len, d, nb=None, first_tile=0):
    ct = ctx_len // TM
    return pl.BlockSpec((nb, None, 6, d), lambda b, i: (b, jnp.where(i + first_tile >= ct, 1, 0), 0, 0))


def _inproj(xs, mods, w_p, cos_a, sin_a, mla_consts, cos_c, sin_c, ctx_len):
    bn, s, d = xs.shape
    n_tiles = s // TM
    n_w = w_p.shape[1]
    nb = _batch_rows(bn)
    row = lambda width: pl.BlockSpec((nb, TM, width), lambda b, i: (b, i, 0))
    col = lambda width: pl.BlockSpec((nb, width, TM), lambda b, i: (b, 0, i))
    full = lambda a: pl.BlockSpec(a.shape, lambda b, i: (0,) * a.ndim)
    tab = pl.BlockSpec((TM, LANES), lambda b, i: (i, 0))
    hw = C_HEADS * LANES
    outs = [jax.ShapeDtypeStruct((bn, A_QK_COLS, s), BF16),
            jax.ShapeDtypeStruct((bn, s, A_QK_COLS), BF16),
            jax.ShapeDtypeStruct((bn, A_WIDTH, s), BF16),
            jax.ShapeDtypeStruct((bn, s, N_B), F32),
            jax.ShapeDtypeStruct((bn, hw, s), BF16),
            jax.ShapeDtypeStruct((bn, s, hw), BF16),
            jax.ShapeDtypeStruct((bn, C_WIDTH, s), BF16)]
    return pl.pallas_call(
        _inproj_kernel,
        grid=(bn // nb, n_tiles),
        in_specs=[row(d), _mod_spec(ctx_len, d, nb), full(w_p), tab, tab]
                 + [full(a) for a in mla_consts] + [tab, tab],
        out_specs=[col(A_QK_COLS), row(A_QK_COLS), col(A_WIDTH), row(N_B), col(hw), row(hw), col(C_WIDTH)],
        out_shape=outs,
        compiler_params=_params(("parallel", "parallel")),
        name="inproj",
    )(xs, mods, w_p, cos_a, sin_a, *mla_consts, cos_c, sin_c)


def _mla_heads(pc, nb, qg_ref, kvg_ref, wq_ref, wk_ref, wv_ref, cos, sin, q_ref, k_ref, v_ref):
    cq = pc[:, :C_Q_RANK]
    cq = cq * lax.rsqrt(jnp.mean(cq * cq, axis=-1, keepdims=True) + RMS_EPS) * qg_ref[...]
    ckv = pc[:, C_Q_RANK:C_Q_RANK + C_KV_RANK]
    ckv = ckv * lax.rsqrt(jnp.mean(ckv * ckv, axis=-1, keepdims=True) + RMS_EPS) * kvg_ref[...]
    cqb, ckvb = cq.astype(BF16), ckv.astype(BF16)
    scale = (C_NOPE + C_ROPE) ** -0.5 * LOG2E
    rows = lambda a, r: a[r * TM:(r + 1) * TM]
    kpe = [_rope(rows(pc, r)[:, C_Q_RANK + C_KV_RANK:], cos, sin, C_ROPE // 4) for r in range(nb)]
    for h in range(C_HEADS):
        sl = slice(h * LANES, (h + 1) * LANES)
        qh = _dot(cqb, wq_ref[:, sl])
        kh = _dot(ckvb, wk_ref[:, sl])
        for r in range(nb):
            q_ref[r, sl, :] = (_rope(rows(qh, r), cos, sin, C_ROPE // 4) * scale).T.astype(BF16)
            k_ref[r, :, sl] = (rows(kh, r) + kpe[r]).astype(BF16)
    for j in range(C_WIDTH // LANES):
        sl = slice(j * LANES, (j + 1) * LANES)
        vj = _dot(ckvb, wv_ref[:, sl])
        for r in range(nb):
            v_ref[r, sl, :] = rows(vj, r).T.astype(BF16)


def _attn_kernel(lam_ref, g_ref, *refs, modes, ctx_tiles, ctx_len, tk, first_tile, lam_init):
    n_sets = len(modes)
    in_refs = [refs[3 * t:3 * t + 3] for t in range(n_sets)]
    o_refs = refs[3 * n_sets:]
    qi = pl.program_id(1) + first_tile
    tq = in_refs[0][0].shape[1]
    half = lax.broadcasted_iota(jnp.int32, (LANES, 1), 0) < (LANES // 2)
    chains = []
    for t, (qt_ref, k_ref, vt_ref) in enumerate(in_refs):
        heads = qt_ref.shape[0] // LANES
        for g in range(heads):
            qt = qt_ref[g * LANES:(g + 1) * LANES, :]
            if modes[t] == "diff":
                qt = jnp.concatenate([jnp.where(half, qt, jnp.zeros_like(qt)),
                                      jnp.where(half, jnp.zeros_like(qt), qt)], axis=1)
            chains.append((t, g, qt, vt_ref.shape[0] // heads, qt.shape[1]))
    s_len = in_refs[0][1].shape[0]

    def fold(x, reduce):
        rows = x.shape[0]
        while rows > 8:
            g = max(d for d in range(2, 9) if (rows // 8) % d == 0)
            rows //= g
            x = reduce(x.reshape(g, rows, x.shape[1]), axis=0)
        return x

    def absorb(kcs, vcs, carry):
        n = len(chains)
        ss = [_dot(kcs[t][:, g * LANES:(g + 1) * LANES], q) for (t, g, q, _, _) in chains]
        m_new = [jnp.maximum(carry[c][0], jnp.max(fold(ss[c], jnp.max), axis=0, keepdims=True)) for c in range(n)]
        alpha = [jnp.exp2(carry[c][0] - m_new[c]) for c in range(n)]
        ps = [jnp.exp2(ss[c] - m_new[c]).astype(BF16) for c in range(n)]
        ones = jnp.ones((ONES_ROWS, kcs[0].shape[0]), BF16)
        acc = [alpha[c] * carry[c][1]
               + _dot(jnp.concatenate([vcs[t][g * dv:(g + 1) * dv], ones], axis=0), ps[c])
               for c, (t, g, _, dv, _) in enumerate(chains)]
        return tuple((m_new[c], acc[c]) for c in range(n))

    def finish(stats):
        outs = [a[:dv] / a[dv:dv + 1] for (_, a), (_, _, _, dv, _) in zip(stats, chains)]
        for t in range(n_sets):
            mine = [outs[c] for c, ch in enumerate(chains) if ch[0] == t]
            if modes[t] == "diff":
                lp = lam_ref[...]
                lam = (jnp.exp(jnp.sum(lp[0:1] * lp[1:2], axis=-1, keepdims=True))
                       - jnp.exp(jnp.sum(lp[2:3] * lp[3:4], axis=-1, keepdims=True)) + lam_init)
                for g, og in enumerate(mine):
                    o = og[:, :tq] - lam * og[:, tq:]
                    o = o * lax.rsqrt(jnp.mean(o * o, axis=0, keepdims=True) + RMS_EPS) * g_ref[...]
                    o = o * (1.0 - lam_init)
                    o_refs[t][:, g * LANES:(g + 1) * LANES] = o.T.astype(o_refs[t].dtype)
            else:
                per_slab = LANES // mine[0].shape[0]
                for j in range(len(mine) // per_slab):
                    o = jnp.concatenate(mine[j * per_slab:(j + 1) * per_slab], axis=0)
                    o_refs[t][:, j * LANES:(j + 1) * LANES] = o.T.astype(o_refs[t].dtype)

    def run(first_keys, n_more):
        init = tuple((jnp.full((1, w), -1e30, F32), jnp.zeros((dv + ONES_ROWS, w), F32))
                     for (_, _, _, dv, w) in chains)
        stats = absorb([r[1][0:first_keys, :] for r in in_refs], [r[2][:, 0:first_keys] for r in in_refs], init)

        def body(j, stats):
            off = pl.multiple_of(first_keys + j * tk, LANES)
            return absorb([r[1][pl.ds(off, tk), :] for r in in_refs],
                          [r[2][:, pl.ds(off, tk)] for r in in_refs], stats)

        finish(lax.fori_loop(0, n_more, body, stats))

    @pl.when(qi < ctx_tiles)
    def _():
        run(ctx_len, 0)

    @pl.when(qi >= ctx_tiles)
    def _():
        run(ctx_len + tk, (s_len - ctx_len) // tk - 1)


def _attention(sets, lam_p, g, *, ctx_len, first_tile, lam_init):
    bn, s, _ = sets[0][2].shape
    tk = math.gcd(s - ctx_len, ATT_TK)
    assert tk % LANES == 0 and ctx_len % LANES == 0
    kern = functools.partial(_attn_kernel, modes=tuple(m for m, _, _, _ in sets), ctx_tiles=ctx_len // TM,
                             ctx_len=ctx_len, tk=tk, first_tile=first_tile, lam_init=lam_init)
    in_specs = [pl.BlockSpec(lam_p.shape, lambda b, i: (0, 0)), pl.BlockSpec(g.shape, lambda b, i: (0, 0))]
    args, out_specs, out_shapes = [lam_p, g], [], []
    for _, qt, k, vt in sets:
        in_specs += [pl.BlockSpec((None, qt.shape[1], TM), lambda b, i: (b, 0, i + first_tile)),
                     pl.BlockSpec((None, s, k.shape[2]), lambda b, i: (b, 0, 0), pipeline_mode=pl.Buffered(1)),
                     pl.BlockSpec((None, vt.shape[1], s), lambda b, i: (b, 0, 0), pipeline_mode=pl.Buffered(1))]
        args += [qt, k, vt]
        out_specs.append(pl.BlockSpec((None, TM, vt.shape[1]), lambda b, i: (b, i + first_tile, 0)))
        out_shapes.append(jax.ShapeDtypeStruct((bn, s, vt.shape[1]), BF16))
    return pl.pallas_call(
        kern,
        grid=(bn, s // TM - first_tile),
        in_specs=in_specs,
        out_specs=out_specs,
        out_shape=out_shapes,
        compiler_params=_params(("parallel", "parallel")),
        name="attention",
    )(*args)


def _rwkv_prep_kernel(pb_ref, prev_ref, next_ref, mu_ref, w0_ref, w2_ref, a0_ref, a2_ref, g2_ref,
                      kk_ref, ka_ref, rk_ref,
                      r_out, v_out, kkn_out, ld_out, kd_out, beta_out, g_out, bonus_out,
                      *, ctx_tiles, n_tiles):
    i = pl.program_id(1)
    x = pb_ref[...]
    row = lax.broadcasted_iota(jnp.int32, (TM, 1), 0)
    has_prev = jnp.logical_and(i != 0, i != ctx_tiles)
    has_next = jnp.logical_and(i != ctx_tiles - 1, i != n_tiles - 1)
    prev_edge = jnp.where(has_prev, prev_ref[7:8, :], 0.0)
    next_edge = jnp.where(has_next, next_ref[0:1, :], 0.0)
    xp = jnp.where(row == 0, prev_edge, pltpu.roll(x, 1, 0))
    xn = jnp.where(row == TM - 1, next_edge, pltpu.roll(x, TM - 1, 0))
    z = x + mu_ref[...] * (0.5 * (xp + xn) - x)

    r = z[:, :B_WIDTH]
    k = z[:, B_WIDTH:2 * B_WIDTH]
    v = z[:, 2 * B_WIDTH:3 * B_WIDTH]
    o = 3 * B_WIDTH
    wd = z[:, o:o + 2 * B_DECAY_RANK]
    o += 2 * B_DECAY_RANK
    ad = z[:, o:o + 2 * B_A_RANK]
    o += 2 * B_A_RANK
    gd = z[:, o:]

    u = w0_ref[...] + _dot1(jnp.tanh(wd), w2_ref[...])
    ld = -math.exp(-0.5) * _sigmoid(u)
    lr = _sigmoid(a0_ref[...] + _dot1(ad, a2_ref[...]))
    g_out[...] = _dot1(_sigmoid(gd), g2_ref[...])

    ones = _group_ones(B_WIDTH, B_HEAD)
    kk = k * kk_ref[...]
    norm = jnp.sqrt(_dot_exact_rhs(kk * kk, ones))
    kkn = kk / jnp.maximum(norm, 1e-12)
    ka = ka_ref[...]
    kd_sum = jnp.zeros_like(k)
    for d in range(2):
        lr_d = lr[:, d * B_WIDTH:(d + 1) * B_WIDTH]
        kd = k * (1.0 + (lr_d - 1.0) * ka)
        kd_sum = kd_sum + kd
        ld_out[d] = ld[:, d * B_WIDTH:(d + 1) * B_WIDTH]
        kd_out[d] = kd
        beta_out[d] = kkn * lr_d
    bonus_out[...] = _dot_exact_rhs(r * kd_sum * rk_ref[...], ones) * v
    r_out[...] = r
    v_out[...] = v
    kkn_out[...] = kkn


def _rwkv_prep(pb, mu, w0, w2bd, a0, a2bd, g2, k_k, k_a, r_k, ctx_len):
    bn, s, _ = pb.shape
    n_tiles = s // TM
    eight = TM // 8
    row = lambda width: pl.BlockSpec((None, TM, width), lambda b, i: (b, i, 0))
    drow = pl.BlockSpec((2, None, TM, B_WIDTH), lambda b, i: (0, b, i, 0))
    full = lambda a: pl.BlockSpec(a.shape, lambda b, i: (0,) * a.ndim)
    one = jax.ShapeDtypeStruct((bn, s, B_WIDTH), F32)
    two = jax.ShapeDtypeStruct((2, bn, s, B_WIDTH), F32)
    kern = functools.partial(_rwkv_prep_kernel, ctx_tiles=ctx_len // TM, n_tiles=n_tiles)
    params = (mu, w0, w2bd, a0, a2bd, g2, k_k, k_a, r_k)
    return pl.pallas_call(
        kern,
        grid=(bn, n_tiles),
        in_specs=[row(N_B),
                  pl.BlockSpec((None, 8, N_B), lambda b, i: (b, jnp.maximum(i * eight - 1, 0), 0)),
                  pl.BlockSpec((None, 8, N_B), lambda b, i: (b, jnp.minimum((i + 1) * eight, s // 8 - 1), 0)),
                  ] + [full(p) for p in params],
        out_specs=[row(B_WIDTH), row(B_WIDTH), row(B_WIDTH), drow, drow, drow, row(B_WIDTH), row(B_WIDTH)],
        out_shape=[one, one, one, two, two, two, one, one],
        compiler_params=_params(("parallel", "parallel")),
        name="rwkv_prep",
    )(pb, pb, pb, *params)


def _pair_diag(x):
    lo = lax.broadcasted_iota(jnp.int32, (1, LANES), 1) < B_HEAD
    z = jnp.zeros_like(x)
    return jnp.concatenate([jnp.where(lo, x, z), jnp.where(lo, z, x)], axis=0)


def _pair_pick(x):
    lo = lax.broadcasted_iota(jnp.int32, (1, LANES), 1) < B_HEAD
    return jnp.where(lo, x[:B_HEAD], x[B_HEAD:])


def _rwkv_pair_kernel(rf_ref, vf_ref, kkf_ref, rb_ref, vb_ref, kkb_ref,
                      ldf_ref, kdf_ref, betaf_ref, ldb_ref, kdb_ref, betab_ref,
                      yf_ref, yb_ref, h_ref):
    c = pl.program_id(1)

    @pl.when(c == 0)
    def _():
        h_ref[...] = jnp.zeros_like(h_ref)

    n_pairs = B_WIDTH // LANES
    ti = lax.broadcasted_iota(jnp.int32, (CHUNK, LANES), 0)
    si = lax.broadcasted_iota(jnp.int32, (CHUNK, LANES), 1) % CHUNK
    t64 = lax.broadcasted_iota(jnp.int32, (CHUNK, CHUNK), 0)
    s64 = lax.broadcasted_iota(jnp.int32, (CHUNK, CHUNK), 1)
    eye = ti == si
    dirs = ((rf_ref, vf_ref, kkf_ref, ldf_ref, kdf_ref, betaf_ref, False),
            (rb_ref, vb_ref, kkb_ref, ldb_ref, kdb_ref, betab_ref, True))

    units = []
    n_rows = rf_ref.shape[0]
    for bi in range(n_rows):
      for d, (r_ref, v_ref, kk_ref, ld_ref, kd_ref, beta_ref, rev) in enumerate(dirs):
        strict = (si > ti) if rev else (ti > si)
        incl = (si >= ti) if rev else (ti >= si)
        tri = ((s64 >= t64) if rev else (t64 >= s64)).astype(BF16)
        ld_all = ld_ref[bi]
        cl_all = _dot_exact_lhs(tri, ld_all)
        for p in range(n_pairs):
            sl = slice(p * LANES, (p + 1) * LANES)
            ld, cl = ld_all[:, sl], cl_all[:, sl]
            total = jnp.sum(ld, axis=0, keepdims=True)
            inv_gam = jnp.exp(-cl)
            to_end = jnp.exp(total - cl)
            kk, kd, beta = kk_ref[bi, :, sl], kd_ref[bi, :, sl], beta_ref[bi, :, sl]
            units.append(dict(
                bi=bi, d=d, p=p, sl=sl, strict=strict, incl=incl, v=v_ref[bi, :, sl].astype(BF16),
                a_bar=-kk * jnp.exp(cl - ld), r_bar=r_ref[bi, :, sl] * jnp.exp(cl),
                b_til=(beta * inv_gam).astype(BF16), k_til=(kd * inv_gam).astype(BF16),
                b_hat=(beta * to_end).astype(BF16), k_hat=(kd * to_end).astype(BF16),
                gam_c=jnp.exp(total)))

    for u in units:
        x_mat = jnp.concatenate([u["a_bar"], u["r_bar"]], axis=0).astype(BF16)
        rhs = jnp.concatenate([_pair_diag(u["b_til"]), _pair_diag(u["k_til"])], axis=0)
        xbk = _dot(x_mat, rhs, NT)
        u["n_ab"] = jnp.where(u["strict"], xbk[:CHUNK, :LANES], 0.0)
        u["l_rb"] = jnp.where(u["incl"], xbk[CHUNK:, :LANES], 0.0)
        n_ak = jnp.where(u["strict"], xbk[:CHUNK, LANES:], 0.0)
        l_rk = jnp.where(u["incl"], xbk[CHUNK:, LANES:], 0.0)
        u["nl"] = jnp.concatenate([n_ak, l_rk], axis=0).astype(BF16)
    for u in units:
        nv = _dot(u["nl"], _pair_diag(u["v"]))
        u["w"], u["u0"], u["lrkv"] = u["a_bar"], nv[:CHUNK], nv[CHUNK:]
        u["npow"] = u["n_ab"].astype(BF16)

    steps = int(math.log2(CHUNK))
    for kstep in range(steps):
        for u in units:
            rhs = jnp.concatenate([_pair_diag(u["w"].astype(BF16)), _pair_diag(u["u0"].astype(BF16))], axis=1)
            upd = _dot(u["npow"], rhs)
            u["w"] = u["w"] + upd[:, :LANES]
            u["u0"] = u["u0"] + upd[:, LANES:]
        if kstep + 1 < steps:
            for u in units:
                u["npow"] = _dot(u["npow"], _pair_diag(u["npow"])).astype(BF16)

    for u in units:
        wb, ub = u["w"].astype(BF16), u["u0"].astype(BF16)
        lx = _dot(u["l_rb"].astype(BF16), jnp.concatenate([_pair_diag(wb), _pair_diag(ub)], axis=1))
        u["p_mat"] = u["r_bar"] + lx[:, :LANES]
        u["y0"] = u["lrkv"] + lx[:, LANES:]
        lhs = jnp.concatenate([u["b_hat"], u["k_hat"]], axis=0)
        rhs = jnp.concatenate([jnp.concatenate([wb, ub], axis=1),
                               jnp.concatenate([jnp.zeros_like(wb), u["v"]], axis=1)], axis=0)
        mg = _dot(lhs, rhs, TN)
        u["m_full"] = _pair_pick(mg[:, :LANES]) + jnp.where(eye, u["gam_c"], 0.0)
        u["g_mat"] = _pair_pick(mg[:, LANES:])

    for u in units:
        bi, d, p = u["bi"], u["d"], u["p"]
        h0 = h_ref[bi, d, p]
        a0, a1 = _split(jnp.concatenate([u["p_mat"], u["m_full"]], axis=0), 2)
        h_hi, h_lo = _split(h0, 2)
        bh, bl = _pair_diag(h_hi), _pair_diag(h_lo)
        out = _dot(a0, bh) + (_dot(a0, bl) + _dot(a1, bh))
        (yb_ref if d else yf_ref)[bi, :, u["sl"]] = out[:CHUNK] + u["y0"]
        h_ref[bi, d, p] = out[CHUNK:] + u["g_mat"]


def _rwkv_pairs(r, v, kk, ld, kd, beta, ctx_len):
    bn, s, _ = r.shape
    n_chunks = s // CHUNK
    ctx_chunks = ctx_len // CHUNK
    rows = RWKV_ROWS if bn % RWKV_ROWS == 0 else 1

    def back(c):
        return jnp.where(c < ctx_chunks, ctx_chunks - 1 - c, n_chunks - 1 + ctx_chunks - c)

    fwd = pl.BlockSpec((rows, CHUNK, B_WIDTH), lambda b, c: (b, c, 0))
    bwd = pl.BlockSpec((rows, CHUNK, B_WIDTH), lambda b, c: (b, back(c), 0))
    fwd_d = pl.BlockSpec((None, rows, CHUNK, B_WIDTH), lambda b, c: (0, b, c, 0))
    bwd_d = pl.BlockSpec((None, rows, CHUNK, B_WIDTH), lambda b, c: (1, b, back(c), 0))
    y = jax.ShapeDtypeStruct((bn, s, B_WIDTH), F32)
    return pl.pallas_call(
        _rwkv_pair_kernel,
        grid=(bn // rows, n_chunks),
        in_specs=[fwd, fwd, fwd, bwd, bwd, bwd, fwd_d, fwd_d, fwd_d, bwd_d, bwd_d, bwd_d],
        out_specs=[fwd, bwd],
        out_shape=[y, y],
        scratch_shapes=[pltpu.VMEM((rows, 2, B_WIDTH // LANES, B_HEAD, LANES), F32)],
        compiler_params=_params(("parallel", "arbitrary")),
        name="rwkv_scan",
    )(r, v, kk, r, v, kk, ld, kd, beta, ld, kd, beta)


def _outproj_kernel(x_ref, mod_ref, a_ref, yf_ref, yb_ref, bonus_ref, g_ref, c_ref,
                    wa_ref, wb_ref, wc_ref, lnxg_ref, lnxb_ref, ln1g_ref, ln1b_ref, o_ref, *, alpha):
    nb = x_ref.shape[0]
    stack = lambda ref: jnp.concatenate([ref[r] for r in range(nb)], axis=0)
    y = stack(yf_ref) + stack(yb_ref) + stack(bonus_ref)
    ones = _group_ones(B_WIDTH, B_HEAD)
    inv = 1.0 / B_HEAD
    mu = _dot_exact_rhs(y, ones) * inv
    yc = y - mu
    var = _dot_exact_rhs(yc * yc, ones) * inv
    yn = yc * lax.rsqrt(var + GN_EPS) * lnxg_ref[...] + lnxb_ref[...]
    bmix = (yn * stack(g_ref)).astype(BF16)
    o = _dot(stack(a_ref), wa_ref[...]) + _dot(bmix, wb_ref[...]) + _dot(stack(c_ref), wc_ref[...])
    for r in range(nb):
        o_ref[r] = (_ln(alpha * x_ref[r] + mod_ref[r, 2:3] * o[r * TM:(r + 1) * TM], LN_EPS) * ln1g_ref[...]
                    + ln1b_ref[...])


def _outproj(xs, mods, a_out, yf, yb, bonus, g, c_out, wa, wb, wc, lnxg, lnxb, ln1g, ln1b, ctx_len, first_tile,
             alpha):
    bn, s, d = xs.shape
    nb = _batch_rows(bn)
    row = lambda width: pl.BlockSpec((nb, TM, width), lambda b, i: (b, i + first_tile, 0))
    full = lambda a: pl.BlockSpec(a.shape, lambda b, i: (0,) * a.ndim)
    consts = (wa, wb, wc, lnxg, lnxb, ln1g, ln1b)
    return pl.pallas_call(
        functools.partial(_outproj_kernel, alpha=alpha),
        grid=(bn // nb, s // TM - first_tile),
        in_specs=[row(d), _mod_spec(ctx_len, d, nb, first_tile),
                  row(A_WIDTH), row(B_WIDTH), row(B_WIDTH), row(B_WIDTH), row(B_WIDTH), row(C_WIDTH)]
                 + [full(p) for p in consts],
        out_specs=row(d),
        out_shape=jax.ShapeDtypeStruct((bn, s, d), F32),
        compiler_params=_params(("parallel", "parallel")),
        name="outproj",
    )(xs, mods, a_out, yf, yb, bonus, g, c_out, *consts)


def _swiglu_rows(h, w1_ref, w3_ref, w2_ref):
    acc = jnp.zeros((h.shape[0], w2_ref.shape[-1]), F32)
    for j in range(w1_ref.shape[-1] // FF_CHUNK):
        sl = slice(j * FF_CHUNK, (j + 1) * FF_CHUNK)
        u = _dot(h, w1_ref[:, sl])
        t = _dot(h, w3_ref[:, sl])
        acc = acc + _dot((_silu(u) * t).astype(BF16), w2_ref[sl, :])
    return acc


def _ffn_kernel(x_ref, mod_ref, w1_ref, w3_ref, w2_ref, g_ref, b_ref, o_ref, *, alpha):
    nb = x_ref.shape[0]
    h = jnp.concatenate([(_ln(x_ref[r], LN_EPS) * (1.0 + mod_ref[r, 4:5]) + mod_ref[r, 3:4]).astype(BF16)
                         for r in range(nb)], axis=0)
    f = _swiglu_rows(h, w1_ref, w3_ref, w2_ref)
    for r in range(nb):
        o_ref[r] = (_ln(alpha * x_ref[r] + mod_ref[r, 5:6] * f[r * TM:(r + 1) * TM], LN_EPS) * g_ref[...]
                    + b_ref[...])


def _ffn(x1, mods, w1, w3, w2, g, b, ctx_len, alpha):
    bn, s, d = x1.shape
    nb = _batch_rows(bn)
    row = pl.BlockSpec((nb, TM, d), lambda bb, i: (bb, i, 0))
    resident = lambda a: pl.BlockSpec(a.shape, lambda bb, i: (0,) * a.ndim, pipeline_mode=pl.Buffered(1))
    full = lambda a: pl.BlockSpec(a.shape, lambda bb, i: (0,) * a.ndim)
    return pl.pallas_call(
        functools.partial(_ffn_kernel, alpha=alpha),
        grid=(bn // nb, s // TM),
        in_specs=[row, _mod_spec(ctx_len, d, nb),
                  resident(w1), resident(w3), resident(w2), full(g), full(b)],
        out_specs=row,
        out_shape=jax.ShapeDtypeStruct((bn, s, d), F32),
        compiler_params=_params(("parallel", "parallel")),
        name="ffn",
    )(x1, mods, w1, w3, w2, g, b)


def _moe_pre_kernel(x_ref, mod_ref, router_ref, h_ref, logit_ref):
    mod = mod_ref[...]
    h = _ln(x_ref[...], LN_EPS) * (1.0 + mod[4:5]) + mod[3:4]
    h_ref[...] = h.astype(h_ref.dtype)
    logit_ref[...] = _dot6(h, router_ref[...])


def _moe_pre(x1, mods, router_p, ctx_len):
    bn, s, d = x1.shape
    ct = ctx_len // TM
    lt = (s - ctx_len) // TM
    return pl.pallas_call(
        _moe_pre_kernel,
        grid=(bn, lt),
        in_specs=[pl.BlockSpec((None, TM, d), lambda b, i: (b, i + ct, 0)),
                  pl.BlockSpec((None, None, 6, d), lambda b, i: (b, 1, 0, 0)),
                  pl.BlockSpec(router_p.shape, lambda b, i: (0, 0))],
        out_specs=[pl.BlockSpec((TM, d), lambda b, i: (b * lt + i, 0)),
                   pl.BlockSpec((TM, LANES), lambda b, i: (b * lt + i, 0))],
        out_shape=[jax.ShapeDtypeStruct((bn * lt * TM, d), BF16),
                   jax.ShapeDtypeStruct((bn * lt * TM, LANES), F32)],
        compiler_params=_params(("parallel", "parallel")),
        name="moe_pre",
    )(x1, mods, router_p)


def _slot_onehot(slots_ref, block):
    sl = slots_ref[...]
    s_iota = lax.broadcasted_iota(jnp.int32, (MOE_ROWS, MOE_TILE), 0) + block * MOE_ROWS
    hit = jnp.logical_or(sl[0:1] == s_iota, sl[1:2] == s_iota)
    return jnp.where(hit, 1.0, 0.0).astype(BF16)


def _dispatch_kernel(wb_ref, wc_ref, wf_ref, wv_ref, slots_ref, h_ref, o_ref):
    w = pl.program_id(0)
    part = lambda: _dot(_slot_onehot(slots_ref, wb_ref[w]), h_ref[...])

    @pl.when(wf_ref[w] == 1)
    def _():
        o_ref[...] = part().astype(o_ref.dtype)

    @pl.when(jnp.logical_and(wf_ref[w] == 0, wv_ref[w] == 1))
    def _():
        o_ref[...] = (o_ref[...].astype(F32) + part()).astype(o_ref.dtype)


def _dispatch(h, slots, work):
    t, d = h.shape
    wb, wc, wf, wv = work
    n_slots = (t * TOP_K // MOE_ROWS + N_EXPERTS) * MOE_ROWS
    grid_spec = pltpu.PrefetchScalarGridSpec(
        num_scalar_prefetch=4,
        grid=(wb.shape[0],),
        in_specs=[pl.BlockSpec((TOP_K, MOE_TILE), lambda w, b, c, f, v: (0, c[w])),
                  pl.BlockSpec((MOE_TILE, d), lambda w, b, c, f, v: (c[w], 0))],
        out_specs=pl.BlockSpec((MOE_ROWS, d), lambda w, b, c, f, v: (b[w], 0)),
    )
    return pl.pallas_call(
        _dispatch_kernel,
        grid_spec=grid_spec,
        out_shape=jax.ShapeDtypeStruct((n_slots, d), BF16),
        compiler_params=_params(("arbitrary",)),
        name="moe_dispatch",
    )(wb, wc, wf, wv, slots, h)


def _collect_kernel(wb_ref, wc_ref, wf_ref, wv_ref, slots_ref, scol_ref, gate_ref, y_ref, o_ref):
    w = pl.program_id(0)

    def part():
        base = wb_ref[w] * MOE_ROWS
        scol = scol_ref[...] - base
        gates = jnp.where(jnp.logical_and(scol >= 0, scol < MOE_ROWS), gate_ref[...], 0.0)
        gate = jnp.sum(gates, axis=1, keepdims=True)
        return _dot(_slot_onehot(slots_ref, wb_ref[w]), y_ref[...], TN) * gate

    @pl.when(wf_ref[w] == 1)
    def _():
        o_ref[...] = part()

    @pl.when(jnp.logical_and(wf_ref[w] == 0, wv_ref[w] == 1))
    def _():
        o_ref[...] = o_ref[...] + part()


def _collect(y_buf, slots, slots_col, gates, work):
    n_slots, d = y_buf.shape
    t = slots.shape[1]
    wb, wc, wf, wv = work
    tile = lambda w, b, c, f, v: (c[w], 0)
    grid_spec = pltpu.PrefetchScalarGridSpec(
        num_scalar_prefetch=4,
        grid=(wb.shape[0],),
        in_specs=[pl.BlockSpec((TOP_K, MOE_TILE), lambda w, b, c, f, v: (0, c[w])),
                  pl.BlockSpec((MOE_TILE, TOP_K), tile), pl.BlockSpec((MOE_TILE, TOP_K), tile),
                  pl.BlockSpec((MOE_ROWS, d), lambda w, b, c, f, v: (b[w], 0))],
        out_specs=pl.BlockSpec((MOE_TILE, d), tile),
    )
    return pl.pallas_call(
        _collect_kernel,
        grid_spec=grid_spec,
        out_shape=jax.ShapeDtypeStruct((t, d), F32),
        compiler_params=_params(("arbitrary",)),
        name="moe_collect",
    )(wb, wc, wf, wv, slots, slots_col, gates, y_buf)


def _expert_kernel(be_ref, nb_ref, x_ref, w1_ref, w3_ref, w2_ref, o_ref):
    i = pl.program_id(0)

    @pl.when(i < nb_ref[0])
    def _():
        o_ref[...] = _swiglu_rows(x_ref[...], w1_ref, w3_ref, w2_ref).astype(o_ref.dtype)

    @pl.when(i >= nb_ref[0])
    def _():
        o_ref[...] = jnp.zeros_like(o_ref)


def _experts(buf, block_e, n_used, w1, w3, w2):
    n, d = buf.shape
    n_blocks = n // MOE_ROWS
    ff = w1.shape[-1]
    wspec = lambda shape: pl.BlockSpec((None,) + shape, lambda i, be, nb: (be[i], 0, 0),
                                       pipeline_mode=pl.Buffered(1))
    used = lambda i, be, nb: (jnp.minimum(i, nb[0] - 1), 0)
    grid_spec = pltpu.PrefetchScalarGridSpec(
        num_scalar_prefetch=2,
        grid=(n_blocks,),
        in_specs=[pl.BlockSpec((MOE_ROWS, d), used), wspec((d, ff)), wspec((d, ff)), wspec((ff, d))],
        out_specs=pl.BlockSpec((MOE_ROWS, d), lambda i, be, nb: (i, 0)),
    )
    return pl.pallas_call(
        _expert_kernel,
        grid_spec=grid_spec,
        out_shape=jax.ShapeDtypeStruct((n, d), BF16),
        compiler_params=_params(("arbitrary",)),
        name="experts",
    )(block_e, n_used, buf, w1, w3, w2)


def _combine_kernel(x_ref, mod_ref, y_ref, g_ref, b_ref, o_ref, *, alpha):
    mod = mod_ref[...]
    o_ref[...] = _ln(alpha * x_ref[...] + mod[5:6] * y_ref[...], LN_EPS) * g_ref[...] + b_ref[...]


def _combine(x1, mods, y_tok, g, b, ctx_len, alpha):
    bn, s, d = x1.shape
    ct = ctx_len // TM
    lt = (s - ctx_len) // TM
    full = lambda a: pl.BlockSpec(a.shape, lambda bb, i: (0,) * a.ndim)
    return pl.pallas_call(
        functools.partial(_combine_kernel, alpha=alpha),
        grid=(bn, lt),
        in_specs=[pl.BlockSpec((None, TM, d), lambda bb, i: (bb, i + ct, 0)),
                  pl.BlockSpec((None, None, 6, d), lambda bb, i: (bb, 1, 0, 0)),
                  pl.BlockSpec((TM, d), lambda bb, i: (bb * lt + i, 0)),
                  full(g), full(b)],
        out_specs=pl.BlockSpec((None, TM, d), lambda bb, i: (bb, i, 0)),
        out_shape=jax.ShapeDtypeStruct((bn, lt * TM, d), F32),
        compiler_params=_params(("parallel", "parallel")),
        name="moe_combine",
    )(x1, mods, y_tok, g, b)


def _work_lists(rank_at_tile, counts, pstart, block_e, n_used, n_blocks, n_tiles):
    n_work = n_tiles * N_EXPERTS + n_blocks
    blocks = jnp.arange(n_blocks, dtype=jnp.int32)
    used = blocks < n_used
    r0 = blocks * MOE_ROWS - pstart[block_e]
    r_last = jnp.minimum(r0 + MOE_ROWS, counts[block_e]) - 1
    cols = rank_at_tile.T[block_e]
    find = jax.vmap(lambda col, val: jnp.searchsorted(col, val, side="right"))
    lo = jnp.clip(find(cols, r0) - 1, 0, n_tiles - 1)
    hi = jnp.clip(find(cols, r_last) - 1, 0, n_tiles - 1)
    n_b = jnp.where(used, hi - lo + 1, 0)
    ends = jnp.cumsum(n_b)
    starts = ends - n_b
    total = ends[-1]
    w = jnp.arange(n_work, dtype=jnp.int32)
    valid = w < total
    wl = jnp.minimum(w, total - 1)
    blk = jnp.minimum(jnp.searchsorted(ends, wl, side="right"), n_blocks - 1).astype(jnp.int32)
    tile = (lo[blk] + (wl - starts[blk])).astype(jnp.int32)
    first = jnp.logical_and(valid, w == starts[blk])
    as_i32 = lambda a: a.astype(jnp.int32)
    by_block = (blk, tile, as_i32(first), as_i32(valid))
    order = jnp.argsort(jnp.where(valid, tile, n_tiles), stable=True)
    order = order[jnp.minimum(w, total - 1)]
    tile2, blk2 = tile[order], blk[order]
    first2 = jnp.logical_and(valid, jnp.concatenate([jnp.ones((1,), bool), tile2[1:] != tile2[:-1]]))
    by_tile = (blk2, tile2, as_i32(first2), as_i32(valid))
    return by_block, by_tile


def _moe_layer(x1, mods, router, w1, w3, w2, g, b, ctx_len, alpha):
    d = x1.shape[-1]
    router_p = jnp.pad(router, ((0, 0), (0, LANES - N_EXPERTS)))
    h, logits = _moe_pre(x1, mods, router_p, ctx_len)
    t = h.shape[0]
    top_v, top_i = lax.top_k(logits[:, :N_EXPERTS], TOP_K)
    gates = jax.nn.softmax(top_v, axis=-1)
    e_flat = top_i.reshape(-1)
    onehot = (e_flat[:, None] == jnp.arange(N_EXPERTS)[None, :]).astype(jnp.int32)
    ranks = jnp.cumsum(onehot, axis=0) - onehot
    rank = jnp.sum(ranks * onehot, axis=1)
    counts = jnp.sum(onehot, axis=0)
    padded = (counts + MOE_ROWS - 1) // MOE_ROWS * MOE_ROWS
    pend = jnp.cumsum(padded)
    pstart = pend - padded
    slot = (pstart[e_flat] + rank).astype(jnp.int32)
    n_blocks = t * TOP_K // MOE_ROWS + N_EXPERTS
    assert t % MOE_TILE == 0
    n_tiles = t // MOE_TILE
    block_e = jnp.minimum(jnp.searchsorted(pend, jnp.arange(n_blocks) * MOE_ROWS, side="right"),
                          N_EXPERTS - 1).astype(jnp.int32)
    n_used = (pend[-1:] // MOE_ROWS).astype(jnp.int32)
    rank_at_tile = jnp.concatenate([ranks[::MOE_TILE * TOP_K], counts[None, :]], axis=0)
    by_block, by_tile = _work_lists(rank_at_tile, counts, pstart, block_e, n_used, n_blocks, n_tiles)
    slots_col = slot.reshape(t, TOP_K)
    slots = slots_col.T
    buf = _dispatch(h, slots, by_block)
    y_buf = _experts(buf, block_e, n_used, w1, w3, w2)
    y_tok = _collect(y_buf, slots, slots_col, gates, by_tile)
    return _combine(x1, mods, y_tok, g, b, ctx_len, alpha)


def _rope_tables(n_rows, ctx_len, dim, lane_lo):
    quarter = dim // 4
    inv = ROPE_THETA ** (-jnp.arange(quarter, dtype=F32) / quarter)
    rows = jnp.repeat(jnp.arange(n_rows, dtype=F32), GRID_W)
    cols = jnp.tile(jnp.arange(GRID_W, dtype=F32), n_rows)
    ang = jnp.concatenate([rows[:, None] * inv, rows[:, None] * inv,
                           cols[:, None] * inv, cols[:, None] * inv], axis=-1)
    sign = jnp.tile(jnp.concatenate([-jnp.ones(quarter, F32), jnp.ones(quarter, F32)]), 2)
    cos, sin = jnp.cos(ang), jnp.sin(ang) * sign
    length = cos.shape[0]
    if lane_lo == 0:
        reps = LANES // dim
        cos, sin = jnp.tile(cos, (1, reps)), jnp.tile(sin, (1, reps))
    else:
        pad = ((0, 0), (lane_lo, LANES - lane_lo - dim))
        cos = jnp.pad(cos, pad, constant_values=1.0)
        sin = jnp.pad(sin, pad)
    cos = jnp.concatenate([jnp.ones((ctx_len, LANES), F32), cos], axis=0)
    sin = jnp.concatenate([jnp.zeros((ctx_len, LANES), F32), sin], axis=0)
    return cos, sin


def _block_diag2(w):
    z = jnp.zeros_like(w[0])
    return jnp.concatenate([jnp.concatenate([w[0], z], axis=1), jnp.concatenate([z, w[1]], axis=1)], axis=0)


def _lambda_init(layer):
    return 0.8 - 0.6 * math.exp(-0.3 * layer)


def kernel(x, c, ctx, c_ctx, ada_w, ada_b, w_in, w_out, ln1_g, ln1_b, ln2_g, ln2_b,
           lam_q1, lam_k1, lam_q2, lam_k2, diff_norm_g, shift_mu, w0, w2, a0, a2, g2,
           k_k, k_a, r_k, lnx_g, lnx_b, q_norm_g, w_uq, kv_norm_g, w_ukv,
           ff_w1, ff_w3, ff_w2, router, moe_w1, moe_w3, moe_w2):
    bn, seq, d = x.shape
    ctx_len = ctx.shape[1]
    depth = ada_w.shape[0]
    assert d == D_MODEL and seq % TM == 0 and ctx_len % TM == 0 and seq % GRID_W == 0
    alpha = (2.0 * depth) ** 0.25
    n_grid_rows = seq // GRID_W
    cos_a, sin_a = _rope_tables(n_grid_rows, ctx_len, A_QK_DIM, 0)
    cos_c, sin_c = _rope_tables(n_grid_rows, ctx_len, C_ROPE, C_NOPE)

    cond_rows = 8 * ((bn + 1 + 7) // 8)
    cond = jnp.zeros((cond_rows, d), F32).at[:bn].set(c).at[bn].set(c_ctx)
    xs = jnp.concatenate([ctx, x], axis=1)

    for i in range(depth):
        with_ctx = i < depth - 1
        m = _ada(cond, ada_w[i], ada_b[i]).reshape(cond_rows, 6, d)
        mods = jnp.stack([jnp.broadcast_to(m[bn], (bn, 6, d)), m[:bn]], axis=1)

        wc = w_in[i][:, N_A + N_B:]
        kpe_w = jnp.pad(wc[:, C_Q_RANK + C_KV_RANK:], ((0, 0), (C_NOPE, LANES - C_NOPE - C_ROPE)))
        w_p = jnp.concatenate([w_in[i][:, :N_A + N_B], wc[:, :C_Q_RANK + C_KV_RANK], kpe_w], axis=1).astype(BF16)
        wq = w_uq[i].reshape(C_Q_RANK, C_HEADS, C_NOPE + C_ROPE)
        wq_p = jnp.pad(wq, ((0, 0), (0, 0), (0, LANES - C_NOPE - C_ROPE))).reshape(C_Q_RANK, -1).astype(BF16)
        wkv = w_ukv[i].reshape(C_KV_RANK, C_HEADS, C_NOPE + C_V)
        wk_p = jnp.pad(wkv[:, :, :C_NOPE], ((0, 0), (0, 0), (0, LANES - C_NOPE))).reshape(C_KV_RANK, -1).astype(BF16)
        wv_p = wkv[:, :, C_NOPE:].reshape(C_KV_RANK, -1).astype(BF16)
        mla_consts = (q_norm_g[i].reshape(1, -1), kv_norm_g[i].reshape(1, -1), wq_p, wk_p, wv_p)
        qa, ka, va, pb, qc, kc, vc = _inproj(xs, mods, w_p, cos_a, sin_a, mla_consts, cos_c, sin_c, ctx_len)

        lam_p = jnp.stack([lam_q1[i], lam_k1[i], lam_q2[i], lam_k2[i]])
        g_col = jnp.broadcast_to(diff_norm_g[i][:, None], (A_V_DIM, TM))
        first_tile = 0 if with_ctx else ctx_len // TM
        a_out, c_out = _attention([("diff", qa, ka, va), ("mla", qc, kc, vc)], lam_p, g_col, ctx_len=ctx_len,
                                  first_tile=first_tile, lam_init=_lambda_init(i))
        r, v, kkn, ld, kd, beta, gate, bonus = _rwkv_prep(
            pb, shift_mu[i].reshape(1, -1), w0[i].reshape(1, -1), _block_diag2(w2[i]).astype(BF16),
            a0[i].reshape(1, -1), _block_diag2(a2[i]).astype(BF16), g2[i].astype(BF16),
            k_k[i].reshape(1, -1), k_a[i].reshape(1, -1), r_k[i].reshape(1, -1), ctx_len)
        yf, yb = _rwkv_pairs(r, v, kkn, ld, kd, beta, ctx_len)

        wo = w_out[i].astype(BF16)
        x1 = _outproj(xs, mods, a_out, yf, yb, bonus, gate, c_out,
                      wo[:A_WIDTH], wo[A_WIDTH:A_WIDTH + B_WIDTH], wo[A_WIDTH + B_WIDTH:],
                      lnx_g[i].reshape(1, -1), lnx_b[i].reshape(1, -1),
                      ln1_g[i].reshape(1, -1), ln1_b[i].reshape(1, -1), ctx_len, first_tile, alpha)

        j = i // 2
        g2n, b2n = ln2_g[i].reshape(1, -1), ln2_b[i].reshape(1, -1)
        if i % 2 == 0:
            xs = _ffn(x1, mods, ff_w1[j].astype(BF16), ff_w3[j].astype(BF16), ff_w2[j].astype(BF16),
                      g2n, b2n, ctx_len, alpha)
        else:
            if with_ctx:
                raise NotImplementedError("routed FFN on the context rows is not needed at this depth")
            return _moe_layer(x1, mods, router[j], moe_w1[j].astype(BF16), moe_w3[j].astype(BF16),
                              moe_w2[j].astype(BF16), g2n, b2n, ctx_len, alpha)
    return xs[:, ctx_len:]
```

```python
import functools
import math

import jax
import jax.numpy as jnp
from jax import lax
from jax.experimental import pallas as pl
from jax.experimental.pallas import tpu as pltpu

F32 = jnp.float32
BF16 = jnp.bfloat16

D_MODEL = 1024
GRID_W = 64
ROPE_THETA = 10000.0
A_HEADS, A_QK_DIM, A_V_DIM = 4, 64, 128
A_WIDTH = A_HEADS * A_V_DIM
A_QK_COLS = 2 * A_HEADS * A_QK_DIM
B_HEADS, B_HEAD = 4, 64
B_WIDTH = B_HEADS * B_HEAD
B_DECAY_RANK, B_A_RANK, B_GATE_RANK = 64, 64, 128
C_HEADS, C_NOPE, C_ROPE, C_V = 4, 64, 32, 64
C_WIDTH = C_HEADS * C_V
C_Q_RANK, C_KV_RANK = 256, 128
N_A = 2 * A_QK_COLS + A_WIDTH
N_B = 3 * B_WIDTH + 2 * B_DECAY_RANK + 2 * B_A_RANK + B_GATE_RANK
N_C = C_Q_RANK + C_KV_RANK + C_ROPE
N_C_PAD = C_Q_RANK + C_KV_RANK + 128
D_FF = 3584
N_EXPERTS = 8
TOP_K = 2
LN_EPS = 1e-6
RMS_EPS = 1e-6
GN_EPS = 64e-5

LOG2E = math.log2(math.e)
LANES = 128
TM = 256
CHUNK = 64
ROWS_B = 2
RWKV_ROWS = 4
ATT_TK = 2048
ONES_ROWS = 16
MOE_ROWS = 512
MOE_TILE = 2048
FF_CHUNK = 512
VMEM_LIMIT = 56 * 1024 * 1024

NN = (((1,), (0,)), ((), ()))
NT = (((1,), (1,)), ((), ()))
TN = (((0,), (0,)), ((), ()))


def _params(sem, vmem=VMEM_LIMIT, flags=None):
    return pltpu.CompilerParams(dimension_semantics=sem, vmem_limit_bytes=vmem, flags=flags)


def _split(x, n):
    parts, r = [], x
    for _ in range(n):
        p = r.astype(BF16)
        parts.append(p)
        r = r - p.astype(F32)
    return parts


def _dot(a, b, dn=NN):
    return lax.dot_general(a, b, dn, preferred_element_type=F32)


def _dot1(a, b, dn=NN):
    return _dot(a.astype(BF16), b.astype(BF16), dn)


def _dot3(a, b, dn=NN):
    a0, a1 = _split(a, 2)
    b0, b1 = _split(b, 2)
    return _dot(a0, b0, dn) + (_dot(a0, b1, dn) + _dot(a1, b0, dn))


def _dot6(a, b, dn=NN):
    a0, a1, a2 = _split(a, 3)
    b0, b1, b2 = _split(b, 3)
    lo = _dot(a1, b1, dn) + (_dot(a0, b2, dn) + _dot(a2, b0, dn))
    return _dot(a0, b0, dn) + ((_dot(a0, b1, dn) + _dot(a1, b0, dn)) + lo)


def _dot_exact_lhs(a_bf16, b, dn=NN, n=3):
    parts = _split(b, n)
    out = _dot(a_bf16, parts[-1], dn)
    for p in parts[-2::-1]:
        out = out + _dot(a_bf16, p, dn)
    return out


def _dot_exact_rhs(a, b_bf16, dn=NN, n=3):
    parts = _split(a, n)
    out = _dot(parts[-1], b_bf16, dn)
    for p in parts[-2::-1]:
        out = out + _dot(p, b_bf16, dn)
    return out


def _ln(x, eps):
    mu = jnp.mean(x, axis=-1, keepdims=True)
    xc = x - mu
    return xc * lax.rsqrt(jnp.mean(xc * xc, axis=-1, keepdims=True) + eps)


def _sigmoid(x):
    return 1.0 / (1.0 + jnp.exp(-x))


def _silu(x):
    return x * _sigmoid(x)


def _group_ones(width, group):
    r = lax.broadcasted_iota(jnp.int32, (width, width), 0) // group
    c = lax.broadcasted_iota(jnp.int32, (width, width), 1) // group
    return (r == c).astype(BF16)


def _partner(x, half):
    lane = lax.broadcasted_iota(jnp.int32, x.shape, 1)
    up = pltpu.roll(x, LANES - half, 1)
    dn = pltpu.roll(x, half, 1)
    return jnp.where((lane % (2 * half)) < half, up, dn)


def _rope(x, cos, sin, half):
    return x * cos + _partner(x, half) * sin


def _ada_kernel(c_ref, w_ref, b_ref, o_ref):
    o_ref[...] = _dot3(_silu(c_ref[...]), w_ref[...]) + b_ref[...]


def _ada(cond, w, b):
    rows, d = cond.shape
    n = w.shape[1]
    tn = 1536
    return pl.pallas_call(
        _ada_kernel,
        grid=(n // tn,),
        in_specs=[pl.BlockSpec((rows, d), lambda j: (0, 0)),
                  pl.BlockSpec((d, tn), lambda j: (0, j)),
                  pl.BlockSpec((1, tn), lambda j: (0, j))],
        out_specs=pl.BlockSpec((rows, tn), lambda j: (0, j)),
        out_shape=jax.ShapeDtypeStruct((rows, n), F32),
        compiler_params=_params(("parallel",)),
        name="ada",
    )(cond, w, b.reshape(1, n))


def _inproj_kernel(x_ref, mod_ref, w_ref, cos_ref, sin_ref,
                   qg_ref, kvg_ref, wq_ref, wk_ref, wv_ref, cosc_ref, sinc_ref,
                   q_ref, k_ref, v_ref, pb_ref, cq_ref, ck_ref, cv_ref):
    nb = x_ref.shape[0]
    h = jnp.concatenate([(_ln(x_ref[r], LN_EPS) * (1.0 + mod_ref[r, 1:2]) + mod_ref[r, 0:1]).astype(BF16)
                         for r in range(nb)], axis=0)
    cos, sin = cos_ref[...], sin_ref[...]
    scale = A_QK_DIM ** -0.5 * LOG2E
    rows = lambda a, r: a[r * TM:(r + 1) * TM]
    for j in range(A_QK_COLS // LANES):
        sl = slice(j * LANES, (j + 1) * LANES)
        qj = _dot(h, w_ref[:, sl])
        kj = _dot(h, w_ref[:, A_QK_COLS + j * LANES:A_QK_COLS + (j + 1) * LANES])
        vj = _dot(h, w_ref[:, 2 * A_QK_COLS + j * LANES:2 * A_QK_COLS + (j + 1) * LANES])
        for r in range(nb):
            q_ref[r, sl, :] = (_rope(rows(qj, r), cos, sin, A_QK_DIM // 4) * scale).T.astype(BF16)
            k_ref[r, :, sl] = _rope(rows(kj, r), cos, sin, A_QK_DIM // 4).astype(BF16)
            v_ref[r, sl, :] = rows(vj, r).T.astype(BF16)
    pb = _dot(h, w_ref[:, N_A:N_A + N_B])
    for r in range(nb):
        pb_ref[r] = rows(pb, r)
    _mla_heads(_dot(h, w_ref[:, N_A + N_B:]), nb, qg_ref, kvg_ref, wq_ref, wk_ref, wv_ref,
               cosc_ref[...], sinc_ref[...], cq_ref, ck_ref, cv_ref)


def _batch_rows(bn):
    return ROWS_B if bn % ROWS_B == 0 else 1


def _mod_spec(ctx_len, d, nb=None, first_tile=0):
    ct = ctx_len // TM
    return pl.BlockSpec((nb, None, 6, d), lambda b, i: (b, jnp.where(i + first_tile >= ct, 1, 0), 0, 0))


def _inproj(xs, mods, w_p, cos_a, sin_a, mla_consts, cos_c, sin_c, ctx_len):
    bn, s, d = xs.shape
    n_tiles = s // TM
    n_w = w_p.shape[1]
    nb = _batch_rows(bn)
    row = lambda width: pl.BlockSpec((nb, TM, width), lambda b, i: (b, i, 0))
    col = lambda width: pl.BlockSpec((nb, width, TM), lambda b, i: (b, 0, i))
    full = lambda a: pl.BlockSpec(a.shape, lambda b, i: (0,) * a.ndim)
    tab = pl.BlockSpec((TM, LANES), lambda b, i: (i, 0))
    hw = C_HEADS * LANES
    outs = [jax.ShapeDtypeStruct((bn, A_QK_COLS, s), BF16),
            jax.ShapeDtypeStruct((bn, s, A_QK_COLS), BF16),
            jax.ShapeDtypeStruct((bn, A_WIDTH, s), BF16),
            jax.ShapeDtypeStruct((bn, s, N_B), F32),
            jax.ShapeDtypeStruct((bn, hw, s), BF16),
            jax.ShapeDtypeStruct((bn, s, hw), BF16),
            jax.ShapeDtypeStruct((bn, C_WIDTH, s), BF16)]
    return pl.pallas_call(
        _inproj_kernel,
        grid=(bn // nb, n_tiles),
        in_specs=[row(d), _mod_spec(ctx_len, d, nb), full(w_p), tab, tab]
                 + [full(a) for a in mla_consts] + [tab, tab],
        out_specs=[col(A_QK_COLS), row(A_QK_COLS), col(A_WIDTH), row(N_B), col(hw), row(hw), col(C_WIDTH)],
        out_shape=outs,
        compiler_params=_params(("parallel", "parallel")),
        name="inproj",
    )(xs, mods, w_p, cos_a, sin_a, *mla_consts, cos_c, sin_c)


def _mla_heads(pc, nb, qg_ref, kvg_ref, wq_ref, wk_ref, wv_ref, cos, sin, q_ref, k_ref, v_ref):
    cq = pc[:, :C_Q_RANK]
    cq = cq * lax.rsqrt(jnp.mean(cq * cq, axis=-1, keepdims=True) + RMS_EPS) * qg_ref[...]
    ckv = pc[:, C_Q_RANK:C_Q_RANK + C_KV_RANK]
    ckv = ckv * lax.rsqrt(jnp.mean(ckv * ckv, axis=-1, keepdims=True) + RMS_EPS) * kvg_ref[...]
    cqb, ckvb = cq.astype(BF16), ckv.astype(BF16)
    scale = (C_NOPE + C_ROPE) ** -0.5 * LOG2E
    rows = lambda a, r: a[r * TM:(r + 1) * TM]
    kpe = [_rope(rows(pc, r)[:, C_Q_RANK + C_KV_RANK:], cos, sin, C_ROPE // 4) for r in range(nb)]
    for h in range(C_HEADS):
        sl = slice(h * LANES, (h + 1) * LANES)
        qh = _dot(cqb, wq_ref[:, sl])
        kh = _dot(ckvb, wk_ref[:, sl])
        for r in range(nb):
            q_ref[r, sl, :] = (_rope(rows(qh, r), cos, sin, C_ROPE // 4) * scale).T.astype(BF16)
            k_ref[r, :, sl] = (rows(kh, r) + kpe[r]).astype(BF16)
    for j in range(C_WIDTH // LANES):
        sl = slice(j * LANES, (j + 1) * LANES)
        vj = _dot(ckvb, wv_ref[:, sl])
        for r in range(nb):
            v_ref[r, sl, :] = rows(vj, r).T.astype(BF16)


def _attn_kernel(lam_ref, g_ref, *refs, modes, ctx_tiles, ctx_len, tk, first_tile, lam_init):
    n_sets = len(modes)
    in_refs = [refs[3 * t:3 * t + 3] for t in range(n_sets)]
    o_refs = refs[3 * n_sets:]
    qi = pl.program_id(1) + first_tile
    tq = in_refs[0][0].shape[1]
    half = lax.broadcasted_iota(jnp.int32, (LANES, 1), 0) < (LANES // 2)
    chains = []
    for t, (qt_ref, k_ref, vt_ref) in enumerate(in_refs):
        heads = qt_ref.shape[0] // LANES
        for g in range(heads):
            qt = qt_ref[g * LANES:(g + 1) * LANES, :]
            if modes[t] == "diff":
                qt = jnp.concatenate([jnp.where(half, qt, jnp.zeros_like(qt)),
                                      jnp.where(half, jnp.zeros_like(qt), qt)], axis=1)
            chains.append((t, g, qt, vt_ref.shape[0] // heads, qt.shape[1]))
    s_len = in_refs[0][1].shape[0]

    def fold(x, reduce):
        rows = x.shape[0]
        while rows > 8:
            g = max(d for d in range(2, 9) if (rows // 8) % d == 0)
            rows //= g
            x = reduce(x.reshape(g, rows, x.shape[1]), axis=0)
        return x

    def absorb(kcs, vcs, carry):
        n = len(chains)
        ss = [_dot(kcs[t][:, g * LANES:(g + 1) * LANES], q) for (t, g, q, _, _) in chains]
        m_new = [jnp.maximum(carry[c][0], jnp.max(fold(ss[c], jnp.max), axis=0, keepdims=True)) for c in range(n)]
        alpha = [jnp.exp2(carry[c][0] - m_new[c]) for c in range(n)]
        ps = [jnp.exp2(ss[c] - m_new[c]).astype(BF16) for c in range(n)]
        ones = jnp.ones((ONES_ROWS, kcs[0].shape[0]), BF16)
        acc = [alpha[c] * carry[c][1]
               + _dot(jnp.concatenate([vcs[t][g * dv:(g + 1) * dv], ones], axis=0), ps[c])
               for c, (t, g, _, dv, _) in enumerate(chains)]
        return tuple((m_new[c], acc[c]) for c in range(n))

    def finish(stats):
        outs = [a[:dv] / a[dv:dv + 1] for (_, a), (_, _, _, dv, _) in zip(stats, chains)]
        for t in range(n_sets):
            mine = [outs[c] for c, ch in enumerate(chains) if ch[0] == t]
            if modes[t] == "diff":
                lp = lam_ref[...]
                lam = (jnp.exp(jnp.sum(lp[0:1] * lp[1:2], axis=-1, keepdims=True))
                       - jnp.exp(jnp.sum(lp[2:3] * lp[3:4], axis=-1, keepdims=True)) + lam_init)
                for g, og in enumerate(mine):
                    o = og[:, :tq] - lam * og[:, tq:]
                    o = o * lax.rsqrt(jnp.mean(o * o, axis=0, keepdims=True) + RMS_EPS) * g_ref[...]
                    o = o * (1.0 - lam_init)
                    o_refs[t][:, g * LANES:(g + 1) * LANES] = o.T.astype(o_refs[t].dtype)
            else:
                per_slab = LANES // mine[0].shape[0]
                for j in range(len(mine) // per_slab):
                    o = jnp.concatenate(mine[j * per_slab:(j + 1) * per_slab], axis=0)
                    o_refs[t][:, j * LANES:(j + 1) * LANES] = o.T.astype(o_refs[t].dtype)

    def run(first_keys, n_more):
        init = tuple((jnp.full((1, w), -1e30, F32), jnp.zeros((dv + ONES_ROWS, w), F32))
                     for (_, _, _, dv, w) in chains)
        stats = absorb([r[1][0:first_keys, :] for r in in_refs], [r[2][:, 0:first_keys] for r in in_refs], init)

        def body(j, stats):
            off = pl.multiple_of(first_keys + j * tk, LANES)
            return absorb([r[1][pl.ds(off, tk), :] for r in in_refs],
                          [r[2][:, pl.ds(off, tk)] for r in in_refs], stats)

        finish(lax.fori_loop(0, n_more, body, stats))

    @pl.when(qi < ctx_tiles)
    def _():
        run(ctx_len, 0)

    @pl.when(qi >= ctx_tiles)
    def _():
        run(ctx_len + tk, (s_len - ctx_len) // tk - 1)


def _attention(sets, lam_p, g, *, ctx_len, first_tile, lam_init):
    bn, s, _ = sets[0][2].shape
    tk = math.gcd(s - ctx_len, ATT_TK)
    assert tk % LANES == 0 and ctx_len % LANES == 0
    kern = functools.partial(_attn_kernel, modes=tuple(m for m, _, _, _ in sets), ctx_tiles=ctx_len // TM,
                             ctx_len=ctx_len, tk=tk, first_tile=first_tile, lam_init=lam_init)
    in_specs = [pl.BlockSpec(lam_p.shape, lambda b, i: (0, 0)), pl.BlockSpec(g.shape, lambda b, i: (0, 0))]
    args, out_specs, out_shapes = [lam_p, g], [], []
    for _, qt, k, vt in sets:
        in_specs += [pl.BlockSpec((None, qt.shape[1], TM), lambda b, i: (b, 0, i + first_tile)),
                     pl.BlockSpec((None, s, k.shape[2]), lambda b, i: (b, 0, 0), pipeline_mode=pl.Buffered(1)),
                     pl.BlockSpec((None, vt.shape[1], s), lambda b, i: (b, 0, 0), pipeline_mode=pl.Buffered(1))]
        args += [qt, k, vt]
        out_specs.append(pl.BlockSpec((None, TM, vt.shape[1]), lambda b, i: (b, i + first_tile, 0)))
        out_shapes.append(jax.ShapeDtypeStruct((bn, s, vt.shape[1]), BF16))
    return pl.pallas_call(
        kern,
        grid=(bn, s // TM - first_tile),
        in_specs=in_specs,
        out_specs=out_specs,
        out_shape=out_shapes,
        compiler_params=_params(("parallel", "parallel")),
        name="attention",
    )(*args)


def _rwkv_prep_kernel(pb_ref, prev_ref, next_ref, mu_ref, w0_ref, w2_ref, a0_ref, a2_ref, g2_ref,
                      kk_ref, ka_ref, rk_ref,
                      r_out, v_out, kkn_out, ld_out, kd_out, beta_out, g_out, bonus_out,
                      *, ctx_tiles, n_tiles):
    i = pl.program_id(1)
    x = pb_ref[...]
    row = lax.broadcasted_iota(jnp.int32, (TM, 1), 0)
    has_prev = jnp.logical_and(i != 0, i != ctx_tiles)
    has_next = jnp.logical_and(i != ctx_tiles - 1, i != n_tiles - 1)
    prev_edge = jnp.where(has_prev, prev_ref[7:8, :], 0.0)
    next_edge = jnp.where(has_next, next_ref[0:1, :], 0.0)
    xp = jnp.where(row == 0, prev_edge, pltpu.roll(x, 1, 0))
    xn = jnp.where(row == TM - 1, next_edge, pltpu.roll(x, TM - 1, 0))
    z = x + mu_ref[...] * (0.5 * (xp + xn) - x)

    r = z[:, :B_WIDTH]
    k = z[:, B_WIDTH:2 * B_WIDTH]
    v = z[:, 2 * B_WIDTH:3 * B_WIDTH]
    o = 3 * B_WIDTH
    wd = z[:, o:o + 2 * B_DECAY_RANK]
    o += 2 * B_DECAY_RANK
    ad = z[:, o:o + 2 * B_A_RANK]
    o += 2 * B_A_RANK
    gd = z[:, o:]

    u = w0_ref[...] + _dot1(jnp.tanh(wd), w2_ref[...])
    ld = -math.exp(-0.5) * _sigmoid(u)
    lr = _sigmoid(a0_ref[...] + _dot1(ad, a2_ref[...]))
    g_out[...] = _dot1(_sigmoid(gd), g2_ref[...])

    ones = _group_ones(B_WIDTH, B_HEAD)
    kk = k * kk_ref[...]
    norm = jnp.sqrt(_dot_exact_rhs(kk * kk, ones))
    kkn = kk / jnp.maximum(norm, 1e-12)
    ka = ka_ref[...]
    kd_sum = jnp.zeros_like(k)
    for d in range(2):
        lr_d = lr[:, d * B_WIDTH:(d + 1) * B_WIDTH]
        kd = k * (1.0 + (lr_d - 1.0) * ka)
        kd_sum = kd_sum + kd
        ld_out[d] = ld[:, d * B_WIDTH:(d + 1) * B_WIDTH]
        kd_out[d] = kd
        beta_out[d] = kkn * lr_d
    bonus_out[...] = _dot_exact_rhs(r * kd_sum * rk_ref[...], ones) * v
    r_out[...] = r
    v_out[...] = v
    kkn_out[...] = kkn


def _rwkv_prep(pb, mu, w0, w2bd, a0, a2bd, g2, k_k, k_a, r_k, ctx_len):
    bn, s, _ = pb.shape
    n_tiles = s // TM
    eight = TM // 8
    row = lambda width: pl.BlockSpec((None, TM, width), lambda b, i: (b, i, 0))
    drow = pl.BlockSpec((2, None, TM, B_WIDTH), lambda b, i: (0, b, i, 0))
    full = lambda a: pl.BlockSpec(a.shape, lambda b, i: (0,) * a.ndim)
    one = jax.ShapeDtypeStruct((bn, s, B_WIDTH), F32)
    two = jax.ShapeDtypeStruct((2, bn, s, B_WIDTH), F32)
    kern = functools.partial(_rwkv_prep_kernel, ctx_tiles=ctx_len // TM, n_tiles=n_tiles)
    params = (mu, w0, w2bd, a0, a2bd, g2, k_k, k_a, r_k)
    return pl.pallas_call(
        kern,
        grid=(bn, n_tiles),
        in_specs=[row(N_B),
                  pl.BlockSpec((None, 8, N_B), lambda b, i: (b, jnp.maximum(i * eight - 1, 0), 0)),
                  pl.BlockSpec((None, 8, N_B), lambda b, i: (b, jnp.minimum((i + 1) * eight, s // 8 - 1), 0)),
                  ] + [full(p) for p in params],
        out_specs=[row(B_WIDTH), row(B_WIDTH), row(B_WIDTH), drow, drow, drow, row(B_WIDTH), row(B_WIDTH)],
        out_shape=[one, one, one, two, two, two, one, one],
        compiler_params=_params(("parallel", "parallel")),
        name="rwkv_prep",
    )(pb, pb, pb, *params)


def _pair_diag(x):
    lo = lax.broadcasted_iota(jnp.int32, (1, LANES), 1) < B_HEAD
    z = jnp.zeros_like(x)
    return jnp.concatenate([jnp.where(lo, x, z), jnp.where(lo, z, x)], axis=0)


def _pair_pick(x):
    lo = lax.broadcasted_iota(jnp.int32, (1, LANES), 1) < B_HEAD
    return jnp.where(lo, x[:B_HEAD], x[B_HEAD:])


def _rwkv_pair_kernel(rf_ref, vf_ref, kkf_ref, rb_ref, vb_ref, kkb_ref,
                      ldf_ref, kdf_ref, betaf_ref, ldb_ref, kdb_ref, betab_ref,
                      yf_ref, yb_ref, h_ref):
    c = pl.program_id(1)

    @pl.when(c == 0)
    def _():
        h_ref[...] = jnp.zeros_like(h_ref)

    n_pairs = B_WIDTH // LANES
    ti = lax.broadcasted_iota(jnp.int32, (CHUNK, LANES), 0)
    si = lax.broadcasted_iota(jnp.int32, (CHUNK, LANES), 1) % CHUNK
    t64 = lax.broadcasted_iota(jnp.int32, (CHUNK, CHUNK), 0)
    s64 = lax.broadcasted_iota(jnp.int32, (CHUNK, CHUNK), 1)
    eye = ti == si
    dirs = ((rf_ref, vf_ref, kkf_ref, ldf_ref, kdf_ref, betaf_ref, False),
            (rb_ref, vb_ref, kkb_ref, ldb_ref, kdb_ref, betab_ref, True))

    units = []
    n_rows = rf_ref.shape[0]
    for bi in range(n_rows):
      for d, (r_ref, v_ref, kk_ref, ld_ref, kd_ref, beta_ref, rev) in enumerate(dirs):
        strict = (si > ti) if rev else (ti > si)
        incl = (si >= ti) if rev else (ti >= si)
        tri = ((s64 >= t64) if rev else (t64 >= s64)).astype(BF16)
        ld_all = ld_ref[bi]
        cl_all = _dot_exact_lhs(tri, ld_all)
        for p in range(n_pairs):
            sl = slice(p * LANES, (p + 1) * LANES)
            ld, cl = ld_all[:, sl], cl_all[:, sl]
            total = jnp.sum(ld, axis=0, keepdims=True)
            inv_gam = jnp.exp(-cl)
            to_end = jnp.exp(total - cl)
            kk, kd, beta = kk_ref[bi, :, sl], kd_ref[bi, :, sl], beta_ref[bi, :, sl]
            units.append(dict(
                bi=bi, d=d, p=p, sl=sl, strict=strict, incl=incl, v=v_ref[bi, :, sl].astype(BF16),
                a_bar=-kk * jnp.exp(cl - ld), r_bar=r_ref[bi, :, sl] * jnp.exp(cl),
                b_til=(beta * inv_gam).astype(BF16), k_til=(kd * inv_gam).astype(BF16),
                b_hat=(beta * to_end).astype(BF16), k_hat=(kd * to_end).astype(BF16),
                gam_c=jnp.exp(total)))

    for u in units:
        x_mat = jnp.concatenate([u["a_bar"], u["r_bar"]], axis=0).astype(BF16)
        rhs = jnp.concatenate([_pair_diag(u["b_til"]), _pair_diag(u["k_til"])], axis=0)
        xbk = _dot(x_mat, rhs, NT)
        u["n_ab"] = jnp.where(u["strict"], xbk[:CHUNK, :LANES], 0.0)
        u["l_rb"] = jnp.where(u["incl"], xbk[CHUNK:, :LANES], 0.0)
        n_ak = jnp.where(u["strict"], xbk[:CHUNK, LANES:], 0.0)
        l_rk = jnp.where(u["incl"], xbk[CHUNK:, LANES:], 0.0)
        u["nl"] = jnp.concatenate([n_ak, l_rk], axis=0).astype(BF16)
    for u in units:
        nv = _dot(u["nl"], _pair_diag(u["v"]))
        u["w"], u["u0"], u["lrkv"] = u["a_bar"], nv[:CHUNK], nv[CHUNK:]
        u["npow"] = u["n_ab"].astype(BF16)

    steps = int(math.log2(CHUNK))
    for kstep in range(steps):
        for u in units:
            rhs = jnp.concatenate([_pair_diag(u["w"].astype(BF16)), _pair_diag(u["u0"].astype(BF16))], axis=1)
            upd = _dot(u["npow"], rhs)
            u["w"] = u["w"] + upd[:, :LANES]
            u["u0"] = u["u0"] + upd[:, LANES:]
        if kstep + 1 < steps:
            for u in units:
                u["npow"] = _dot(u["npow"], _pair_diag(u["npow"])).astype(BF16)

    for u in units:
        wb, ub = u["w"].astype(BF16), u["u0"].astype(BF16)
        lx = _dot(u["l_rb"].astype(BF16), jnp.concatenate([_pair_diag(wb), _pair_diag(ub)], axis=1))
        u["p_mat"] = u["r_bar"] + lx[:, :LANES]
        u["y0"] = u["lrkv"] + lx[:, LANES:]
        lhs = jnp.concatenate([u["b_hat"], u["k_hat"]], axis=0)
        rhs = jnp.concatenate([jnp.concatenate([wb, ub], axis=1),
                               jnp.concatenate([jnp.zeros_like(wb), u["v"]], axis=1)], axis=0)
        mg = _dot(lhs, rhs, TN)
        u["m_full"] = _pair_pick(mg[:, :LANES]) + jnp.where(eye, u["gam_c"], 0.0)
        u["g_mat"] = _pair_pick(mg[:, LANES:])

    for u in units:
        bi, d, p = u["bi"], u["d"], u["p"]
        h0 = h_ref[bi, d, p]
        a0, a1 = _split(jnp.concatenate([u["p_mat"], u["m_full"]], axis=0), 2)
        h_hi, h_lo = _split(h0, 2)
        bh, bl = _pair_diag(h_hi), _pair_diag(h_lo)
        out = _dot(a0, bh) + (_dot(a0, bl) + _dot(a1, bh))
        (yb_ref if d else yf_ref)[bi, :, u["sl"]] = out[:CHUNK] + u["y0"]
        h_ref[bi, d, p] = out[CHUNK:] + u["g_mat"]


def _rwkv_pairs(r, v, kk, ld, kd, beta, ctx_len):
    bn, s, _ = r.shape
    n_chunks = s // CHUNK
    ctx_chunks = ctx_len // CHUNK
    rows = RWKV_ROWS if bn % RWKV_ROWS == 0 else 1

    def back(c):
        return jnp.where(c < ctx_chunks, ctx_chunks - 1 - c, n_chunks - 1 + ctx_chunks - c)

    fwd = pl.BlockSpec((rows, CHUNK, B_WIDTH), lambda b, c: (b, c, 0))
    bwd = pl.BlockSpec((rows, CHUNK, B_WIDTH), lambda b, c: (b, back(c), 0))
    fwd_d = pl.BlockSpec((None, rows, CHUNK, B_WIDTH), lambda b, c: (0, b, c, 0))
    bwd_d = pl.BlockSpec((None, rows, CHUNK, B_WIDTH), lambda b, c: (1, b, back(c), 0))
    y = jax.ShapeDtypeStruct((bn, s, B_WIDTH), F32)
    return pl.pallas_call(
        _rwkv_pair_kernel,
        grid=(bn // rows, n_chunks),
        in_specs=[fwd, fwd, fwd, bwd, bwd, bwd, fwd_d, fwd_d, fwd_d, bwd_d, bwd_d, bwd_d],
        out_specs=[fwd, bwd],
        out_shape=[y, y],
        scratch_shapes=[pltpu.VMEM((rows, 2, B_WIDTH // LANES, B_HEAD, LANES), F32)],
        compiler_params=_params(("parallel", "arbitrary")),
        name="rwkv_scan",
    )(r, v, kk, r, v, kk, ld, kd, beta, ld, kd, beta)


def _outproj_kernel(x_ref, mod_ref, a_ref, yf_ref, yb_ref, bonus_ref, g_ref, c_ref,
                    wa_ref, wb_ref, wc_ref, lnxg_ref, lnxb_ref, ln1g_ref, ln1b_ref, o_ref, *, alpha):
    nb = x_ref.shape[0]
    stack = lambda ref: jnp.concatenate([ref[r] for r in range(nb)], axis=0)
    y = stack(yf_ref) + stack(yb_ref) + stack(bonus_ref)
    ones = _group_ones(B_WIDTH, B_HEAD)
    inv = 1.0 / B_HEAD
    mu = _dot_exact_rhs(y, ones) * inv
    yc = y - mu
    var = _dot_exact_rhs(yc * yc, ones) * inv
    yn = yc * lax.rsqrt(var + GN_EPS) * lnxg_ref[...] + lnxb_ref[...]
    bmix = (yn * stack(g_ref)).astype(BF16)
    o = _dot(stack(a_ref), wa_ref[...]) + _dot(bmix, wb_ref[...]) + _dot(stack(c_ref), wc_ref[...])
    for r in range(nb):
        o_ref[r] = (_ln(alpha * x_ref[r] + mod_ref[r, 2:3] * o[r * TM:(r + 1) * TM], LN_EPS) * ln1g_ref[...]
                    + ln1b_ref[...])


def _outproj(xs, mods, a_out, yf, yb, bonus, g, c_out, wa, wb, wc, lnxg, lnxb, ln1g, ln1b, ctx_len, first_tile,
             alpha):
    bn, s, d = xs.shape
    nb = _batch_rows(bn)
    row = lambda width: pl.BlockSpec((nb, TM, width), lambda b, i: (b, i + first_tile, 0))
    full = lambda a: pl.BlockSpec(a.shape, lambda b, i: (0,) * a.ndim)
    consts = (wa, wb, wc, lnxg, lnxb, ln1g, ln1b)
    return pl.pallas_call(
        functools.partial(_outproj_kernel, alpha=alpha),
        grid=(bn // nb, s // TM - first_tile),
        in_specs=[row(d), _mod_spec(ctx_len, d, nb, first_tile),
                  row(A_WIDTH), row(B_WIDTH), row(B_WIDTH), row(B_WIDTH), row(B_WIDTH), row(C_WIDTH)]
                 + [full(p) for p in consts],
        out_specs=row(d),
        out_shape=jax.ShapeDtypeStruct((bn, s, d), F32),
        compiler_params=_params(("parallel", "parallel")),
        name="outproj",
    )(xs, mods, a_out, yf, yb, bonus, g, c_out, *consts)


def _swiglu_rows(h, w1_ref, w3_ref, w2_ref):
    acc = jnp.zeros((h.shape[0], w2_ref.shape[-1]), F32)
    for j in range(w1_ref.shape[-1] // FF_CHUNK):
        sl = slice(j * FF_CHUNK, (j + 1) * FF_CHUNK)
        u = _dot(h, w1_ref[:, sl])
        t = _dot(h, w3_ref[:, sl])
        acc = acc + _dot((_silu(u) * t).astype(BF16), w2_ref[sl, :])
    return acc


def _ffn_kernel(x_ref, mod_ref, w1_ref, w3_ref, w2_ref, g_ref, b_ref, o_ref, *, alpha):
    nb = x_ref.shape[0]
    h = jnp.concatenate([(_ln(x_ref[r], LN_EPS) * (1.0 + mod_ref[r, 4:5]) + mod_ref[r, 3:4]).astype(BF16)
                         for r in range(nb)], axis=0)
    f = _swiglu_rows(h, w1_ref, w3_ref, w2_ref)
    for r in range(nb):
        o_ref[r] = (_ln(alpha * x_ref[r] + mod_ref[r, 5:6] * f[r * TM:(r + 1) * TM], LN_EPS) * g_ref[...]
                    + b_ref[...])


def _ffn(x1, mods, w1, w3, w2, g, b, ctx_len, alpha):
    bn, s, d = x1.shape
    nb = _batch_rows(bn)
    row = pl.BlockSpec((nb, TM, d), lambda bb, i: (bb, i, 0))
    resident = lambda a: pl.BlockSpec(a.shape, lambda bb, i: (0,) * a.ndim, pipeline_mode=pl.Buffered(1))
    full = lambda a: pl.BlockSpec(a.shape, lambda bb, i: (0,) * a.ndim)
    return pl.pallas_call(
        functools.partial(_ffn_kernel, alpha=alpha),
        grid=(bn // nb, s // TM),
        in_specs=[row, _mod_spec(ctx_len, d, nb),
                  resident(w1), resident(w3), resident(w2), full(g), full(b)],
        out_specs=row,
        out_shape=jax.ShapeDtypeStruct((bn, s, d), F32),
        compiler_params=_params(("parallel", "parallel")),
        name="ffn",
    )(x1, mods, w1, w3, w2, g, b)


def _moe_pre_kernel(x_ref, mod_ref, router_ref, h_ref, logit_ref):
    mod = mod_ref[...]
    h = _ln(x_ref[...], LN_EPS) * (1.0 + mod[4:5]) + mod[3:4]
    h_ref[...] = h.astype(h_ref.dtype)
    logit_ref[...] = _dot6(h, router_ref[...])


def _moe_pre(x1, mods, router_p, ctx_len):
    bn, s, d = x1.shape
    ct = ctx_len // TM
    lt = (s - ctx_len) // TM
    return pl.pallas_call(
        _moe_pre_kernel,
        grid=(bn, lt),
        in_specs=[pl.BlockSpec((None, TM, d), lambda b, i: (b, i + ct, 0)),
                  pl.BlockSpec((None, None, 6, d), lambda b, i: (b, 1, 0, 0)),
                  pl.BlockSpec(router_p.shape, lambda b, i: (0, 0))],
        out_specs=[pl.BlockSpec((TM, d), lambda b, i: (b * lt + i, 0)),
                   pl.BlockSpec((TM, LANES), lambda b, i: (b * lt + i, 0))],
        out_shape=[jax.ShapeDtypeStruct((bn * lt * TM, d), BF16),
                   jax.ShapeDtypeStruct((bn * lt * TM, LANES), F32)],
        compiler_params=_params(("parallel", "parallel")),
        name="moe_pre",
    )(x1, mods, router_p)


def _slot_onehot(slots_ref, block):
    sl = slots_ref[...]
    s_iota = lax.broadcasted_iota(jnp.int32, (MOE_ROWS, MOE_TILE), 0) + block * MOE_ROWS
    hit = jnp.logical_or(sl[0:1] == s_iota, sl[1:2] == s_iota)
    return jnp.where(hit, 1.0, 0.0).astype(BF16)


def _dispatch_kernel(wb_ref, wc_ref, wf_ref, wv_ref, slots_ref, h_ref, o_ref):
    w = pl.program_id(0)
    part = lambda: _dot(_slot_onehot(slots_ref, wb_ref[w]), h_ref[...])

    @pl.when(wf_ref[w] == 1)
    def _():
        o_ref[...] = part().astype(o_ref.dtype)

    @pl.when(jnp.logical_and(wf_ref[w] == 0, wv_ref[w] == 1))
    def _():
        o_ref[...] = (o_ref[...].astype(F32) + part()).astype(o_ref.dtype)


def _dispatch(h, slots, work):
    t, d = h.shape
    wb, wc, wf, wv = work
    n_slots = (t * TOP_K // MOE_ROWS + N_EXPERTS) * MOE_ROWS
    grid_spec = pltpu.PrefetchScalarGridSpec(
        num_scalar_prefetch=4,
        grid=(wb.shape[0],),
        in_specs=[pl.BlockSpec((TOP_K, MOE_TILE), lambda w, b, c, f, v: (0, c[w])),
                  pl.BlockSpec((MOE_TILE, d), lambda w, b, c, f, v: (c[w], 0))],
        out_specs=pl.BlockSpec((MOE_ROWS, d), lambda w, b, c, f, v: (b[w], 0)),
    )
    return pl.pallas_call(
        _dispatch_kernel,
        grid_spec=grid_spec,
        out_shape=jax.ShapeDtypeStruct((n_slots, d), BF16),
        compiler_params=_params(("arbitrary",)),
        name="moe_dispatch",
    )(wb, wc, wf, wv, slots, h)


def _collect_kernel(wb_ref, wc_ref, wf_ref, wv_ref, slots_ref, scol_ref, gate_ref, y_ref, o_ref):
    w = pl.program_id(0)

    def part():
        base = wb_ref[w] * MOE_ROWS
        scol = scol_ref[...] - base
        gates = jnp.where(jnp.logical_and(scol >= 0, scol < MOE_ROWS), gate_ref[...], 0.0)
        gate = jnp.sum(gates, axis=1, keepdims=True)
        return _dot(_slot_onehot(slots_ref, wb_ref[w]), y_ref[...], TN) * gate

    @pl.when(wf_ref[w] == 1)
    def _():
        o_ref[...] = part()

    @pl.when(jnp.logical_and(wf_ref[w] == 0, wv_ref[w] == 1))
    def _():
        o_ref[...] = o_ref[...] + part()


def _collect(y_buf, slots, slots_col, gates, work):
    n_slots, d = y_buf.shape
    t = slots.shape[1]
    wb, wc, wf, wv = work
    tile = lambda w, b, c, f, v: (c[w], 0)
    grid_spec = pltpu.PrefetchScalarGridSpec(
        num_scalar_prefetch=4,
        grid=(wb.shape[0],),
        in_specs=[pl.BlockSpec((TOP_K, MOE_TILE), lambda w, b, c, f, v: (0, c[w])),
                  pl.BlockSpec((MOE_TILE, TOP_K), tile), pl.BlockSpec((MOE_TILE, TOP_K), tile),
                  pl.BlockSpec((MOE_ROWS, d), lambda w, b, c, f, v: (b[w], 0))],
        out_specs=pl.BlockSpec((MOE_TILE, d), tile),
    )
    return pl.pallas_call(
        _collect_kernel,
        grid_spec=grid_spec,
        out_shape=jax.ShapeDtypeStruct((t, d), F32),
        compiler_params=_params(("arbitrary",)),
        name="moe_collect",
    )(wb, wc, wf, wv, slots, slots_col, gates, y_buf)


def _expert_kernel(be_ref, nb_ref, x_ref, w1_ref, w3_ref, w2_ref, o_ref):
    i = pl.program_id(0)

    @pl.when(i < nb_ref[0])
    def _():
        o_ref[...] = _swiglu_rows(x_ref[...], w1_ref, w3_ref, w2_ref).astype(o_ref.dtype)

    @pl.when(i >= nb_ref[0])
    def _():
        o_ref[...] = jnp.zeros_like(o_ref)


def _experts(buf, block_e, n_used, w1, w3, w2):
    n, d = buf.shape
    n_blocks = n // MOE_ROWS
    ff = w1.shape[-1]
    wspec = lambda shape: pl.BlockSpec((None,) + shape, lambda i, be, nb: (be[i], 0, 0),
                                       pipeline_mode=pl.Buffered(1))
    used = lambda i, be, nb: (jnp.minimum(i, nb[0] - 1), 0)
    grid_spec = pltpu.PrefetchScalarGridSpec(
        num_scalar_prefetch=2,
        grid=(n_blocks,),
        in_specs=[pl.BlockSpec((MOE_ROWS, d), used), wspec((d, ff)), wspec((d, ff)), wspec((ff, d))],
        out_specs=pl.BlockSpec((MOE_ROWS, d), lambda i, be, nb: (i, 0)),
    )
    return pl.pallas_call(
        _expert_kernel,
        grid_spec=grid_spec,
        out_shape=jax.ShapeDtypeStruct((n, d), BF16),
        compiler_params=_params(("arbitrary",)),
        name="experts",
    )(block_e, n_used, buf, w1, w3, w2)


def _combine_kernel(x_ref, mod_ref, y_ref, g_ref, b_ref, o_ref, *, alpha):
    mod = mod_ref[...]
    o_ref[...] = _ln(alpha * x_ref[...] + mod[5:6] * y_ref[...], LN_EPS) * g_ref[...] + b_ref[...]


def _combine(x1, mods, y_tok, g, b, ctx_len, alpha):
    bn, s, d = x1.shape
    ct = ctx_len // TM
    lt = (s - ctx_len) // TM
    full = lambda a: pl.BlockSpec(a.shape, lambda bb, i: (0,) * a.ndim)
    return pl.pallas_call(
        functools.partial(_combine_kernel, alpha=alpha),
        grid=(bn, lt),
        in_specs=[pl.BlockSpec((None, TM, d), lambda bb, i: (bb, i + ct, 0)),
                  pl.BlockSpec((None, None, 6, d), lambda bb, i: (bb, 1, 0, 0)),
                  pl.BlockSpec((TM, d), lambda bb, i: (bb * lt + i, 0)),
                  full(g), full(b)],
        out_specs=pl.BlockSpec((None, TM, d), lambda bb, i: (bb, i, 0)),
        out_shape=jax.ShapeDtypeStruct((bn, lt * TM, d), F32),
        compiler_params=_params(("parallel", "parallel")),
        name="moe_combine",
    )(x1, mods, y_tok, g, b)


def _work_lists(rank_at_tile, counts, pstart, block_e, n_used, n_blocks, n_tiles):
    n_work = n_tiles * N_EXPERTS + n_blocks
    blocks = jnp.arange(n_blocks, dtype=jnp.int32)
    used = blocks < n_used
    r0 = blocks * MOE_ROWS - pstart[block_e]
    r_last = jnp.minimum(r0 + MOE_ROWS, counts[block_e]) - 1
    cols = rank_at_tile.T[block_e]
    find = jax.vmap(lambda col, val: jnp.searchsorted(col, val, side="right"))
    lo = jnp.clip(find(cols, r0) - 1, 0, n_tiles - 1)
    hi = jnp.clip(find(cols, r_last) - 1, 0, n_tiles - 1)
    n_b = jnp.where(used, hi - lo + 1, 0)
    ends = jnp.cumsum(n_b)
    starts = ends - n_b
    total = ends[-1]
    w = jnp.arange(n_work, dtype=jnp.int32)
    valid = w < total
    wl = jnp.minimum(w, total - 1)
    blk = jnp.minimum(jnp.searchsorted(ends, wl, side="right"), n_blocks - 1).astype(jnp.int32)
    tile = (lo[blk] + (wl - starts[blk])).astype(jnp.int32)
    first = jnp.logical_and(valid, w == starts[blk])
    as_i32 = lambda a: a.astype(jnp.int32)
    by_block = (blk, tile, as_i32(first), as_i32(valid))
    order = jnp.argsort(jnp.where(valid, tile, n_tiles), stable=True)
    order = order[jnp.minimum(w, total - 1)]
    tile2, blk2 = tile[order], blk[order]
    first2 = jnp.logical_and(valid, jnp.concatenate([jnp.ones((1,), bool), tile2[1:] != tile2[:-1]]))
    by_tile = (blk2, tile2, as_i32(first2), as_i32(valid))
    return by_block, by_tile


def _moe_layer(x1, mods, router, w1, w3, w2, g, b, ctx_len, alpha):
    d = x1.shape[-1]
    router_p = jnp.pad(router, ((0, 0), (0, LANES - N_EXPERTS)))
    h, logits = _moe_pre(x1, mods, router_p, ctx_len)
    t = h.shape[0]
    top_v, top_i = lax.top_k(logits[:, :N_EXPERTS], TOP_K)
    gates = jax.nn.softmax(top_v, axis=-1)
    e_flat = top_i.reshape(-1)
    onehot = (e_flat[:, None] == jnp.arange(N_EXPERTS)[None, :]).astype(jnp.int32)
    ranks = jnp.cumsum(onehot, axis=0) - onehot
    rank = jnp.sum(ranks * onehot, axis=1)
    counts = jnp.sum(onehot, axis=0)
    padded = (counts + MOE_ROWS - 1) // MOE_ROWS * MOE_ROWS
    pend = jnp.cumsum(padded)
    pstart = pend - padded
    slot = (pstart[e_flat] + rank).astype(jnp.int32)
    n_blocks = t * TOP_K // MOE_ROWS + N_EXPERTS
    assert t % MOE_TILE == 0
    n_tiles = t // MOE_TILE
    block_e = jnp.minimum(jnp.searchsorted(pend, jnp.arange(n_blocks) * MOE_ROWS, side="right"),
                          N_EXPERTS - 1).astype(jnp.int32)
    n_used = (pend[-1:] // MOE_ROWS).astype(jnp.int32)
    rank_at_tile = jnp.concatenate([ranks[::MOE_TILE * TOP_K], counts[None, :]], axis=0)
    by_block, by_tile = _work_lists(rank_at_tile, counts, pstart, block_e, n_used, n_blocks, n_tiles)
    slots_col = slot.reshape(t, TOP_K)
    slots = slots_col.T
    buf = _dispatch(h, slots, by_block)
    y_buf = _experts(buf, block_e, n_used, w1, w3, w2)
    y_tok = _collect(y_buf, slots, slots_col, gates, by_tile)
    return _combine(x1, mods, y_tok, g, b, ctx_len, alpha)


def _rope_tables(n_rows, ctx_len, dim, lane_lo):
    quarter = dim // 4
    inv = ROPE_THETA ** (-jnp.arange(quarter, dtype=F32) / quarter)
    rows = jnp.repeat(jnp.arange(n_rows, dtype=F32), GRID_W)
    cols = jnp.tile(jnp.arange(GRID_W, dtype=F32), n_rows)
    ang = jnp.concatenate([rows[:, None] * inv, rows[:, None] * inv,
                           cols[:, None] * inv, cols[:, None] * inv], axis=-1)
    sign = jnp.tile(jnp.concatenate([-jnp.ones(quarter, F32), jnp.ones(quarter, F32)]), 2)
    cos, sin = jnp.cos(ang), jnp.sin(ang) * sign
    length = cos.shape[0]
    if lane_lo == 0:
        reps = LANES // dim
        cos, sin = jnp.tile(cos, (1, reps)), jnp.tile(sin, (1, reps))
    else:
        pad = ((0, 0), (lane_lo, LANES - lane_lo - dim))
        cos = jnp.pad(cos, pad, constant_values=1.0)
        sin = jnp.pad(sin, pad)
    cos = jnp.concatenate([jnp.ones((ctx_len, LANES), F32), cos], axis=0)
    sin = jnp.concatenate([jnp.zeros((ctx_len, LANES), F32), sin], axis=0)
    return cos, sin


def _block_diag2(w):
    z = jnp.zeros_like(w[0])
    return jnp.concatenate([jnp.concatenate([w[0], z], axis=1), jnp.concatenate([z, w[1]], axis=1)], axis=0)


def _lambda_init(layer):
    return 0.8 - 0.6 * math.exp(-0.3 * layer)


def kernel(x, c, ctx, c_ctx, ada_w, ada_b, w_in, w_out, ln1_g, ln1_b, ln2_g, ln2_b,
           lam_q1, lam_k1, lam_q2, lam_k2, diff_norm_g, shift_mu, w0, w2, a0, a2, g2,
           k_k, k_a, r_k, lnx_g, lnx_b, q_norm_g, w_uq, kv_norm_g, w_ukv,
           ff_w1, ff_w3, ff_w2, router, moe_w1, moe_w3, moe_w2):
    bn, seq, d = x.shape
    ctx_len = ctx.shape[1]
    depth = ada_w.shape[0]
    assert d == D_MODEL and seq % TM == 0 and ctx_len % TM == 0 and seq % GRID_W == 0
    alpha = (2.0 * depth) ** 0.25
    n_grid_rows = seq // GRID_W
    cos_a, sin_a = _rope_tables(n_grid_rows, ctx_len, A_QK_DIM, 0)
    cos_c, sin_c = _rope_tables(n_grid_rows, ctx_len, C_ROPE, C_NOPE)

    cond_rows = 8 * ((bn + 1 + 7) // 8)
    cond = jnp.zeros((cond_rows, d), F32).at[:bn].set(c).at[bn].set(c_ctx)
    xs = jnp.concatenate([ctx, x], axis=1)

    for i in range(depth):
        with_ctx = i < depth - 1
        m = _ada(cond, ada_w[i], ada_b[i]).reshape(cond_rows, 6, d)
        mods = jnp.stack([jnp.broadcast_to(m[bn], (bn, 6, d)), m[:bn]], axis=1)

        wc = w_in[i][:, N_A + N_B:]
        kpe_w = jnp.pad(wc[:, C_Q_RANK + C_KV_RANK:], ((0, 0), (C_NOPE, LANES - C_NOPE - C_ROPE)))
        w_p = jnp.concatenate([w_in[i][:, :N_A + N_B], wc[:, :C_Q_RANK + C_KV_RANK], kpe_w], axis=1).astype(BF16)
        wq = w_uq[i].reshape(C_Q_RANK, C_HEADS, C_NOPE + C_ROPE)
        wq_p = jnp.pad(wq, ((0, 0), (0, 0), (0, LANES - C_NOPE - C_ROPE))).reshape(C_Q_RANK, -1).astype(BF16)
        wkv = w_ukv[i].reshape(C_KV_RANK, C_HEADS, C_NOPE + C_V)
        wk_p = jnp.pad(wkv[:, :, :C_NOPE], ((0, 0), (0, 0), (0, LANES - C_NOPE))).reshape(C_KV_RANK, -1).astype(BF16)
        wv_p = wkv[:, :, C_NOPE:].reshape(C_KV_RANK, -1).astype(BF16)
        mla_consts = (q_norm_g[i].reshape(1, -1), kv_norm_g[i].reshape(1, -1), wq_p, wk_p, wv_p)
        qa, ka, va, pb, qc, kc, vc = _inproj(xs, mods, w_p, cos_a, sin_a, mla_consts, cos_c, sin_c, ctx_len)

        lam_p = jnp.stack([lam_q1[i], lam_k1[i], lam_q2[i], lam_k2[i]])
        g_col = jnp.broadcast_to(diff_norm_g[i][:, None], (A_V_DIM, TM))
        first_tile = 0 if with_ctx else ctx_len // TM
        a_out, c_out = _attention([("diff", qa, ka, va), ("mla", qc, kc, vc)], lam_p, g_col, ctx_len=ctx_len,
                                  first_tile=first_tile, lam_init=_lambda_init(i))
        r, v, kkn, ld, kd, beta, gate, bonus = _rwkv_prep(
            pb, shift_mu[i].reshape(1, -1), w0[i].reshape(1, -1), _block_diag2(w2[i]).astype(BF16),
            a0[i].reshape(1, -1), _block_diag2(a2[i]).astype(BF16), g2[i].astype(BF16),
            k_k[i].reshape(1, -1), k_a[i].reshape(1, -1), r_k[i].reshape(1, -1), ctx_len)
        yf, yb = _rwkv_pairs(r, v, kkn, ld, kd, beta, ctx_len)

        wo = w_out[i].astype(BF16)
        x1 = _outproj(xs, mods, a_out, yf, yb, bonus, gate, c_out,
                      wo[:A_WIDTH], wo[A_WIDTH:A_WIDTH + B_WIDTH], wo[A_WIDTH + B_WIDTH:],
                      lnx_g[i].reshape(1, -1), lnx_b[i].reshape(1, -1),
                      ln1_g[i].reshape(1, -1), ln1_b[i].reshape(1, -1), ctx_len, first_tile, alpha)

        j = i // 2
        g2n, b2n = ln2_g[i].reshape(1, -1), ln2_b[i].reshape(1, -1)
        if i % 2 == 0:
            xs = _ffn(x1, mods, ff_w1[j].astype(BF16), ff_w3[j].astype(BF16), ff_w2[j].astype(BF16),
                      g2n, b2n, ctx_len, alpha)
        else:
            if with_ctx:
                raise NotImplementedError("routed FFN on the context rows is not needed at this depth")
            return _moe_layer(x1, mods, router[j], moe_w1[j].astype(BF16), moe_w3[j].astype(BF16),
                              moe_w2[j].astype(BF16), g2n, b2n, ctx_len, alpha)
    return xs[:, ctx_len:]
```

```python
import functools
import math

import jax
import jax.numpy as jnp
from jax import lax
from jax.experimental import pallas as pl
from jax.experimental.pallas import tpu as pltpu

F32 = jnp.float32
BF16 = jnp.bfloat16

D_MODEL = 1024
GRID_W = 64
ROPE_THETA = 10000.0
A_HEADS, A_QK_DIM, A_V_DIM = 4, 64, 128
A_WIDTH = A_HEADS * A_V_DIM
A_QK_COLS = 2 * A_HEADS * A_QK_DIM
B_HEADS, B_HEAD = 4, 64
B_WIDTH = B_HEADS * B_HEAD
B_DECAY_RANK, B_A_RANK, B_GATE_RANK = 64, 64, 128
C_HEADS, C_NOPE, C_ROPE, C_V = 4, 64, 32, 64
C_WIDTH = C_HEADS * C_V
C_Q_RANK, C_KV_RANK = 256, 128
N_A = 2 * A_QK_COLS + A_WIDTH
N_B = 3 * B_WIDTH + 2 * B_DECAY_RANK + 2 * B_A_RANK + B_GATE_RANK
N_C = C_Q_RANK + C_KV_RANK + C_ROPE
N_C_PAD = C_Q_RANK + C_KV_RANK + 128
D_FF = 3584
N_EXPERTS = 8
TOP_K = 2
LN_EPS = 1e-6
RMS_EPS = 1e-6
GN_EPS = 64e-5

LOG2E = math.log2(math.e)
LANES = 128
TM = 256
CHUNK = 64
ROWS_B = 2
RWKV_ROWS = 4
ATT_TK = 2048
ONES_ROWS = 16
MOE_ROWS = 512
MOE_TILE = 1024
FF_CHUNK = 512
VMEM_LIMIT = 56 * 1024 * 1024

NN = (((1,), (0,)), ((), ()))
NT = (((1,), (1,)), ((), ()))
TN = (((0,), (0,)), ((), ()))


def _params(sem, vmem=VMEM_LIMIT, flags=None):
    return pltpu.CompilerParams(dimension_semantics=sem, vmem_limit_bytes=vmem, flags=flags)


def _split(x, n):
    parts, r = [], x
    for _ in range(n):
        p = r.astype(BF16)
        parts.append(p)
        r = r - p.astype(F32)
    return parts


def _dot(a, b, dn=NN):
    return lax.dot_general(a, b, dn, preferred_element_type=F32)


def _dot1(a, b, dn=NN):
    return _dot(a.astype(BF16), b.astype(BF16), dn)


def _dot3(a, b, dn=NN):
    a0, a1 = _split(a, 2)
    b0, b1 = _split(b, 2)
    return _dot(a0, b0, dn) + (_dot(a0, b1, dn) + _dot(a1, b0, dn))


def _dot6(a, b, dn=NN):
    a0, a1, a2 = _split(a, 3)
    b0, b1, b2 = _split(b, 3)
    lo = _dot(a1, b1, dn) + (_dot(a0, b2, dn) + _dot(a2, b0, dn))
    return _dot(a0, b0, dn) + ((_dot(a0, b1, dn) + _dot(a1, b0, dn)) + lo)


def _dot_exact_lhs(a_bf16, b, dn=NN, n=3):
    parts = _split(b, n)
    out = _dot(a_bf16, parts[-1], dn)
    for p in parts[-2::-1]:
        out = out + _dot(a_bf16, p, dn)
    return out


def _dot_exact_rhs(a, b_bf16, dn=NN, n=3):
    parts = _split(a, n)
    out = _dot(parts[-1], b_bf16, dn)
    for p in parts[-2::-1]:
        out = out + _dot(p, b_bf16, dn)
    return out


def _ln(x, eps):
    mu = jnp.mean(x, axis=-1, keepdims=True)
    xc = x - mu
    return xc * lax.rsqrt(jnp.mean(xc * xc, axis=-1, keepdims=True) + eps)


def _sigmoid(x):
    return 1.0 / (1.0 + jnp.exp(-x))


def _silu(x):
    return x * _sigmoid(x)


def _group_ones(width, group):
    r = lax.broadcasted_iota(jnp.int32, (width, width), 0) // group
    c = lax.broadcasted_iota(jnp.int32, (width, width), 1) // group
    return (r == c).astype(BF16)


def _partner(x, half):
    lane = lax.broadcasted_iota(jnp.int32, x.shape, 1)
    up = pltpu.roll(x, LANES - half, 1)
    dn = pltpu.roll(x, half, 1)
    return jnp.where((lane % (2 * half)) < half, up, dn)


def _rope(x, cos, sin, half):
    return x * cos + _partner(x, half) * sin


def _ada_kernel(c_ref, w_ref, b_ref, o_ref):
    o_ref[...] = _dot3(_silu(c_ref[...]), w_ref[...]) + b_ref[...]


def _ada(cond, w, b):
    rows, d = cond.shape
    n = w.shape[1]
    tn = 1536
    return pl.pallas_call(
        _ada_kernel,
        grid=(n // tn,),
        in_specs=[pl.BlockSpec((rows, d), lambda j: (0, 0)),
                  pl.BlockSpec((d, tn), lambda j: (0, j)),
                  pl.BlockSpec((1, tn), lambda j: (0, j))],
        out_specs=pl.BlockSpec((rows, tn), lambda j: (0, j)),
        out_shape=jax.ShapeDtypeStruct((rows, n), F32),
        compiler_params=_params(("parallel",)),
        name="ada",
    )(cond, w, b.reshape(1, n))


def _inproj_kernel(x_ref, mod_ref, w_ref, cos_ref, sin_ref,
                   qg_ref, kvg_ref, wq_ref, wk_ref, wv_ref, cosc_ref, sinc_ref,
                   q_ref, k_ref, v_ref, pb_ref, cq_ref, ck_ref, cv_ref):
    nb = x_ref.shape[0]
    h = jnp.concatenate([(_ln(x_ref[r], LN_EPS) * (1.0 + mod_ref[r, 1:2]) + mod_ref[r, 0:1]).astype(BF16)
                         for r in range(nb)], axis=0)
    cos, sin = cos_ref[...], sin_ref[...]
    scale = A_QK_DIM ** -0.5 * LOG2E
    rows = lambda a, r: a[r * TM:(r + 1) * TM]
    for j in range(A_QK_COLS // LANES):
        sl = slice(j * LANES, (j + 1) * LANES)
        qj = _dot(h, w_ref[:, sl])
        kj = _dot(h, w_ref[:, A_QK_COLS + j * LANES:A_QK_COLS + (j + 1) * LANES])
        vj = _dot(h, w_ref[:, 2 * A_QK_COLS + j * LANES:2 * A_QK_COLS + (j + 1) * LANES])
        for r in range(nb):
            q_ref[r, sl, :] = (_rope(rows(qj, r), cos, sin, A_QK_DIM // 4) * scale).T.astype(BF16)
            k_ref[r, :, sl] = _rope(rows(kj, r), cos, sin, A_QK_DIM // 4).astype(BF16)
            v_ref[r, sl, :] = rows(vj, r).T.astype(BF16)
    pb = _dot(h, w_ref[:, N_A:N_A + N_B])
    for r in range(nb):
        pb_ref[r] = rows(pb, r)
    _mla_heads(_dot(h, w_ref[:, N_A + N_B:]), nb, qg_ref, kvg_ref, wq_ref, wk_ref, wv_ref,
               cosc_ref[...], sinc_ref[...], cq_ref, ck_ref, cv_ref)


def _batch_rows(bn):
    return ROWS_B if bn % ROWS_B == 0 else 1


def _mod_spec(ctx_len, d, nb=None, first_tile=0):
    ct = ctx_len // TM
    return pl.BlockSpec((nb, None, 6, d), lambda b, i: (b, jnp.where(i + first_tile >= ct, 1, 0), 0, 0))


def _inproj(xs, mods, w_p, cos_a, sin_a, mla_consts, cos_c, sin_c, ctx_len):
    bn, s, d = xs.shape
    n_tiles = s // TM
    n_w = w_p.shape[1]
    nb = _batch_rows(bn)
    row = lambda width: pl.BlockSpec((nb, TM, width), lambda b, i: (b, i, 0))
    col = lambda width: pl.BlockSpec((nb, width, TM), lambda b, i: (b, 0, i))
    full = lambda a: pl.BlockSpec(a.shape, lambda b, i: (0,) * a.ndim)
    tab = pl.BlockSpec((TM, LANES), lambda b, i: (i, 0))
    hw = C_HEADS * LANES
    outs = [jax.ShapeDtypeStruct((bn, A_QK_COLS, s), BF16),
            jax.ShapeDtypeStruct((bn, s, A_QK_COLS), BF16),
            jax.ShapeDtypeStruct((bn, A_WIDTH, s), BF16),
            jax.ShapeDtypeStruct((bn, s, N_B), F32),
            jax.ShapeDtypeStruct((bn, hw, s), BF16),
            jax.ShapeDtypeStruct((bn, s, hw), BF16),
            jax.ShapeDtypeStruct((bn, C_WIDTH, s), BF16)]
    return pl.pallas_call(
        _inproj_kernel,
        grid=(bn // nb, n_tiles),
        in_specs=[row(d), _mod_spec(ctx_len, d, nb), full(w_p), tab, tab]
                 + [full(a) for a in mla_consts] + [tab, tab],
        out_specs=[col(A_QK_COLS), row(A_QK_COLS), col(A_WIDTH), row(N_B), col(hw), row(hw), col(C_WIDTH)],
        out_shape=outs,
        compiler_params=_params(("parallel", "parallel")),
        name="inproj",
    )(xs, mods, w_p, cos_a, sin_a, *mla_consts, cos_c, sin_c)


def _mla_heads(pc, nb, qg_ref, kvg_ref, wq_ref, wk_ref, wv_ref, cos, sin, q_ref, k_ref, v_ref):
    cq = pc[:, :C_Q_RANK]
    cq = cq * lax.rsqrt(jnp.mean(cq * cq, axis=-1, keepdims=True) + RMS_EPS) * qg_ref[...]
    ckv = pc[:, C_Q_RANK:C_Q_RANK + C_KV_RANK]
    ckv = ckv * lax.rsqrt(jnp.mean(ckv * ckv, axis=-1, keepdims=True) + RMS_EPS) * kvg_ref[...]
    cqb, ckvb = cq.astype(BF16), ckv.astype(BF16)
    scale = (C_NOPE + C_ROPE) ** -0.5 * LOG2E
    rows = lambda a, r: a[r * TM:(r + 1) * TM]
    kpe = [_rope(rows(pc, r)[:, C_Q_RANK + C_KV_RANK:], cos, sin, C_ROPE // 4) for r in range(nb)]
    for h in range(C_HEADS):
        sl = slice(h * LANES, (h + 1) * LANES)
        qh = _dot(cqb, wq_ref[:, sl])
        kh = _dot(ckvb, wk_ref[:, sl])
        for r in range(nb):
            q_ref[r, sl, :] = (_rope(rows(qh, r), cos, sin, C_ROPE // 4) * scale).T.astype(BF16)
            k_ref[r, :, sl] = (rows(kh, r) + kpe[r]).astype(BF16)
    for j in range(C_WIDTH // LANES):
        sl = slice(j * LANES, (j + 1) * LANES)
        vj = _dot(ckvb, wv_ref[:, sl])
        for r in range(nb):
            v_ref[r, sl, :] = rows(vj, r).T.astype(BF16)


def _attn_kernel(lam_ref, g_ref, *refs, modes, ctx_tiles, ctx_len, tk, first_tile, lam_init):
    n_sets = len(modes)
    in_refs = [refs[3 * t:3 * t + 3] for t in range(n_sets)]
    o_refs = refs[3 * n_sets:]
    qi = pl.program_id(1) + first_tile
    tq = in_refs[0][0].shape[1]
    half = lax.broadcasted_iota(jnp.int32, (LANES, 1), 0) < (LANES // 2)
    chains = []
    for t, (qt_ref, k_ref, vt_ref) in enumerate(in_refs):
        heads = qt_ref.shape[0] // LANES
        for g in range(heads):
            qt = qt_ref[g * LANES:(g + 1) * LANES, :]
            if modes[t] == "diff":
                qt = jnp.concatenate([jnp.where(half, qt, jnp.zeros_like(qt)),
                                      jnp.where(half, jnp.zeros_like(qt), qt)], axis=1)
            chains.append((t, g, qt, vt_ref.shape[0] // heads, qt.shape[1]))
    s_len = in_refs[0][1].shape[0]

    def fold(x, reduce):
        rows = x.shape[0]
        while rows > 8:
            g = max(d for d in range(2, 9) if (rows // 8) % d == 0)
            rows //= g
            x = reduce(x.reshape(g, rows, x.shape[1]), axis=0)
        return x

    def absorb(kcs, vcs, carry):
        n = len(chains)
        ss = [_dot(kcs[t][:, g * LANES:(g + 1) * LANES], q) for (t, g, q, _, _) in chains]
        m_new = [jnp.maximum(carry[c][0], jnp.max(fold(ss[c], jnp.max), axis=0, keepdims=True)) for c in range(n)]
        alpha = [jnp.exp2(carry[c][0] - m_new[c]) for c in range(n)]
        ps = [jnp.exp2(ss[c] - m_new[c]).astype(BF16) for c in range(n)]
        ones = jnp.ones((ONES_ROWS, kcs[0].shape[0]), BF16)
        acc = [alpha[c] * carry[c][1]
               + _dot(jnp.concatenate([vcs[t][g * dv:(g + 1) * dv], ones], axis=0), ps[c])
               for c, (t, g, _, dv, _) in enumerate(chains)]
        return tuple((m_new[c], acc[c]) for c in range(n))

    def finish(stats):
        outs = [a[:dv] / a[dv:dv + 1] for (_, a), (_, _, _, dv, _) in zip(stats, chains)]
        for t in range(n_sets):
            mine = [outs[c] for c, ch in enumerate(chains) if ch[0] == t]
            if modes[t] == "diff":
                lp = lam_ref[...]
                lam = (jnp.exp(jnp.sum(lp[0:1] * lp[1:2], axis=-1, keepdims=True))
                       - jnp.exp(jnp.sum(lp[2:3] * lp[3:4], axis=-1, keepdims=True)) + lam_init)
                for g, og in enumerate(mine):
                    o = og[:, :tq] - lam * og[:, tq:]
                    o = o * lax.rsqrt(jnp.mean(o * o, axis=0, keepdims=True) + RMS_EPS) * g_ref[...]
                    o = o * (1.0 - lam_init)
                    o_refs[t][:, g * LANES:(g + 1) * LANES] = o.T.astype(o_refs[t].dtype)
            else:
                per_slab = LANES // mine[0].shape[0]
                for j in range(len(mine) // per_slab):
                    o = jnp.concatenate(mine[j * per_slab:(j + 1) * per_slab], axis=0)
                    o_refs[t][:, j * LANES:(j + 1) * LANES] = o.T.astype(o_refs[t].dtype)

    def run(first_keys, n_more):
        init = tuple((jnp.full((1, w), -1e30, F32), jnp.zeros((dv + ONES_ROWS, w), F32))
                     for (_, _, _, dv, w) in chains)
        stats = absorb([r[1][0:first_keys, :] for r in in_refs], [r[2][:, 0:first_keys] for r in in_refs], init)

        def body(j, stats):
            off = pl.multiple_of(first_keys + j * tk, LANES)
            return absorb([r[1][pl.ds(off, tk), :] for r in in_refs],
                          [r[2][:, pl.ds(off, tk)] for r in in_refs], stats)

        finish(lax.fori_loop(0, n_more, body, stats))

    @pl.when(qi < ctx_tiles)
    def _():
        run(ctx_len, 0)

    @pl.when(qi >= ctx_tiles)
    def _():
        run(ctx_len + tk, (s_len - ctx_len) // tk - 1)


def _attention(sets, lam_p, g, *, ctx_len, first_tile, lam_init):
    bn, s, _ = sets[0][2].shape
    tk = math.gcd(s - ctx_len, ATT_TK)
    assert tk % LANES == 0 and ctx_len % LANES == 0
    kern = functools.partial(_attn_kernel, modes=tuple(m for m, _, _, _ in sets), ctx_tiles=ctx_len // TM,
                             ctx_len=ctx_len, tk=tk, first_tile=first_tile, lam_init=lam_init)
    in_specs = [pl.BlockSpec(lam_p.shape, lambda b, i: (0, 0)), pl.BlockSpec(g.shape, lambda b, i: (0, 0))]
    args, out_specs, out_shapes = [lam_p, g], [], []
    for _, qt, k, vt in sets:
        in_specs += [pl.BlockSpec((None, qt.shape[1], TM), lambda b, i: (b, 0, i + first_tile)),
                     pl.BlockSpec((None, s, k.shape[2]), lambda b, i: (b, 0, 0), pipeline_mode=pl.Buffered(1)),
                     pl.BlockSpec((None, vt.shape[1], s), lambda b, i: (b, 0, 0), pipeline_mode=pl.Buffered(1))]
        args += [qt, k, vt]
        out_specs.append(pl.BlockSpec((None, TM, vt.shape[1]), lambda b, i: (b, i + first_tile, 0)))
        out_shapes.append(jax.ShapeDtypeStruct((bn, s, vt.shape[1]), BF16))
    return pl.pallas_call(
        kern,
        grid=(bn, s // TM - first_tile),
        in_specs=in_specs,
        out_specs=out_specs,
        out_shape=out_shapes,
        compiler_params=_params(("parallel", "parallel")),
        name="attention",
    )(*args)


def _rwkv_prep_kernel(pb_ref, prev_ref, next_ref, mu_ref, w0_ref, w2_ref, a0_ref, a2_ref, g2_ref,
                      kk_ref, ka_ref, rk_ref,
                      r_out, v_out, kkn_out, ld_out, kd_out, beta_out, g_out, bonus_out,
                      *, ctx_tiles, n_tiles):
    i = pl.program_id(1)
    x = pb_ref[...]
    row = lax.broadcasted_iota(jnp.int32, (TM, 1), 0)
    has_prev = jnp.logical_and(i != 0, i != ctx_tiles)
    has_next = jnp.logical_and(i != ctx_tiles - 1, i != n_tiles - 1)
    prev_edge = jnp.where(has_prev, prev_ref[7:8, :], 0.0)
    next_edge = jnp.where(has_next, next_ref[0:1, :], 0.0)
    xp = jnp.where(row == 0, prev_edge, pltpu.roll(x, 1, 0))
    xn = jnp.where(row == TM - 1, next_edge, pltpu.roll(x, TM - 1, 0))
    z = x + mu_ref[...] * (0.5 * (xp + xn) - x)

    r = z[:, :B_WIDTH]
    k = z[:, B_WIDTH:2 * B_WIDTH]
    v = z[:, 2 * B_WIDTH:3 * B_WIDTH]
    o = 3 * B_WIDTH
    wd = z[:, o:o + 2 * B_DECAY_RANK]
    o += 2 * B_DECAY_RANK
    ad = z[:, o:o + 2 * B_A_RANK]
    o += 2 * B_A_RANK
    gd = z[:, o:]

    u = w0_ref[...] + _dot1(jnp.tanh(wd), w2_ref[...])
    ld = -math.exp(-0.5) * _sigmoid(u)
    lr = _sigmoid(a0_ref[...] + _dot1(ad, a2_ref[...]))
    g_out[...] = _dot1(_sigmoid(gd), g2_ref[...])

    ones = _group_ones(B_WIDTH, B_HEAD)
    kk = k * kk_ref[...]
    norm = jnp.sqrt(_dot_exact_rhs(kk * kk, ones))
    kkn = kk / jnp.maximum(norm, 1e-12)
    ka = ka_ref[...]
    kd_sum = jnp.zeros_like(k)
    for d in range(2):
        lr_d = lr[:, d * B_WIDTH:(d + 1) * B_WIDTH]
        kd = k * (1.0 + (lr_d - 1.0) * ka)
        kd_sum = kd_sum + kd
        ld_out[d] = ld[:, d * B_WIDTH:(d + 1) * B_WIDTH]
        kd_out[d] = kd
        beta_out[d] = kkn * lr_d
    bonus_out[...] = _dot_exact_rhs(r * kd_sum * rk_ref[...], ones) * v
    r_out[...] = r
    v_out[...] = v
    kkn_out[...] = kkn


def _rwkv_prep(pb, mu, w0, w2bd, a0, a2bd, g2, k_k, k_a, r_k, ctx_len):
    bn, s, _ = pb.shape
    n_tiles = s // TM
    eight = TM // 8
    row = lambda width: pl.BlockSpec((None, TM, width), lambda b, i: (b, i, 0))
    drow = pl.BlockSpec((2, None, TM, B_WIDTH), lambda b, i: (0, b, i, 0))
    full = lambda a: pl.BlockSpec(a.shape, lambda b, i: (0,) * a.ndim)
    one = jax.ShapeDtypeStruct((bn, s, B_WIDTH), F32)
    two = jax.ShapeDtypeStruct((2, bn, s, B_WIDTH), F32)
    kern = functools.partial(_rwkv_prep_kernel, ctx_tiles=ctx_len // TM, n_tiles=n_tiles)
    params = (mu, w0, w2bd, a0, a2bd, g2, k_k, k_a, r_k)
    return pl.pallas_call(
        kern,
        grid=(bn, n_tiles),
        in_specs=[row(N_B),
                  pl.BlockSpec((None, 8, N_B), lambda b, i: (b, jnp.maximum(i * eight - 1, 0), 0)),
                  pl.BlockSpec((None, 8, N_B), lambda b, i: (b, jnp.minimum((i + 1) * eight, s // 8 - 1), 0)),
                  ] + [full(p) for p in params],
        out_specs=[row(B_WIDTH), row(B_WIDTH), row(B_WIDTH), drow, drow, drow, row(B_WIDTH), row(B_WIDTH)],
        out_shape=[one, one, one, two, two, two, one, one],
        compiler_params=_params(("parallel", "parallel")),
        name="rwkv_prep",
    )(pb, pb, pb, *params)


def _pair_diag(x):
    lo = lax.broadcasted_iota(jnp.int32, (1, LANES), 1) < B_HEAD
    z = jnp.zeros_like(x)
    return jnp.concatenate([jnp.where(lo, x, z), jnp.where(lo, z, x)], axis=0)


def _pair_pick(x):
    lo = lax.broadcasted_iota(jnp.int32, (1, LANES), 1) < B_HEAD
    return jnp.where(lo, x[:B_HEAD], x[B_HEAD:])


def _rwkv_pair_kernel(rf_ref, vf_ref, kkf_ref, rb_ref, vb_ref, kkb_ref,
                      ldf_ref, kdf_ref, betaf_ref, ldb_ref, kdb_ref, betab_ref,
                      yf_ref, yb_ref, h_ref):
    c = pl.program_id(1)

    @pl.when(c == 0)
    def _():
        h_ref[...] = jnp.zeros_like(h_ref)

    n_pairs = B_WIDTH // LANES
    ti = lax.broadcasted_iota(jnp.int32, (CHUNK, LANES), 0)
    si = lax.broadcasted_iota(jnp.int32, (CHUNK, LANES), 1) % CHUNK
    t64 = lax.broadcasted_iota(jnp.int32, (CHUNK, CHUNK), 0)
    s64 = lax.broadcasted_iota(jnp.int32, (CHUNK, CHUNK), 1)
    eye = ti == si
    dirs = ((rf_ref, vf_ref, kkf_ref, ldf_ref, kdf_ref, betaf_ref, False),
            (rb_ref, vb_ref, kkb_ref, ldb_ref, kdb_ref, betab_ref, True))

    units = []
    n_rows = rf_ref.shape[0]
    for bi in range(n_rows):
      for d, (r_ref, v_ref, kk_ref, ld_ref, kd_ref, beta_ref, rev) in enumerate(dirs):
        strict = (si > ti) if rev else (ti > si)
        incl = (si >= ti) if rev else (ti >= si)
        tri = ((s64 >= t64) if rev else (t64 >= s64)).astype(BF16)
        ld_all = ld_ref[bi]
        cl_all = _dot_exact_lhs(tri, ld_all)
        for p in range(n_pairs):
            sl = slice(p * LANES, (p + 1) * LANES)
            ld, cl = ld_all[:, sl], cl_all[:, sl]
            total = jnp.sum(ld, axis=0, keepdims=True)
            inv_gam = jnp.exp(-cl)
            to_end = jnp.exp(total - cl)
            kk, kd, beta = kk_ref[bi, :, sl], kd_ref[bi, :, sl], beta_ref[bi, :, sl]
            units.append(dict(
                bi=bi, d=d, p=p, sl=sl, strict=strict, incl=incl, v=v_ref[bi, :, sl].astype(BF16),
                a_bar=-kk * jnp.exp(cl - ld), r_bar=r_ref[bi, :, sl] * jnp.exp(cl),
                b_til=(beta * inv_gam).astype(BF16), k_til=(kd * inv_gam).astype(BF16),
                b_hat=(beta * to_end).astype(BF16), k_hat=(kd * to_end).astype(BF16),
                gam_c=jnp.exp(total)))

    for u in units:
        x_mat = jnp.concatenate([u["a_bar"], u["r_bar"]], axis=0).astype(BF16)
        rhs = jnp.concatenate([_pair_diag(u["b_til"]), _pair_diag(u["k_til"])], axis=0)
        xbk = _dot(x_mat, rhs, NT)
        u["n_ab"] = jnp.where(u["strict"], xbk[:CHUNK, :LANES], 0.0)
        u["l_rb"] = jnp.where(u["incl"], xbk[CHUNK:, :LANES], 0.0)
        n_ak = jnp.where(u["strict"], xbk[:CHUNK, LANES:], 0.0)
        l_rk = jnp.where(u["incl"], xbk[CHUNK:, LANES:], 0.0)
        u["nl"] = jnp.concatenate([n_ak, l_rk], axis=0).astype(BF16)
    for u in units:
        nv = _dot(u["nl"], _pair_diag(u["v"]))
        u["w"], u["u0"], u["lrkv"] = u["a_bar"], nv[:CHUNK], nv[CHUNK:]
        u["npow"] = u["n_ab"].astype(BF16)

    steps = int(math.log2(CHUNK))
    for kstep in range(steps):
        for u in units:
            rhs = jnp.concatenate([_pair_diag(u["w"].astype(BF16)), _pair_diag(u["u0"].astype(BF16))], axis=1)
            upd = _dot(u["npow"], rhs)
            u["w"] = u["w"] + upd[:, :LANES]
            u["u0"] = u["u0"] + upd[:, LANES:]
        if kstep + 1 < steps:
            for u in units:
                u["npow"] = _dot(u["npow"], _pair_diag(u["npow"])).astype(BF16)

    for u in units:
        wb, ub = u["w"].astype(BF16), u["u0"].astype(BF16)
        lx = _dot(u["l_rb"].astype(BF16), jnp.concatenate([_pair_diag(wb), _pair_diag(ub)], axis=1))
        u["p_mat"] = u["r_bar"] + lx[:, :LANES]
        u["y0"] = u["lrkv"] + lx[:, LANES:]
        lhs = jnp.concatenate([u["b_hat"], u["k_hat"]], axis=0)
        rhs = jnp.concatenate([jnp.concatenate([wb, ub], axis=1),
                               jnp.concatenate([jnp.zeros_like(wb), u["v"]], axis=1)], axis=0)
        mg = _dot(lhs, rhs, TN)
        u["m_full"] = _pair_pick(mg[:, :LANES]) + jnp.where(eye, u["gam_c"], 0.0)
        u["g_mat"] = _pair_pick(mg[:, LANES:])

    for u in units:
        bi, d, p = u["bi"], u["d"], u["p"]
        h0 = h_ref[bi, d, p]
        a0, a1 = _split(jnp.concatenate([u["p_mat"], u["m_full"]], axis=0), 2)
        h_hi, h_lo = _split(h0, 2)
        bh, bl = _pair_diag(h_hi), _pair_diag(h_lo)
        out = _dot(a0, bh) + (_dot(a0, bl) + _dot(a1, bh))
        (yb_ref if d else yf_ref)[bi, :, u["sl"]] = out[:CHUNK] + u["y0"]
        h_ref[bi, d, p] = out[CHUNK:] + u["g_mat"]


def _rwkv_pairs(r, v, kk, ld, kd, beta, ctx_len):
    bn, s, _ = r.shape
    n_chunks = s // CHUNK
    ctx_chunks = ctx_len // CHUNK
    rows = RWKV_ROWS if bn % RWKV_ROWS == 0 else 1

    def back(c):
        return jnp.where(c < ctx_chunks, ctx_chunks - 1 - c, n_chunks - 1 + ctx_chunks - c)

    fwd = pl.BlockSpec((rows, CHUNK, B_WIDTH), lambda b, c: (b, c, 0))
    bwd = pl.BlockSpec((rows, CHUNK, B_WIDTH), lambda b, c: (b, back(c), 0))
    fwd_d = pl.BlockSpec((None, rows, CHUNK, B_WIDTH), lambda b, c: (0, b, c, 0))
    bwd_d = pl.BlockSpec((None, rows, CHUNK, B_WIDTH), lambda b, c: (1, b, back(c), 0))
    y = jax.ShapeDtypeStruct((bn, s, B_WIDTH), F32)
    return pl.pallas_call(
        _rwkv_pair_kernel,
        grid=(bn // rows, n_chunks),
        in_specs=[fwd, fwd, fwd, bwd, bwd, bwd, fwd_d, fwd_d, fwd_d, bwd_d, bwd_d, bwd_d],
        out_specs=[fwd, bwd],
        out_shape=[y, y],
        scratch_shapes=[pltpu.VMEM((rows, 2, B_WIDTH // LANES, B_HEAD, LANES), F32)],
        compiler_params=_params(("parallel", "arbitrary")),
        name="rwkv_scan",
    )(r, v, kk, r, v, kk, ld, kd, beta, ld, kd, beta)


def _outproj_kernel(x_ref, mod_ref, a_ref, yf_ref, yb_ref, bonus_ref, g_ref, c_ref,
                    wa_ref, wb_ref, wc_ref, lnxg_ref, lnxb_ref, ln1g_ref, ln1b_ref, o_ref, *, alpha):
    nb = x_ref.shape[0]
    stack = lambda ref: jnp.concatenate([ref[r] for r in range(nb)], axis=0)
    y = stack(yf_ref) + stack(yb_ref) + stack(bonus_ref)
    ones = _group_ones(B_WIDTH, B_HEAD)
    inv = 1.0 / B_HEAD
    mu = _dot_exact_rhs(y, ones) * inv
    yc = y - mu
    var = _dot_exact_rhs(yc * yc, ones) * inv
    yn = yc * lax.rsqrt(var + GN_EPS) * lnxg_ref[...] + lnxb_ref[...]
    bmix = (yn * stack(g_ref)).astype(BF16)
    o = _dot(stack(a_ref), wa_ref[...]) + _dot(bmix, wb_ref[...]) + _dot(stack(c_ref), wc_ref[...])
    for r in range(nb):
        o_ref[r] = (_ln(alpha * x_ref[r] + mod_ref[r, 2:3] * o[r * TM:(r + 1) * TM], LN_EPS) * ln1g_ref[...]
                    + ln1b_ref[...])


def _outproj(xs, mods, a_out, yf, yb, bonus, g, c_out, wa, wb, wc, lnxg, lnxb, ln1g, ln1b, ctx_len, first_tile,
             alpha):
    bn, s, d = xs.shape
    nb = _batch_rows(bn)
    row = lambda width: pl.BlockSpec((nb, TM, width), lambda b, i: (b, i + first_tile, 0))
    full = lambda a: pl.BlockSpec(a.shape, lambda b, i: (0,) * a.ndim)
    consts = (wa, wb, wc, lnxg, lnxb, ln1g, ln1b)
    return pl.pallas_call(
        functools.partial(_outproj_kernel, alpha=alpha),
        grid=(bn // nb, s // TM - first_tile),
        in_specs=[row(d), _mod_spec(ctx_len, d, nb, first_tile),
                  row(A_WIDTH), row(B_WIDTH), row(B_WIDTH), row(B_WIDTH), row(B_WIDTH), row(C_WIDTH)]
                 + [full(p) for p in consts],
        out_specs=row(d),
        out_shape=jax.ShapeDtypeStruct((bn, s, d), F32),
        compiler_params=_params(("parallel", "parallel")),
        name="outproj",
    )(xs, mods, a_out, yf, yb, bonus, g, c_out, *consts)


def _swiglu_rows(h, w1_ref, w3_ref, w2_ref):
    acc = jnp.zeros((h.shape[0], w2_ref.shape[-1]), F32)
    for j in range(w1_ref.shape[-1] // FF_CHUNK):
        sl = slice(j * FF_CHUNK, (j + 1) * FF_CHUNK)
        u = _dot(h, w1_ref[:, sl])
        t = _dot(h, w3_ref[:, sl])
        acc = acc + _dot((_silu(u) * t).astype(BF16), w2_ref[sl, :])
    return acc


def _ffn_kernel(x_ref, mod_ref, w1_ref, w3_ref, w2_ref, g_ref, b_ref, o_ref, *, alpha):
    nb = x_ref.shape[0]
    h = jnp.concatenate([(_ln(x_ref[r], LN_EPS) * (1.0 + mod_ref[r, 4:5]) + mod_ref[r, 3:4]).astype(BF16)
                         for r in range(nb)], axis=0)
    f = _swiglu_rows(h, w1_ref, w3_ref, w2_ref)
    for r in range(nb):
        o_ref[r] = (_ln(alpha * x_ref[r] + mod_ref[r, 5:6] * f[r * TM:(r + 1) * TM], LN_EPS) * g_ref[...]
                    + b_ref[...])


def _ffn(x1, mods, w1, w3, w2, g, b, ctx_len, alpha):
    bn, s, d = x1.shape
    nb = _batch_rows(bn)
    row = pl.BlockSpec((nb, TM, d), lambda bb, i: (bb, i, 0))
    resident = lambda a: pl.BlockSpec(a.shape, lambda bb, i: (0,) * a.ndim, pipeline_mode=pl.Buffered(1))
    full = lambda a: pl.BlockSpec(a.shape, lambda bb, i: (0,) * a.ndim)
    return pl.pallas_call(
        functools.partial(_ffn_kernel, alpha=alpha),
        grid=(bn // nb, s // TM),
        in_specs=[row, _mod_spec(ctx_len, d, nb),
                  resident(w1), resident(w3), resident(w2), full(g), full(b)],
        out_specs=row,
        out_shape=jax.ShapeDtypeStruct((bn, s, d), F32),
        compiler_params=_params(("parallel", "parallel")),
        name="ffn",
    )(x1, mods, w1, w3, w2, g, b)


def _moe_pre_kernel(x_ref, mod_ref, router_ref, h_ref, logit_ref):
    mod = mod_ref[...]
    h = _ln(x_ref[...], LN_EPS) * (1.0 + mod[4:5]) + mod[3:4]
    h_ref[...] = h.astype(h_ref.dtype)
    logit_ref[...] = _dot6(h, router_ref[...])


def _moe_pre(x1, mods, router_p, ctx_len):
    bn, s, d = x1.shape
    ct = ctx_len // TM
    lt = (s - ctx_len) // TM
    return pl.pallas_call(
        _moe_pre_kernel,
        grid=(bn, lt),
        in_specs=[pl.BlockSpec((None, TM, d), lambda b, i: (b, i + ct, 0)),
                  pl.BlockSpec((None, None, 6, d), lambda b, i: (b, 1, 0, 0)),
                  pl.BlockSpec(router_p.shape, lambda b, i: (0, 0))],
        out_specs=[pl.BlockSpec((TM, d), lambda b, i: (b * lt + i, 0)),
                   pl.BlockSpec((TM, LANES), lambda b, i: (b * lt + i, 0))],
        out_shape=[jax.ShapeDtypeStruct((bn * lt * TM, d), BF16),
                   jax.ShapeDtypeStruct((bn * lt * TM, LANES), F32)],
        compiler_params=_params(("parallel", "parallel")),
        name="moe_pre",
    )(x1, mods, router_p)


def _slot_onehot(slots_ref, block):
    sl = slots_ref[...]
    s_iota = lax.broadcasted_iota(jnp.int32, (MOE_ROWS, MOE_TILE), 0) + block * MOE_ROWS
    hit = jnp.logical_or(sl[0:1] == s_iota, sl[1:2] == s_iota)
    return jnp.where(hit, 1.0, 0.0).astype(BF16)


def _dispatch_kernel(wb_ref, wc_ref, wf_ref, wv_ref, slots_ref, h_ref, o_ref):
    w = pl.program_id(0)
    part = lambda: _dot(_slot_onehot(slots_ref, wb_ref[w]), h_ref[...])

    @pl.when(wf_ref[w] == 1)
    def _():
        o_ref[...] = part().astype(o_ref.dtype)

    @pl.when(jnp.logical_and(wf_ref[w] == 0, wv_ref[w] == 1))
    def _():
        o_ref[...] = (o_ref[...].astype(F32) + part()).astype(o_ref.dtype)


def _dispatch(h, slots, work):
    t, d = h.shape
    wb, wc, wf, wv = work
    n_slots = (t * TOP_K // MOE_ROWS + N_EXPERTS) * MOE_ROWS
    grid_spec = pltpu.PrefetchScalarGridSpec(
        num_scalar_prefetch=4,
        grid=(wb.shape[0],),
        in_specs=[pl.BlockSpec((TOP_K, MOE_TILE), lambda w, b, c, f, v: (0, c[w])),
                  pl.BlockSpec((MOE_TILE, d), lambda w, b, c, f, v: (c[w], 0))],
        out_specs=pl.BlockSpec((MOE_ROWS, d), lambda w, b, c, f, v: (b[w], 0)),
    )
    return pl.pallas_call(
        _dispatch_kernel,
        grid_spec=grid_spec,
        out_shape=jax.ShapeDtypeStruct((n_slots, d), BF16),
        compiler_params=_params(("arbitrary",)),
        name="moe_dispatch",
    )(wb, wc, wf, wv, slots, h)


def _collect_kernel(wb_ref, wc_ref, wf_ref, wv_ref, slots_ref, scol_ref, gate_ref, y_ref, o_ref):
    w = pl.program_id(0)

    def part():
        base = wb_ref[w] * MOE_ROWS
        scol = scol_ref[...] - base
        gates = jnp.where(jnp.logical_and(scol >= 0, scol < MOE_ROWS), gate_ref[...], 0.0)
        gate = jnp.sum(gates, axis=1, keepdims=True)
        return _dot(_slot_onehot(slots_ref, wb_ref[w]), y_ref[...], TN) * gate

    @pl.when(wf_ref[w] == 1)
    def _():
        o_ref[...] = part()

    @pl.when(jnp.logical_and(wf_ref[w] == 0, wv_ref[w] == 1))
    def _():
        o_ref[...] = o_ref[...] + part()


def _collect(y_buf, slots, slots_col, gates, work):
    n_slots, d = y_buf.shape
    t = slots.shape[1]
    wb, wc, wf, wv = work
    tile = lambda w, b, c, f, v: (c[w], 0)
    grid_spec = pltpu.PrefetchScalarGridSpec(
        num_scalar_prefetch=4,
        grid=(wb.shape[0],),
        in_specs=[pl.BlockSpec((TOP_K, MOE_TILE), lambda w, b, c, f, v: (0, c[w])),
                  pl.BlockSpec((MOE_TILE, TOP_K), tile), pl.BlockSpec((MOE_TILE, TOP_K), tile),
                  pl.BlockSpec((MOE_ROWS, d), lambda w, b, c, f, v: (b[w], 0))],
        out_specs=pl.BlockSpec((MOE_TILE, d), tile),
    )
    return pl.pallas_call(
        _collect_kernel,
        grid_spec=grid_spec,
        out_shape=jax.ShapeDtypeStruct((t, d), F32),
        compiler_params=_params(("arbitrary",)),
        name="moe_collect",
    )(wb, wc, wf, wv, slots, slots_col, gates, y_buf)


def _expert_kernel(be_ref, nb_ref, x_ref, w1_ref, w3_ref, w2_ref, o_ref):
    i = pl.program_id(0)

    @pl.when(i < nb_ref[0])
    def _():
        o_ref[...] = _swiglu_rows(x_ref[...], w1_ref, w3_ref, w2_ref).astype(o_ref.dtype)

    @pl.when(i >= nb_ref[0])
    def _():
        o_ref[...] = jnp.zeros_like(o_ref)


def _experts(buf, block_e, n_used, w1, w3, w2):
    n, d = buf.shape
    n_blocks = n // MOE_ROWS
    ff = w1.shape[-1]
    wspec = lambda shape: pl.BlockSpec((None,) + shape, lambda i, be, nb: (be[i], 0, 0),
                                       pipeline_mode=pl.Buffered(1))
    used = lambda i, be, nb: (jnp.minimum(i, nb[0] - 1), 0)
    grid_spec = pltpu.PrefetchScalarGridSpec(
        num_scalar_prefetch=2,
        grid=(n_blocks,),
        in_specs=[pl.BlockSpec((MOE_ROWS, d), used), wspec((d, ff)), wspec((d, ff)), wspec((ff, d))],
        out_specs=pl.BlockSpec((MOE_ROWS, d), lambda i, be, nb: (i, 0)),
    )
    return pl.pallas_call(
        _expert_kernel,
        grid_spec=grid_spec,
        out_shape=jax.ShapeDtypeStruct((n, d), BF16),
        compiler_params=_params(("arbitrary",)),
        name="experts",
    )(block_e, n_used, buf, w1, w3, w2)


def _combine_kernel(x_ref, mod_ref, y_ref, g_ref, b_ref, o_ref, *, alpha):
    mod = mod_ref[...]
    o_ref[...] = _ln(alpha * x_ref[...] + mod[5:6] * y_ref[...], LN_EPS) * g_ref[...] + b_ref[...]


def _combine(x1, mods, y_tok, g, b, ctx_len, alpha):
    bn, s, d = x1.shape
    ct = ctx_len // TM
    lt = (s - ctx_len) // TM
    full = lambda a: pl.BlockSpec(a.shape, lambda bb, i: (0,) * a.ndim)
    return pl.pallas_call(
        functools.partial(_combine_kernel, alpha=alpha),
        grid=(bn, lt),
        in_specs=[pl.BlockSpec((None, TM, d), lambda bb, i: (bb, i + ct, 0)),
                  pl.BlockSpec((None, None, 6, d), lambda bb, i: (bb, 1, 0, 0)),
                  pl.BlockSpec((TM, d), lambda bb, i: (bb * lt + i, 0)),
                  full(g), full(b)],
        out_specs=pl.BlockSpec((None, TM, d), lambda bb, i: (bb, i, 0)),
        out_shape=jax.ShapeDtypeStruct((bn, lt * TM, d), F32),
        compiler_params=_params(("parallel", "parallel")),
        name="moe_combine",
    )(x1, mods, y_tok, g, b)


def _work_lists(rank_at_tile, counts, pstart, block_e, n_used, n_blocks, n_tiles):
    n_work = n_tiles * N_EXPERTS + n_blocks
    blocks = jnp.arange(n_blocks, dtype=jnp.int32)
    used = blocks < n_used
    r0 = blocks * MOE_ROWS - pstart[block_e]
    r_last = jnp.minimum(r0 + MOE_ROWS, counts[block_e]) - 1
    cols = rank_at_tile.T[block_e]
    find = jax.vmap(lambda col, val: jnp.searchsorted(col, val, side="right"))
    lo = jnp.clip(find(cols, r0) - 1, 0, n_tiles - 1)
    hi = jnp.clip(find(cols, r_last) - 1, 0, n_tiles - 1)
    n_b = jnp.where(used, hi - lo + 1, 0)
    ends = jnp.cumsum(n_b)
    starts = ends - n_b
    total = ends[-1]
    w = jnp.arange(n_work, dtype=jnp.int32)
    valid = w < total
    wl = jnp.minimum(w, total - 1)
    blk = jnp.minimum(jnp.searchsorted(ends, wl, side="right"), n_blocks - 1).astype(jnp.int32)
    tile = (lo[blk] + (wl - starts[blk])).astype(jnp.int32)
    first = jnp.logical_and(valid, w == starts[blk])
    as_i32 = lambda a: a.astype(jnp.int32)
    by_block = (blk, tile, as_i32(first), as_i32(valid))
    order = jnp.argsort(jnp.where(valid, tile, n_tiles), stable=True)
    order = order[jnp.minimum(w, total - 1)]
    tile2, blk2 = tile[order], blk[order]
    first2 = jnp.logical_and(valid, jnp.concatenate([jnp.ones((1,), bool), tile2[1:] != tile2[:-1]]))
    by_tile = (blk2, tile2, as_i32(first2), as_i32(valid))
    return by_block, by_tile


def _moe_layer(x1, mods, router, w1, w3, w2, g, b, ctx_len, alpha):
    d = x1.shape[-1]
    router_p = jnp.pad(router, ((0, 0), (0, LANES - N_EXPERTS)))
    h, logits = _moe_pre(x1, mods, router_p, ctx_len)
    t = h.shape[0]
    top_v, top_i = lax.top_k(logits[:, :N_EXPERTS], TOP_K)
    gates = jax.nn.softmax(top_v, axis=-1)
    e_flat = top_i.reshape(-1)
    onehot = (e_flat[:, None] == jnp.arange(N_EXPERTS)[None, :]).astype(jnp.int32)
    ranks = jnp.cumsum(onehot, axis=0) - onehot
    rank = jnp.sum(ranks * onehot, axis=1)
    counts = jnp.sum(onehot, axis=0)
    padded = (counts + MOE_ROWS - 1) // MOE_ROWS * MOE_ROWS
    pend = jnp.cumsum(padded)
    pstart = pend - padded
    slot = (pstart[e_flat] + rank).astype(jnp.int32)
    n_blocks = t * TOP_K // MOE_ROWS + N_EXPERTS
    assert t % MOE_TILE == 0
    n_tiles = t // MOE_TILE
    block_e = jnp.minimum(jnp.searchsorted(pend, jnp.arange(n_blocks) * MOE_ROWS, side="right"),
                          N_EXPERTS - 1).astype(jnp.int32)
    n_used = (pend[-1:] // MOE_ROWS).astype(jnp.int32)
    rank_at_tile = jnp.concatenate([ranks[::MOE_TILE * TOP_K], counts[None, :]], axis=0)
    by_block, by_tile = _work_lists(rank_at_tile, counts, pstart, block_e, n_used, n_blocks, n_tiles)
    slots_col = slot.reshape(t, TOP_K)
    slots = slots_col.T
    buf = _dispatch(h, slots, by_block)
    y_buf = _experts(buf, block_e, n_used, w1, w3, w2)
    y_tok = _collect(y_buf, slots, slots_col, gates, by_tile)
    return _combine(x1, mods, y_tok, g, b, ctx_len, alpha)


def _rope_tables(n_rows, ctx_len, dim, lane_lo):
    quarter = dim // 4
    inv = ROPE_THETA ** (-jnp.arange(quarter, dtype=F32) / quarter)
    rows = jnp.repeat(jnp.arange(n_rows, dtype=F32), GRID_W)
    cols = jnp.tile(jnp.arange(GRID_W, dtype=F32), n_rows)
    ang = jnp.concatenate([rows[:, None] * inv, rows[:, None] * inv,
                           cols[:, None] * inv, cols[:, None] * inv], axis=-1)
    sign = jnp.tile(jnp.concatenate([-jnp.ones(quarter, F32), jnp.ones(quarter, F32)]), 2)
    cos, sin = jnp.cos(ang), jnp.sin(ang) * sign
    length = cos.shape[0]
    if lane_lo == 0:
        reps = LANES // dim
        cos, sin = jnp.tile(cos, (1, reps)), jnp.tile(sin, (1, reps))
    else:
        pad = ((0, 0), (lane_lo, LANES - lane_lo - dim))
        cos = jnp.pad(cos, pad, constant_values=1.0)
        sin = jnp.pad(sin, pad)
    cos = jnp.concatenate([jnp.ones((ctx_len, LANES), F32), cos], axis=0)
    sin = jnp.concatenate([jnp.zeros((ctx_len, LANES), F32), sin], axis=0)
    return cos, sin


def _block_diag2(w):
    z = jnp.zeros_like(w[0])
    return jnp.concatenate([jnp.concatenate([w[0], z], axis=1), jnp.concatenate([z, w[1]], axis=1)], axis=0)


def _lambda_init(layer):
    return 0.8 - 0.6 * math.exp(-0.3 * layer)


def kernel(x, c, ctx, c_ctx, ada_w, ada_b, w_in, w_out, ln1_g, ln1_b, ln2_g, ln2_b,
           lam_q1, lam_k1, lam_q2, lam_k2, diff_norm_g, shift_mu, w0, w2, a0, a2, g2,
           k_k, k_a, r_k, lnx_g, lnx_b, q_norm_g, w_uq, kv_norm_g, w_ukv,
           ff_w1, ff_w3, ff_w2, router, moe_w1, moe_w3, moe_w2):
    bn, seq, d = x.shape
    ctx_len = ctx.shape[1]
    depth = ada_w.shape[0]
    assert d == D_MODEL and seq % TM == 0 and ctx_len % TM == 0 and seq % GRID_W == 0
    alpha = (2.0 * depth) ** 0.25
    n_grid_rows = seq // GRID_W
    cos_a, sin_a = _rope_tables(n_grid_rows, ctx_len, A_QK_DIM, 0)
    cos_c, sin_c = _rope_tables(n_grid_rows, ctx_len, C_ROPE, C_NOPE)

    cond_rows = 8 * ((bn + 1 + 7) // 8)
    cond = jnp.zeros((cond_rows, d), F32).at[:bn].set(c).at[bn].set(c_ctx)
    xs = jnp.concatenate([ctx, x], axis=1)

    for i in range(depth):
        with_ctx = i < depth - 1
        m = _ada(cond, ada_w[i], ada_b[i]).reshape(cond_rows, 6, d)
        mods = jnp.stack([jnp.broadcast_to(m[bn], (bn, 6, d)), m[:bn]], axis=1)

        wc = w_in[i][:, N_A + N_B:]
        kpe_w = jnp.pad(wc[:, C_Q_RANK + C_KV_RANK:], ((0, 0), (C_NOPE, LANES - C_NOPE - C_ROPE)))
        w_p = jnp.concatenate([w_in[i][:, :N_A + N_B], wc[:, :C_Q_RANK + C_KV_RANK], kpe_w], axis=1).astype(BF16)
        wq = w_uq[i].reshape(C_Q_RANK, C_HEADS, C_NOPE + C_ROPE)
        wq_p = jnp.pad(wq, ((0, 0), (0, 0), (0, LANES - C_NOPE - C_ROPE))).reshape(C_Q_RANK, -1).astype(BF16)
        wkv = w_ukv[i].reshape(C_KV_RANK, C_HEADS, C_NOPE + C_V)
        wk_p = jnp.pad(wkv[:, :, :C_NOPE], ((0, 0), (0, 0), (0, LANES - C_NOPE))).reshape(C_KV_RANK, -1).astype(BF16)
        wv_p = wkv[:, :, C_NOPE:].reshape(C_KV_RANK, -1).astype(BF16)
        mla_consts = (q_norm_g[i].reshape(1, -1), kv_norm_g[i].reshape(1, -1), wq_p, wk_p, wv_p)
        qa, ka, va, pb, qc, kc, vc = _inproj(xs, mods, w_p, cos_a, sin_a, mla_consts, cos_c, sin_c, ctx_len)

        lam_p = jnp.stack([lam_q1[i], lam_k1[i], lam_q2[i], lam_k2[i]])
        g_col = jnp.broadcast_to(diff_norm_g[i][:, None], (A_V_DIM, TM))
        first_tile = 0 if with_ctx else ctx_len // TM
        a_out, c_out = _attention([("diff", qa, ka, va), ("mla", qc, kc, vc)], lam_p, g_col, ctx_len=ctx_len,
                                  first_tile=first_tile, lam_init=_lambda_init(i))
        r, v, kkn, ld, kd, beta, gate, bonus = _rwkv_prep(
            pb, shift_mu[i].reshape(1, -1), w0[i].reshape(1, -1), _block_diag2(w2[i]).astype(BF16),
            a0[i].reshape(1, -1), _block_diag2(a2[i]).astype(BF16), g2[i].astype(BF16),
            k_k[i].reshape(1, -1), k_a[i].reshape(1, -1), r_k[i].reshape(1, -1), ctx_len)
        yf, yb = _rwkv_pairs(r, v, kkn, ld, kd, beta, ctx_len)

        wo = w_out[i].astype(BF16)
        x1 = _outproj(xs, mods, a_out, yf, yb, bonus, gate, c_out,
                      wo[:A_WIDTH], wo[A_WIDTH:A_WIDTH + B_WIDTH], wo[A_WIDTH + B_WIDTH:],
                      lnx_g[i].reshape(1, -1), lnx_b[i].reshape(1, -1),
                      ln1_g[i].reshape(1, -1), ln1_b[i].reshape(1, -1), ctx_len, first_tile, alpha)

        j = i // 2
        g2n, b2n = ln2_g[i].reshape(1, -1), ln2_b[i].reshape(1, -1)
        if i % 2 == 0:
            xs = _ffn(x1, mods, ff_w1[j].astype(BF16), ff_w3[j].astype(BF16), ff_w2[j].astype(BF16),
                      g2n, b2n, ctx_len, alpha)
        else:
            if with_ctx:
                raise NotImplementedError("routed FFN on the context rows is not needed at this depth")
            return _moe_layer(x1, mods, router[j], moe_w1[j].astype(BF16), moe_w3[j].astype(BF16),
                              moe_w2[j].astype(BF16), g2n, b2n, ctx_len, alpha)
    return xs[:, ctx_len:]
```

```python
import functools
import math

import jax
import jax.numpy as jnp
from jax import lax
from jax.experimental import pallas as pl
from jax.experimental.pallas import tpu as pltpu

F32 = jnp.float32
BF16 = jnp.bfloat16

D_MODEL = 1024
GRID_W = 64
ROPE_THETA = 10000.0
A_HEADS, A_QK_DIM, A_V_DIM = 4, 64, 128
A_WIDTH = A_HEADS * A_V_DIM
A_QK_COLS = 2 * A_HEADS * A_QK_DIM
B_HEADS, B_HEAD = 4, 64
B_WIDTH = B_HEADS * B_HEAD
B_DECAY_RANK, B_A_RANK, B_GATE_RANK = 64, 64, 128
C_HEADS, C_NOPE, C_ROPE, C_V = 4, 64, 32, 64
C_WIDTH = C_HEADS * C_V
C_Q_RANK, C_KV_RANK = 256, 128
N_A = 2 * A_QK_COLS + A_WIDTH
N_B = 3 * B_WIDTH + 2 * B_DECAY_RANK + 2 * B_A_RANK + B_GATE_RANK
N_C = C_Q_RANK + C_KV_RANK + C_ROPE
N_C_PAD = C_Q_RANK + C_KV_RANK + 128
D_FF = 3584
N_EXPERTS = 8
TOP_K = 2
LN_EPS = 1e-6
RMS_EPS = 1e-6
GN_EPS = 64e-5

LOG2E = math.log2(math.e)
LANES = 128
TM = 256
CHUNK = 64
ROWS_B = 2
RWKV_ROWS = 4
ATT_TK = 2048
ONES_ROWS = 16
MOE_ROWS = 512
MOE_TILE = 1024
FF_CHUNK = 512
VMEM_LIMIT = 56 * 1024 * 1024

NN = (((1,), (0,)), ((), ()))
NT = (((1,), (1,)), ((), ()))
TN = (((0,), (0,)), ((), ()))


def _params(sem, vmem=VMEM_LIMIT, flags=None):
    return pltpu.CompilerParams(dimension_semantics=sem, vmem_limit_bytes=vmem, flags=flags)


def _split(x, n):
    parts, r = [], x
    for _ in range(n):
        p = r.astype(BF16)
        parts.append(p)
        r = r - p.astype(F32)
    return parts


def _dot(a, b, dn=NN):
    return lax.dot_general(a, b, dn, preferred_element_type=F32)


def _dot1(a, b, dn=NN):
    return _dot(a.astype(BF16), b.astype(BF16), dn)


def _dot3(a, b, dn=NN):
    a0, a1 = _split(a, 2)
    b0, b1 = _split(b, 2)
    return _dot(a0, b0, dn) + (_dot(a0, b1, dn) + _dot(a1, b0, dn))


def _dot6(a, b, dn=NN):
    a0, a1, a2 = _split(a, 3)
    b0, b1, b2 = _split(b, 3)
    lo = _dot(a1, b1, dn) + (_dot(a0, b2, dn) + _dot(a2, b0, dn))
    return _dot(a0, b0, dn) + ((_dot(a0, b1, dn) + _dot(a1, b0, dn)) + lo)


def _dot_exact_lhs(a_bf16, b, dn=NN, n=3):
    parts = _split(b, n)
    out = _dot(a_bf16, parts[-1], dn)
    for p in parts[-2::-1]:
        out = out + _dot(a_bf16, p, dn)
    return out


def _dot_exact_rhs(a, b_bf16, dn=NN, n=3):
    parts = _split(a, n)
    out = _dot(parts[-1], b_bf16, dn)
    for p in parts[-2::-1]:
        out = out + _dot(p, b_bf16, dn)
    return out


def _ln(x, eps):
    mu = jnp.mean(x, axis=-1, keepdims=True)
    xc = x - mu
    return xc * lax.rsqrt(jnp.mean(xc * xc, axis=-1, keepdims=True) + eps)


def _sigmoid(x):
    return 1.0 / (1.0 + jnp.exp(-x))


def _silu(x):
    return x * _sigmoid(x)


def _group_ones(width, group):
    r = lax.broadcasted_iota(jnp.int32, (width, width), 0) // group
    c = lax.broadcasted_iota(jnp.int32, (width, width), 1) // group
    return (r == c).astype(BF16)


def _partner(x, half):
    lane = lax.broadcasted_iota(jnp.int32, x.shape, 1)
    up = pltpu.roll(x, LANES - half, 1)
    dn = pltpu.roll(x, half, 1)
    return jnp.where((lane % (2 * half)) < half, up, dn)


def _rope(x, cos, sin, half):
    return x * cos + _partner(x, half) * sin


def _ada_kernel(c_ref, w_ref, b_ref, o_ref):
    o_ref[...] = _dot3(_silu(c_ref[...]), w_ref[...]) + b_ref[...]


def _ada(cond, w, b):
    rows, d = cond.shape
    n = w.shape[1]
    tn = 1536
    return pl.pallas_call(
        _ada_kernel,
        grid=(n // tn,),
        in_specs=[pl.BlockSpec((rows, d), lambda j: (0, 0)),
                  pl.BlockSpec((d, tn), lambda j: (0, j)),
                  pl.BlockSpec((1, tn), lambda j: (0, j))],
        out_specs=pl.BlockSpec((rows, tn), lambda j: (0, j)),
        out_shape=jax.ShapeDtypeStruct((rows, n), F32),
        compiler_params=_params(("parallel",)),
        name="ada",
    )(cond, w, b.reshape(1, n))


def _inproj_kernel(x_ref, mod_ref, w_ref, cos_ref, sin_ref,
                   qg_ref, kvg_ref, wq_ref, wk_ref, wv_ref, cosc_ref, sinc_ref,
                   q_ref, k_ref, v_ref, pb_ref, cq_ref, ck_ref, cv_ref):
    nb = x_ref.shape[0]
    h = jnp.concatenate([(_ln(x_ref[r], LN_EPS) * (1.0 + mod_ref[r, 1:2]) + mod_ref[r, 0:1]).astype(BF16)
                         for r in range(nb)], axis=0)
    cos, sin = cos_ref[...], sin_ref[...]
    scale = A_QK_DIM ** -0.5 * LOG2E
    rows = lambda a, r: a[r * TM:(r + 1) * TM]
    for j in range(A_QK_COLS // LANES):
        sl = slice(j * LANES, (j + 1) * LANES)
        qj = _dot(h, w_ref[:, sl])
        kj = _dot(h, w_ref[:, A_QK_COLS + j * LANES:A_QK_COLS + (j + 1) * LANES])
        vj = _dot(h, w_ref[:, 2 * A_QK_COLS + j * LANES:2 * A_QK_COLS + (j + 1) * LANES])
        for r in range(nb):
            q_ref[r, sl, :] = (_rope(rows(qj, r), cos, sin, A_QK_DIM // 4) * scale).T.astype(BF16)
            k_ref[r, :, sl] = _rope(rows(kj, r), cos, sin, A_QK_DIM // 4).astype(BF16)
            v_ref[r, sl, :] = rows(vj, r).T.astype(BF16)
    pb = _dot(h, w_ref[:, N_A:N_A + N_B])
    for r in range(nb):
        pb_ref[r] = rows(pb, r)
    _mla_heads(_dot(h, w_ref[:, N_A + N_B:]), nb, qg_ref, kvg_ref, wq_ref, wk_ref, wv_ref,
               cosc_ref[...], sinc_ref[...], cq_ref, ck_ref, cv_ref)


def _batch_rows(bn):
    return ROWS_B if bn % ROWS_B == 0 else 1


def _mod_spec(ctx_len, d, nb=None, first_tile=0):
    ct = ctx_len // TM
    return pl.BlockSpec((nb, None, 6, d), lambda b, i: (b, jnp.where(i + first_tile >= ct, 1, 0), 0, 0))


def _inproj(xs, mods, w_p, cos_a, sin_a, mla_consts, cos_c, sin_c, ctx_len):
    bn, s, d = xs.shape
    n_tiles = s // TM
    n_w = w_p.shape[1]
    nb = _batch_rows(bn)
    row = lambda width: pl.BlockSpec((nb, TM, width), lambda b, i: (b, i, 0))
    col = lambda width: pl.BlockSpec((nb, width, TM), lambda b, i: (b, 0, i))
    full = lambda a: pl.BlockSpec(a.shape, lambda b, i: (0,) * a.ndim)
    tab = pl.BlockSpec((TM, LANES), lambda b, i: (i, 0))
    hw = C_HEADS * LANES
    outs = [jax.ShapeDtypeStruct((bn, A_QK_COLS, s), BF16),
            jax.ShapeDtypeStruct((bn, s, A_QK_COLS), BF16),
            jax.ShapeDtypeStruct((bn, A_WIDTH, s), BF16),
            jax.ShapeDtypeStruct((bn, s, N_B), F32),
            jax.ShapeDtypeStruct((bn, hw, s), BF16),
            jax.ShapeDtypeStruct((bn, s, hw), BF16),
            jax.ShapeDtypeStruct((bn, C_WIDTH, s), BF16)]
    return pl.pallas_call(
        _inproj_kernel,
        grid=(bn // nb, n_tiles),
        in_specs=[row(d), _mod_spec(ctx_len, d, nb), full(w_p), tab, tab]
                 + [full(a) for a in mla_consts] + [tab, tab],
        out_specs=[col(A_QK_COLS), row(A_QK_COLS), col(A_WIDTH), row(N_B), col(hw), row(hw), col(C_WIDTH)],
        out_shape=outs,
        compiler_params=_params(("parallel", "parallel")),
        name="inproj",
    )(xs, mods, w_p, cos_a, sin_a, *mla_consts, cos_c, sin_c)


def _mla_heads(pc, nb, qg_ref, kvg_ref, wq_ref, wk_ref, wv_ref, cos, sin, q_ref, k_ref, v_ref):
    cq = pc[:, :C_Q_RANK]
    cq = cq * lax.rsqrt(jnp.mean(cq * cq, axis=-1, keepdims=True) + RMS_EPS) * qg_ref[...]
    ckv = pc[:, C_Q_RANK:C_Q_RANK + C_KV_RANK]
    ckv = ckv * lax.rsqrt(jnp.mean(ckv * ckv, axis=-1, keepdims=True) + RMS_EPS) * kvg_ref[...]
    cqb, ckvb = cq.astype(BF16), ckv.astype(BF16)
    scale = (C_NOPE + C_ROPE) ** -0.5 * LOG2E
    rows = lambda a, r: a[r * TM:(r + 1) * TM]
    kpe = [_rope(rows(pc, r)[:, C_Q_RANK + C_KV_RANK:], cos, sin, C_ROPE // 4) for r in range(nb)]
    for h in range(C_HEADS):
        sl = slice(h * LANES, (h + 1) * LANES)
        qh = _dot(cqb, wq_ref[:, sl])
        kh = _dot(ckvb, wk_ref[:, sl])
        for r in range(nb):
            q_ref[r, sl, :] = (_rope(rows(qh, r), cos, sin, C_ROPE // 4) * scale).T.astype(BF16)
            k_ref[r, :, sl] = (rows(kh, r) + kpe[r]).astype(BF16)
    for j in range(C_WIDTH // LANES):
        sl = slice(j * LANES, (j + 1) * LANES)
        vj = _dot(ckvb, wv_ref[:, sl])
        for r in range(nb):
            v_ref[r, sl, :] = rows(vj, r).T.astype(BF16)


def _attn_kernel(lam_ref, g_ref, *refs, modes, ctx_tiles, ctx_len, tk, first_tile, lam_init):
    n_sets = len(modes)
    in_refs = [refs[3 * t:3 * t + 3] for t in range(n_sets)]
    o_refs = refs[3 * n_sets:]
    qi = pl.program_id(1) + first_tile
    tq = in_refs[0][0].shape[1]
    half = lax.broadcasted_iota(jnp.int32, (LANES, 1), 0) < (LANES // 2)
    chains = []
    for t, (qt_ref, k_ref, vt_ref) in enumerate(in_refs):
        heads = qt_ref.shape[0] // LANES
        for g in range(heads):
            qt = qt_ref[g * LANES:(g + 1) * LANES, :]
            if modes[t] == "diff":
                qt = jnp.concatenate([jnp.where(half, qt, jnp.zeros_like(qt)),
                                      jnp.where(half, jnp.zeros_like(qt), qt)], axis=1)
            chains.append((t, g, qt, vt_ref.shape[0] // heads, qt.shape[1]))
    s_len = in_refs[0][1].shape[0]

    def fold(x, reduce):
        rows = x.shape[0]
        while rows > 8:
            g = max(d for d in range(2, 9) if (rows // 8) % d == 0)
            rows //= g
            x = reduce(x.reshape(g, rows, x.shape[1]), axis=0)
        return x

    def absorb(kcs, vcs, carry):
        n = len(chains)
        ss = [_dot(kcs[t][:, g * LANES:(g + 1) * LANES], q) for (t, g, q, _, _) in chains]
        m_new = [jnp.maximum(carry[c][0], jnp.max(fold(ss[c], jnp.max), axis=0, keepdims=True)) for c in range(n)]
        alpha = [jnp.exp2(carry[c][0] - m_new[c]) for c in range(n)]
        ps = [jnp.exp2(ss[c] - m_new[c]).astype(BF16) for c in range(n)]
        ones = jnp.ones((ONES_ROWS, kcs[0].shape[0]), BF16)
        acc = [alpha[c] * carry[c][1]
               + _dot(jnp.concatenate([vcs[t][g * dv:(g + 1) * dv], ones], axis=0), ps[c])
               for c, (t, g, _, dv, _) in enumerate(chains)]
        return tuple((m_new[c], acc[c]) for c in range(n))

    def finish(stats):
        outs = [a[:dv] / a[dv:dv + 1] for (_, a), (_, _, _, dv, _) in zip(stats, chains)]
        for t in range(n_sets):
            mine = [outs[c] for c, ch in enumerate(chains) if ch[0] == t]
            if modes[t] == "diff":
                lp = lam_ref[...]
                lam = (jnp.exp(jnp.sum(lp[0:1] * lp[1:2], axis=-1, keepdims=True))
                       - jnp.exp(jnp.sum(lp[2:3] * lp[3:4], axis=-1, keepdims=True)) + lam_init)
                for g, og in enumerate(mine):
                    o = og[:, :tq] - lam * og[:, tq:]
                    o = o * lax.rsqrt(jnp.mean(o * o, axis=0, keepdims=True) + RMS_EPS) * g_ref[...]
                    o = o * (1.0 - lam_init)
                    o_refs[t][:, g * LANES:(g + 1) * LANES] = o.T.astype(o_refs[t].dtype)
            else:
                per_slab = LANES // mine[0].shape[0]
                for j in range(len(mine) // per_slab):
                    o = jnp.concatenate(mine[j * per_slab:(j + 1) * per_slab], axis=0)
                    o_refs[t][:, j * LANES:(j + 1) * LANES] = o.T.astype(o_refs[t].dtype)

    def run(first_keys, n_more):
        init = tuple((jnp.full((1, w), -1e30, F32), jnp.zeros((dv + ONES_ROWS, w), F32))
                     for (_, _, _, dv, w) in chains)
        stats = absorb([r[1][0:first_keys, :] for r in in_refs], [r[2][:, 0:first_keys] for r in in_refs], init)

        def body(j, stats):
            off = pl.multiple_of(first_keys + j * tk, LANES)
            return absorb([r[1][pl.ds(off, tk), :] for r in in_refs],
                          [r[2][:, pl.ds(off, tk)] for r in in_refs], stats)

        finish(lax.fori_loop(0, n_more, body, stats))

    @pl.when(qi < ctx_tiles)
    def _():
        run(ctx_len, 0)

    @pl.when(qi >= ctx_tiles)
    def _():
        run(ctx_len + tk, (s_len - ctx_len) // tk - 1)


def _attention(sets, lam_p, g, *, ctx_len, first_tile, lam_init):
    bn, s, _ = sets[0][2].shape
    tk = math.gcd(s - ctx_len, ATT_TK)
    assert tk % LANES == 0 and ctx_len % LANES == 0
    kern = functools.partial(_attn_kernel, modes=tuple(m for m, _, _, _ in sets), ctx_tiles=ctx_len // TM,
                             ctx_len=ctx_len, tk=tk, first_tile=first_tile, lam_init=lam_init)
    in_specs = [pl.BlockSpec(lam_p.shape, lambda b, i: (0, 0)), pl.BlockSpec(g.shape, lambda b, i: (0, 0))]
    args, out_specs, out_shapes = [lam_p, g], [], []
    for _, qt, k, vt in sets:
        in_specs += [pl.BlockSpec((None, qt.shape[1], TM), lambda b, i: (b, 0, i + first_tile)),
                     pl.BlockSpec((None, s, k.shape[2]), lambda b, i: (b, 0, 0), pipeline_mode=pl.Buffered(1)),
                     pl.BlockSpec((None, vt.shape[1], s), lambda b, i: (b, 0, 0), pipeline_mode=pl.Buffered(1))]
        args += [qt, k, vt]
        out_specs.append(pl.BlockSpec((None, TM, vt.shape[1]), lambda b, i: (b, i + first_tile, 0)))
        out_shapes.append(jax.ShapeDtypeStruct((bn, s, vt.shape[1]), BF16))
    return pl.pallas_call(
        kern,
        grid=(bn, s // TM - first_tile),
        in_specs=in_specs,
        out_specs=out_specs,
        out_shape=out_shapes,
        compiler_params=_params(("parallel", "parallel")),
        name="attention",
    )(*args)


def _rwkv_prep_kernel(pb_ref, prev_ref, next_ref, mu_ref, w0_ref, w2_ref, a0_ref, a2_ref, g2_ref,
                      kk_ref, ka_ref, rk_ref,
                      r_out, v_out, kkn_out, ld_out, kd_out, beta_out, g_out, bonus_out,
                      *, ctx_tiles, n_tiles):
    i = pl.program_id(1)
    x = pb_ref[...]
    row = lax.broadcasted_iota(jnp.int32, (TM, 1), 0)
    has_prev = jnp.logical_and(i != 0, i != ctx_tiles)
    has_next = jnp.logical_and(i != ctx_tiles - 1, i != n_tiles - 1)
    prev_edge = jnp.where(has_prev, prev_ref[7:8, :], 0.0)
    next_edge = jnp.where(has_next, next_ref[0:1, :], 0.0)
    xp = jnp.where(row == 0, prev_edge, pltpu.roll(x, 1, 0))
    xn = jnp.where(row == TM - 1, next_edge, pltpu.roll(x, TM - 1, 0))
    z = x + mu_ref[...] * (0.5 * (xp + xn) - x)

    r = z[:, :B_WIDTH]
    k = z[:, B_WIDTH:2 * B_WIDTH]
    v = z[:, 2 * B_WIDTH:3 * B_WIDTH]
    o = 3 * B_WIDTH
    wd = z[:, o:o + 2 * B_DECAY_RANK]
    o += 2 * B_DECAY_RANK
    ad = z[:, o:o + 2 * B_A_RANK]
    o += 2 * B_A_RANK
    gd = z[:, o:]

    u = w0_ref[...] + _dot1(jnp.tanh(wd), w2_ref[...])
    ld = -math.exp(-0.5) * _sigmoid(u)
    lr = _sigmoid(a0_ref[...] + _dot1(ad, a2_ref[...]))
    g_out[...] = _dot1(_sigmoid(gd), g2_ref[...])

    ones = _group_ones(B_WIDTH, B_HEAD)
    kk = k * kk_ref[...]
    norm = jnp.sqrt(_dot_exact_rhs(kk * kk, ones))
    kkn = kk / jnp.maximum(norm, 1e-12)
    ka = ka_ref[...]
    kd_sum = jnp.zeros_like(k)
    for d in range(2):
        lr_d = lr[:, d * B_WIDTH:(d + 1) * B_WIDTH]
        kd = k * (1.0 + (lr_d - 1.0) * ka)
        kd_sum = kd_sum + kd
        ld_out[d] = ld[:, d * B_WIDTH:(d + 1) * B_WIDTH]
        kd_out[d] = kd
        beta_out[d] = kkn * lr_d
    bonus_out[...] = _dot_exact_rhs(r * kd_sum * rk_ref[...], ones) * v
    r_out[...] = r
    v_out[...] = v
    kkn_out[...] = kkn


def _rwkv_prep(pb, mu, w0, w2bd, a0, a2bd, g2, k_k, k_a, r_k, ctx_len):
    bn, s, _ = pb.shape
    n_tiles = s // TM
    eight = TM // 8
    row = lambda width: pl.BlockSpec((None, TM, width), lambda b, i: (b, i, 0))
    drow = pl.BlockSpec((2, None, TM, B_WIDTH), lambda b, i: (0, b, i, 0))
    full = lambda a: pl.BlockSpec(a.shape, lambda b, i: (0,) * a.ndim)
    one = jax.ShapeDtypeStruct((bn, s, B_WIDTH), F32)
    two = jax.ShapeDtypeStruct((2, bn, s, B_WIDTH), F32)
    kern = functools.partial(_rwkv_prep_kernel, ctx_tiles=ctx_len // TM, n_tiles=n_tiles)
    params = (mu, w0, w2bd, a0, a2bd, g2, k_k, k_a, r_k)
    return pl.pallas_call(
        kern,
        grid=(bn, n_tiles),
        in_specs=[row(N_B),
                  pl.BlockSpec((None, 8, N_B), lambda b, i: (b, jnp.maximum(i * eight - 1, 0), 0)),
                  pl.BlockSpec((None, 8, N_B), lambda b, i: (b, jnp.minimum((i + 1) * eight, s // 8 - 1), 0)),
                  ] + [full(p) for p in params],
        out_specs=[row(B_WIDTH), row(B_WIDTH), row(B_WIDTH), drow, drow, drow, row(B_WIDTH), row(B_WIDTH)],
        out_shape=[one, one, one, two, two, two, one, one],
        compiler_params=_params(("parallel", "parallel")),
        name="rwkv_prep",
    )(pb, pb, pb, *params)


def _pair_diag(x):
    lo = lax.broadcasted_iota(jnp.int32, (1, LANES), 1) < B_HEAD
    z = jnp.zeros_like(x)
    return jnp.concatenate([jnp.where(lo, x, z), jnp.where(lo, z, x)], axis=0)


def _pair_pick(x):
    lo = lax.broadcasted_iota(jnp.int32, (1, LANES), 1) < B_HEAD
    return jnp.where(lo, x[:B_HEAD], x[B_HEAD:])


def _rwkv_pair_kernel(rf_ref, vf_ref, kkf_ref, rb_ref, vb_ref, kkb_ref,
                      ldf_ref, kdf_ref, betaf_ref, ldb_ref, kdb_ref, betab_ref,
                      yf_ref, yb_ref, h_ref):
    c = pl.program_id(1)

    @pl.when(c == 0)
    def _():
        h_ref[...] = jnp.zeros_like(h_ref)

    n_pairs = B_WIDTH // LANES
    ti = lax.broadcasted_iota(jnp.int32, (CHUNK, LANES), 0)
    si = lax.broadcasted_iota(jnp.int32, (CHUNK, LANES), 1) % CHUNK
    t64 = lax.broadcasted_iota(jnp.int32, (CHUNK, CHUNK), 0)
    s64 = lax.broadcasted_iota(jnp.int32, (CHUNK, CHUNK), 1)
    eye = ti == si
    dirs = ((rf_ref, vf_ref, kkf_ref, ldf_ref, kdf_ref, betaf_ref, False),
            (rb_ref, vb_ref, kkb_ref, ldb_ref, kdb_ref, betab_ref, True))

    units = []
    n_rows = rf_ref.shape[0]
    for bi in range(n_rows):
      for d, (r_ref, v_ref, kk_ref, ld_ref, kd_ref, beta_ref, rev) in enumerate(dirs):
        strict = (si > ti) if rev else (ti > si)
        incl = (si >= ti) if rev else (ti >= si)
        tri = ((s64 >= t64) if rev else (t64 >= s64)).astype(BF16)
        ld_all = ld_ref[bi]
        cl_all = _dot_exact_lhs(tri, ld_all)
        for p in range(n_pairs):
            sl = slice(p * LANES, (p + 1) * LANES)
            ld, cl = ld_all[:, sl], cl_all[:, sl]
            total = jnp.sum(ld, axis=0, keepdims=True)
            inv_gam = jnp.exp(-cl)
            to_end = jnp.exp(total - cl)
            kk, kd, beta = kk_ref[bi, :, sl], kd_ref[bi, :, sl], beta_ref[bi, :, sl]
            units.append(dict(
                bi=bi, d=d, p=p, sl=sl, strict=strict, incl=incl, v=v_ref[bi, :, sl].astype(BF16),
                a_bar=-kk * jnp.exp(cl - ld), r_bar=r_ref[bi, :, sl] * jnp.exp(cl),
                b_til=(beta * inv_gam).astype(BF16), k_til=(kd * inv_gam).astype(BF16),
                b_hat=(beta * to_end).astype(BF16), k_hat=(kd * to_end).astype(BF16),
                gam_c=jnp.exp(total)))

    for u in units:
        x_mat = jnp.concatenate([u["a_bar"], u["r_bar"]], axis=0).astype(BF16)
        rhs = jnp.concatenate([_pair_diag(u["b_til"]), _pair_diag(u["k_til"])], axis=0)
        xbk = _dot(x_mat, rhs, NT)
        u["n_ab"] = jnp.where(u["strict"], xbk[:CHUNK, :LANES], 0.0)
        u["l_rb"] = jnp.where(u["incl"], xbk[CHUNK:, :LANES], 0.0)
        n_ak = jnp.where(u["strict"], xbk[:CHUNK, LANES:], 0.0)
        l_rk = jnp.where(u["incl"], xbk[CHUNK:, LANES:], 0.0)
        u["nl"] = jnp.concatenate([n_ak, l_rk], axis=0).astype(BF16)
    for u in units:
        nv = _dot(u["nl"], _pair_diag(u["v"]))
        u["w"], u["u0"], u["lrkv"] = u["a_bar"], nv[:CHUNK], nv[CHUNK:]
        u["npow"] = u["n_ab"].astype(BF16)

    steps = int(math.log2(CHUNK))
    for kstep in range(steps):
        for u in units:
            rhs = jnp.concatenate([_pair_diag(u["w"].astype(BF16)), _pair_diag(u["u0"].astype(BF16))], axis=1)
            upd = _dot(u["npow"], rhs)
            u["w"] = u["w"] + upd[:, :LANES]
            u["u0"] = u["u0"] + upd[:, LANES:]
        if kstep + 1 < steps:
            for u in units:
                u["npow"] = _dot(u["npow"], _pair_diag(u["npow"])).astype(BF16)

    for u in units:
        wb, ub = u["w"].astype(BF16), u["u0"].astype(BF16)
        lx = _dot(u["l_rb"].astype(BF16), jnp.concatenate([_pair_diag(wb), _pair_diag(ub)], axis=1))
        u["p_mat"] = u["r_bar"] + lx[:, :LANES]
        u["y0"] = u["lrkv"] + lx[:, LANES:]
        lhs = jnp.concatenate([u["b_hat"], u["k_hat"]], axis=0)
        rhs = jnp.concatenate([jnp.concatenate([wb, ub], axis=1),
                               jnp.concatenate([jnp.zeros_like(wb), u["v"]], axis=1)], axis=0)
        mg = _dot(lhs, rhs, TN)
        u["m_full"] = _pair_pick(mg[:, :LANES]) + jnp.where(eye, u["gam_c"], 0.0)
        u["g_mat"] = _pair_pick(mg[:, LANES:])

    for u in units:
        bi, d, p = u["bi"], u["d"], u["p"]
        h0 = h_ref[bi, d, p]
        a0, a1 = _split(jnp.concatenate([u["p_mat"], u["m_full"]], axis=0), 2)
        h_hi, h_lo = _split(h0, 2)
        bh, bl = _pair_diag(h_hi), _pair_diag(h_lo)
        out = _dot(a0, bh) + (_dot(a0, bl) + _dot(a1, bh))
        (yb_ref if d else yf_ref)[bi, :, u["sl"]] = out[:CHUNK] + u["y0"]
        h_ref[bi, d, p] = out[CHUNK:] + u["g_mat"]


def _rwkv_pairs(r, v, kk, ld, kd, beta, ctx_len):
    bn, s, _ = r.shape
    n_chunks = s // CHUNK
    ctx_chunks = ctx_len // CHUNK
    rows = RWKV_ROWS if bn % RWKV_ROWS == 0 else 1

    def back(c):
        return jnp.where(c < ctx_chunks, ctx_chunks - 1 - c, n_chunks - 1 + ctx_chunks - c)

    fwd = pl.BlockSpec((rows, CHUNK, B_WIDTH), lambda b, c: (b, c, 0))
    bwd = pl.BlockSpec((rows, CHUNK, B_WIDTH), lambda b, c: (b, back(c), 0))
    fwd_d = pl.BlockSpec((None, rows, CHUNK, B_WIDTH), lambda b, c: (0, b, c, 0))
    bwd_d = pl.BlockSpec((None, rows, CHUNK, B_WIDTH), lambda b, c: (1, b, back(c), 0))
    y = jax.ShapeDtypeStruct((bn, s, B_WIDTH), F32)
    return pl.pallas_call(
        _rwkv_pair_kernel,
        grid=(bn // rows, n_chunks),
        in_specs=[fwd, fwd, fwd, bwd, bwd, bwd, fwd_d, fwd_d, fwd_d, bwd_d, bwd_d, bwd_d],
        out_specs=[fwd, bwd],
        out_shape=[y, y],
        scratch_shapes=[pltpu.VMEM((rows, 2, B_WIDTH // LANES, B_HEAD, LANES), F32)],
        compiler_params=_params(("parallel", "arbitrary")),
        name="rwkv_scan",
    )(r, v, kk, r, v, kk, ld, kd, beta, ld, kd, beta)


def _outproj_kernel(x_ref, mod_ref, a_ref, yf_ref, yb_ref, bonus_ref, g_ref, c_ref,
                    wa_ref, wb_ref, wc_ref, lnxg_ref, lnxb_ref, ln1g_ref, ln1b_ref, o_ref, *, alpha):
    nb = x_ref.shape[0]
    stack = lambda ref: jnp.concatenate([ref[r] for r in range(nb)], axis=0)
    y = stack(yf_ref) + stack(yb_ref) + stack(bonus_ref)
    ones = _group_ones(B_WIDTH, B_HEAD)
    inv = 1.0 / B_HEAD
    mu = _dot_exact_rhs(y, ones) * inv
    yc = y - mu
    var = _dot_exact_rhs(yc * yc, ones) * inv
    yn = yc * lax.rsqrt(var + GN_EPS) * lnxg_ref[...] + lnxb_ref[...]
    bmix = (yn * stack(g_ref)).astype(BF16)
    o = _dot(stack(a_ref), wa_ref[...]) + _dot(bmix, wb_ref[...]) + _dot(stack(c_ref), wc_ref[...])
    for r in range(nb):
        o_ref[r] = (_ln(alpha * x_ref[r] + mod_ref[r, 2:3] * o[r * TM:(r + 1) * TM], LN_EPS) * ln1g_ref[...]
                    + ln1b_ref[...])


def _outproj(xs, mods, a_out, yf, yb, bonus, g, c_out, wa, wb, wc, lnxg, lnxb, ln1g, ln1b, ctx_len, first_tile,
             alpha):
    bn, s, d = xs.shape
    nb = _batch_rows(bn)
    row = lambda width: pl.BlockSpec((nb, TM, width), lambda b, i: (b, i + first_tile, 0))
    full = lambda a: pl.BlockSpec(a.shape, lambda b, i: (0,) * a.ndim)
    consts = (wa, wb, wc, lnxg, lnxb, ln1g, ln1b)
    return pl.pallas_call(
        functools.partial(_outproj_kernel, alpha=alpha),
        grid=(bn // nb, s // TM - first_tile),
        in_specs=[row(d), _mod_spec(ctx_len, d, nb, first_tile),
                  row(A_WIDTH), row(B_WIDTH), row(B_WIDTH), row(B_WIDTH), row(B_WIDTH), row(C_WIDTH)]
                 + [full(p) for p in consts],
        out_specs=row(d),
        out_shape=jax.ShapeDtypeStruct((bn, s, d), F32),
        compiler_params=_params(("parallel", "parallel")),
        name="outproj",
    )(xs, mods, a_out, yf, yb, bonus, g, c_out, *consts)


def _swiglu_rows(h, w1_ref, w3_ref, w2_ref):
    acc = jnp.zeros((h.shape[0], w2_ref.shape[-1]), F32)
    for j in range(w1_ref.shape[-1] // FF_CHUNK):
        sl = slice(j * FF_CHUNK, (j + 1) * FF_CHUNK)
        u = _dot(h, w1_ref[:, sl])
        t = _dot(h, w3_ref[:, sl])
        acc = acc + _dot((_silu(u) * t).astype(BF16), w2_ref[sl, :])
    return acc


def _ffn_kernel(x_ref, mod_ref, w1_ref, w3_ref, w2_ref, g_ref, b_ref, o_ref, *, alpha):
    nb = x_ref.shape[0]
    h = jnp.concatenate([(_ln(x_ref[r], LN_EPS) * (1.0 + mod_ref[r, 4:5]) + mod_ref[r, 3:4]).astype(BF16)
                         for r in range(nb)], axis=0)
    f = _swiglu_rows(h, w1_ref, w3_ref, w2_ref)
    for r in range(nb):
        o_ref[r] = (_ln(alpha * x_ref[r] + mod_ref[r, 5:6] * f[r * TM:(r + 1) * TM], LN_EPS) * g_ref[...]
                    + b_ref[...])


def _ffn(x1, mods, w1, w3, w2, g, b, ctx_len, alpha):
    bn, s, d = x1.shape
    nb = _batch_rows(bn)
    row = pl.BlockSpec((nb, TM, d), lambda bb, i: (bb, i, 0))
    resident = lambda a: pl.BlockSpec(a.shape, lambda bb, i: (0,) * a.ndim, pipeline_mode=pl.Buffered(1))
    full = lambda a: pl.BlockSpec(a.shape, lambda bb, i: (0,) * a.ndim)
    return pl.pallas_call(
        functools.partial(_ffn_kernel, alpha=alpha),
        grid=(bn // nb, s // TM),
        in_specs=[row, _mod_spec(ctx_len, d, nb),
                  resident(w1), resident(w3), resident(w2), full(g), full(b)],
        out_specs=row,
        out_shape=jax.ShapeDtypeStruct((bn, s, d), F32),
        compiler_params=_params(("parallel", "parallel")),
        name="ffn",
    )(x1, mods, w1, w3, w2, g, b)


def _moe_pre_kernel(x_ref, mod_ref, router_ref, h_ref, logit_ref):
    mod = mod_ref[...]
    h = _ln(x_ref[...], LN_EPS) * (1.0 + mod[4:5]) + mod[3:4]
    h_ref[...] = h.astype(h_ref.dtype)
    logit_ref[...] = _dot6(h, router_ref[...])


def _moe_pre(x1, mods, router_p, ctx_len):
    bn, s, d = x1.shape
    ct = ctx_len // TM
    lt = (s - ctx_len) // TM
    return pl.pallas_call(
        _moe_pre_kernel,
        grid=(bn, lt),
        in_specs=[pl.BlockSpec((None, TM, d), lambda b, i: (b, i + ct, 0)),
                  pl.BlockSpec((None, None, 6, d), lambda b, i: (b, 1, 0, 0)),
                  pl.BlockSpec(router_p.shape, lambda b, i: (0, 0))],
        out_specs=[pl.BlockSpec((TM, d), lambda b, i: (b * lt + i, 0)),
                   pl.BlockSpec((TM, LANES), lambda b, i: (b * lt + i, 0))],
        out_shape=[jax.ShapeDtypeStruct((bn * lt * TM, d), BF16),
                   jax.ShapeDtypeStruct((bn * lt * TM, LANES), F32)],
        compiler_params=_params(("parallel", "parallel")),
        name="moe_pre",
    )(x1, mods, router_p)


def _slot_onehot(slots_ref, block):
    sl = slots_ref[...]
    s_iota = lax.broadcasted_iota(jnp.int32, (MOE_ROWS, MOE_TILE), 0) + block * MOE_ROWS
    hit = jnp.logical_or(sl[0:1] == s_iota, sl[1:2] == s_iota)
    return jnp.where(hit, 1.0, 0.0).astype(BF16)


def _dispatch_kernel(wb_ref, wc_ref, wf_ref, wv_ref, slots_ref, h_ref, o_ref):
    w = pl.program_id(0)
    part = lambda: _dot(_slot_onehot(slots_ref, wb_ref[w]), h_ref[...])

    @pl.when(wf_ref[w] == 1)
    def _():
        o_ref[...] = part().astype(o_ref.dtype)

    @pl.when(jnp.logical_and(wf_ref[w] == 0, wv_ref[w] == 1))
    def _():
        o_ref[...] = (o_ref[...].astype(F32) + part()).astype(o_ref.dtype)


def _dispatch(h, slots, work):
    t, d = h.shape
    wb, wc, wf, wv = work
    n_slots = (t * TOP_K // MOE_ROWS + N_EXPERTS) * MOE_ROWS
    grid_spec = pltpu.PrefetchScalarGridSpec(
        num_scalar_prefetch=4,
        grid=(wb.shape[0],),
        in_specs=[pl.BlockSpec((TOP_K, MOE_TILE), lambda w, b, c, f, v: (0, c[w])),
                  pl.BlockSpec((MOE_TILE, d), lambda w, b, c, f, v: (c[w], 0))],
        out_specs=pl.BlockSpec((MOE_ROWS, d), lambda w, b, c, f, v: (b[w], 0)),
    )
    return pl.pallas_call(
        _dispatch_kernel,
        grid_spec=grid_spec,
        out_shape=jax.ShapeDtypeStruct((n_slots, d), BF16),
        compiler_params=_params(("arbitrary",)),
        name="moe_dispatch",
    )(wb, wc, wf, wv, slots, h)


def _collect_kernel(wb_ref, wc_ref, wf_ref, wv_ref, wl_ref, slots_ref, scol_ref, gate_ref, y_ref,
                    mod_ref, g_ref, b_ref, *rest, alpha):
    x_refs, o_ref, acc_ref = rest[:-2], rest[-2], rest[-1]
    w = pl.program_id(0)

    def part():
        base = wb_ref[w] * MOE_ROWS
        scol = scol_ref[...] - base
        gates = jnp.where(jnp.logical_and(scol >= 0, scol < MOE_ROWS), gate_ref[...], 0.0)
        gate = jnp.sum(gates, axis=1, keepdims=True)
        return _dot(_slot_onehot(slots_ref, wb_ref[w]), y_ref[...], TN) * gate

    @pl.when(wf_ref[w] == 1)
    def _():
        acc_ref[...] = part()

    @pl.when(jnp.logical_and(wf_ref[w] == 0, wv_ref[w] == 1))
    def _():
        acc_ref[...] = acc_ref[...] + part()

    @pl.when(wl_ref[w] == 1)
    def _():
        mod = mod_ref[...]
        for j, x_ref in enumerate(x_refs):
            f = acc_ref[j * TM:(j + 1) * TM, :]
            o_ref[j * TM:(j + 1) * TM, :] = (_ln(alpha * x_ref[...] + mod[5:6] * f, LN_EPS) * g_ref[...]
                                             + b_ref[...])


def _collect(y_buf, slots, slots_col, gates, work, x1, mods, g, b, ctx_len, alpha):
    n_slots, d = y_buf.shape
    bn, s, _ = x1.shape
    seq = s - ctx_len
    assert seq % MOE_TILE == 0 and MOE_TILE % TM == 0
    tpb = seq // MOE_TILE
    sub = MOE_TILE // TM
    ct = ctx_len // TM
    wb, wc, wf, wv, wl = work
    tile = lambda w, b_, c, f, v, l: (c[w], 0)
    x_spec = lambda j: pl.BlockSpec((None, TM, d),
                                    lambda w, b_, c, f, v, l: (c[w] // tpb, ct + (c[w] % tpb) * sub + j, 0))
    const = lambda a: pl.BlockSpec(a.shape, lambda w, b_, c, f, v, l: (0,) * a.ndim)
    grid_spec = pltpu.PrefetchScalarGridSpec(
        num_scalar_prefetch=5,
        grid=(wb.shape[0],),
        in_specs=[pl.BlockSpec((TOP_K, MOE_TILE), lambda w, b_, c, f, v, l: (0, c[w])),
                  pl.BlockSpec((MOE_TILE, TOP_K), tile), pl.BlockSpec((MOE_TILE, TOP_K), tile),
                  pl.BlockSpec((MOE_ROWS, d), lambda w, b_, c, f, v, l: (b_[w], 0)),
                  pl.BlockSpec((None, None, 6, d), lambda w, b_, c, f, v, l: (c[w] // tpb, 1, 0, 0)),
                  const(g), const(b)] + [x_spec(j) for j in range(sub)],
        out_specs=pl.BlockSpec((None, MOE_TILE, d), lambda w, b_, c, f, v, l: (c[w] // tpb, c[w] % tpb, 0)),
        scratch_shapes=[pltpu.VMEM((MOE_TILE, d), F32)],
    )
    return pl.pallas_call(
        functools.partial(_collect_kernel, alpha=alpha),
        grid_spec=grid_spec,
        out_shape=jax.ShapeDtypeStruct((bn, seq, d), F32),
        compiler_params=_params(("arbitrary",)),
        name="moe_collect",
    )(wb, wc, wf, wv, wl, slots, slots_col, gates, y_buf, mods, g, b, *([x1] * sub))


def _expert_kernel(be_ref, nb_ref, x_ref, w1_ref, w3_ref, w2_ref, o_ref):
    i = pl.program_id(0)

    @pl.when(i < nb_ref[0])
    def _():
        o_ref[...] = _swiglu_rows(x_ref[...], w1_ref, w3_ref, w2_ref).astype(o_ref.dtype)

    @pl.when(i >= nb_ref[0])
    def _():
        o_ref[...] = jnp.zeros_like(o_ref)


def _experts(buf, block_e, n_used, w1, w3, w2):
    n, d = buf.shape
    n_blocks = n // MOE_ROWS
    ff = w1.shape[-1]
    wspec = lambda shape: pl.BlockSpec((None,) + shape, lambda i, be, nb: (be[i], 0, 0),
                                       pipeline_mode=pl.Buffered(1))
    used = lambda i, be, nb: (jnp.minimum(i, nb[0] - 1), 0)
    grid_spec = pltpu.PrefetchScalarGridSpec(
        num_scalar_prefetch=2,
        grid=(n_blocks,),
        in_specs=[pl.BlockSpec((MOE_ROWS, d), used), wspec((d, ff)), wspec((d, ff)), wspec((ff, d))],
        out_specs=pl.BlockSpec((MOE_ROWS, d), lambda i, be, nb: (i, 0)),
    )
    return pl.pallas_call(
        _expert_kernel,
        grid_spec=grid_spec,
        out_shape=jax.ShapeDtypeStruct((n, d), BF16),
        compiler_params=_params(("arbitrary",)),
        name="experts",
    )(block_e, n_used, buf, w1, w3, w2)


def _work_lists(rank_at_tile, counts, pstart, block_e, n_used, n_blocks, n_tiles):
    n_work = n_tiles * N_EXPERTS + n_blocks
    blocks = jnp.arange(n_blocks, dtype=jnp.int32)
    used = blocks < n_used
    r0 = blocks * MOE_ROWS - pstart[block_e]
    r_last = jnp.minimum(r0 + MOE_ROWS, counts[block_e]) - 1
    cols = rank_at_tile.T[block_e]
    find = jax.vmap(lambda col, val: jnp.searchsorted(col, val, side="right"))
    lo = jnp.clip(find(cols, r0) - 1, 0, n_tiles - 1)
    hi = jnp.clip(find(cols, r_last) - 1, 0, n_tiles - 1)
    n_b = jnp.where(used, hi - lo + 1, 0)
    ends = jnp.cumsum(n_b)
    starts = ends - n_b
    total = ends[-1]
    w = jnp.arange(n_work, dtype=jnp.int32)
    valid = w < total
    wl = jnp.minimum(w, total - 1)
    blk = jnp.minimum(jnp.searchsorted(ends, wl, side="right"), n_blocks - 1).astype(jnp.int32)
    tile = (lo[blk] + (wl - starts[blk])).astype(jnp.int32)
    first = jnp.logical_and(valid, w == starts[blk])
    as_i32 = lambda a: a.astype(jnp.int32)
    by_block = (blk, tile, as_i32(first), as_i32(valid))
    order = jnp.argsort(jnp.where(valid, tile, n_tiles), stable=True)
    order = order[jnp.minimum(w, total - 1)]
    tile2, blk2 = tile[order], blk[order]
    first2 = jnp.logical_and(valid, jnp.concatenate([jnp.ones((1,), bool), tile2[1:] != tile2[:-1]]))
    last2 = jnp.logical_and(valid, jnp.concatenate([tile2[1:] != tile2[:-1], jnp.ones((1,), bool)]))
    last2 = jnp.logical_or(last2, w == total - 1)
    by_tile = (blk2, tile2, as_i32(first2), as_i32(valid), as_i32(last2))
    return by_block, by_tile


def _moe_layer(x1, mods, router, w1, w3, w2, g, b, ctx_len, alpha):
    d = x1.shape[-1]
    router_p = jnp.pad(router, ((0, 0), (0, LANES - N_EXPERTS)))
    h, logits = _moe_pre(x1, mods, router_p, ctx_len)
    t = h.shape[0]
    top_v, top_i = lax.top_k(logits[:, :N_EXPERTS], TOP_K)
    gates = jax.nn.softmax(top_v, axis=-1)
    e_flat = top_i.reshape(-1)
    onehot = (e_flat[:, None] == jnp.arange(N_EXPERTS)[None, :]).astype(jnp.int32)
    ranks = jnp.cumsum(onehot, axis=0) - onehot
    rank = jnp.sum(ranks * onehot, axis=1)
    counts = jnp.sum(onehot, axis=0)
    padded = (counts + MOE_ROWS - 1) // MOE_ROWS * MOE_ROWS
    pend = jnp.cumsum(padded)
    pstart = pend - padded
    slot = (pstart[e_flat] + rank).astype(jnp.int32)
    n_blocks = t * TOP_K // MOE_ROWS + N_EXPERTS
    assert t % MOE_TILE == 0
    n_tiles = t // MOE_TILE
    block_e = jnp.minimum(jnp.searchsorted(pend, jnp.arange(n_blocks) * MOE_ROWS, side="right"),
                          N_EXPERTS - 1).astype(jnp.int32)
    n_used = (pend[-1:] // MOE_ROWS).astype(jnp.int32)
    rank_at_tile = jnp.concatenate([ranks[::MOE_TILE * TOP_K], counts[None, :]], axis=0)
    by_block, by_tile = _work_lists(rank_at_tile, counts, pstart, block_e, n_used, n_blocks, n_tiles)
    slots_col = slot.reshape(t, TOP_K)
    slots = slots_col.T
    buf = _dispatch(h, slots, by_block)
    y_buf = _experts(buf, block_e, n_used, w1, w3, w2)
    return _collect(y_buf, slots, slots_col, gates, by_tile, x1, mods, g, b, ctx_len, alpha)


def _rope_tables(n_rows, ctx_len, dim, lane_lo):
    quarter = dim // 4
    inv = ROPE_THETA ** (-jnp.arange(quarter, dtype=F32) / quarter)
    rows = jnp.repeat(jnp.arange(n_rows, dtype=F32), GRID_W)
    cols = jnp.tile(jnp.arange(GRID_W, dtype=F32), n_rows)
    ang = jnp.concatenate([rows[:, None] * inv, rows[:, None] * inv,
                           cols[:, None] * inv, cols[:, None] * inv], axis=-1)
    sign = jnp.tile(jnp.concatenate([-jnp.ones(quarter, F32), jnp.ones(quarter, F32)]), 2)
    cos, sin = jnp.cos(ang), jnp.sin(ang) * sign
    length = cos.shape[0]
    if lane_lo == 0:
        reps = LANES // dim
        cos, sin = jnp.tile(cos, (1, reps)), jnp.tile(sin, (1, reps))
    else:
        pad = ((0, 0), (lane_lo, LANES - lane_lo - dim))
        cos = jnp.pad(cos, pad, constant_values=1.0)
        sin = jnp.pad(sin, pad)
    cos = jnp.concatenate([jnp.ones((ctx_len, LANES), F32), cos], axis=0)
    sin = jnp.concatenate([jnp.zeros((ctx_len, LANES), F32), sin], axis=0)
    return cos, sin


def _block_diag2(w):
    z = jnp.zeros_like(w[0])
    return jnp.concatenate([jnp.concatenate([w[0], z], axis=1), jnp.concatenate([z, w[1]], axis=1)], axis=0)


def _lambda_init(layer):
    return 0.8 - 0.6 * math.exp(-0.3 * layer)


def kernel(x, c, ctx, c_ctx, ada_w, ada_b, w_in, w_out, ln1_g, ln1_b, ln2_g, ln2_b,
           lam_q1, lam_k1, lam_q2, lam_k2, diff_norm_g, shift_mu, w0, w2, a0, a2, g2,
           k_k, k_a, r_k, lnx_g, lnx_b, q_norm_g, w_uq, kv_norm_g, w_ukv,
           ff_w1, ff_w3, ff_w2, router, moe_w1, moe_w3, moe_w2):
    bn, seq, d = x.shape
    ctx_len = ctx.shape[1]
    depth = ada_w.shape[0]
    assert d == D_MODEL and seq % TM == 0 and ctx_len % TM == 0 and seq % GRID_W == 0
    alpha = (2.0 * depth) ** 0.25
    n_grid_rows = seq // GRID_W
    cos_a, sin_a = _rope_tables(n_grid_rows, ctx_len, A_QK_DIM, 0)
    cos_c, sin_c = _rope_tables(n_grid_rows, ctx_len, C_ROPE, C_NOPE)

    cond_rows = 8 * ((bn + 1 + 7) // 8)
    cond = jnp.zeros((cond_rows, d), F32).at[:bn].set(c).at[bn].set(c_ctx)
    xs = jnp.concatenate([ctx, x], axis=1)

    for i in range(depth):
        with_ctx = i < depth - 1
        m = _ada(cond, ada_w[i], ada_b[i]).reshape(cond_rows, 6, d)
        mods = jnp.stack([jnp.broadcast_to(m[bn], (bn, 6, d)), m[:bn]], axis=1)

        wc = w_in[i][:, N_A + N_B:]
        kpe_w = jnp.pad(wc[:, C_Q_RANK + C_KV_RANK:], ((0, 0), (C_NOPE, LANES - C_NOPE - C_ROPE)))
        w_p = jnp.concatenate([w_in[i][:, :N_A + N_B], wc[:, :C_Q_RANK + C_KV_RANK], kpe_w], axis=1).astype(BF16)
        wq = w_uq[i].reshape(C_Q_RANK, C_HEADS, C_NOPE + C_ROPE)
        wq_p = jnp.pad(wq, ((0, 0), (0, 0), (0, LANES - C_NOPE - C_ROPE))).reshape(C_Q_RANK, -1).astype(BF16)
        wkv = w_ukv[i].reshape(C_KV_RANK, C_HEADS, C_NOPE + C_V)
        wk_p = jnp.pad(wkv[:, :, :C_NOPE], ((0, 0), (0, 0), (0, LANES - C_NOPE))).reshape(C_KV_RANK, -1).astype(BF16)
        wv_p = wkv[:, :, C_NOPE:].reshape(C_KV_RANK, -1).astype(BF16)
        mla_consts = (q_norm_g[i].reshape(1, -1), kv_norm_g[i].reshape(1, -1), wq_p, wk_p, wv_p)
        qa, ka, va, pb, qc, kc, vc = _inproj(xs, mods, w_p, cos_a, sin_a, mla_consts, cos_c, sin_c, ctx_len)

        lam_p = jnp.stack([lam_q1[i], lam_k1[i], lam_q2[i], lam_k2[i]])
        g_col = jnp.broadcast_to(diff_norm_g[i][:, None], (A_V_DIM, TM))
        first_tile = 0 if with_ctx else ctx_len // TM
        a_out, c_out = _attention([("diff", qa, ka, va), ("mla", qc, kc, vc)], lam_p, g_col, ctx_len=ctx_len,
                                  first_tile=first_tile, lam_init=_lambda_init(i))
        r, v, kkn, ld, kd, beta, gate, bonus = _rwkv_prep(
            pb, shift_mu[i].reshape(1, -1), w0[i].reshape(1, -1), _block_diag2(w2[i]).astype(BF16),
            a0[i].reshape(1, -1), _block_diag2(a2[i]).astype(BF16), g2[i].astype(BF16),
            k_k[i].reshape(1, -1), k_a[i].reshape(1, -1), r_k[i].reshape(1, -1), ctx_len)
        yf, yb = _rwkv_pairs(r, v, kkn, ld, kd, beta, ctx_len)

        wo = w_out[i].astype(BF16)
        x1 = _outproj(xs, mods, a_out, yf, yb, bonus, gate, c_out,
                      wo[:A_WIDTH], wo[A_WIDTH:A_WIDTH + B_WIDTH], wo[A_WIDTH + B_WIDTH:],
                      lnx_g[i].reshape(1, -1), lnx_b[i].reshape(1, -1),
                      ln1_g[i].reshape(1, -1), ln1_b[i].reshape(1, -1), ctx_len, first_tile, alpha)

        j = i // 2
        g2n, b2n = ln2_g[i].reshape(1, -1), ln2_b[i].reshape(1, -1)
        if i % 2 == 0:
            xs = _ffn(x1, mods, ff_w1[j].astype(BF16), ff_w3[j].astype(BF16), ff_w2[j].astype(BF16),
                      g2n, b2n, ctx_len, alpha)
        else:
            if with_ctx:
                raise NotImplementedError("routed FFN on the context rows is not needed at this depth")
            return _moe_layer(x1, mods, router[j], moe_w1[j].astype(BF16), moe_w3[j].astype(BF16),
                              moe_w2[j].astype(BF16), g2n, b2n, ctx_len, alpha)
    return xs[:, ctx_len:]
```
